```python
import functools
import jax, jax.numpy as jnp
from jax import lax
import numpy as np


D_MODEL = 1024
BATCH = 2
SEQ = 8192
DEPTH = 1
DEC_BATCH = 8
DEC_SEQ = 32
PAST_LEN = 1024

CHUNK = 64
MIX_WIDTH = D_MODEL
SB_WIDTH = MIX_WIDTH // 2
SB_HEADS = 8
SB_HEAD_DIM = SB_WIDTH // SB_HEADS
SB_BLOCK = 128
SB_SCALE = SB_HEAD_DIM ** -0.5
GLA_WIDTH = MIX_WIDTH - SB_WIDTH
GLA_HEADS = 4
GLA_KEY_WIDTH = GLA_WIDTH // 2
GLA_HEAD_K = GLA_KEY_WIDTH // GLA_HEADS
GLA_HEAD_V = GLA_WIDTH // GLA_HEADS
GLA_SCALE = GLA_HEAD_K ** -0.5
GATE_RANK = 16
GATE_TAU = 16.0
D_FF = ((8 * D_MODEL // 3 + 127) // 128) * 128
IN_WIDTH = 3 * SB_WIDTH + 2 * GLA_KEY_WIDTH + 2 * GLA_WIDTH + GATE_RANK
EPS = 1e-6

kernel_name = 'hymba_stickbreak_gla_macaron_stream_step'


def rmsnorm(x, g):
    xf = x.astype(jnp.float32)
    y = xf * lax.rsqrt(jnp.mean(xf * xf, axis=-1, keepdims=True) + EPS)
    return (y * g.astype(jnp.float32)).astype(x.dtype)


def swiglu(h, w_gate, w_up, w_down):
    return (jax.nn.silu(h @ w_gate) * (h @ w_up)) @ w_down


def sb_weights_apply(q, k, v, q_pos, k_pos):
    z = jnp.einsum('bqhd,bkhd->bhqk', q.astype(jnp.float32), k.astype(jnp.float32)) * SB_SCALE
    causal = k_pos[None, :] < q_pos[:, None]
    log_fail = jnp.where(causal, jax.nn.log_sigmoid(-z), 0.0)
    after = lax.cumsum(log_fail, axis=3, reverse=True) - log_fail
    w = jnp.where(causal, jnp.exp(jax.nn.log_sigmoid(z) + after), 0.0)
    return jnp.einsum('bhqk,bkhd->bqhd', w, v.astype(jnp.float32)).astype(v.dtype)


def sb_prompt(q, k, v):
    B, S, H, dh = q.shape
    k_pos = jnp.arange(S)

    def one_block(i):
        start = i * SB_BLOCK
        qb = lax.dynamic_slice_in_dim(q, start, SB_BLOCK, axis=1)
        return sb_weights_apply(qb, k, v, start + jnp.arange(SB_BLOCK), k_pos)

    out = lax.map(one_block, jnp.arange(S // SB_BLOCK))
    return jnp.moveaxis(out, 0, 1).reshape(B, S, H, dh)


def sb_sample(q, k_new, v_new, cache_k, cache_v):
    P = cache_k.shape[1]
    T = q.shape[1]
    k = jnp.concatenate([cache_k.astype(k_new.dtype), k_new], axis=1)
    v = jnp.concatenate([cache_v.astype(v_new.dtype), v_new], axis=1)
    return sb_weights_apply(q, k, v, P + jnp.arange(T), jnp.arange(P + T))


def gla_scan(q, k, v, log_a, s0, chunk):
    B, T, H, dk = q.shape
    dv = v.shape[-1]
    n = T // chunk

    def to_chunks(a):
        return a.astype(jnp.float32).reshape(B, n, chunk, H, a.shape[-1]).transpose(1, 0, 3, 2, 4)

    mask = jnp.tril(jnp.ones((chunk, chunk), dtype=bool))

    def step(s, inp):
        qc, kc, vc, gc = inp
        b = jnp.cumsum(gc, axis=2)
        o_inter = jnp.einsum('bhtd,bhde->bhte', qc * jnp.exp(b), s)
        diff = b[:, :, :, None, :] - b[:, :, None, :, :]
        decay = jnp.exp(jnp.where(mask[:, :, None], diff, -jnp.inf))
        scores = jnp.einsum('bhtd,bhsd,bhtsd->bhts', qc, kc, decay)
        o_intra = jnp.einsum('bhts,bhse->bhte', scores, vc)
        b_last = b[:, :, -1:, :]
        s_new = jnp.exp(b_last[:, :, 0, :])[..., None] * s + jnp.einsum(
            'bhsd,bhse->bhde', kc * jnp.exp(b_last - b), vc)
        return s_new, o_inter + o_intra

    s_fin, o = lax.scan(step, s0.astype(jnp.float32),
                        (to_chunks(q), to_chunks(k), to_chunks(v), to_chunks(log_a)))
    o = o.transpose(1, 0, 3, 2, 4).reshape(B, T, H, dv)
    return o.astype(v.dtype), s_fin


def token_mixer(h, w_in, w_gate_up, b_gate, g_q, g_k, g_sb_out, g_gla_out, w_out, sb_fn, s0, gla_chunk):
    B, T, _ = h.shape
    proj = h @ w_in
    sizes = (SB_WIDTH, SB_WIDTH, SB_WIDTH, GLA_KEY_WIDTH, GLA_KEY_WIDTH, GLA_WIDTH, GLA_WIDTH)
    points, acc = [], 0
    for sz in sizes:
        acc += sz
        points.append(acc)
    q_sb, k_sb, v_sb, q_g, k_g, v_g, r_g, lr_g = jnp.split(proj, points, axis=-1)
    q_sb = rmsnorm(q_sb.reshape(B, T, SB_HEADS, SB_HEAD_DIM), g_q)
    k_sb = rmsnorm(k_sb.reshape(B, T, SB_HEADS, SB_HEAD_DIM), g_k)
    v_sb = v_sb.reshape(B, T, SB_HEADS, SB_HEAD_DIM)
    o_sb = rmsnorm(sb_fn(q_sb, k_sb, v_sb), g_sb_out).reshape(B, T, SB_WIDTH)
    q_g = q_g.reshape(B, T, GLA_HEADS, GLA_HEAD_K) * GLA_SCALE
    k_g = k_g.reshape(B, T, GLA_HEADS, GLA_HEAD_K)
    v_g = v_g.reshape(B, T, GLA_HEADS, GLA_HEAD_V)
    log_a = jax.nn.log_sigmoid((lr_g @ w_gate_up + b_gate).astype(jnp.float32)) / GATE_TAU
    log_a = log_a.reshape(B, T, GLA_HEADS, GLA_HEAD_K)
    o_g, s_fin = gla_scan(q_g, k_g, v_g, log_a, s0, gla_chunk)
    o_g = rmsnorm(o_g, g_gla_out).reshape(B, T, GLA_WIDTH) * jax.nn.silu(r_g)
    y = jnp.concatenate([o_sb, o_g], axis=-1) @ w_out
    return y, k_sb, v_sb, s_fin.astype(h.dtype)


def layer(x, lw, sb_fn, s0, gla_chunk):
    (g_ffn1, w1g, w1u, w1d, g_mix, w_in, w_gate_up, b_gate, g_q, g_k, g_sb_out, g_gla_out,
     w_out, g_ffn2, w2g, w2u, w2d, g_final) = lw
    x = x + 0.5 * swiglu(rmsnorm(x, g_ffn1), w1g, w1u, w1d)
    mix, k, v, s = token_mixer(rmsnorm(x, g_mix), w_in, w_gate_up, b_gate, g_q, g_k, g_sb_out,
                               g_gla_out, w_out, sb_fn, s0, gla_chunk)
    x = x + mix
    x = x + 0.5 * swiglu(rmsnorm(x, g_ffn2), w2g, w2u, w2d)
    return rmsnorm(x, g_final), k, v, s


def setup_inputs(seed: int = 0) -> dict:
    key = jax.random.key(seed)
    ks = jax.random.split(key, 32)

    def nrm(k, shape, scale):
        return jax.random.normal(k, shape, jnp.float32) * scale

    def gain(k, n):
        return 1.0 + nrm(k, (DEPTH, n), 0.05)

    return {
        'x_prompt': nrm(ks[0], (BATCH, SEQ, D_MODEL), 1.0),
        'x_sample': nrm(ks[1], (DEC_BATCH, DEC_SEQ, D_MODEL), 1.0),
        'cache_sb_k': nrm(ks[2], (DEPTH, DEC_BATCH, PAST_LEN, SB_HEADS, SB_HEAD_DIM), 1.0),
        'cache_sb_v': nrm(ks[3], (DEPTH, DEC_BATCH, PAST_LEN, SB_HEADS, SB_HEAD_DIM), 1.0),
        'state_gla': nrm(ks[4], (DEPTH, DEC_BATCH, GLA_HEADS, GLA_HEAD_K, GLA_HEAD_V), 0.5),
        'g_ffn1': gain(ks[5], D_MODEL),
        'w_ffn1_gate': nrm(ks[6], (DEPTH, D_MODEL, D_FF), D_MODEL ** -0.5),
        'w_ffn1_up': nrm(ks[7], (DEPTH, D_MODEL, D_FF), D_MODEL ** -0.5),
        'w_ffn1_down': nrm(ks[8], (DEPTH, D_FF, D_MODEL), D_FF ** -0.5),
        'g_mix': gain(ks[9], D_MODEL),
        'w_in': nrm(ks[10], (DEPTH, D_MODEL, IN_WIDTH), D_MODEL ** -0.5),
        'w_gate_up': nrm(ks[11], (DEPTH, GATE_RANK, GLA_KEY_WIDTH), GATE_RANK ** -0.5),
        'b_gate': nrm(ks[12], (DEPTH, GLA_KEY_WIDTH), 0.1),
        'g_q': gain(ks[13], SB_HEAD_DIM),
        'g_k': gain(ks[14], SB_HEAD_DIM),
        'g_sb_out': gain(ks[15], SB_HEAD_DIM),
        'g_gla_out': gain(ks[16], GLA_HEAD_V),
        'w_out': nrm(ks[17], (DEPTH, MIX_WIDTH, D_MODEL), MIX_WIDTH ** -0.5),
        'g_ffn2': gain(ks[18], D_MODEL),
        'w_ffn2_gate': nrm(ks[19], (DEPTH, D_MODEL, D_FF), D_MODEL ** -0.5),
        'w_ffn2_up': nrm(ks[20], (DEPTH, D_MODEL, D_FF), D_MODEL ** -0.5),
        'w_ffn2_down': nrm(ks[21], (DEPTH, D_FF, D_MODEL), D_FF ** -0.5),
        'g_final': gain(ks[22], D_MODEL),
    }


def reference(x_prompt, x_sample, cache_sb_k, cache_sb_v, state_gla, g_ffn1, w_ffn1_gate, w_ffn1_up,
              w_ffn1_down, g_mix, w_in, w_gate_up, b_gate, g_q, g_k, g_sb_out, g_gla_out, w_out,
              g_ffn2, w_ffn2_gate, w_ffn2_up, w_ffn2_down, g_final):
    y_p, y_s = x_prompt, x_sample
    pk, pv, ps, sk, sv, ss = [], [], [], [], [], []
    for l in range(DEPTH):
        lw = (g_ffn1[l], w_ffn1_gate[l], w_ffn1_up[l], w_ffn1_down[l], g_mix[l], w_in[l],
              w_gate_up[l], b_gate[l], g_q[l], g_k[l], g_sb_out[l], g_gla_out[l], w_out[l],
              g_ffn2[l], w_ffn2_gate[l], w_ffn2_up[l], w_ffn2_down[l], g_final[l])
        s0 = jnp.zeros((x_prompt.shape[0], GLA_HEADS, GLA_HEAD_K, GLA_HEAD_V), jnp.float32)
        y_p, k_p, v_p, s_p = layer(y_p, lw, sb_prompt, s0, CHUNK)
        sb_fn = functools.partial(sb_sample, cache_k=cache_sb_k[l], cache_v=cache_sb_v[l])
        y_s, k_s, v_s, s_s = layer(y_s, lw, sb_fn, state_gla[l], x_sample.shape[1])
        pk.append(k_p); pv.append(v_p); ps.append(s_p)
        sk.append(k_s); sv.append(v_s); ss.append(s_s)
    prompt_sb_k = jnp.stack(pk)
    prompt_sb_v = jnp.stack(pv)
    prompt_gla_state = jnp.stack(ps)
    sample_sb_k = jnp.stack(sk)
    sample_sb_v = jnp.stack(sv)
    sample_gla_state = jnp.stack(ss)
    return (y_p, y_s, prompt_sb_k, prompt_sb_v, prompt_gla_state, sample_sb_k, sample_sb_v, sample_gla_state)
```

```python
import functools

import numpy as np
import jax
import jax.numpy as jnp
from jax import lax
from jax.experimental import pallas as pl
from jax.experimental.pallas import tpu as pltpu

F32 = jnp.float32
BF16 = jnp.bfloat16

D_MODEL = 1024
SB_HEADS = 8
SB_HEAD_DIM = 64
SB_WIDTH = SB_HEADS * SB_HEAD_DIM
SB_SCALE = SB_HEAD_DIM ** -0.5
GLA_HEADS = 4
GLA_HEAD_K = 64
GLA_HEAD_V = 128
GLA_KEY_WIDTH = GLA_HEADS * GLA_HEAD_K
GLA_WIDTH = GLA_HEADS * GLA_HEAD_V
GLA_SCALE = GLA_HEAD_K ** -0.5
GATE_RANK = 16
GATE_TAU = 16.0
D_FF = 2816
EPS = 1e-6

LANES = 128
GATE_PAD = LANES
MAIN_WIDTH = 3 * SB_WIDTH + 2 * GLA_KEY_WIDTH + 2 * GLA_WIDTH
IN_PAD = MAIN_WIDTH + GATE_PAD

FF_CHUNK = 256
SB_BLOCK = 128
GLA_CHUNK = 128
SB_STOP = -105.0

VMEM_LIMIT = 56 * 1024 * 1024


def _dot(a, b):
    return jnp.dot(a, b, preferred_element_type=F32)


def _dot_nt(a, b):
    return lax.dot_general(a, b, (((1,), (1,)), ((), ())), preferred_element_type=F32)


def _dot_tn(a, b):
    return lax.dot_general(a, b, (((0,), (0,)), ((), ())), preferred_element_type=F32)


def _split_bf16(x):
    hi = x.astype(BF16)
    lo = (x - hi.astype(F32)).astype(BF16)
    return hi, lo


def _dot_hilo(x, m):
    hi, lo = _split_bf16(x)
    return _dot(hi, m) + _dot(lo, m)


def _rms(x, g):
    ms = jnp.mean(x * x, axis=-1, keepdims=True)
    return x * lax.rsqrt(ms + EPS) * g


def _log_sigmoid(x):
    return jnp.minimum(x, 0.0) - jnp.log1p(jnp.exp(-jnp.abs(x)))


def _const_spec(shape):
    nd = len(shape)
    return pl.BlockSpec(shape, lambda *_: (0,) * nd, pipeline_mode=pl.Buffered(1))


def _params(n_grid):
    return pltpu.CompilerParams(dimension_semantics=("arbitrary",) * n_grid,
                                vmem_limit_bytes=VMEM_LIMIT)


def _ffn_kernel(x_ref, g_ref, wg_ref, wu_ref, wd_ref, gfin_ref, o_ref, a_ref, *, final_norm):
    x = x_ref[...]
    h = _rms(x, g_ref[...]).astype(BF16)
    for c in range(D_FF // FF_CHUNK):
        sl = slice(c * FF_CHUNK, (c + 1) * FF_CHUNK)
        gt = _dot(h, wg_ref[:, sl])
        up = _dot(h, wu_ref[:, sl])
        a_ref[:, sl] = (gt * jax.nn.sigmoid(gt) * up).astype(BF16)
    y = x + 0.5 * _dot(a_ref[...], wd_ref[...])
    if final_norm:
        y = _rms(y, gfin_ref[...])
    o_ref[...] = y


def _ffn(x, g, wg, wu, wd, gfin, *, tm, final_norm, name):
    n = x.shape[0]
    row = lambda i: (i, 0)
    return pl.pallas_call(
        functools.partial(_ffn_kernel, final_norm=final_norm),
        grid=(n // tm,),
        in_specs=[pl.BlockSpec((tm, D_MODEL), row),
                  _const_spec((1, D_MODEL)),
                  _const_spec((D_MODEL, D_FF)),
                  _const_spec((D_MODEL, D_FF)),
                  _const_spec((D_FF, D_MODEL)),
                  _const_spec((1, D_MODEL))],
        out_specs=pl.BlockSpec((tm, D_MODEL), row),
        out_shape=jax.ShapeDtypeStruct((n, D_MODEL), F32),
        scratch_shapes=[pltpu.VMEM((tm, D_FF), BF16)],
        compiler_params=_params(1),
        name=name,
    )(x, g, wg, wu, wd, gfin)


def _mixin_kernel(x_ref, g_ref, w_ref, wgu_ref, bg_ref, gq_ref, gk_ref, grp_ref,
                  qsb_ref, ksb_ref, ksb16_ref, vsb_ref, vsb16_ref,
                  qg_ref, kg_ref, vg16_ref, r_ref, la_ref):
    h = _rms(x_ref[...], g_ref[...]).astype(BF16)
    grp = grp_ref[...]

    def proj(lo, width):
        return _dot(h, w_ref[:, lo:lo + width])

    def head_norm(y, gain):
        ms = _dot_hilo(y * y, grp) * (1.0 / SB_HEAD_DIM)
        return y * lax.rsqrt(ms + EPS) * gain

    q = head_norm(proj(0, SB_WIDTH), gq_ref[...])
    qsb_ref[...] = (q * SB_SCALE).astype(BF16)
    k = head_norm(proj(SB_WIDTH, SB_WIDTH), gk_ref[...])
    ksb_ref[...] = k
    ksb16_ref[...] = k.astype(BF16)
    v = proj(2 * SB_WIDTH, SB_WIDTH)
    vsb_ref[...] = v
    vsb16_ref[...] = v.astype(BF16)
    off = 3 * SB_WIDTH
    qg_ref[...] = proj(off, GLA_KEY_WIDTH) * GLA_SCALE
    kg_ref[...] = proj(off + GLA_KEY_WIDTH, GLA_KEY_WIDTH)
    vg16_ref[...] = proj(off + 2 * GLA_KEY_WIDTH, GLA_WIDTH).astype(BF16)
    r_ref[...] = proj(off + 2 * GLA_KEY_WIDTH + GLA_WIDTH, GLA_WIDTH)
    lr = proj(MAIN_WIDTH, GATE_PAD).astype(BF16)
    gate = _dot(lr, wgu_ref[...]) + bg_ref[...]
    la_ref[...] = _log_sigmoid(gate) * (1.0 / GATE_TAU)


def _mixin(x, g, w, wgu, bg, gq, gk, grp, *, tm):
    n = x.shape[0]
    row = lambda i: (i, 0)
    widths = [(SB_WIDTH, BF16), (SB_WIDTH, F32), (SB_WIDTH, BF16), (SB_WIDTH, F32), (SB_WIDTH, BF16),
              (GLA_KEY_WIDTH, F32), (GLA_KEY_WIDTH, F32), (GLA_WIDTH, BF16), (GLA_WIDTH, F32),
              (GLA_KEY_WIDTH, F32)]
    return pl.pallas_call(
        _mixin_kernel,
        grid=(n // tm,),
        in_specs=[pl.BlockSpec((tm, D_MODEL), row),
                  _const_spec((1, D_MODEL)),
                  _const_spec((D_MODEL, IN_PAD)),
                  _const_spec((GATE_PAD, GLA_KEY_WIDTH)),
                  _const_spec((1, GLA_KEY_WIDTH)),
                  _const_spec((1, SB_WIDTH)),
                  _const_spec((1, SB_WIDTH)),
                  _const_spec((SB_WIDTH, SB_WIDTH))],
        out_specs=[pl.BlockSpec((tm, wd), row) for wd, _ in widths],
        out_shape=[jax.ShapeDtypeStruct((n, wd), dt) for wd, dt in widths],
        compiler_params=_params(1),
        name="mixer_in",
    )(x, g, w, wgu, bg, gq, gk, grp)


def _sb_block(qm, kblk, vblk, uo, carry, acc, mask):
    z = _dot_nt(qm, kblk)
    t = jnp.log1p(jnp.exp(-jnp.abs(z)))
    log_fail = -jnp.maximum(z, 0.0) - t
    log_beta = jnp.minimum(z, 0.0) - t
    if mask is not None:
        log_fail = jnp.where(mask, log_fail, 0.0)
    sums = _dot_hilo(log_fail, uo)
    after = carry + sums[:, :SB_BLOCK]
    w = jnp.exp(log_beta + after)
    if mask is not None:
        w = jnp.where(mask, w, 0.0)
    acc = acc + _dot(w.astype(BF16), vblk)
    carry = carry + sums[:, SB_BLOCK:]
    return carry, acc


def _sb_sweep(q2, diag_k, diag_v, prev_k, prev_v, n_prev, uo, carry_ref, acc_ref, o_ref):
    tq = q2.shape[0]
    lane = lax.broadcasted_iota(jnp.int32, (tq, LANES), 1)
    rowi = lax.broadcasted_iota(jnp.int32, (tq, SB_BLOCK), 0)
    coli = lax.broadcasted_iota(jnp.int32, (tq, SB_BLOCK), 1)
    mask = coli < rowi
    head_lanes = (lane < SB_HEAD_DIM, lane >= SB_HEAD_DIM)
    qm = [jnp.where(hl, q2, jnp.zeros_like(q2)) for hl in head_lanes]
    zero = jnp.zeros((tq, SB_BLOCK), F32)
    for h in range(2):
        c, a = _sb_block(qm[h], diag_k, diag_v, uo, zero, zero, mask)
        carry_ref[h] = c
        acc_ref[h] = a

    def live():
        return jnp.max(jnp.maximum(carry_ref[0], carry_ref[1])) > SB_STOP

    def cond(st):
        kb, go = st
        return jnp.logical_and(kb >= 0, go)

    def body(st):
        kb, _ = st
        kblk = prev_k(kb)
        vblk = prev_v(kb)
        for h in range(2):
            c, a = _sb_block(qm[h], kblk, vblk, uo, carry_ref[h], acc_ref[h], None)
            carry_ref[h] = c
            acc_ref[h] = a
        return kb - 1, live()

    lax.while_loop(cond, body, (n_prev - 1, live()))
    o_ref[...] = jnp.where(head_lanes[0], acc_ref[0], acc_ref[1])


def _sb_prompt_kernel(q_ref, k_ref, v_ref, uo_ref, o_ref, carry_ref, acc_ref):
    i = pl.program_id(2)

    def blk(ref):
        return lambda kb: ref[pl.ds(pl.multiple_of(kb * SB_BLOCK, SB_BLOCK), SB_BLOCK), :]

    _sb_sweep(q_ref[...], blk(k_ref)(i), blk(v_ref)(i), blk(k_ref), blk(v_ref), i,
              uo_ref[...], carry_ref, acc_ref, o_ref)


def _sb_prompt(q16, k16, v16, uo, *, batch, seq):
    nq = seq // SB_BLOCK
    pairs = SB_WIDTH // LANES
    kv_spec = pl.BlockSpec((seq, LANES), lambda b, p, i: (b, p))
    q_spec = pl.BlockSpec((SB_BLOCK, LANES), lambda b, p, i: (b * nq + i, p))
    return pl.pallas_call(
        _sb_prompt_kernel,
        grid=(batch, pairs, nq),
        in_specs=[q_spec, kv_spec, kv_spec, _const_spec((SB_BLOCK, 2 * SB_BLOCK))],
        out_specs=q_spec,
        out_shape=jax.ShapeDtypeStruct((batch * seq, SB_WIDTH), F32),
        scratch_shapes=[pltpu.VMEM((2, SB_BLOCK, SB_BLOCK), F32),
                        pltpu.VMEM((2, SB_BLOCK, LANES), F32)],
        compiler_params=_params(3),
        name="sb_prompt",
    )(q16, k16, v16, uo)


def _sb_sample_kernel(q_ref, kn_ref, vn_ref, kc_ref, vc_ref, uo_ref, o_ref, carry_ref, acc_ref, *, n_cache):
    def blk(ref):
        return lambda kb: ref[pl.ds(pl.multiple_of(kb * SB_BLOCK, SB_BLOCK), SB_BLOCK), :]

    _sb_sweep(q_ref[...], kn_ref[...], vn_ref[...], blk(kc_ref), blk(vc_ref), n_cache,
              uo_ref[...], carry_ref, acc_ref, o_ref)


def _sb_sample(q16, kn16, vn16, kc16, vc16, uo, *, batch, tq, past):
    pairs = SB_WIDTH // LANES
    q_spec = pl.BlockSpec((tq, LANES), lambda b, p: (b, p))
    new_spec = pl.BlockSpec((SB_BLOCK, LANES), lambda b, p: (b, p))
    cache_spec = pl.BlockSpec((past, LANES), lambda b, p: (b, p))
    return pl.pallas_call(
        functools.partial(_sb_sample_kernel, n_cache=past // SB_BLOCK),
        grid=(batch, pairs),
        in_specs=[q_spec, new_spec, new_spec, cache_spec, cache_spec,
                  _const_spec((SB_BLOCK, 2 * SB_BLOCK))],
        out_specs=q_spec,
        out_shape=jax.ShapeDtypeStruct((batch * tq, SB_WIDTH), F32),
        scratch_shapes=[pltpu.VMEM((2, tq, SB_BLOCK), F32),
                        pltpu.VMEM((2, tq, LANES), F32)],
        compiler_params=_params(2),
        name="sb_sample",
    )(q16, kn16, vn16, kc16, vc16, uo)


def _gla_levels(c):
    return int(np.log2(c))


def _gla_tables(c):
    nlev = _gla_levels(c)
    t = np.arange(c)[:, None]
    j = np.arange(c)[None, :]
    mats = [(j <= t), (j > t)]
    for lev in range(1, nlev + 1):
        m = 1 << (lev - 1)
        start = (t // (2 * m)) * (2 * m)
        mats.append(((t % (2 * m)) >= m) & (j >= start + m) & (j <= t))
    for lev in range(1, nlev + 1):
        m = 1 << (lev - 1)
        start = (t // (2 * m)) * (2 * m)
        mats.append(((t % (2 * m)) < m) & (j > t) & (j <= start + m - 1))
    mats = np.concatenate([a.astype(np.float32) for a in mats], axis=0)
    x = t ^ j
    level = np.where(j > t, -1, np.where(x == 0, 0, np.floor(np.log2(np.maximum(x, 1))).astype(np.int64) + 1))
    return jnp.asarray(mats, BF16), jnp.asarray(level, jnp.int32)


def _gla_kernel(q_ref, k_ref, la_ref, v_ref, s0_ref, mats_ref, lv_ref, o_ref, sfin_ref, st_ref):
    c = GLA_CHUNK
    nlev = _gla_levels(c)
    step = pl.program_id(1)

    @pl.when(step == 0)
    def _():
        st_ref[...] = s0_ref[0]

    q = q_ref[...]
    k = k_ref[...]
    sums = _dot_hilo_left(mats_ref[...], la_ref[...])

    def rows(i):
        return sums[i * c:(i + 1) * c]

    b = rows(0)
    qe = (q * jnp.exp(b)).astype(BF16)
    ke = (k * jnp.exp(rows(1))).astype(BF16)
    b_last = b[c - 1:c, :]
    q_lev = [q.astype(BF16)] + [(q * jnp.exp(rows(1 + lev))).astype(BF16) for lev in range(1, nlev + 1)]
    k_lev = [k.astype(BF16)] + [(k * jnp.exp(rows(1 + nlev + lev))).astype(BF16) for lev in range(1, nlev + 1)]
    lv = lv_ref[...]
    lane = lax.broadcasted_iota(jnp.int32, (c, LANES), 1)
    head_lanes = (lane < GLA_HEAD_K, lane >= GLA_HEAD_K)
    lane_sq = lax.broadcasted_iota(jnp.int32, (GLA_HEAD_V, LANES), 1) < GLA_HEAD_K

    def pick(x, hl):
        return jnp.where(hl, x, jnp.zeros_like(x))

    for p in range(GLA_HEADS // 2):
        sl = slice(p * LANES, (p + 1) * LANES)
        st = st_ref[p]
        st16 = st.astype(BF16)
        upd = []
        for r in range(2):
            hd = 2 * p + r
            vh = v_ref[:, hd * GLA_HEAD_V:(hd + 1) * GLA_HEAD_V]
            scores = jnp.zeros((c, c), F32)
            for lev in range(nlev + 1):
                zl = _dot_nt(pick(q_lev[lev][:, sl], head_lanes[r]), k_lev[lev][:, sl])
                scores = jnp.where(lv == lev, zl, scores)
            o_h = _dot_nt(pick(qe[:, sl], head_lanes[r]), st16) + _dot(scores.astype(BF16), vh)
            o_ref[:, hd * GLA_HEAD_V:(hd + 1) * GLA_HEAD_V] = o_h
            upd.append(_dot_tn(vh, ke[:, sl]))
        st_ref[p] = st * jnp.exp(b_last[:, sl]) + jnp.where(lane_sq, upd[0], upd[1])

    @pl.when(step == pl.num_programs(1) - 1)
    def _():
        sfin_ref[0] = st_ref[...]


def _dot_hilo_left(m, x):
    hi, lo = _split_bf16(x)
    return _dot(m, hi) + _dot(m, lo)


def _gla(qg, kg, la, vg16, s0, mats, lv, *, batch, seq):
    c = GLA_CHUNK
    nc = seq // c
    pairs = GLA_HEADS // 2
    row = lambda b, t: (b * nc + t, 0)
    st_spec = pl.BlockSpec((1, pairs, GLA_HEAD_V, LANES), lambda b, t: (b, 0, 0, 0))
    return pl.pallas_call(
        _gla_kernel,
        grid=(batch, nc),
        in_specs=[pl.BlockSpec((c, GLA_KEY_WIDTH), row),
                  pl.BlockSpec((c, GLA_KEY_WIDTH), row),
                  pl.BlockSpec((c, GLA_KEY_WIDTH), row),
                  pl.BlockSpec((c, GLA_WIDTH), row),
                  st_spec,
                  _const_spec(mats.shape),
                  _const_spec(lv.shape)],
        out_specs=[pl.BlockSpec((c, GLA_WIDTH), row), st_spec],
        out_shape=[jax.ShapeDtypeStruct((batch * seq, GLA_WIDTH), F32),
                   jax.ShapeDtypeStruct((batch, pairs, GLA_HEAD_V, LANES), F32)],
        scratch_shapes=[pltpu.VMEM((pairs, GLA_HEAD_V, LANES), F32)],
        compiler_params=_params(2),
        name="gla",
    )(qg, kg, la, vg16, s0, mats, lv)


def _mixout_kernel(x_ref, osb_ref, og_ref, r_ref, gsb_ref, ggla_ref, grp_sb_ref, grp_gla_ref, w_ref, o_ref):
    def head_norm(y, grp, width, gain):
        ms = _dot_hilo(y * y, grp) * (1.0 / width)
        return y * lax.rsqrt(ms + EPS) * gain

    o_sb = head_norm(osb_ref[...], grp_sb_ref[...], SB_HEAD_DIM, gsb_ref[...])
    r = r_ref[...]
    o_g = head_norm(og_ref[...], grp_gla_ref[...], GLA_HEAD_V, ggla_ref[...]) * (r * jax.nn.sigmoid(r))
    mix = _dot(o_sb.astype(BF16), w_ref[:SB_WIDTH, :]) + _dot(o_g.astype(BF16), w_ref[SB_WIDTH:, :])
    o_ref[...] = x_ref[...] + mix


def _mixout(x, osb, og, r, gsb, ggla, grp_sb, grp_gla, w, *, tm):
    n = x.shape[0]
    row = lambda i: (i, 0)
    return pl.pallas_call(
        _mixout_kernel,
        grid=(n // tm,),
        in_specs=[pl.BlockSpec((tm, D_MODEL), row),
                  pl.BlockSpec((tm, SB_WIDTH), row),
                  pl.BlockSpec((tm, GLA_WIDTH), row),
                  pl.BlockSpec((tm, GLA_WIDTH), row),
                  _const_spec((1, SB_WIDTH)),
                  _const_spec((1, GLA_WIDTH)),
                  _const_spec((SB_WIDTH, SB_WIDTH)),
                  _const_spec((GLA_WIDTH, GLA_WIDTH)),
                  _const_spec((D_MODEL, D_MODEL))],
        out_specs=pl.BlockSpec((tm, D_MODEL), row),
        out_shape=jax.ShapeDtypeStruct((n, D_MODEL), F32),
        compiler_params=_params(1),
        name="mixer_out",
    )(x, osb, og, r, gsb, ggla, grp_sb, grp_gla, w)


def _group_ones(width, group):
    idx = np.arange(width) // group
    return jnp.asarray(idx[:, None] == idx[None, :], BF16)


def _suffix_and_total():
    j = np.arange(SB_BLOCK)[:, None]
    s = np.arange(SB_BLOCK)[None, :]
    return jnp.asarray(np.concatenate([j > s, np.ones((SB_BLOCK, SB_BLOCK), bool)], axis=1), BF16)


def _state_to_kernel(s):
    b = s.shape[0]
    s = s.reshape(b, GLA_HEADS // 2, 2, GLA_HEAD_K, GLA_HEAD_V)
    return s.transpose(0, 1, 4, 2, 3).reshape(b, GLA_HEADS // 2, GLA_HEAD_V, LANES)


def _state_from_kernel(s):
    b = s.shape[0]
    s = s.reshape(b, GLA_HEADS // 2, GLA_HEAD_V, 2, GLA_HEAD_K)
    return s.transpose(0, 1, 3, 4, 2).reshape(b, GLA_HEADS, GLA_HEAD_K, GLA_HEAD_V)


def _layer(x, lw, consts, *, batch, seq, tm, cache=None, state=None):
    (g1, w1g, w1u, w1d, g_mix, w_in, wgu, bg, gq, gk, gsb, ggla, w_out, g2, w2g, w2u, w2d, gfin) = lw
    grp64, grp128, uo, mats, lv = consts
    x1 = _ffn(x, g1, w1g, w1u, w1d, gfin, tm=tm, final_norm=False, name="ffn1")
    (qsb16, ksb, ksb16, vsb, vsb16, qg, kg, vg16, r, la) = _mixin(x1, g_mix, w_in, wgu, bg, gq, gk, grp64, tm=tm)

    if cache is None:
        o_sb = _sb_prompt(qsb16, ksb16, vsb16, uo, batch=batch, seq=seq)
    else:
        pad = lambda a: jnp.pad(a.reshape(batch, seq, SB_WIDTH),
                                ((0, 0), (0, SB_BLOCK - seq), (0, 0))).reshape(batch * SB_BLOCK, SB_WIDTH)
        kc16, vc16 = cache
        o_sb = _sb_sample(qsb16, pad(ksb16), pad(vsb16), kc16, vc16, uo,
                          batch=batch, tq=seq, past=kc16.shape[0] // batch)

    c = GLA_CHUNK
    seq_pad = -(-seq // c) * c
    if seq_pad != seq:
        padt = lambda a: jnp.pad(a.reshape(batch, seq, a.shape[-1]),
                                 ((0, 0), (0, seq_pad - seq), (0, 0))).reshape(batch * seq_pad, a.shape[-1])
        qg_p, kg_p, la_p, vg_p = padt(qg), padt(kg), padt(la), padt(vg16)
    else:
        qg_p, kg_p, la_p, vg_p = qg, kg, la, vg16
    if state is None:
        state = jnp.zeros((batch, GLA_HEADS // 2, GLA_HEAD_V, LANES), F32)
    o_g, s_fin = _gla(qg_p, kg_p, la_p, vg_p, state, mats, lv, batch=batch, seq=seq_pad)
    if seq_pad != seq:
        o_g = o_g.reshape(batch, seq_pad, GLA_WIDTH)[:, :seq].reshape(batch * seq, GLA_WIDTH)

    x2 = _mixout(x1, o_sb, o_g, r, gsb, ggla, grp64, grp128, w_out, tm=tm)
    y = _ffn(x2, g2, w2g, w2u, w2d, gfin, tm=tm, final_norm=True, name="ffn2")
    return y, ksb, vsb, _state_from_kernel(s_fin)


def kernel(x_prompt, x_sample, cache_sb_k, cache_sb_v, state_gla, g_ffn1, w_ffn1_gate, w_ffn1_up,
           w_ffn1_down, g_mix, w_in, w_gate_up, b_gate, g_q, g_k, g_sb_out, g_gla_out, w_out,
           g_ffn2, w_ffn2_gate, w_ffn2_up, w_ffn2_down, g_final):
    depth = w_in.shape[0]
    batch, seq, _ = x_prompt.shape
    dec_batch, dec_seq, _ = x_sample.shape
    past = cache_sb_k.shape[2]
    consts = (_group_ones(SB_WIDTH, SB_HEAD_DIM), _group_ones(GLA_WIDTH, GLA_HEAD_V),
              _suffix_and_total(), *_gla_tables(GLA_CHUNK))

    y_p = x_prompt.reshape(batch * seq, D_MODEL)
    y_s = x_sample.reshape(dec_batch * dec_seq, D_MODEL)
    outs = [[] for _ in range(6)]
    for l in range(depth):
        row = lambda a: a[l].reshape(1, -1)
        w_main = w_in[l][:, :MAIN_WIDTH]
        w_lr = jnp.pad(w_in[l][:, MAIN_WIDTH:], ((0, 0), (0, GATE_PAD - GATE_RANK)))
        lw = (row(g_ffn1), w_ffn1_gate[l].astype(BF16), w_ffn1_up[l].astype(BF16), w_ffn1_down[l].astype(BF16),
              row(g_mix), jnp.concatenate([w_main, w_lr], axis=1).astype(BF16),
              jnp.pad(w_gate_up[l], ((0, GATE_PAD - GATE_RANK), (0, 0))).astype(BF16), row(b_gate),
              jnp.tile(g_q[l], SB_HEADS).reshape(1, -1), jnp.tile(g_k[l], SB_HEADS).reshape(1, -1),
              jnp.tile(g_sb_out[l], SB_HEADS).reshape(1, -1), jnp.tile(g_gla_out[l], GLA_HEADS).reshape(1, -1),
              w_out[l].astype(BF16), row(g_ffn2), w_ffn2_gate[l].astype(BF16), w_ffn2_up[l].astype(BF16),
              w_ffn2_down[l].astype(BF16), row(g_final))
        y_p, k_p, v_p, s_p = _layer(y_p, lw, consts, batch=batch, seq=seq, tm=512)
        cache = (cache_sb_k[l].reshape(dec_batch * past, SB_WIDTH).astype(BF16),
                 cache_sb_v[l].reshape(dec_batch * past, SB_WIDTH).astype(BF16))
        y_s, k_s, v_s, s_s = _layer(y_s, lw, consts, batch=dec_batch, seq=dec_seq, tm=256,
                                    cache=cache, state=_state_to_kernel(state_gla[l]))
        shape_p = (batch, seq, SB_HEADS, SB_HEAD_DIM)
        shape_s = (dec_batch, dec_seq, SB_HEADS, SB_HEAD_DIM)
        for lst, val in zip(outs, (k_p.reshape(shape_p), v_p.reshape(shape_p), s_p,
                                   k_s.reshape(shape_s), v_s.reshape(shape_s), s_s)):
            lst.append(val)
    return (y_p.reshape(batch, seq, D_MODEL), y_s.reshape(dec_batch, dec_seq, D_MODEL),
            *[jnp.stack(o) for o in outs])
```

```python
import functools

import numpy as np
import jax
import jax.numpy as jnp
from jax import lax
from jax.experimental import pallas as pl
from jax.experimental.pallas import tpu as pltpu

F32 = jnp.float32
BF16 = jnp.bfloat16

D_MODEL = 1024
SB_HEADS = 8
SB_HEAD_DIM = 64
SB_WIDTH = SB_HEADS * SB_HEAD_DIM
SB_SCALE = SB_HEAD_DIM ** -0.5
GLA_HEADS = 4
GLA_HEAD_K = 64
GLA_HEAD_V = 128
GLA_KEY_WIDTH = GLA_HEADS * GLA_HEAD_K
GLA_WIDTH = GLA_HEADS * GLA_HEAD_V
GLA_SCALE = GLA_HEAD_K ** -0.5
GATE_RANK = 16
GATE_TAU = 16.0
D_FF = 2816
EPS = 1e-6

LANES = 128
GATE_PAD = LANES
MAIN_WIDTH = 3 * SB_WIDTH + 2 * GLA_KEY_WIDTH + 2 * GLA_WIDTH
IN_PAD = MAIN_WIDTH + GATE_PAD

FF_CHUNK = 256
SB_BLOCK = 128
SB_WINDOW = 3
SB_PAIRS_PER_STEP = 2
GLA_CHUNK = 128
SB_STOP = -105.0

VMEM_LIMIT = 56 * 1024 * 1024


def _dot(a, b):
    return jnp.dot(a, b, preferred_element_type=F32)


def _dot_nt(a, b):
    return lax.dot_general(a, b, (((1,), (1,)), ((), ())), preferred_element_type=F32)


def _dot_tn(a, b):
    return lax.dot_general(a, b, (((0,), (0,)), ((), ())), preferred_element_type=F32)


def _split_bf16(x):
    hi = x.astype(BF16)
    lo = (x - hi.astype(F32)).astype(BF16)
    return hi, lo


def _dot_hilo(x, m):
    hi, lo = _split_bf16(x)
    return _dot(hi, m) + _dot(lo, m)


def _rms(x, g):
    ms = jnp.mean(x * x, axis=-1, keepdims=True)
    return x * lax.rsqrt(ms + EPS) * g


def _log_sigmoid(x):
    return jnp.minimum(x, 0.0) - jnp.log1p(jnp.exp(-jnp.abs(x)))


def _const_spec(shape):
    nd = len(shape)
    return pl.BlockSpec(shape, lambda *_: (0,) * nd, pipeline_mode=pl.Buffered(1))


def _params(n_grid):
    return pltpu.CompilerParams(dimension_semantics=("arbitrary",) * n_grid,
                                vmem_limit_bytes=VMEM_LIMIT)


def _half_step_ffn(x, g_ref, wg_ref, wu_ref, wd_ref, a_ref):
    h = _rms(x, g_ref[...]).astype(BF16)
    for c in range(D_FF // FF_CHUNK):
        sl = slice(c * FF_CHUNK, (c + 1) * FF_CHUNK)
        gt = _dot(h, wg_ref[:, sl])
        up = _dot(h, wu_ref[:, sl])
        a_ref[:, sl] = (gt * jax.nn.sigmoid(gt) * up).astype(BF16)
    return x + 0.5 * _dot(a_ref[...], wd_ref[...])


def _ffn_kernel(x_ref, g_ref, wg_ref, wu_ref, wd_ref, o_ref, a_ref):
    o_ref[...] = _half_step_ffn(x_ref[...], g_ref, wg_ref, wu_ref, wd_ref, a_ref)


def _ffn_specs():
    return [_const_spec((1, D_MODEL)), _const_spec((D_MODEL, D_FF)), _const_spec((D_MODEL, D_FF)),
            _const_spec((D_FF, D_MODEL))]


def _ffn(x, g, wg, wu, wd, *, tm):
    n = x.shape[0]
    row = lambda i: (i, 0)
    return pl.pallas_call(
        _ffn_kernel,
        grid=(n // tm,),
        in_specs=[pl.BlockSpec((tm, D_MODEL), row)] + _ffn_specs(),
        out_specs=pl.BlockSpec((tm, D_MODEL), row),
        out_shape=jax.ShapeDtypeStruct((n, D_MODEL), F32),
        scratch_shapes=[pltpu.VMEM((tm, D_FF), BF16)],
        compiler_params=_params(1),
        name="ffn1",
    )(x, g, wg, wu, wd)


def _mixin_kernel(x_ref, g_ref, w_ref, wgu_ref, bg_ref, gq_ref, gk_ref, grp_ref,
                  qsb_ref, ksb_ref, ksb16_ref, vsb_ref, vsb16_ref,
                  qg_ref, kg_ref, vg16_ref, r_ref, la_ref):
    h = _rms(x_ref[...], g_ref[...]).astype(BF16)
    grp = grp_ref[...]

    def proj(lo, width):
        return _dot(h, w_ref[:, lo:lo + width])

    def head_norm(y, gain):
        ms = _dot_hilo(y * y, grp) * (1.0 / SB_HEAD_DIM)
        return y * lax.rsqrt(ms + EPS) * gain

    q = head_norm(proj(0, SB_WIDTH), gq_ref[...])
    qsb_ref[...] = (q * SB_SCALE).astype(BF16)
    k = head_norm(proj(SB_WIDTH, SB_WIDTH), gk_ref[...])
    ksb_ref[...] = k
    ksb16_ref[...] = k.astype(BF16)
    v = proj(2 * SB_WIDTH, SB_WIDTH)
    vsb_ref[...] = v
    vsb16_ref[...] = v.astype(BF16)
    off = 3 * SB_WIDTH
    qg_ref[...] = proj(off, GLA_KEY_WIDTH) * GLA_SCALE
    kg_ref[...] = proj(off + GLA_KEY_WIDTH, GLA_KEY_WIDTH)
    vg16_ref[...] = proj(off + 2 * GLA_KEY_WIDTH, GLA_WIDTH).astype(BF16)
    r_ref[...] = proj(off + 2 * GLA_KEY_WIDTH + GLA_WIDTH, GLA_WIDTH)
    lr = proj(MAIN_WIDTH, GATE_PAD).astype(BF16)
    gate = _dot(lr, wgu_ref[...]) + bg_ref[...]
    la_ref[...] = _log_sigmoid(gate) * (1.0 / GATE_TAU)


def _mixin(x, g, w, wgu, bg, gq, gk, grp, *, tm):
    n = x.shape[0]
    row = lambda i: (i, 0)
    widths = [(SB_WIDTH, BF16), (SB_WIDTH, F32), (SB_WIDTH, BF16), (SB_WIDTH, F32), (SB_WIDTH, BF16),
              (GLA_KEY_WIDTH, F32), (GLA_KEY_WIDTH, F32), (GLA_WIDTH, BF16), (GLA_WIDTH, F32),
              (GLA_KEY_WIDTH, F32)]
    return pl.pallas_call(
        _mixin_kernel,
        grid=(n // tm,),
        in_specs=[pl.BlockSpec((tm, D_MODEL), row),
                  _const_spec((1, D_MODEL)),
                  _const_spec((D_MODEL, IN_PAD)),
                  _const_spec((GATE_PAD, GLA_KEY_WIDTH)),
                  _const_spec((1, GLA_KEY_WIDTH)),
                  _const_spec((1, SB_WIDTH)),
                  _const_spec((1, SB_WIDTH)),
                  _const_spec((SB_WIDTH, SB_WIDTH))],
        out_specs=[pl.BlockSpec((tm, wd), row) for wd, _ in widths],
        out_shape=[jax.ShapeDtypeStruct((n, wd), dt) for wd, dt in widths],
        compiler_params=_params(1),
        name="mixer_in",
    )(x, g, w, wgu, bg, gq, gk, grp)


def _sb_log_terms(z):
    t = jnp.log(1.0 + jnp.exp(-jnp.abs(z)))
    return -jnp.maximum(z, 0.0) - t, jnp.minimum(z, 0.0) - t


def _sb_kernel(q_ref, k_ref, v_ref, uwin_ref, uo_ref, o_ref, carry_ref, acc_ref, *, q_block):
    tq = q_ref.shape[0]
    pairs = q_ref.shape[1] // LANES
    heads = 2 * pairs
    i = pl.program_id(2) if q_block is None else q_block
    lane = lax.broadcasted_iota(jnp.int32, (tq, LANES), 1)
    head_lanes = (lane < SB_HEAD_DIM, lane >= SB_HEAD_DIM)
    rowi = lax.broadcasted_iota(jnp.int32, (heads * tq, SB_BLOCK), 0)
    coli = lax.broadcasted_iota(jnp.int32, (heads * tq, SB_BLOCK), 1)
    causal = coli < (rowi & (tq - 1))

    def pair_lanes(p):
        return slice(p * LANES, (p + 1) * LANES)

    def q_pair(p):
        q2 = q_ref[:, pair_lanes(p)]
        zero = jnp.zeros_like(q2)
        return jnp.concatenate([jnp.where(hl, q2, zero) for hl in head_lanes], axis=0)

    def rows(ref, p, first_block, nblocks):
        start = first_block * SB_BLOCK
        if not isinstance(start, int):
            start = pl.multiple_of(start, SB_BLOCK)
        return ref[pl.ds(start, nblocks * SB_BLOCK), pair_lanes(p)]

    def sweep(first_block, nblocks, usum, own, carry):
        nk = nblocks * SB_BLOCK

        def masked(x):
            if not own:
                return x
            tail = jnp.where(causal, x[:, nk - SB_BLOCK:], 0.0)
            return tail if nblocks == 1 else jnp.concatenate([x[:, :nk - SB_BLOCK], tail], axis=1)

        z = jnp.concatenate([_dot_nt(q_pair(p), rows(k_ref, p, first_block, nblocks))
                             for p in range(pairs)], axis=0)
        log_fail, log_beta = _sb_log_terms(z)
        hi, lo = _split_bf16(masked(log_fail))
        sums = _dot(jnp.concatenate([hi, lo], axis=0), usum)
        sums = sums[:heads * tq] + sums[heads * tq:]
        after = sums[:, :nk] if carry is None else sums[:, :nk] + carry
        w = masked(jnp.exp(log_beta + after)).astype(BF16)
        pv = jnp.concatenate([_dot(w[2 * p * tq:(2 * p + 2) * tq], rows(v_ref, p, first_block, nblocks))
                              for p in range(pairs)], axis=0)
        if carry is None:
            carry_ref[...] = sums[:, nk:]
            acc_ref[...] = pv
        else:
            carry_ref[...] = carry + sums[:, nk:]
            acc_ref[...] += pv

    def own_block_only():
        sweep(i, 1, uo_ref[...], True, None)

    def full_window():
        sweep(i - (SB_WINDOW - 1), SB_WINDOW, uwin_ref[...], True, None)

    if q_block is None:
        pl.when(i >= SB_WINDOW - 1)(full_window)
        pl.when(i < SB_WINDOW - 1)(own_block_only)
        n_prev = jnp.where(i >= SB_WINDOW - 1, i - (SB_WINDOW - 1), i)
    else:
        assert q_block >= SB_WINDOW - 1
        full_window()
        n_prev = q_block - (SB_WINDOW - 1)

    def live():
        return jnp.max(carry_ref[...]) > SB_STOP

    def body(st):
        kb, _ = st
        sweep(kb, 1, uo_ref[...], False, carry_ref[...])
        return kb - 1, live()

    lax.while_loop(lambda st: jnp.logical_and(st[0] >= 0, st[1]), body, (n_prev - 1, live()))
    for p in range(pairs):
        o_ref[:, pair_lanes(p)] = jnp.where(head_lanes[0], acc_ref[2 * p * tq:(2 * p + 1) * tq],
                                            acc_ref[(2 * p + 1) * tq:(2 * p + 2) * tq])


def _sb(q16, k16, v16, uwin, uo, *, batch, tq, keys, q_block):
    nq = keys // SB_BLOCK if q_block is None else 1
    width = SB_PAIRS_PER_STEP * LANES
    kv_spec = pl.BlockSpec((keys, width), lambda b, p, i: (b, p))
    q_spec = pl.BlockSpec((tq, width), lambda b, p, i: (b * nq + i, p))
    heads = 2 * SB_PAIRS_PER_STEP
    return pl.pallas_call(
        functools.partial(_sb_kernel, q_block=q_block),
        grid=(batch, SB_WIDTH // width, nq),
        in_specs=[q_spec, kv_spec, kv_spec, _const_spec(uwin.shape), _const_spec(uo.shape)],
        out_specs=q_spec,
        out_shape=jax.ShapeDtypeStruct((batch * nq * tq, SB_WIDTH), F32),
        scratch_shapes=[pltpu.VMEM((heads * tq, SB_BLOCK), F32),
                        pltpu.VMEM((heads * tq, LANES), F32)],
        compiler_params=_params(3),
        name="sb_prompt" if q_block is None else "sb_sample",
    )(q16, k16, v16, uwin, uo)


def _gla_levels(c):
    return int(np.log2(c))


def _gla_tables(c):
    nlev = _gla_levels(c)
    t = np.arange(c)[:, None]
    j = np.arange(c)[None, :]
    mats = [(j <= t), (j > t)]
    for lev in range(1, nlev + 1):
        m = 1 << (lev - 1)
        start = (t // (2 * m)) * (2 * m)
        mats.append(((t % (2 * m)) >= m) & (j >= start + m) & (j <= t))
    for lev in range(1, nlev + 1):
        m = 1 << (lev - 1)
        start = (t // (2 * m)) * (2 * m)
        mats.append(((t % (2 * m)) < m) & (j > t) & (j <= start + m - 1))
    mats = np.concatenate([a.astype(np.float32) for a in mats], axis=0)
    x = t ^ j
    level = np.where(j > t, -1, np.where(x == 0, 0, np.floor(np.log2(np.maximum(x, 1))).astype(np.int64) + 1))
    return jnp.asarray(mats, BF16), jnp.asarray(level, jnp.int32)


def _gla_kernel(q_ref, k_ref, la_ref, v_ref, s0_ref, mats_ref, lv_ref, o_ref, sfin_ref, st_ref):
    c = GLA_CHUNK
    nlev = _gla_levels(c)
    step = pl.program_id(1)

    @pl.when(step == 0)
    def _():
        st_ref[...] = s0_ref[0]

    q = q_ref[...]
    k = k_ref[...]
    sums = _dot_hilo_left(mats_ref[...], la_ref[...])

    def rows(i):
        return sums[i * c:(i + 1) * c]

    b = rows(0)
    qe = (q * jnp.exp(b)).astype(BF16)
    ke = (k * jnp.exp(rows(1))).astype(BF16)
    b_last = b[c - 1:c, :]
    q_lev = [q.astype(BF16)] + [(q * jnp.exp(rows(1 + lev))).astype(BF16) for lev in range(1, nlev + 1)]
    k_lev = [k.astype(BF16)] + [(k * jnp.exp(rows(1 + nlev + lev))).astype(BF16) for lev in range(1, nlev + 1)]
    lv = lv_ref[...]
    lane = lax.broadcasted_iota(jnp.int32, (c, LANES), 1)
    head_lanes = (lane < GLA_HEAD_K, lane >= GLA_HEAD_K)
    lane_sq = lax.broadcasted_iota(jnp.int32, (GLA_HEAD_V, LANES), 1) < GLA_HEAD_K

    def pick(x, hl):
        return jnp.where(hl, x, jnp.zeros_like(x))

    for p in range(GLA_HEADS // 2):
        sl = slice(p * LANES, (p + 1) * LANES)
        st = st_ref[p]
        st16 = st.astype(BF16)
        upd = []
        for r in range(2):
            hd = 2 * p + r
            vh = v_ref[:, hd * GLA_HEAD_V:(hd + 1) * GLA_HEAD_V]
            scores = jnp.zeros((c, c), F32)
            for lev in range(nlev + 1):
                zl = _dot_nt(pick(q_lev[lev][:, sl], head_lanes[r]), k_lev[lev][:, sl])
                scores = jnp.where(lv == lev, zl, scores)
            o_h = _dot_nt(pick(qe[:, sl], head_lanes[r]), st16) + _dot(scores.astype(BF16), vh)
            o_ref[:, hd * GLA_HEAD_V:(hd + 1) * GLA_HEAD_V] = o_h
            upd.append(_dot_tn(vh, ke[:, sl]))
        st_ref[p] = st * jnp.exp(b_last[:, sl]) + jnp.where(lane_sq, upd[0], upd[1])

    @pl.when(step == pl.num_programs(1) - 1)
    def _():
        sfin_ref[0] = st_ref[...]


def _dot_hilo_left(m, x):
    hi, lo = _split_bf16(x)
    return _dot(m, hi) + _dot(m, lo)


def _gla(qg, kg, la, vg16, s0, mats, lv, *, batch, seq):
    c = GLA_CHUNK
    nc = seq // c
    pairs = GLA_HEADS // 2
    row = lambda b, t: (b * nc + t, 0)
    st_spec = pl.BlockSpec((1, pairs, GLA_HEAD_V, LANES), lambda b, t: (b, 0, 0, 0))
    return pl.pallas_call(
        _gla_kernel,
        grid=(batch, nc),
        in_specs=[pl.BlockSpec((c, GLA_KEY_WIDTH), row),
                  pl.BlockSpec((c, GLA_KEY_WIDTH), row),
                  pl.BlockSpec((c, GLA_KEY_WIDTH), row),
                  pl.BlockSpec((c, GLA_WIDTH), row),
                  st_spec,
                  _const_spec(mats.shape),
                  _const_spec(lv.shape)],
        out_specs=[pl.BlockSpec((c, GLA_WIDTH), row), st_spec],
        out_shape=[jax.ShapeDtypeStruct((batch * seq, GLA_WIDTH), F32),
                   jax.ShapeDtypeStruct((batch, pairs, GLA_HEAD_V, LANES), F32)],
        scratch_shapes=[pltpu.VMEM((pairs, GLA_HEAD_V, LANES), F32)],
        compiler_params=_params(2),
        name="gla",
    )(qg, kg, la, vg16, s0, mats, lv)


def _mixout_kernel(x_ref, osb_ref, og_ref, r_ref, gsb_ref, ggla_ref, grp_sb_ref, grp_gla_ref, w_ref,
                   g_ref, wg_ref, wu_ref, wd_ref, gfin_ref, o_ref, a_ref):
    def head_norm(y, grp, width, gain):
        ms = _dot_hilo(y * y, grp) * (1.0 / width)
        return y * lax.rsqrt(ms + EPS) * gain

    o_sb = head_norm(osb_ref[...], grp_sb_ref[...], SB_HEAD_DIM, gsb_ref[...])
    r = r_ref[...]
    o_g = head_norm(og_ref[...], grp_gla_ref[...], GLA_HEAD_V, ggla_ref[...]) * (r * jax.nn.sigmoid(r))
    mix = _dot(o_sb.astype(BF16), w_ref[:SB_WIDTH, :]) + _dot(o_g.astype(BF16), w_ref[SB_WIDTH:, :])
    y = _half_step_ffn(x_ref[...] + mix, g_ref, wg_ref, wu_ref, wd_ref, a_ref)
    o_ref[...] = _rms(y, gfin_ref[...])


def _mixout(x, osb, og, r, gsb, ggla, grp_sb, grp_gla, w, g, wg, wu, wd, gfin, *, tm):
    n = x.shape[0]
    row = lambda i: (i, 0)
    return pl.pallas_call(
        _mixout_kernel,
        grid=(n // tm,),
        in_specs=[pl.BlockSpec((tm, D_MODEL), row),
                  pl.BlockSpec((tm, SB_WIDTH), row),
                  pl.BlockSpec((tm, GLA_WIDTH), row),
                  pl.BlockSpec((tm, GLA_WIDTH), row),
                  _const_spec((1, SB_WIDTH)),
                  _const_spec((1, GLA_WIDTH)),
                  _const_spec((SB_WIDTH, SB_WIDTH)),
                  _const_spec((GLA_WIDTH, GLA_WIDTH)),
                  _const_spec((D_MODEL, D_MODEL))] + _ffn_specs() + [_const_spec((1, D_MODEL))],
        out_specs=pl.BlockSpec((tm, D_MODEL), row),
        out_shape=jax.ShapeDtypeStruct((n, D_MODEL), F32),
        scratch_shapes=[pltpu.VMEM((tm, D_FF), BF16)],
        compiler_params=_params(1),
        name="mixer_out_ffn2",
    )(x, osb, og, r, gsb, ggla, grp_sb, grp_gla, w, g, wg, wu, wd, gfin)


def _group_ones(width, group):
    idx = np.arange(width) // group
    return jnp.asarray(idx[:, None] == idx[None, :], BF16)


def _suffix_and_total(nblocks):
    n = nblocks * SB_BLOCK
    j = np.arange(n)[:, None]
    s = np.arange(n)[None, :]
    return jnp.asarray(np.concatenate([j > s, np.ones((n, SB_BLOCK), bool)], axis=1), BF16)


def _state_to_kernel(s):
    b = s.shape[0]
    s = s.reshape(b, GLA_HEADS // 2, 2, GLA_HEAD_K, GLA_HEAD_V)
    return s.transpose(0, 1, 4, 2, 3).reshape(b, GLA_HEADS // 2, GLA_HEAD_V, LANES)


def _state_from_kernel(s):
    b = s.shape[0]
    s = s.reshape(b, GLA_HEADS // 2, GLA_HEAD_V, 2, GLA_HEAD_K)
    return s.transpose(0, 1, 3, 4, 2).reshape(b, GLA_HEADS, GLA_HEAD_K, GLA_HEAD_V)


def _layer(x, lw, consts, *, batch, seq, tm, cache=None, state=None):
    (g1, w1g, w1u, w1d, g_mix, w_in, wgu, bg, gq, gk, gsb, ggla, w_out, g2, w2g, w2u, w2d, gfin) = lw
    grp64, grp128, uwin, uo, mats, lv = consts
    x1 = _ffn(x, g1, w1g, w1u, w1d, tm=tm)
    (qsb16, ksb, ksb16, vsb, vsb16, qg, kg, vg16, r, la) = _mixin(x1, g_mix, w_in, wgu, bg, gq, gk, grp64, tm=tm)

    if cache is None:
        o_sb = _sb(qsb16, ksb16, vsb16, uwin, uo, batch=batch, tq=SB_BLOCK, keys=seq, q_block=None)
    else:
        past = cache[0].shape[0] // batch
        keys = past + SB_BLOCK

        def with_cache(old, new):
            new = jnp.pad(new.reshape(batch, seq, SB_WIDTH), ((0, 0), (0, SB_BLOCK - seq), (0, 0)))
            return jnp.concatenate([old.reshape(batch, past, SB_WIDTH), new], axis=1).reshape(batch * keys, SB_WIDTH)

        o_sb = _sb(qsb16, with_cache(cache[0], ksb16), with_cache(cache[1], vsb16), uwin, uo,
                   batch=batch, tq=seq, keys=keys, q_block=past // SB_BLOCK)

    c = GLA_CHUNK
    seq_pad = -(-seq // c) * c
    if seq_pad != seq:
        padt = lambda a: jnp.pad(a.reshape(batch, seq, a.shape[-1]),
                                 ((0, 0), (0, seq_pad - seq), (0, 0))).reshape(batch * seq_pad, a.shape[-1])
        qg_p, kg_p, la_p, vg_p = padt(qg), padt(kg), padt(la), padt(vg16)
    else:
        qg_p, kg_p, la_p, vg_p = qg, kg, la, vg16
    if state is None:
        state = jnp.zeros((batch, GLA_HEADS // 2, GLA_HEAD_V, LANES), F32)
    o_g, s_fin = _gla(qg_p, kg_p, la_p, vg_p, state, mats, lv, batch=batch, seq=seq_pad)
    if seq_pad != seq:
        o_g = o_g.reshape(batch, seq_pad, GLA_WIDTH)[:, :seq].reshape(batch * seq, GLA_WIDTH)

    y = _mixout(x1, o_sb, o_g, r, gsb, ggla, grp64, grp128, w_out, g2, w2g, w2u, w2d, gfin, tm=tm)
    return y, ksb, vsb, _state_from_kernel(s_fin)


def kernel(x_prompt, x_sample, cache_sb_k, cache_sb_v, state_gla, g_ffn1, w_ffn1_gate, w_ffn1_up,
           w_ffn1_down, g_mix, w_in, w_gate_up, b_gate, g_q, g_k, g_sb_out, g_gla_out, w_out,
           g_ffn2, w_ffn2_gate, w_ffn2_up, w_ffn2_down, g_final):
    depth = w_in.shape[0]
    batch, seq, _ = x_prompt.shape
    dec_batch, dec_seq, _ = x_sample.shape
    past = cache_sb_k.shape[2]
    consts = (_group_ones(SB_WIDTH, SB_HEAD_DIM), _group_ones(GLA_WIDTH, GLA_HEAD_V),
              _suffix_and_total(SB_WINDOW), _suffix_and_total(1), *_gla_tables(GLA_CHUNK))

    y_p = x_prompt.reshape(batch * seq, D_MODEL)
    y_s = x_sample.reshape(dec_batch * dec_seq, D_MODEL)
    outs = [[] for _ in range(6)]
    for l in range(depth):
        row = lambda a: a[l].reshape(1, -1)
        w_main = w_in[l][:, :MAIN_WIDTH]
        w_lr = jnp.pad(w_in[l][:, MAIN_WIDTH:], ((0, 0), (0, GATE_PAD - GATE_RANK)))
        lw = (row(g_ffn1), w_ffn1_gate[l].astype(BF16), w_ffn1_up[l].astype(BF16), w_ffn1_down[l].astype(BF16),
              row(g_mix), jnp.concatenate([w_main, w_lr], axis=1).astype(BF16),
              jnp.pad(w_gate_up[l], ((0, GATE_PAD - GATE_RANK), (0, 0))).astype(BF16), row(b_gate),
              jnp.tile(g_q[l], SB_HEADS).reshape(1, -1), jnp.tile(g_k[l], SB_HEADS).reshape(1, -1),
              jnp.tile(g_sb_out[l], SB_HEADS).reshape(1, -1), jnp.tile(g_gla_out[l], GLA_HEADS).reshape(1, -1),
              w_out[l].astype(BF16), row(g_ffn2), w_ffn2_gate[l].astype(BF16), w_ffn2_up[l].astype(BF16),
              w_ffn2_down[l].astype(BF16), row(g_final))
        y_p, k_p, v_p, s_p = _layer(y_p, lw, consts, batch=batch, seq=seq, tm=512)
        cache = (cache_sb_k[l].reshape(dec_batch * past, SB_WIDTH).astype(BF16),
                 cache_sb_v[l].reshape(dec_batch * past, SB_WIDTH).astype(BF16))
        y_s, k_s, v_s, s_s = _layer(y_s, lw, consts, batch=dec_batch, seq=dec_seq, tm=256,
                                    cache=cache, state=_state_to_kernel(state_gla[l]))
        shape_p = (batch, seq, SB_HEADS, SB_HEAD_DIM)
        shape_s = (dec_batch, dec_seq, SB_HEADS, SB_HEAD_DIM)
        for lst, val in zip(outs, (k_p.reshape(shape_p), v_p.reshape(shape_p), s_p,
                                   k_s.reshape(shape_s), v_s.reshape(shape_s), s_s)):
            lst.append(val)
    return (y_p.reshape(batch, seq, D_MODEL), y_s.reshape(dec_batch, dec_seq, D_MODEL),
            *[jnp.stack(o) for o in outs])
```

```python
import functools

import numpy as np
import jax
import jax.numpy as jnp
from jax import lax
from jax.experimental import pallas as pl
from jax.experimental.pallas import tpu as pltpu

F32 = jnp.float32
BF16 = jnp.bfloat16

D_MODEL = 1024
SB_HEADS = 8
SB_HEAD_DIM = 64
SB_WIDTH = SB_HEADS * SB_HEAD_DIM
SB_SCALE = SB_HEAD_DIM ** -0.5
GLA_HEADS = 4
GLA_HEAD_K = 64
GLA_HEAD_V = 128
GLA_KEY_WIDTH = GLA_HEADS * GLA_HEAD_K
GLA_WIDTH = GLA_HEADS * GLA_HEAD_V
GLA_SCALE = GLA_HEAD_K ** -0.5
GATE_RANK = 16
GATE_TAU = 16.0
D_FF = 2816
EPS = 1e-6

LANES = 128
GATE_PAD = LANES
MAIN_WIDTH = 3 * SB_WIDTH + 2 * GLA_KEY_WIDTH + 2 * GLA_WIDTH
IN_PAD = MAIN_WIDTH + GATE_PAD

FF_CHUNK = 256
SB_BLOCK = 128
SB_WINDOW = 3
SB_PAIRS_PER_STEP = 4
GLA_CHUNK = 128
GLA_DIRECT_MIN = -40.0
SB_STOP = -105.0

VMEM_LIMIT = 56 * 1024 * 1024


def _dot(a, b):
    return jnp.dot(a, b, preferred_element_type=F32)


def _dot_nt(a, b):
    return lax.dot_general(a, b, (((1,), (1,)), ((), ())), preferred_element_type=F32)


def _dot_tn(a, b):
    return lax.dot_general(a, b, (((0,), (0,)), ((), ())), preferred_element_type=F32)


def _split_bf16(x):
    hi = x.astype(BF16)
    lo = (x - hi.astype(F32)).astype(BF16)
    return hi, lo


def _dot_hilo(x, m):
    hi, lo = _split_bf16(x)
    return _dot(hi, m) + _dot(lo, m)


def _rms(x, g):
    ms = jnp.mean(x * x, axis=-1, keepdims=True)
    return x * lax.rsqrt(ms + EPS) * g


def _log_sigmoid(x):
    return jnp.minimum(x, 0.0) - jnp.log1p(jnp.exp(-jnp.abs(x)))


def _const_spec(shape):
    nd = len(shape)
    return pl.BlockSpec(shape, lambda *_: (0,) * nd, pipeline_mode=pl.Buffered(1))


def _params(n_grid):
    return pltpu.CompilerParams(dimension_semantics=("arbitrary",) * n_grid,
                                vmem_limit_bytes=VMEM_LIMIT)


def _half_step_ffn(x, g_ref, wg_ref, wu_ref, wd_ref, a_ref):
    h = _rms(x, g_ref[...]).astype(BF16)
    for c in range(D_FF // FF_CHUNK):
        sl = slice(c * FF_CHUNK, (c + 1) * FF_CHUNK)
        gt = _dot(h, wg_ref[:, sl])
        up = _dot(h, wu_ref[:, sl])
        a_ref[:, sl] = (gt * jax.nn.sigmoid(gt) * up).astype(BF16)
    return x + 0.5 * _dot(a_ref[...], wd_ref[...])


def _ffn_kernel(x_ref, g_ref, wg_ref, wu_ref, wd_ref, o_ref, a_ref):
    o_ref[...] = _half_step_ffn(x_ref[...], g_ref, wg_ref, wu_ref, wd_ref, a_ref)


def _ffn_specs():
    return [_const_spec((1, D_MODEL)), _const_spec((D_MODEL, D_FF)), _const_spec((D_MODEL, D_FF)),
            _const_spec((D_FF, D_MODEL))]


def _ffn(x, g, wg, wu, wd, *, tm):
    n = x.shape[0]
    row = lambda i: (i, 0)
    return pl.pallas_call(
        _ffn_kernel,
        grid=(n // tm,),
        in_specs=[pl.BlockSpec((tm, D_MODEL), row)] + _ffn_specs(),
        out_specs=pl.BlockSpec((tm, D_MODEL), row),
        out_shape=jax.ShapeDtypeStruct((n, D_MODEL), F32),
        scratch_shapes=[pltpu.VMEM((tm, D_FF), BF16)],
        compiler_params=_params(1),
        name="ffn1",
    )(x, g, wg, wu, wd)


def _store_heads(ref, y):
    for hd in range(SB_HEADS):
        ref[:, hd, :] = y[:, hd * SB_HEAD_DIM:(hd + 1) * SB_HEAD_DIM]


def _mixin_kernel(x_ref, g_ref, w_ref, wgu_ref, bg_ref, gq_ref, gk_ref, grp_ref,
                  qsb_ref, ksb_ref, ksb16_ref, vsb_ref, vsb16_ref,
                  qg_ref, kg_ref, vg16_ref, r_ref, la_ref):
    h = _rms(x_ref[...], g_ref[...]).astype(BF16)
    grp = grp_ref[...]

    def proj(lo, width):
        return _dot(h, w_ref[:, lo:lo + width])

    def head_norm(y, gain):
        ms = _dot_hilo(y * y, grp) * (1.0 / SB_HEAD_DIM)
        return y * lax.rsqrt(ms + EPS) * gain

    q = head_norm(proj(0, SB_WIDTH), gq_ref[...])
    qsb_ref[...] = (q * SB_SCALE).astype(BF16)
    k = head_norm(proj(SB_WIDTH, SB_WIDTH), gk_ref[...])
    _store_heads(ksb_ref, k)
    ksb16_ref[...] = k.astype(BF16)
    v = proj(2 * SB_WIDTH, SB_WIDTH)
    _store_heads(vsb_ref, v)
    vsb16_ref[...] = v.astype(BF16)
    off = 3 * SB_WIDTH
    qg_ref[...] = proj(off, GLA_KEY_WIDTH) * GLA_SCALE
    kg_ref[...] = proj(off + GLA_KEY_WIDTH, GLA_KEY_WIDTH)
    vg16_ref[...] = proj(off + 2 * GLA_KEY_WIDTH, GLA_WIDTH).astype(BF16)
    r_ref[...] = proj(off + 2 * GLA_KEY_WIDTH + GLA_WIDTH, GLA_WIDTH)
    lr = proj(MAIN_WIDTH, GATE_PAD).astype(BF16)
    gate = _dot(lr, wgu_ref[...]) + bg_ref[...]
    la_ref[...] = _log_sigmoid(gate) * (1.0 / GATE_TAU)


def _mixin(x, g, w, wgu, bg, gq, gk, grp, *, tm):
    n = x.shape[0]
    row = lambda i: (i, 0)
    heads = (SB_HEADS, SB_HEAD_DIM)
    widths = [((SB_WIDTH,), BF16), (heads, F32), ((SB_WIDTH,), BF16), (heads, F32), ((SB_WIDTH,), BF16),
              ((GLA_KEY_WIDTH,), F32), ((GLA_KEY_WIDTH,), F32), ((GLA_WIDTH,), BF16), ((GLA_WIDTH,), F32),
              ((GLA_KEY_WIDTH,), F32)]
    return pl.pallas_call(
        _mixin_kernel,
        grid=(n // tm,),
        in_specs=[pl.BlockSpec((tm, D_MODEL), row),
                  _const_spec((1, D_MODEL)),
                  _const_spec((D_MODEL, IN_PAD)),
                  _const_spec((GATE_PAD, GLA_KEY_WIDTH)),
                  _const_spec((1, GLA_KEY_WIDTH)),
                  _const_spec((1, SB_WIDTH)),
                  _const_spec((1, SB_WIDTH)),
                  _const_spec((SB_WIDTH, SB_WIDTH))],
        out_specs=[pl.BlockSpec((tm,) + wd, lambda i, nd=len(wd): (i,) + (0,) * nd) for wd, _ in widths],
        out_shape=[jax.ShapeDtypeStruct((n,) + wd, dt) for wd, dt in widths],
        compiler_params=_params(1),
        name="mixer_in",
    )(x, g, w, wgu, bg, gq, gk, grp)


def _sb_log_terms(z):
    t = jnp.log(1.0 + jnp.exp(-jnp.abs(z)))
    return -jnp.maximum(z, 0.0) - t, jnp.minimum(z, 0.0) - t


def _sb_kernel(q_ref, k_ref, v_ref, upair_ref, uo_ref, o_ref, carry_ref, acc_ref, *, q_block):
    tq = q_ref.shape[0]
    pairs = q_ref.shape[1] // LANES
    heads = 2 * pairs
    i = pl.program_id(2) if q_block is None else q_block
    lane = lax.broadcasted_iota(jnp.int32, (tq, LANES), 1)
    head_lanes = (lane < SB_HEAD_DIM, lane >= SB_HEAD_DIM)
    rowi = lax.broadcasted_iota(jnp.int32, (heads * tq, SB_BLOCK), 0)
    coli = lax.broadcasted_iota(jnp.int32, (heads * tq, SB_BLOCK), 1)
    causal = coli < (rowi & (tq - 1))

    def pair_lanes(p):
        return slice(p * LANES, (p + 1) * LANES)

    def q_pair(p):
        q2 = q_ref[:, pair_lanes(p)]
        zero = jnp.zeros_like(q2)
        return jnp.concatenate([jnp.where(hl, q2, zero) for hl in head_lanes], axis=0)

    def rows(ref, p, first_block, nblocks):
        start = first_block * SB_BLOCK
        if not isinstance(start, int):
            start = pl.multiple_of(start, SB_BLOCK)
        return ref[pl.ds(start, nblocks * SB_BLOCK), pair_lanes(p)]

    def suffix_sums(x):
        nb = x.shape[1] // SB_BLOCK
        rows_ = x.shape[0]
        both = jnp.concatenate(_split_bf16(x), axis=0)

        def mm(cols, m):
            y = _dot(both[:, cols], m)
            return y[:rows_] + y[rows_:]

        last = mm(slice((nb - 1) * SB_BLOCK, nb * SB_BLOCK), uo_ref[...])
        within = [None] * (nb - 1) + [last[:, :SB_BLOCK]]
        for j in range(0, nb - 1, 2):
            if j + 1 < nb - 1:
                y = mm(slice(j * SB_BLOCK, (j + 2) * SB_BLOCK), upair_ref[...])
                within[j], within[j + 1] = y[:, :SB_BLOCK], y[:, SB_BLOCK:]
            else:
                within[j] = mm(slice(j * SB_BLOCK, (j + 1) * SB_BLOCK), uo_ref[...])[:, :SB_BLOCK]
        later = last[:, SB_BLOCK:]
        out = [within[nb - 1]]
        for j in range(nb - 2, -1, -1):
            out.insert(0, within[j] + later)
            later = later + jnp.sum(x[:, j * SB_BLOCK:(j + 1) * SB_BLOCK], axis=1, keepdims=True)
        return (out[0] if nb == 1 else jnp.concatenate(out, axis=1)), later

    def sweep(first_block, nblocks, own, carry):
        nk = nblocks * SB_BLOCK

        def masked(x):
            if not own:
                return x
            tail = jnp.where(causal, x[:, nk - SB_BLOCK:], 0.0)
            return tail if nblocks == 1 else jnp.concatenate([x[:, :nk - SB_BLOCK], tail], axis=1)

        z = jnp.concatenate([_dot_nt(q_pair(p), rows(k_ref, p, first_block, nblocks))
                             for p in range(pairs)], axis=0)
        log_fail, log_beta = _sb_log_terms(z)
        after, total = suffix_sums(masked(log_fail))
        if carry is not None:
            after = after + carry
        w = masked(jnp.exp(log_beta + after)).astype(BF16)
        pv = jnp.concatenate([_dot(w[2 * p * tq:(2 * p + 2) * tq], rows(v_ref, p, first_block, nblocks))
                              for p in range(pairs)], axis=0)
        if carry is None:
            carry_ref[...] = total
            acc_ref[...] = pv
        else:
            carry_ref[...] = carry + total
            acc_ref[...] += pv

    def own_block_only():
        sweep(i, 1, True, None)

    def full_window():
        sweep(i - (SB_WINDOW - 1), SB_WINDOW, True, None)

    if q_block is None:
        pl.when(i >= SB_WINDOW - 1)(full_window)
        pl.when(i < SB_WINDOW - 1)(own_block_only)
        n_prev = jnp.where(i >= SB_WINDOW - 1, i - (SB_WINDOW - 1), i)
    else:
        assert q_block >= SB_WINDOW - 1
        full_window()
        n_prev = q_block - (SB_WINDOW - 1)

    def live():
        return jnp.max(carry_ref[...]) > SB_STOP

    def body(st):
        kb, _ = st
        sweep(kb, 1, False, carry_ref[...])
        return kb - 1, live()

    lax.while_loop(lambda st: jnp.logical_and(st[0] >= 0, st[1]), body, (n_prev - 1, live()))
    for p in range(pairs):
        o_ref[:, pair_lanes(p)] = jnp.where(head_lanes[0], acc_ref[2 * p * tq:(2 * p + 1) * tq],
                                            acc_ref[(2 * p + 1) * tq:(2 * p + 2) * tq])


def _sb(q16, k16, v16, upair, uo, *, batch, tq, keys, q_block):
    nq = keys // SB_BLOCK if q_block is None else 1
    width = SB_PAIRS_PER_STEP * LANES
    kv_spec = pl.BlockSpec((keys, width), lambda b, p, i: (b, p))
    q_spec = pl.BlockSpec((tq, width), lambda b, p, i: (b * nq + i, p))
    heads = 2 * SB_PAIRS_PER_STEP
    return pl.pallas_call(
        functools.partial(_sb_kernel, q_block=q_block),
        grid=(batch, SB_WIDTH // width, nq),
        in_specs=[q_spec, kv_spec, kv_spec, _const_spec(upair.shape), _const_spec(uo.shape)],
        out_specs=q_spec,
        out_shape=jax.ShapeDtypeStruct((batch * nq * tq, SB_WIDTH), F32),
        scratch_shapes=[pltpu.VMEM((heads * tq, SB_BLOCK), F32),
                        pltpu.VMEM((heads * tq, LANES), F32)],
        compiler_params=_params(3),
        name="sb_prompt" if q_block is None else "sb_sample",
    )(q16, k16, v16, upair, uo)


def _gla_levels(c):
    return int(np.log2(c))


def _gla_tables(c):
    nlev = _gla_levels(c)
    t = np.arange(c)[:, None]
    j = np.arange(c)[None, :]
    mats = [(j <= t), (j > t)]
    for lev in range(1, nlev + 1):
        m = 1 << (lev - 1)
        start = (t // (2 * m)) * (2 * m)
        mats.append(((t % (2 * m)) >= m) & (j >= start + m) & (j <= t))
    for lev in range(1, nlev + 1):
        m = 1 << (lev - 1)
        start = (t // (2 * m)) * (2 * m)
        mats.append(((t % (2 * m)) < m) & (j > t) & (j <= start + m - 1))
    mats = np.concatenate([a.astype(np.float32) for a in mats], axis=0)
    x = t ^ j
    level = np.where(j > t, -1, np.where(x == 0, 0, np.floor(np.log2(np.maximum(x, 1))).astype(np.int64) + 1))
    return jnp.asarray(mats, BF16), jnp.asarray(level, jnp.int32)


def _gla_kernel(q_ref, k_ref, la_ref, v_ref, s0_ref, mats_ref, lv_ref, o_ref, sfin_ref, st_ref):
    c = GLA_CHUNK
    nlev = _gla_levels(c)
    step = pl.program_id(1)

    @pl.when(step == 0)
    def _():
        st_ref[...] = s0_ref[0]

    lane = lax.broadcasted_iota(jnp.int32, (c, LANES), 1)
    head_lanes = (lane < GLA_HEAD_K, lane >= GLA_HEAD_K)
    lane_sq = lax.broadcasted_iota(jnp.int32, (GLA_HEAD_V, LANES), 1) < GLA_HEAD_K

    def pick(x, hl):
        return jnp.where(hl, x, jnp.zeros_like(x))

    def stack_heads(x):
        return jnp.concatenate([pick(x, hl) for hl in head_lanes], axis=0)

    def finish(p, scores, qe, ke, b_last):
        sl = slice(p * LANES, (p + 1) * LANES)
        st = st_ref[p]
        inter = _dot_nt(stack_heads(qe[:, sl]), st.astype(BF16))
        upd = []
        for r in range(2):
            hd = 2 * p + r
            vh = v_ref[:, hd * GLA_HEAD_V:(hd + 1) * GLA_HEAD_V]
            o_ref[:, hd * GLA_HEAD_V:(hd + 1) * GLA_HEAD_V] = (
                inter[r * c:(r + 1) * c] + _dot(scores[r * c:(r + 1) * c], vh))
            upd.append(_dot_tn(vh, ke[:, sl]))
        st_ref[p] = st * jnp.exp(b_last[:, sl]) + jnp.where(lane_sq, upd[0], upd[1])

    b = _dot_hilo_left(mats_ref[:c, :], la_ref[...])
    b_last = b[c - 1:c, :]
    mild = jnp.min(b_last) >= GLA_DIRECT_MIN

    @pl.when(mild)
    def _():
        q = q_ref[...]
        k = k_ref[...]
        qe = (q * jnp.exp(b)).astype(BF16)
        kd = (k * jnp.exp(-b)).astype(BF16)
        ke = (k * jnp.exp(b_last - b)).astype(BF16)
        rowi = lax.broadcasted_iota(jnp.int32, (2 * c, c), 0) & (c - 1)
        causal = lax.broadcasted_iota(jnp.int32, (2 * c, c), 1) <= rowi
        for p in range(GLA_HEADS // 2):
            sl = slice(p * LANES, (p + 1) * LANES)
            scores = jnp.where(causal, _dot_nt(stack_heads(qe[:, sl]), kd[:, sl]), 0.0)
            finish(p, scores.astype(BF16), qe, ke, b_last)

    @pl.when(jnp.logical_not(mild))
    def _():
        q = q_ref[...]
        k = k_ref[...]
        sums = _dot_hilo_left(mats_ref[...], la_ref[...])

        def rows(i):
            return sums[i * c:(i + 1) * c]

        qe = (q * jnp.exp(rows(0))).astype(BF16)
        ke = (k * jnp.exp(rows(1))).astype(BF16)
        q_lev = [q.astype(BF16)] + [(q * jnp.exp(rows(1 + lev))).astype(BF16) for lev in range(1, nlev + 1)]
        k_lev = [k.astype(BF16)] + [(k * jnp.exp(rows(1 + nlev + lev))).astype(BF16) for lev in range(1, nlev + 1)]
        lv = jnp.concatenate([lv_ref[...], lv_ref[...]], axis=0)
        for p in range(GLA_HEADS // 2):
            sl = slice(p * LANES, (p + 1) * LANES)
            scores = jnp.zeros((2 * c, c), F32)
            for lev in range(nlev + 1):
                scores = jnp.where(lv == lev, _dot_nt(stack_heads(q_lev[lev][:, sl]), k_lev[lev][:, sl]), scores)
            finish(p, scores.astype(BF16), qe, ke, rows(0)[c - 1:c, :])

    @pl.when(step == pl.num_programs(1) - 1)
    def _():
        sfin_ref[0] = st_ref[...]


def _dot_hilo_left(m, x):
    hi, lo = _split_bf16(x)
    return _dot(m, hi) + _dot(m, lo)


def _gla(qg, kg, la, vg16, s0, mats, lv, *, batch, seq):
    c = GLA_CHUNK
    nc = seq // c
    pairs = GLA_HEADS // 2
    row = lambda b, t: (b * nc + t, 0)
    st_spec = pl.BlockSpec((1, pairs, GLA_HEAD_V, LANES), lambda b, t: (b, 0, 0, 0))
    return pl.pallas_call(
        _gla_kernel,
        grid=(batch, nc),
        in_specs=[pl.BlockSpec((c, GLA_KEY_WIDTH), row),
                  pl.BlockSpec((c, GLA_KEY_WIDTH), row),
                  pl.BlockSpec((c, GLA_KEY_WIDTH), row),
                  pl.BlockSpec((c, GLA_WIDTH), row),
                  st_spec,
                  _const_spec(mats.shape),
                  _const_spec(lv.shape)],
        out_specs=[pl.BlockSpec((c, GLA_WIDTH), row), st_spec],
        out_shape=[jax.ShapeDtypeStruct((batch * seq, GLA_WIDTH), F32),
                   jax.ShapeDtypeStruct((batch, pairs, GLA_HEAD_V, LANES), F32)],
        scratch_shapes=[pltpu.VMEM((pairs, GLA_HEAD_V, LANES), F32)],
        compiler_params=_params(2),
        name="gla",
    )(qg, kg, la, vg16, s0, mats, lv)


def _mixout_kernel(x_ref, osb_ref, og_ref, r_ref, gsb_ref, ggla_ref, grp_sb_ref, grp_gla_ref, w_ref,
                   g_ref, wg_ref, wu_ref, wd_ref, gfin_ref, o_ref, a_ref):
    def head_norm(y, grp, width, gain):
        ms = _dot_hilo(y * y, grp) * (1.0 / width)
        return y * lax.rsqrt(ms + EPS) * gain

    o_sb = head_norm(osb_ref[...], grp_sb_ref[...], SB_HEAD_DIM, gsb_ref[...])
    r = r_ref[...]
    o_g = head_norm(og_ref[...], grp_gla_ref[...], GLA_HEAD_V, ggla_ref[...]) * (r * jax.nn.sigmoid(r))
    mix = _dot(o_sb.astype(BF16), w_ref[:SB_WIDTH, :]) + _dot(o_g.astype(BF16), w_ref[SB_WIDTH:, :])
    y = _half_step_ffn(x_ref[...] + mix, g_ref, wg_ref, wu_ref, wd_ref, a_ref)
    o_ref[...] = _rms(y, gfin_ref[...])


def _mixout(x, osb, og, r, gsb, ggla, grp_sb, grp_gla, w, g, wg, wu, wd, gfin, *, tm):
    n = x.shape[0]
    row = lambda i: (i, 0)
    return pl.pallas_call(
        _mixout_kernel,
        grid=(n // tm,),
        in_specs=[pl.BlockSpec((tm, D_MODEL), row),
                  pl.BlockSpec((tm, SB_WIDTH), row),
                  pl.BlockSpec((tm, GLA_WIDTH), row),
                  pl.BlockSpec((tm, GLA_WIDTH), row),
                  _const_spec((1, SB_WIDTH)),
                  _const_spec((1, GLA_WIDTH)),
                  _const_spec((SB_WIDTH, SB_WIDTH)),
                  _const_spec((GLA_WIDTH, GLA_WIDTH)),
                  _const_spec((D_MODEL, D_MODEL))] + _ffn_specs() + [_const_spec((1, D_MODEL))],
        out_specs=pl.BlockSpec((tm, D_MODEL), row),
        out_shape=jax.ShapeDtypeStruct((n, D_MODEL), F32),
        scratch_shapes=[pltpu.VMEM((tm, D_FF), BF16)],
        compiler_params=_params(1),
        name="mixer_out_ffn2",
    )(x, osb, og, r, gsb, ggla, grp_sb, grp_gla, w, g, wg, wu, wd, gfin)


def _group_ones(width, group):
    idx = np.arange(width) // group
    return jnp.asarray(idx[:, None] == idx[None, :], BF16)


def _suffix_matrices():
    j = np.arange(SB_BLOCK)[:, None]
    s = np.arange(SB_BLOCK)[None, :]
    later = j > s
    zero = np.zeros_like(later)
    pair = np.block([[later, zero], [zero, later]])
    single = np.concatenate([later, np.ones_like(later)], axis=1)
    return jnp.asarray(pair, BF16), jnp.asarray(single, BF16)


def _state_to_kernel(s):
    b = s.shape[0]
    s = s.reshape(b, GLA_HEADS // 2, 2, GLA_HEAD_K, GLA_HEAD_V)
    return s.transpose(0, 1, 4, 2, 3).reshape(b, GLA_HEADS // 2, GLA_HEAD_V, LANES)


def _state_from_kernel(s):
    b = s.shape[0]
    s = s.reshape(b, GLA_HEADS // 2, GLA_HEAD_V, 2, GLA_HEAD_K)
    return s.transpose(0, 1, 3, 4, 2).reshape(b, GLA_HEADS, GLA_HEAD_K, GLA_HEAD_V)


def _layer(x, lw, consts, *, batch, seq, tm, cache=None, state=None):
    (g1, w1g, w1u, w1d, g_mix, w_in, wgu, bg, gq, gk, gsb, ggla, w_out, g2, w2g, w2u, w2d, gfin) = lw
    grp64, grp128, upair, uo, mats, lv = consts
    x1 = _ffn(x, g1, w1g, w1u, w1d, tm=tm)
    (qsb16, ksb, ksb16, vsb, vsb16, qg, kg, vg16, r, la) = _mixin(x1, g_mix, w_in, wgu, bg, gq, gk, grp64, tm=tm)

    if cache is None:
        o_sb = _sb(qsb16, ksb16, vsb16, upair, uo, batch=batch, tq=SB_BLOCK, keys=seq, q_block=None)
    else:
        past = cache[0].shape[0] // batch
        keys = past + SB_BLOCK

        def with_cache(old, new):
            new = jnp.pad(new.reshape(batch, seq, SB_WIDTH), ((0, 0), (0, SB_BLOCK - seq), (0, 0)))
            return jnp.concatenate([old.reshape(batch, past, SB_WIDTH), new], axis=1).reshape(batch * keys, SB_WIDTH)

        o_sb = _sb(qsb16, with_cache(cache[0], ksb16), with_cache(cache[1], vsb16), upair, uo,
                   batch=batch, tq=seq, keys=keys, q_block=past // SB_BLOCK)

    c = GLA_CHUNK
    seq_pad = -(-seq // c) * c
    if seq_pad != seq:
        padt = lambda a: jnp.pad(a.reshape(batch, seq, a.shape[-1]),
                                 ((0, 0), (0, seq_pad - seq), (0, 0))).reshape(batch * seq_pad, a.shape[-1])
        qg_p, kg_p, la_p, vg_p = padt(qg), padt(kg), padt(la), padt(vg16)
    else:
        qg_p, kg_p, la_p, vg_p = qg, kg, la, vg16
    if state is None:
        state = jnp.zeros((batch, GLA_HEADS // 2, GLA_HEAD_V, LANES), F32)
    o_g, s_fin = _gla(qg_p, kg_p, la_p, vg_p, state, mats, lv, batch=batch, seq=seq_pad)
    if seq_pad != seq:
        o_g = o_g.reshape(batch, seq_pad, GLA_WIDTH)[:, :seq].reshape(batch * seq, GLA_WIDTH)

    y = _mixout(x1, o_sb, o_g, r, gsb, ggla, grp64, grp128, w_out, g2, w2g, w2u, w2d, gfin, tm=tm)
    return y, ksb, vsb, _state_from_kernel(s_fin)


def kernel(x_prompt, x_sample, cache_sb_k, cache_sb_v, state_gla, g_ffn1, w_ffn1_gate, w_ffn1_up,
           w_ffn1_down, g_mix, w_in, w_gate_up, b_gate, g_q, g_k, g_sb_out, g_gla_out, w_out,
           g_ffn2, w_ffn2_gate, w_ffn2_up, w_ffn2_down, g_final):
    depth = w_in.shape[0]
    batch, seq, _ = x_prompt.shape
    dec_batch, dec_seq, _ = x_sample.shape
    past = cache_sb_k.shape[2]
    consts = (_group_ones(SB_WIDTH, SB_HEAD_DIM), _group_ones(GLA_WIDTH, GLA_HEAD_V),
              *_suffix_matrices(), *_gla_tables(GLA_CHUNK))

    y_p = x_prompt.reshape(batch * seq, D_MODEL)
    y_s = x_sample.reshape(dec_batch * dec_seq, D_MODEL)
    outs = [[] for _ in range(6)]
    for l in range(depth):
        row = lambda a: a[l].reshape(1, -1)
        w_main = w_in[l][:, :MAIN_WIDTH]
        w_lr = jnp.pad(w_in[l][:, MAIN_WIDTH:], ((0, 0), (0, GATE_PAD - GATE_RANK)))
        lw = (row(g_ffn1), w_ffn1_gate[l].astype(BF16), w_ffn1_up[l].astype(BF16), w_ffn1_down[l].astype(BF16),
              row(g_mix), jnp.concatenate([w_main, w_lr], axis=1).astype(BF16),
              jnp.pad(w_gate_up[l], ((0, GATE_PAD - GATE_RANK), (0, 0))).astype(BF16), row(b_gate),
              jnp.tile(g_q[l], SB_HEADS).reshape(1, -1), jnp.tile(g_k[l], SB_HEADS).reshape(1, -1),
              jnp.tile(g_sb_out[l], SB_HEADS).reshape(1, -1), jnp.tile(g_gla_out[l], GLA_HEADS).reshape(1, -1),
              w_out[l].astype(BF16), row(g_ffn2), w_ffn2_gate[l].astype(BF16), w_ffn2_up[l].astype(BF16),
              w_ffn2_down[l].astype(BF16), row(g_final))
        y_p, k_p, v_p, s_p = _layer(y_p, lw, consts, batch=batch, seq=seq, tm=512)
        cache = (cache_sb_k[l].reshape(dec_batch * past, SB_WIDTH).astype(BF16),
                 cache_sb_v[l].reshape(dec_batch * past, SB_WIDTH).astype(BF16))
        y_s, k_s, v_s, s_s = _layer(y_s, lw, consts, batch=dec_batch, seq=dec_seq, tm=256,
                                    cache=cache, state=_state_to_kernel(state_gla[l]))
        shape_p = (batch, seq, SB_HEADS, SB_HEAD_DIM)
        shape_s = (dec_batch, dec_seq, SB_HEADS, SB_HEAD_DIM)
        for lst, val in zip(outs, (k_p.reshape(shape_p), v_p.reshape(shape_p), s_p,
                                   k_s.reshape(shape_s), v_s.reshape(shape_s), s_s)):
            lst.append(val)
    return (y_p.reshape(batch, seq, D_MODEL), y_s.reshape(dec_batch, dec_seq, D_MODEL),
            *[jnp.stack(o) for o in outs])
```

```python
import functools

import numpy as np
import jax
import jax.numpy as jnp
from jax import lax
from jax.experimental import pallas as pl
from jax.experimental.pallas import tpu as pltpu

F32 = jnp.float32
BF16 = jnp.bfloat16

D_MODEL = 1024
SB_HEADS = 8
SB_HEAD_DIM = 64
SB_WIDTH = SB_HEADS * SB_HEAD_DIM
SB_SCALE = SB_HEAD_DIM ** -0.5
GLA_HEADS = 4
GLA_HEAD_K = 64
GLA_HEAD_V = 128
GLA_KEY_WIDTH = GLA_HEADS * GLA_HEAD_K
GLA_WIDTH = GLA_HEADS * GLA_HEAD_V
GLA_SCALE = GLA_HEAD_K ** -0.5
GATE_RANK = 16
GATE_TAU = 16.0
D_FF = 2816
EPS = 1e-6

LANES = 128
GATE_PAD = LANES
MAIN_WIDTH = 3 * SB_WIDTH + 2 * GLA_KEY_WIDTH + 2 * GLA_WIDTH
IN_PAD = MAIN_WIDTH + GATE_PAD

FF_CHUNK = 256
SB_BLOCK = 128
SB_WINDOW = 3
SB_PAIRS_PER_STEP = 4
GLA_CHUNK = 128
GLA_BATCH_PER_STEP = 2
GLA_DIRECT_MIN = -40.0
SB_STOP = 152.0
LOG2_E = 1.4426950408889634

VMEM_LIMIT = 56 * 1024 * 1024


def _dot(a, b):
    return jnp.dot(a, b, preferred_element_type=F32)


def _dot_nt(a, b):
    return lax.dot_general(a, b, (((1,), (1,)), ((), ())), preferred_element_type=F32)


def _dot_tn(a, b):
    return lax.dot_general(a, b, (((0,), (0,)), ((), ())), preferred_element_type=F32)


def _split_bf16(x):
    hi = x.astype(BF16)
    lo = (x - hi.astype(F32)).astype(BF16)
    return hi, lo


def _dot_hilo(x, m):
    hi, lo = _split_bf16(x)
    return _dot(hi, m) + _dot(lo, m)


def _rms(x, g):
    ms = jnp.mean(x * x, axis=-1, keepdims=True)
    return x * lax.rsqrt(ms + EPS) * g


def _log_sigmoid(x):
    return jnp.minimum(x, 0.0) - jnp.log1p(jnp.exp(-jnp.abs(x)))


def _const_spec(shape):
    nd = len(shape)
    return pl.BlockSpec(shape, lambda *_: (0,) * nd, pipeline_mode=pl.Buffered(1))


def _params(n_grid):
    return pltpu.CompilerParams(dimension_semantics=("arbitrary",) * n_grid,
                                vmem_limit_bytes=VMEM_LIMIT)


def _half_step_ffn(x, g_ref, wg_ref, wu_ref, wd_ref, a_ref):
    h = _rms(x, g_ref[...]).astype(BF16)
    for c in range(D_FF // FF_CHUNK):
        sl = slice(c * FF_CHUNK, (c + 1) * FF_CHUNK)
        gt = _dot(h, wg_ref[:, sl])
        up = _dot(h, wu_ref[:, sl])
        a_ref[:, sl] = (gt * jax.nn.sigmoid(gt) * up).astype(BF16)
    return x + 0.5 * _dot(a_ref[...], wd_ref[...])


def _ffn_kernel(x_ref, g_ref, wg_ref, wu_ref, wd_ref, o_ref, a_ref):
    o_ref[...] = _half_step_ffn(x_ref[...], g_ref, wg_ref, wu_ref, wd_ref, a_ref)


def _ffn_specs():
    return [_const_spec((1, D_MODEL)), _const_spec((D_MODEL, D_FF)), _const_spec((D_MODEL, D_FF)),
            _const_spec((D_FF, D_MODEL))]


def _ffn(x, g, wg, wu, wd, *, tm):
    n = x.shape[0]
    row = lambda i: (i, 0)
    return pl.pallas_call(
        _ffn_kernel,
        grid=(n // tm,),
        in_specs=[pl.BlockSpec((tm, D_MODEL), row)] + _ffn_specs(),
        out_specs=pl.BlockSpec((tm, D_MODEL), row),
        out_shape=jax.ShapeDtypeStruct((n, D_MODEL), F32),
        scratch_shapes=[pltpu.VMEM((tm, D_FF), BF16)],
        compiler_params=_params(1),
        name="ffn1",
    )(x, g, wg, wu, wd)


def _store_heads(ref, y):
    for hd in range(SB_HEADS):
        ref[:, hd, :] = y[:, hd * SB_HEAD_DIM:(hd + 1) * SB_HEAD_DIM]


def _mixin_kernel(x_ref, g_ref, w_ref, wgu_ref, bg_ref, gq_ref, gk_ref, grp_ref,
                  qsb_ref, ksb_ref, ksb16_ref, vsb_ref, vsb16_ref,
                  qg_ref, kg_ref, vg16_ref, r_ref, la_ref):
    h = _rms(x_ref[...], g_ref[...]).astype(BF16)
    grp = grp_ref[...]

    def proj(lo, width):
        return _dot(h, w_ref[:, lo:lo + width])

    def head_norm(y, gain):
        ms = _dot_hilo(y * y, grp) * (1.0 / SB_HEAD_DIM)
        return y * lax.rsqrt(ms + EPS) * gain

    q = head_norm(proj(0, SB_WIDTH), gq_ref[...])
    qsb_ref[...] = (q * SB_SCALE).astype(BF16)
    k = head_norm(proj(SB_WIDTH, SB_WIDTH), gk_ref[...])
    _store_heads(ksb_ref, k)
    ksb16_ref[...] = k.astype(BF16)
    v = proj(2 * SB_WIDTH, SB_WIDTH)
    _store_heads(vsb_ref, v)
    vsb16_ref[...] = v.astype(BF16)
    off = 3 * SB_WIDTH
    qg_ref[...] = proj(off, GLA_KEY_WIDTH) * GLA_SCALE
    kg_ref[...] = proj(off + GLA_KEY_WIDTH, GLA_KEY_WIDTH)
    vg16_ref[...] = proj(off + 2 * GLA_KEY_WIDTH, GLA_WIDTH).astype(BF16)
    r_ref[...] = proj(off + 2 * GLA_KEY_WIDTH + GLA_WIDTH, GLA_WIDTH)
    lr = proj(MAIN_WIDTH, GATE_PAD).astype(BF16)
    gate = _dot(lr, wgu_ref[...]) + bg_ref[...]
    la_ref[...] = _log_sigmoid(gate) * (1.0 / GATE_TAU)


def _mixin(x, g, w, wgu, bg, gq, gk, grp, *, tm):
    n = x.shape[0]
    row = lambda i: (i, 0)
    heads = (SB_HEADS, SB_HEAD_DIM)
    widths = [((SB_WIDTH,), BF16), (heads, F32), ((SB_WIDTH,), BF16), (heads, F32), ((SB_WIDTH,), BF16),
              ((GLA_KEY_WIDTH,), F32), ((GLA_KEY_WIDTH,), F32), ((GLA_WIDTH,), BF16), ((GLA_WIDTH,), F32),
              ((GLA_KEY_WIDTH,), F32)]
    return pl.pallas_call(
        _mixin_kernel,
        grid=(n // tm,),
        in_specs=[pl.BlockSpec((tm, D_MODEL), row),
                  _const_spec((1, D_MODEL)),
                  _const_spec((D_MODEL, IN_PAD)),
                  _const_spec((GATE_PAD, GLA_KEY_WIDTH)),
                  _const_spec((1, GLA_KEY_WIDTH)),
                  _const_spec((1, SB_WIDTH)),
                  _const_spec((1, SB_WIDTH)),
                  _const_spec((SB_WIDTH, SB_WIDTH))],
        out_specs=[pl.BlockSpec((tm,) + wd, lambda i, nd=len(wd): (i,) + (0,) * nd) for wd, _ in widths],
        out_shape=[jax.ShapeDtypeStruct((n,) + wd, dt) for wd, dt in widths],
        compiler_params=_params(1),
        name="mixer_in",
    )(x, g, w, wgu, bg, gq, gk, grp)


def _neg_abs(x):
    sign = jnp.int32(-2 ** 31)
    return lax.bitcast_convert_type(lax.bitcast_convert_type(x, jnp.int32) | sign, F32)


def _sb_log2_terms(z):
    z2 = z * LOG2_E
    t = jnp.log2(1.0 + jnp.exp2(_neg_abs(z2)))
    return jnp.maximum(z2, 0.0) + t, jnp.minimum(z2, 0.0) - t


def _sb_kernel(q_ref, k_ref, v_ref, upair_ref, uo_ref, o_ref, carry_ref, acc_ref, *, q_block):
    tq = q_ref.shape[0]
    pairs = q_ref.shape[1] // LANES
    heads = 2 * pairs
    i = pl.program_id(2) if q_block is None else q_block
    lane = lax.broadcasted_iota(jnp.int32, (tq, LANES), 1)
    head_lanes = (lane < SB_HEAD_DIM, lane >= SB_HEAD_DIM)
    rowi = lax.broadcasted_iota(jnp.int32, (heads * tq, SB_BLOCK), 0)
    coli = lax.broadcasted_iota(jnp.int32, (heads * tq, SB_BLOCK), 1)
    causal = coli < (rowi & (tq - 1))

    def pair_lanes(p):
        return slice(p * LANES, (p + 1) * LANES)

    def q_pair(p):
        q2 = q_ref[:, pair_lanes(p)]
        zero = jnp.zeros_like(q2)
        return jnp.concatenate([jnp.where(hl, q2, zero) for hl in head_lanes], axis=0)

    def rows(ref, p, first_block, nblocks):
        start = first_block * SB_BLOCK
        if not isinstance(start, int):
            start = pl.multiple_of(start, SB_BLOCK)
        return ref[pl.ds(start, nblocks * SB_BLOCK), pair_lanes(p)]

    def suffix_sums(x):
        nb = x.shape[1] // SB_BLOCK
        hi, lo = _split_bf16(x)

        def mm(cols, m2):
            return _dot(jnp.concatenate([hi[:, cols], lo[:, cols]], axis=1), m2)

        last = mm(slice((nb - 1) * SB_BLOCK, nb * SB_BLOCK), uo_ref[...])
        within = [None] * (nb - 1) + [last[:, :SB_BLOCK]]
        for j in range(0, nb - 1, 2):
            if j + 1 < nb - 1:
                y = mm(slice(j * SB_BLOCK, (j + 2) * SB_BLOCK), upair_ref[...])
                within[j], within[j + 1] = y[:, :SB_BLOCK], y[:, SB_BLOCK:]
            else:
                within[j] = mm(slice(j * SB_BLOCK, (j + 1) * SB_BLOCK), uo_ref[...])[:, :SB_BLOCK]
        later = last[:, SB_BLOCK:]
        out = [within[nb - 1]]
        for j in range(nb - 2, -1, -1):
            out.insert(0, within[j] + later)
            later = later + jnp.sum(x[:, j * SB_BLOCK:(j + 1) * SB_BLOCK], axis=1, keepdims=True)
        return (out[0] if nb == 1 else jnp.concatenate(out, axis=1)), later

    def sweep(first_block, nblocks, own, carry):
        nk = nblocks * SB_BLOCK

        def masked(x):
            if not own:
                return x
            tail = jnp.where(causal, x[:, nk - SB_BLOCK:], 0.0)
            return tail if nblocks == 1 else jnp.concatenate([x[:, :nk - SB_BLOCK], tail], axis=1)

        z = jnp.concatenate([_dot_nt(q_pair(p), rows(k_ref, p, first_block, nblocks))
                             for p in range(pairs)], axis=0)
        fail, log_beta = _sb_log2_terms(z)
        after, total = suffix_sums(masked(fail))
        if carry is not None:
            after = after + carry
        w = masked(jnp.exp2(log_beta - after)).astype(BF16)
        pv = jnp.concatenate([_dot(w[2 * p * tq:(2 * p + 2) * tq], rows(v_ref, p, first_block, nblocks))
                              for p in range(pairs)], axis=0)
        if carry is None:
            carry_ref[...] = total
            acc_ref[...] = pv
        else:
            carry_ref[...] = carry + total
            acc_ref[...] += pv

    def own_block_only():
        sweep(i, 1, True, None)

    def full_window():
        sweep(i - (SB_WINDOW - 1), SB_WINDOW, True, None)

    if q_block is None:
        pl.when(i >= SB_WINDOW - 1)(full_window)
        pl.when(i < SB_WINDOW - 1)(own_block_only)
        n_prev = jnp.where(i >= SB_WINDOW - 1, i - (SB_WINDOW - 1), i)
    else:
        assert q_block >= SB_WINDOW - 1
        full_window()
        n_prev = q_block - (SB_WINDOW - 1)

    def live():
        return jnp.min(carry_ref[...]) < SB_STOP

    def body(st):
        kb, _ = st
        sweep(kb, 1, False, carry_ref[...])
        return kb - 1, live()

    lax.while_loop(lambda st: jnp.logical_and(st[0] >= 0, st[1]), body, (n_prev - 1, live()))
    for p in range(pairs):
        o_ref[:, pair_lanes(p)] = jnp.where(head_lanes[0], acc_ref[2 * p * tq:(2 * p + 1) * tq],
                                            acc_ref[(2 * p + 1) * tq:(2 * p + 2) * tq])


def _sb(q16, k16, v16, upair, uo, *, batch, tq, keys, q_block):
    nq = keys // SB_BLOCK if q_block is None else 1
    width = SB_PAIRS_PER_STEP * LANES
    kv_spec = pl.BlockSpec((keys, width), lambda b, p, i: (b, p))
    q_spec = pl.BlockSpec((tq, width), lambda b, p, i: (b * nq + i, p))
    heads = 2 * SB_PAIRS_PER_STEP
    return pl.pallas_call(
        functools.partial(_sb_kernel, q_block=q_block),
        grid=(batch, SB_WIDTH // width, nq),
        in_specs=[q_spec, kv_spec, kv_spec, _const_spec(upair.shape), _const_spec(uo.shape)],
        out_specs=q_spec,
        out_shape=jax.ShapeDtypeStruct((batch * nq * tq, SB_WIDTH), F32),
        scratch_shapes=[pltpu.VMEM((heads * tq, SB_BLOCK), F32),
                        pltpu.VMEM((heads * tq, LANES), F32)],
        compiler_params=_params(3),
        name="sb_prompt" if q_block is None else "sb_sample",
    )(q16, k16, v16, upair, uo)


def _gla_levels(c):
    return int(np.log2(c))


def _gla_tables(c):
    nlev = _gla_levels(c)
    t = np.arange(c)[:, None]
    j = np.arange(c)[None, :]
    mats = [(j <= t), (j > t)]
    for lev in range(1, nlev + 1):
        m = 1 << (lev - 1)
        start = (t // (2 * m)) * (2 * m)
        mats.append(((t % (2 * m)) >= m) & (j >= start + m) & (j <= t))
    for lev in range(1, nlev + 1):
        m = 1 << (lev - 1)
        start = (t // (2 * m)) * (2 * m)
        mats.append(((t % (2 * m)) < m) & (j > t) & (j <= start + m - 1))
    mats = np.concatenate([a.astype(np.float32) for a in mats], axis=0)
    x = t ^ j
    level = np.where(j > t, -1, np.where(x == 0, 0, np.floor(np.log2(np.maximum(x, 1))).astype(np.int64) + 1))
    return jnp.asarray(mats, BF16), jnp.asarray(level, jnp.int32)


def _gla_kernel(q_ref, k_ref, la_ref, v_ref, s0_ref, mats_ref, lv_ref, o_ref, sfin_ref, st_ref):
    c = GLA_CHUNK
    nlev = _gla_levels(c)
    streams = q_ref.shape[0]
    step = pl.program_id(1)

    @pl.when(step == 0)
    def _():
        st_ref[...] = s0_ref[...]

    lane = lax.broadcasted_iota(jnp.int32, (c, LANES), 1)
    head_lanes = (lane < GLA_HEAD_K, lane >= GLA_HEAD_K)
    lane_sq = lax.broadcasted_iota(jnp.int32, (GLA_HEAD_V, LANES), 1) < GLA_HEAD_K

    def pick(x, hl):
        return jnp.where(hl, x, jnp.zeros_like(x))

    def stack_heads(x):
        return jnp.concatenate([pick(x, hl) for hl in head_lanes], axis=0)

    def finish(s, p, scores, qe, ke, b_last):
        sl = slice(p * LANES, (p + 1) * LANES)
        st = st_ref[s, p]
        inter = _dot_nt(stack_heads(qe[:, sl]), st.astype(BF16))
        upd = []
        for r in range(2):
            hd = 2 * p + r
            vh = v_ref[s, :, hd * GLA_HEAD_V:(hd + 1) * GLA_HEAD_V]
            o_ref[s, :, hd * GLA_HEAD_V:(hd + 1) * GLA_HEAD_V] = (
                inter[r * c:(r + 1) * c] + _dot(scores[r * c:(r + 1) * c], vh))
            upd.append(_dot_tn(vh, ke[:, sl]))
        st_ref[s, p] = st * jnp.exp(b_last[:, sl]) + jnp.where(lane_sq, upd[0], upd[1])

    la_all = jnp.concatenate([la_ref[s] for s in range(streams)], axis=1)
    b_all = _dot_hilo_left(mats_ref[:c, :], la_all)
    mild = jnp.min(b_all[c - 1:c, :]) >= GLA_DIRECT_MIN

    @pl.when(mild)
    def _():
        rowi = lax.broadcasted_iota(jnp.int32, (2 * c, c), 0) & (c - 1)
        causal = lax.broadcasted_iota(jnp.int32, (2 * c, c), 1) <= rowi
        for s in range(streams):
            b = b_all[:, s * GLA_KEY_WIDTH:(s + 1) * GLA_KEY_WIDTH]
            b_last = b[c - 1:c, :]
            q = q_ref[s]
            k = k_ref[s]
            qe = (q * jnp.exp(b)).astype(BF16)
            kd = (k * jnp.exp(-b)).astype(BF16)
            ke = (k * jnp.exp(b_last - b)).astype(BF16)
            for p in range(GLA_HEADS // 2):
                sl = slice(p * LANES, (p + 1) * LANES)
                scores = jnp.where(causal, _dot_nt(stack_heads(qe[:, sl]), kd[:, sl]), 0.0)
                finish(s, p, scores.astype(BF16), qe, ke, b_last)

    @pl.when(jnp.logical_not(mild))
    def _():
        lv = jnp.concatenate([lv_ref[...], lv_ref[...]], axis=0)
        for s in range(streams):
            q = q_ref[s]
            k = k_ref[s]
            sums = _dot_hilo_left(mats_ref[...], la_ref[s])

            def rows(i):
                return sums[i * c:(i + 1) * c]

            qe = (q * jnp.exp(rows(0))).astype(BF16)
            ke = (k * jnp.exp(rows(1))).astype(BF16)
            q_lev = [q.astype(BF16)] + [(q * jnp.exp(rows(1 + lev))).astype(BF16) for lev in range(1, nlev + 1)]
            k_lev = [k.astype(BF16)] + [(k * jnp.exp(rows(1 + nlev + lev))).astype(BF16)
                                        for lev in range(1, nlev + 1)]
            for p in range(GLA_HEADS // 2):
                sl = slice(p * LANES, (p + 1) * LANES)
                scores = jnp.zeros((2 * c, c), F32)
                for lev in range(nlev + 1):
                    scores = jnp.where(lv == lev, _dot_nt(stack_heads(q_lev[lev][:, sl]), k_lev[lev][:, sl]),
                                       scores)
                finish(s, p, scores.astype(BF16), qe, ke, rows(0)[c - 1:c, :])

    sfin_ref[...] = st_ref[...]


def _dot_hilo_left(m, x):
    hi, lo = _split_bf16(x)
    return _dot(m, hi) + _dot(m, lo)


def _gla(qg, kg, la, vg16, s0, mats, lv, *, batch, seq):
    c = GLA_CHUNK
    g = GLA_BATCH_PER_STEP
    pairs = GLA_HEADS // 2
    blk = lambda width: pl.BlockSpec((g, c, width), lambda b, t: (b, t, 0))
    st_spec = pl.BlockSpec((g, pairs, GLA_HEAD_V, LANES), lambda b, t: (b, 0, 0, 0))
    per_stream = lambda a: a.reshape(batch, seq, a.shape[-1])
    o, s_fin = pl.pallas_call(
        _gla_kernel,
        grid=(batch // g, seq // c),
        in_specs=[blk(GLA_KEY_WIDTH), blk(GLA_KEY_WIDTH), blk(GLA_KEY_WIDTH), blk(GLA_WIDTH), st_spec,
                  _const_spec(mats.shape), _const_spec(lv.shape)],
        out_specs=[blk(GLA_WIDTH), st_spec],
        out_shape=[jax.ShapeDtypeStruct((batch, seq, GLA_WIDTH), F32),
                   jax.ShapeDtypeStruct((batch, pairs, GLA_HEAD_V, LANES), F32)],
        scratch_shapes=[pltpu.VMEM((g, pairs, GLA_HEAD_V, LANES), F32)],
        compiler_params=_params(2),
        name="gla",
    )(per_stream(qg), per_stream(kg), per_stream(la), per_stream(vg16), s0, mats, lv)
    return o.reshape(batch * seq, GLA_WIDTH), s_fin


def _head_norm_lanes(y, head, gain):
    tm, width = y.shape
    low = lax.broadcasted_iota(jnp.int32, (tm, LANES), 1) < head
    cols = []
    for j in range(width // LANES):
        yj = y[:, j * LANES:(j + 1) * LANES]
        sq = yj * yj

        def inv_rms(part):
            return lax.rsqrt(jnp.sum(part, axis=1, keepdims=True) * (1.0 / head) + EPS)

        if head == LANES:
            cols.append(yj * inv_rms(sq))
        else:
            cols.append(yj * jnp.where(low, inv_rms(jnp.where(low, sq, 0.0)), inv_rms(jnp.where(low, 0.0, sq))))
    return jnp.concatenate(cols, axis=1) * gain


def _mixout_kernel(x_ref, osb_ref, og_ref, r_ref, gsb_ref, ggla_ref, w_ref,
                   g_ref, wg_ref, wu_ref, wd_ref, gfin_ref, o_ref, a_ref):
    o_sb = _head_norm_lanes(osb_ref[...], SB_HEAD_DIM, gsb_ref[...])
    r = r_ref[...]
    o_g = _head_norm_lanes(og_ref[...], GLA_HEAD_V, ggla_ref[...]) * (r * jax.nn.sigmoid(r))
    mix = _dot(o_sb.astype(BF16), w_ref[:SB_WIDTH, :]) + _dot(o_g.astype(BF16), w_ref[SB_WIDTH:, :])
    y = _half_step_ffn(x_ref[...] + mix, g_ref, wg_ref, wu_ref, wd_ref, a_ref)
    o_ref[...] = _rms(y, gfin_ref[...])


def _mixout(x, osb, og, r, gsb, ggla, w, g, wg, wu, wd, gfin, *, tm):
    n = x.shape[0]
    row = lambda i: (i, 0)
    return pl.pallas_call(
        _mixout_kernel,
        grid=(n // tm,),
        in_specs=[pl.BlockSpec((tm, D_MODEL), row),
                  pl.BlockSpec((tm, SB_WIDTH), row),
                  pl.BlockSpec((tm, GLA_WIDTH), row),
                  pl.BlockSpec((tm, GLA_WIDTH), row),
                  _const_spec((1, SB_WIDTH)),
                  _const_spec((1, GLA_WIDTH)),
                  _const_spec((D_MODEL, D_MODEL))] + _ffn_specs() + [_const_spec((1, D_MODEL))],
        out_specs=pl.BlockSpec((tm, D_MODEL), row),
        out_shape=jax.ShapeDtypeStruct((n, D_MODEL), F32),
        scratch_shapes=[pltpu.VMEM((tm, D_FF), BF16)],
        compiler_params=_params(1),
        name="mixer_out_ffn2",
    )(x, osb, og, r, gsb, ggla, w, g, wg, wu, wd, gfin)


def _group_ones(width, group):
    idx = np.arange(width) // group
    return jnp.asarray(idx[:, None] == idx[None, :], BF16)


def _suffix_matrices():
    j = np.arange(SB_BLOCK)[:, None]
    s = np.arange(SB_BLOCK)[None, :]
    later = j > s
    zero = np.zeros_like(later)
    pair = np.block([[later, zero], [zero, later]])
    single = np.concatenate([later, np.ones_like(later)], axis=1)
    twice = lambda m: jnp.asarray(np.concatenate([m, m], axis=0), BF16)
    return twice(pair), twice(single)


def _state_to_kernel(s):
    b = s.shape[0]
    s = s.reshape(b, GLA_HEADS // 2, 2, GLA_HEAD_K, GLA_HEAD_V)
    return s.transpose(0, 1, 4, 2, 3).reshape(b, GLA_HEADS // 2, GLA_HEAD_V, LANES)


def _state_from_kernel(s):
    b = s.shape[0]
    s = s.reshape(b, GLA_HEADS // 2, GLA_HEAD_V, 2, GLA_HEAD_K)
    return s.transpose(0, 1, 3, 4, 2).reshape(b, GLA_HEADS, GLA_HEAD_K, GLA_HEAD_V)


def _layer(x, lw, consts, *, batch, seq, tm, cache=None, state=None):
    (g1, w1g, w1u, w1d, g_mix, w_in, wgu, bg, gq, gk, gsb, ggla, w_out, g2, w2g, w2u, w2d, gfin) = lw
    grp64, upair, uo, mats, lv = consts
    x1 = _ffn(x, g1, w1g, w1u, w1d, tm=tm)
    (qsb16, ksb, ksb16, vsb, vsb16, qg, kg, vg16, r, la) = _mixin(x1, g_mix, w_in, wgu, bg, gq, gk, grp64, tm=tm)

    if cache is None:
        o_sb = _sb(qsb16, ksb16, vsb16, upair, uo, batch=batch, tq=SB_BLOCK, keys=seq, q_block=None)
    else:
        past = cache[0].shape[0] // batch
        keys = past + SB_BLOCK

        def with_cache(old, new):
            new = jnp.pad(new.reshape(batch, seq, SB_WIDTH), ((0, 0), (0, SB_BLOCK - seq), (0, 0)))
            return jnp.concatenate([old.reshape(batch, past, SB_WIDTH), new], axis=1).reshape(batch * keys, SB_WIDTH)

        o_sb = _sb(qsb16, with_cache(cache[0], ksb16), with_cache(cache[1], vsb16), upair, uo,
                   batch=batch, tq=seq, keys=keys, q_block=past // SB_BLOCK)

    c = GLA_CHUNK
    seq_pad = -(-seq // c) * c
    if seq_pad != seq:
        padt = lambda a: jnp.pad(a.reshape(batch, seq, a.shape[-1]),
                                 ((0, 0), (0, seq_pad - seq), (0, 0))).reshape(batch * seq_pad, a.shape[-1])
        qg_p, kg_p, la_p, vg_p = padt(qg), padt(kg), padt(la), padt(vg16)
    else:
        qg_p, kg_p, la_p, vg_p = qg, kg, la, vg16
    if state is None:
        state = jnp.zeros((batch, GLA_HEADS // 2, GLA_HEAD_V, LANES), F32)
    o_g, s_fin = _gla(qg_p, kg_p, la_p, vg_p, state, mats, lv, batch=batch, seq=seq_pad)
    if seq_pad != seq:
        o_g = o_g.reshape(batch, seq_pad, GLA_WIDTH)[:, :seq].reshape(batch * seq, GLA_WIDTH)

    y = _mixout(x1, o_sb, o_g, r, gsb, ggla, w_out, g2, w2g, w2u, w2d, gfin, tm=tm)
    return y, ksb, vsb, _state_from_kernel(s_fin)


def kernel(x_prompt, x_sample, cache_sb_k, cache_sb_v, state_gla, g_ffn1, w_ffn1_gate, w_ffn1_up,
           w_ffn1_down, g_mix, w_in, w_gate_up, b_gate, g_q, g_k, g_sb_out, g_gla_out, w_out,
           g_ffn2, w_ffn2_gate, w_ffn2_up, w_ffn2_down, g_final):
    depth = w_in.shape[0]
    batch, seq, _ = x_prompt.shape
    dec_batch, dec_seq, _ = x_sample.shape
    past = cache_sb_k.shape[2]
    consts = (_group_ones(SB_WIDTH, SB_HEAD_DIM),
              *_suffix_matrices(), *_gla_tables(GLA_CHUNK))

    y_p = x_prompt.reshape(batch * seq, D_MODEL)
    y_s = x_sample.reshape(dec_batch * dec_seq, D_MODEL)
    outs = [[] for _ in range(6)]
    for l in range(depth):
        row = lambda a: a[l].reshape(1, -1)
        w_main = w_in[l][:, :MAIN_WIDTH]
        w_lr = jnp.pad(w_in[l][:, MAIN_WIDTH:], ((0, 0), (0, GATE_PAD - GATE_RANK)))
        lw = (row(g_ffn1), w_ffn1_gate[l].astype(BF16), w_ffn1_up[l].astype(BF16), w_ffn1_down[l].astype(BF16),
              row(g_mix), jnp.concatenate([w_main, w_lr], axis=1).astype(BF16),
              jnp.pad(w_gate_up[l], ((0, GATE_PAD - GATE_RANK), (0, 0))).astype(BF16), row(b_gate),
              jnp.tile(g_q[l], SB_HEADS).reshape(1, -1), jnp.tile(g_k[l], SB_HEADS).reshape(1, -1),
              jnp.tile(g_sb_out[l], SB_HEADS).reshape(1, -1), jnp.tile(g_gla_out[l], GLA_HEADS).reshape(1, -1),
              w_out[l].astype(BF16), row(g_ffn2), w_ffn2_gate[l].astype(BF16), w_ffn2_up[l].astype(BF16),
              w_ffn2_down[l].astype(BF16), row(g_final))
        y_p, k_p, v_p, s_p = _layer(y_p, lw, consts, batch=batch, seq=seq, tm=512)
        cache = (cache_sb_k[l].reshape(dec_batch * past, SB_WIDTH).astype(BF16),
                 cache_sb_v[l].reshape(dec_batch * past, SB_WIDTH).astype(BF16))
        y_s, k_s, v_s, s_s = _layer(y_s, lw, consts, batch=dec_batch, seq=dec_seq, tm=256,
                                    cache=cache, state=_state_to_kernel(state_gla[l]))
        shape_p = (batch, seq, SB_HEADS, SB_HEAD_DIM)
        shape_s = (dec_batch, dec_seq, SB_HEADS, SB_HEAD_DIM)
        for lst, val in zip(outs, (k_p.reshape(shape_p), v_p.reshape(shape_p), s_p,
                                   k_s.reshape(shape_s), v_s.reshape(shape_s), s_s)):
            lst.append(val)
    return (y_p.reshape(batch, seq, D_MODEL), y_s.reshape(dec_batch, dec_seq, D_MODEL),
            *[jnp.stack(o) for o in outs])
```

```python
import functools

import numpy as np
import jax
import jax.numpy as jnp
from jax import lax
from jax.experimental import pallas as pl
from jax.experimental.pallas import tpu as pltpu

F32 = jnp.float32
BF16 = jnp.bfloat16

D_MODEL = 1024
SB_HEADS = 8
SB_HEAD_DIM = 64
SB_WIDTH = SB_HEADS * SB_HEAD_DIM
SB_SCALE = SB_HEAD_DIM ** -0.5
GLA_HEADS = 4
GLA_HEAD_K = 64
GLA_HEAD_V = 128
GLA_KEY_WIDTH = GLA_HEADS * GLA_HEAD_K
GLA_WIDTH = GLA_HEADS * GLA_HEAD_V
GLA_SCALE = GLA_HEAD_K ** -0.5
GATE_RANK = 16
GATE_TAU = 16.0
D_FF = 2816
EPS = 1e-6

LANES = 128
SUBLANES = 8
GATE_PAD = LANES
MAIN_WIDTH = 3 * SB_WIDTH + 2 * GLA_KEY_WIDTH + 2 * GLA_WIDTH
IN_PAD = MAIN_WIDTH + GATE_PAD

FF_CHUNK = 256
SB_BLOCK = 128
SB_WINDOW = 3
SB_PAIRS_PER_STEP = 4
GLA_CHUNK = 128
GLA_BATCH_PER_STEP = 2
GLA_DIRECT_MIN = -40.0
SB_STOP = 152.0
LOG2_E = 1.4426950408889634

VMEM_LIMIT = 56 * 1024 * 1024


def _dot(a, b):
    return jnp.dot(a, b, preferred_element_type=F32)


def _dot_nt(a, b):
    return lax.dot_general(a, b, (((1,), (1,)), ((), ())), preferred_element_type=F32)


def _dot_tn(a, b):
    return lax.dot_general(a, b, (((0,), (0,)), ((), ())), preferred_element_type=F32)


def _split_bf16(x):
    hi = x.astype(BF16)
    lo = (x - hi.astype(F32)).astype(BF16)
    return hi, lo


def _dot_hilo(x, m):
    hi, lo = _split_bf16(x)
    return _dot(hi, m) + _dot(lo, m)


def _rms(x, g):
    ms = jnp.mean(x * x, axis=-1, keepdims=True)
    return x * lax.rsqrt(ms + EPS) * g


def _log_sigmoid(x):
    return jnp.minimum(x, 0.0) - jnp.log1p(jnp.exp(-jnp.abs(x)))


def _const_spec(shape):
    nd = len(shape)
    return pl.BlockSpec(shape, lambda *_: (0,) * nd, pipeline_mode=pl.Buffered(1))


def _params(n_grid):
    return pltpu.CompilerParams(dimension_semantics=("arbitrary",) * n_grid,
                                vmem_limit_bytes=VMEM_LIMIT)


def _half_step_ffn(x, g_ref, wg_ref, wu_ref, wd_ref, a_ref):
    h = _rms(x, g_ref[...]).astype(BF16)
    for c in range(D_FF // FF_CHUNK):
        sl = slice(c * FF_CHUNK, (c + 1) * FF_CHUNK)
        gt = _dot(h, wg_ref[:, sl])
        up = _dot(h, wu_ref[:, sl])
        a_ref[:, sl] = (gt * jax.nn.sigmoid(gt) * up).astype(BF16)
    return x + 0.5 * _dot(a_ref[...], wd_ref[...])


def _ffn_kernel(x_ref, g_ref, wg_ref, wu_ref, wd_ref, o_ref, a_ref):
    o_ref[...] = _half_step_ffn(x_ref[...], g_ref, wg_ref, wu_ref, wd_ref, a_ref)


def _ffn_specs():
    return [_const_spec((1, D_MODEL)), _const_spec((D_MODEL, D_FF)), _const_spec((D_MODEL, D_FF)),
            _const_spec((D_FF, D_MODEL))]


def _ffn(x, g, wg, wu, wd, *, tm):
    n = x.shape[0]
    row = lambda i: (i, 0)
    return pl.pallas_call(
        _ffn_kernel,
        grid=(n // tm,),
        in_specs=[pl.BlockSpec((tm, D_MODEL), row)] + _ffn_specs(),
        out_specs=pl.BlockSpec((tm, D_MODEL), row),
        out_shape=jax.ShapeDtypeStruct((n, D_MODEL), F32),
        scratch_shapes=[pltpu.VMEM((tm, D_FF), BF16)],
        compiler_params=_params(1),
        name="ffn1",
    )(x, g, wg, wu, wd)


def _store_heads(ref, y):
    groups = y.shape[0] // SUBLANES
    sub = lax.broadcasted_iota(jnp.int32, (groups, SUBLANES, LANES), 1)
    rows = []
    for j in range(SB_WIDTH // LANES):
        pair = y[:, j * LANES:(j + 1) * LANES].reshape(groups, SUBLANES, LANES)
        rows += [pair, pltpu.roll(pair, SB_HEAD_DIM, axis=2)]
    for dist in (4, 2, 1):
        low = (sub & dist) == 0
        new = list(rows)
        for h in range(SUBLANES):
            if h & dist == 0:
                a, b = rows[h], rows[h + dist]
                new[h] = jnp.where(low, a, pltpu.roll(b, dist, axis=1))
                new[h + dist] = jnp.where(low, pltpu.roll(a, SUBLANES - dist, axis=1), b)
        rows = new
    for t in range(SUBLANES):
        ref[:, t] = rows[t][:, :, :SB_HEAD_DIM]


def _mixin_kernel(x_ref, g_ref, w_ref, wgu_ref, bg_ref, gq_ref, gk_ref, grp_ref,
                  qsb_ref, ksb_ref, ksb16_ref, vsb_ref, vsb16_ref,
                  qg_ref, kg_ref, vg16_ref, r_ref, la_ref):
    h = _rms(x_ref[...], g_ref[...]).astype(BF16)
    grp = grp_ref[...]

    def proj(lo, width):
        return _dot(h, w_ref[:, lo:lo + width])

    def head_norm(y, gain):
        ms = _dot_hilo(y * y, grp) * (1.0 / SB_HEAD_DIM)
        return y * lax.rsqrt(ms + EPS) * gain

    q = head_norm(proj(0, SB_WIDTH), gq_ref[...])
    qsb_ref[...] = (q * SB_SCALE).astype(BF16)
    k = head_norm(proj(SB_WIDTH, SB_WIDTH), gk_ref[...])
    _store_heads(ksb_ref, k)
    ksb16_ref[...] = k.astype(BF16)
    v = proj(2 * SB_WIDTH, SB_WIDTH)
    _store_heads(vsb_ref, v)
    vsb16_ref[...] = v.astype(BF16)
    off = 3 * SB_WIDTH
    qg_ref[...] = proj(off, GLA_KEY_WIDTH) * GLA_SCALE
    kg_ref[...] = proj(off + GLA_KEY_WIDTH, GLA_KEY_WIDTH)
    vg16_ref[...] = proj(off + 2 * GLA_KEY_WIDTH, GLA_WIDTH).astype(BF16)
    r_ref[...] = proj(off + 2 * GLA_KEY_WIDTH + GLA_WIDTH, GLA_WIDTH)
    lr = proj(MAIN_WIDTH, GATE_PAD).astype(BF16)
    gate = _dot(lr, wgu_ref[...]) + bg_ref[...]
    la_ref[...] = _log_sigmoid(gate) * (1.0 / GATE_TAU)


def _mixin(x, g, w, wgu, bg, gq, gk, grp, *, tm):
    n = x.shape[0]
    row = lambda i: (i, 0)
    heads = (SB_HEADS, SB_HEAD_DIM)
    assert SB_HEADS == SUBLANES and tm % SUBLANES == 0
    lead = lambda rows, wd: (rows // SUBLANES, SUBLANES) if len(wd) == 2 else (rows,)
    widths = [((SB_WIDTH,), BF16), (heads, F32), ((SB_WIDTH,), BF16), (heads, F32), ((SB_WIDTH,), BF16),
              ((GLA_KEY_WIDTH,), F32), ((GLA_KEY_WIDTH,), F32), ((GLA_WIDTH,), BF16), ((GLA_WIDTH,), F32),
              ((GLA_KEY_WIDTH,), F32)]
    return pl.pallas_call(
        _mixin_kernel,
        grid=(n // tm,),
        in_specs=[pl.BlockSpec((tm, D_MODEL), row),
                  _const_spec((1, D_MODEL)),
                  _const_spec((D_MODEL, IN_PAD)),
                  _const_spec((GATE_PAD, GLA_KEY_WIDTH)),
                  _const_spec((1, GLA_KEY_WIDTH)),
                  _const_spec((1, SB_WIDTH)),
                  _const_spec((1, SB_WIDTH)),
                  _const_spec((SB_WIDTH, SB_WIDTH))],
        out_specs=[pl.BlockSpec(lead(tm, wd) + wd, lambda i, nd=len(lead(tm, wd) + wd) - 1: (i,) + (0,) * nd)
                   for wd, _ in widths],
        out_shape=[jax.ShapeDtypeStruct(lead(n, wd) + wd, dt) for wd, dt in widths],
        compiler_params=_params(1),
        name="mixer_in",
    )(x, g, w, wgu, bg, gq, gk, grp)


def _neg_abs(x):
    return -jnp.abs(x)


def _sb_log2_terms(z):
    z2 = z * LOG2_E
    t = jnp.log2(1.0 + jnp.exp2(_neg_abs(z2)))
    return jnp.maximum(z2, 0.0) + t, jnp.minimum(z2, 0.0) - t


def _sb_kernel(q_ref, k_ref, v_ref, upair_ref, uo_ref, o_ref, carry_ref, acc_ref, live_ref, *, q_block):
    tq = q_ref.shape[0]
    pairs = q_ref.shape[1] // LANES
    heads = 2 * pairs
    i = pl.program_id(2) if q_block is None else q_block
    lane = lax.broadcasted_iota(jnp.int32, (tq, LANES), 1)
    head_lanes = (lane < SB_HEAD_DIM, lane >= SB_HEAD_DIM)
    rowi = lax.broadcasted_iota(jnp.int32, (heads * tq, SB_BLOCK), 0)
    coli = lax.broadcasted_iota(jnp.int32, (heads * tq, SB_BLOCK), 1)
    causal = coli < (rowi & (tq - 1))

    def pair_lanes(p):
        return slice(p * LANES, (p + 1) * LANES)

    def q_pair(p):
        q2 = q_ref[:, pair_lanes(p)]
        zero = jnp.zeros_like(q2)
        return jnp.concatenate([jnp.where(hl, q2, zero) for hl in head_lanes], axis=0)

    def rows(ref, p, first_block, nblocks):
        start = first_block * SB_BLOCK
        if not isinstance(start, int):
            start = pl.multiple_of(start, SB_BLOCK)
        return ref[pl.ds(start, nblocks * SB_BLOCK), pair_lanes(p)]

    def suffix_sums(x):
        nb = x.shape[1] // SB_BLOCK
        hi, lo = _split_bf16(x)

        def mm(cols, m2):
            return _dot(jnp.concatenate([hi[:, cols], lo[:, cols]], axis=1), m2)

        within, total = [], []
        if nb >= 3:
            for j in range(0, nb - 1, 2):
                y = mm(slice(j * SB_BLOCK, (j + 2) * SB_BLOCK), upair_ref[...])
                within += [y[:, :SB_BLOCK], y[:, SB_BLOCK:]]
                total += [jnp.sum(x[:, b * SB_BLOCK:(b + 1) * SB_BLOCK], axis=1, keepdims=True) for b in (j, j + 1)]
        for j in range(len(within), nb):
            y = mm(slice(j * SB_BLOCK, (j + 1) * SB_BLOCK), uo_ref[...])
            within.append(y[:, :SB_BLOCK])
            total.append(y[:, SB_BLOCK:])
        later = total[nb - 1]
        out = [within[nb - 1]]
        for j in range(nb - 2, -1, -1):
            out.insert(0, within[j] + later)
            later = later + total[j]
        return (out[0] if nb == 1 else jnp.concatenate(out, axis=1)), later

    def sweep(first_block, nblocks, own, carry, valid=None):
        nk = nblocks * SB_BLOCK

        def masked(x):
            if valid is not None:
                return jnp.where(valid, x, 0.0)
            if not own:
                return x
            tail = jnp.where(causal, x[:, nk - SB_BLOCK:], 0.0)
            return tail if nblocks == 1 else jnp.concatenate([x[:, :nk - SB_BLOCK], tail], axis=1)

        z = jnp.concatenate([_dot_nt(q_pair(p), rows(k_ref, p, first_block, nblocks))
                             for p in range(pairs)], axis=0)
        fail, log_beta = _sb_log2_terms(z)
        after, total = suffix_sums(masked(fail))
        if carry is not None:
            after = after + carry
        w = masked(jnp.exp2(log_beta - after)).astype(BF16)
        pv = jnp.concatenate([_dot(w[2 * p * tq:(2 * p + 2) * tq], rows(v_ref, p, first_block, nblocks))
                              for p in range(pairs)], axis=0)
        if carry is None:
            carry_ref[...] = total
            acc_ref[...] = pv
        else:
            carry_ref[...] = carry + total
            acc_ref[...] += pv

    def live():
        return jnp.min(carry_ref[...]) < SB_STOP

    def mark_live():
        live_ref[0] = jnp.where(live(), 1, 0)

    half = SB_BLOCK // 2

    def by_halves(x, f):
        y = [f(s, jnp.concatenate([x[s * half:(s + 1) * half], x[tq + s * half:tq + (s + 1) * half]], axis=0))
             for s in range(2)]
        return jnp.concatenate([y[0][:half], y[1][:half], y[0][half:], y[1][half:]], axis=0)

    def half_windows():
        starts = [pl.multiple_of(i * SB_BLOCK - (3 - s) * half, half) for s in range(2)]
        nk = 2 * SB_BLOCK
        keep = coli < (rowi & (half - 1)) + half

        def masked(x):
            return jnp.concatenate([x[:, :SB_BLOCK], jnp.where(keep, x[:, SB_BLOCK:], 0.0)], axis=1)

        z = jnp.concatenate(
            [by_halves(q_pair(p), lambda s, lhs: _dot_nt(lhs, k_ref[pl.ds(starts[s], nk), pair_lanes(p)]))
             for p in range(pairs)], axis=0)
        fail, log_beta = _sb_log2_terms(z)
        after, total = suffix_sums(masked(fail))
        w = masked(jnp.exp2(log_beta - after)).astype(BF16)
        carry_ref[...] = total
        acc_ref[...] = jnp.concatenate(
            [by_halves(w[2 * p * tq:(2 * p + 2) * tq],
                       lambda s, lhs: _dot(lhs, v_ref[pl.ds(starts[s], nk), pair_lanes(p)]))
             for p in range(pairs)], axis=0)

    def own_block_only():
        sweep(i, 1, True, None)

    def full_window():
        sweep(i - (SB_WINDOW - 1), SB_WINDOW, True, None)

    if q_block is None:
        assert tq == SB_BLOCK
        windowed = i >= 2
        pl.when(windowed)(half_windows)
        pl.when(jnp.logical_not(windowed))(own_block_only)

        mark_live()

        @pl.when(jnp.logical_and(windowed, live_ref[0] != 0))
        def _():
            unseen = jnp.logical_or(coli < half, (rowi & half) != 0)
            sweep(i - 2, 1, False, carry_ref[...], unseen)
            mark_live()

        n_prev = jnp.where(windowed, i - 2, i)
    else:
        assert q_block >= SB_WINDOW - 1
        full_window()
        mark_live()
        n_prev = q_block - (SB_WINDOW - 1)

    def body(st):
        kb, _ = st
        sweep(kb, 1, False, carry_ref[...])
        return kb - 1, live()

    lax.while_loop(lambda st: jnp.logical_and(st[0] >= 0, st[1]), body, (n_prev - 1, live_ref[0] != 0))
    for p in range(pairs):
        o_ref[:, pair_lanes(p)] = jnp.where(head_lanes[0], acc_ref[2 * p * tq:(2 * p + 1) * tq],
                                            acc_ref[(2 * p + 1) * tq:(2 * p + 2) * tq])


def _sb(q16, k16, v16, upair, uo, *, batch, tq, keys, q_block):
    nq = keys // SB_BLOCK if q_block is None else 1
    width = SB_PAIRS_PER_STEP * LANES
    kv_spec = pl.BlockSpec((keys, width), lambda b, p, i: (b, p))
    q_spec = pl.BlockSpec((tq, width), lambda b, p, i: (b * nq + i, p))
    heads = 2 * SB_PAIRS_PER_STEP
    return pl.pallas_call(
        functools.partial(_sb_kernel, q_block=q_block),
        grid=(batch, SB_WIDTH // width, nq),
        in_specs=[q_spec, kv_spec, kv_spec, _const_spec(upair.shape), _const_spec(uo.shape)],
        out_specs=q_spec,
        out_shape=jax.ShapeDtypeStruct((batch * nq * tq, SB_WIDTH), F32),
        scratch_shapes=[pltpu.VMEM((heads * tq, SB_BLOCK), F32),
                        pltpu.VMEM((heads * tq, LANES), F32),
                        pltpu.SMEM((1,), jnp.int32)],
        compiler_params=_params(3),
        name="sb_prompt" if q_block is None else "sb_sample",
    )(q16, k16, v16, upair, uo)


def _gla_levels(c):
    return int(np.log2(c))


def _gla_tables(c):
    nlev = _gla_levels(c)
    t = np.arange(c)[:, None]
    j = np.arange(c)[None, :]
    mats = [(j <= t), (j > t)]
    for lev in range(1, nlev + 1):
        m = 1 << (lev - 1)
        start = (t // (2 * m)) * (2 * m)
        mats.append(((t % (2 * m)) >= m) & (j >= start + m) & (j <= t))
    for lev in range(1, nlev + 1):
        m = 1 << (lev - 1)
        start = (t // (2 * m)) * (2 * m)
        mats.append(((t % (2 * m)) < m) & (j > t) & (j <= start + m - 1))
    mats = np.concatenate([a.astype(np.float32) for a in mats], axis=0)
    x = t ^ j
    level = np.where(j > t, -1, np.where(x == 0, 0, np.floor(np.log2(np.maximum(x, 1))).astype(np.int64) + 1))
    return jnp.asarray(mats, BF16), jnp.asarray(level, jnp.int32)


def _gla_kernel(q_ref, k_ref, la_ref, v_ref, s0_ref, mats_ref, lv_ref, o_ref, sfin_ref, st_ref):
    c = GLA_CHUNK
    nlev = _gla_levels(c)
    streams = q_ref.shape[0]
    step = pl.program_id(1)

    @pl.when(step == 0)
    def _():
        st_ref[...] = s0_ref[...]

    lane = lax.broadcasted_iota(jnp.int32, (c, LANES), 1)
    head_lanes = (lane < GLA_HEAD_K, lane >= GLA_HEAD_K)
    lane_sq = lax.broadcasted_iota(jnp.int32, (GLA_HEAD_V, LANES), 1) < GLA_HEAD_K

    def pick(x, hl):
        return jnp.where(hl, x, jnp.zeros_like(x))

    def stack_heads(x):
        return jnp.concatenate([pick(x, hl) for hl in head_lanes], axis=0)

    def finish(s, p, scores, qe, ke, b_last):
        sl = slice(p * LANES, (p + 1) * LANES)
        st = st_ref[s, p]
        inter = _dot_nt(stack_heads(qe[:, sl]), st.astype(BF16))
        upd = []
        for r in range(2):
            hd = 2 * p + r
            vh = v_ref[s, :, hd * GLA_HEAD_V:(hd + 1) * GLA_HEAD_V]
            o_ref[s, :, hd * GLA_HEAD_V:(hd + 1) * GLA_HEAD_V] = (
                inter[r * c:(r + 1) * c] + _dot(scores[r * c:(r + 1) * c], vh))
            upd.append(_dot_tn(vh, ke[:, sl]))
        st_ref[s, p] = st * jnp.exp(b_last[:, sl]) + jnp.where(lane_sq, upd[0], upd[1])

    la_all = jnp.concatenate([la_ref[s] for s in range(streams)], axis=1)
    b_all = _dot_hilo_left(mats_ref[:c, :], la_all)
    mild = jnp.min(b_all[c - 1:c, :]) >= GLA_DIRECT_MIN

    @pl.when(mild)
    def _():
        rowi = lax.broadcasted_iota(jnp.int32, (2 * c, c), 0) & (c - 1)
        causal = lax.broadcasted_iota(jnp.int32, (2 * c, c), 1) <= rowi
        for s in range(streams):
            b = b_all[:, s * GLA_KEY_WIDTH:(s + 1) * GLA_KEY_WIDTH]
            b_last = b[c - 1:c, :]
            q = q_ref[s]
            k = k_ref[s]
            qe = (q * jnp.exp(b)).astype(BF16)
            kd = (k * jnp.exp(-b)).astype(BF16)
            ke = (k * jnp.exp(b_last - b)).astype(BF16)
            for p in range(GLA_HEADS // 2):
                sl = slice(p * LANES, (p + 1) * LANES)
                scores = jnp.where(causal, _dot_nt(stack_heads(qe[:, sl]), kd[:, sl]), 0.0)
                finish(s, p, scores.astype(BF16), qe, ke, b_last)

    @pl.when(jnp.logical_not(mild))
    def _():
        lv = jnp.concatenate([lv_ref[...], lv_ref[...]], axis=0)
        for s in range(streams):
            q = q_ref[s]
            k = k_ref[s]
            sums = _dot_hilo_left(mats_ref[...], la_ref[s])

            def rows(i):
                return sums[i * c:(i + 1) * c]

            qe = (q * jnp.exp(rows(0))).astype(BF16)
            ke = (k * jnp.exp(rows(1))).astype(BF16)
            q_lev = [q.astype(BF16)] + [(q * jnp.exp(rows(1 + lev))).astype(BF16) for lev in range(1, nlev + 1)]
            k_lev = [k.astype(BF16)] + [(k * jnp.exp(rows(1 + nlev + lev))).astype(BF16)
                                        for lev in range(1, nlev + 1)]
            for p in range(GLA_HEADS // 2):
                sl = slice(p * LANES, (p + 1) * LANES)
                scores = jnp.zeros((2 * c, c), F32)
                for lev in range(nlev + 1):
                    scores = jnp.where(lv == lev, _dot_nt(stack_heads(q_lev[lev][:, sl]), k_lev[lev][:, sl]),
                                       scores)
                finish(s, p, scores.astype(BF16), qe, ke, rows(0)[c - 1:c, :])

    sfin_ref[...] = st_ref[...]


def _dot_hilo_left(m, x):
    hi, lo = _split_bf16(x)
    return _dot(m, hi) + _dot(m, lo)


def _gla(qg, kg, la, vg16, s0, mats, lv, *, batch, seq):
    c = GLA_CHUNK
    g = GLA_BATCH_PER_STEP
    pairs = GLA_HEADS // 2
    blk = lambda width: pl.BlockSpec((g, c, width), lambda b, t: (b, t, 0))
    st_spec = pl.BlockSpec((g, pairs, GLA_HEAD_V, LANES), lambda b, t: (b, 0, 0, 0))
    per_stream = lambda a: a.reshape(batch, seq, a.shape[-1])
    o, s_fin = pl.pallas_call(
        _gla_kernel,
        grid=(batch // g, seq // c),
        in_specs=[blk(GLA_KEY_WIDTH), blk(GLA_KEY_WIDTH), blk(GLA_KEY_WIDTH), blk(GLA_WIDTH), st_spec,
                  _const_spec(mats.shape), _const_spec(lv.shape)],
        out_specs=[blk(GLA_WIDTH), st_spec],
        out_shape=[jax.ShapeDtypeStruct((batch, seq, GLA_WIDTH), F32),
                   jax.ShapeDtypeStruct((batch, pairs, GLA_HEAD_V, LANES), F32)],
        scratch_shapes=[pltpu.VMEM((g, pairs, GLA_HEAD_V, LANES), F32)],
        compiler_params=_params(2),
        name="gla",
    )(per_stream(qg), per_stream(kg), per_stream(la), per_stream(vg16), s0, mats, lv)
    return o.reshape(batch * seq, GLA_WIDTH), s_fin


def _head_norm_lanes(y, head, gain):
    tm, width = y.shape
    low = lax.broadcasted_iota(jnp.int32, (tm, LANES), 1) < head
    cols = []
    for j in range(width // LANES):
        yj = y[:, j * LANES:(j + 1) * LANES]
        sq = yj * yj

        def inv_rms(part):
            return lax.rsqrt(jnp.sum(part, axis=1, keepdims=True) * (1.0 / head) + EPS)

        if head == LANES:
            cols.append(yj * inv_rms(sq))
        else:
            cols.append(yj * jnp.where(low, inv_rms(jnp.where(low, sq, 0.0)), inv_rms(jnp.where(low, 0.0, sq))))
    return jnp.concatenate(cols, axis=1) * gain


def _mixout_kernel(x_ref, osb_ref, og_ref, r_ref, gsb_ref, ggla_ref, w_ref,
                   g_ref, wg_ref, wu_ref, wd_ref, gfin_ref, o_ref, a_ref):
    o_sb = _head_norm_lanes(osb_ref[...], SB_HEAD_DIM, gsb_ref[...])
    r = r_ref[...]
    o_g = _head_norm_lanes(og_ref[...], GLA_HEAD_V, ggla_ref[...]) * (r * jax.nn.sigmoid(r))
    mix = _dot(o_sb.astype(BF16), w_ref[:SB_WIDTH, :]) + _dot(o_g.astype(BF16), w_ref[SB_WIDTH:, :])
    y = _half_step_ffn(x_ref[...] + mix, g_ref, wg_ref, wu_ref, wd_ref, a_ref)
    o_ref[...] = _rms(y, gfin_ref[...])


def _mixout(x, osb, og, r, gsb, ggla, w, g, wg, wu, wd, gfin, *, tm):
    n = x.shape[0]
    row = lambda i: (i, 0)
    return pl.pallas_call(
        _mixout_kernel,
        grid=(n // tm,),
        in_specs=[pl.BlockSpec((tm, D_MODEL), row),
                  pl.BlockSpec((tm, SB_WIDTH), row),
                  pl.BlockSpec((tm, GLA_WIDTH), row),
                  pl.BlockSpec((tm, GLA_WIDTH), row),
                  _const_spec((1, SB_WIDTH)),
                  _const_spec((1, GLA_WIDTH)),
                  _const_spec((D_MODEL, D_MODEL))] + _ffn_specs() + [_const_spec((1, D_MODEL))],
        out_specs=pl.BlockSpec((tm, D_MODEL), row),
        out_shape=jax.ShapeDtypeStruct((n, D_MODEL), F32),
        scratch_shapes=[pltpu.VMEM((tm, D_FF), BF16)],
        compiler_params=_params(1),
        name="mixer_out_ffn2",
    )(x, osb, og, r, gsb, ggla, w, g, wg, wu, wd, gfin)


def _group_ones(width, group):
    idx = np.arange(width) // group
    return jnp.asarray(idx[:, None] == idx[None, :], BF16)


def _suffix_matrices():
    j = np.arange(SB_BLOCK)[:, None]
    s = np.arange(SB_BLOCK)[None, :]
    later = j > s
    zero = np.zeros_like(later)
    pair = np.block([[later, zero], [zero, later]])
    single = np.concatenate([later, np.ones_like(later)], axis=1)
    twice = lambda m: jnp.asarray(np.concatenate([m, m], axis=0), BF16)
    return twice(pair), twice(single)


def _state_to_kernel(s):
    b = s.shape[0]
    s = s.reshape(b, GLA_HEADS // 2, 2, GLA_HEAD_K, GLA_HEAD_V)
    return s.transpose(0, 1, 4, 2, 3).reshape(b, GLA_HEADS // 2, GLA_HEAD_V, LANES)


def _state_from_kernel(s):
    b = s.shape[0]
    s = s.reshape(b, GLA_HEADS // 2, GLA_HEAD_V, 2, GLA_HEAD_K)
    return s.transpose(0, 1, 3, 4, 2).reshape(b, GLA_HEADS, GLA_HEAD_K, GLA_HEAD_V)


def _layer(x, lw, consts, *, batch, seq, tm, cache=None, state=None):
    (g1, w1g, w1u, w1d, g_mix, w_in, wgu, bg, gq, gk, gsb, ggla, w_out, g2, w2g, w2u, w2d, gfin) = lw
    grp64, upair, uo, mats, lv = consts
    x1 = _ffn(x, g1, w1g, w1u, w1d, tm=tm)
    (qsb16, ksb, ksb16, vsb, vsb16, qg, kg, vg16, r, la) = _mixin(x1, g_mix, w_in, wgu, bg, gq, gk, grp64, tm=tm)

    if cache is None:
        o_sb = _sb(qsb16, ksb16, vsb16, upair, uo, batch=batch, tq=SB_BLOCK, keys=seq, q_block=None)
    else:
        past = cache[0].shape[0] // batch
        keys = past + SB_BLOCK

        def with_cache(old, new):
            new = jnp.pad(new.reshape(batch, seq, SB_WIDTH), ((0, 0), (0, SB_BLOCK - seq), (0, 0)))
            return jnp.concatenate([old.reshape(batch, past, SB_WIDTH), new], axis=1).reshape(batch * keys, SB_WIDTH)

        o_sb = _sb(qsb16, with_cache(cache[0], ksb16), with_cache(cache[1], vsb16), upair, uo,
                   batch=batch, tq=seq, keys=keys, q_block=past // SB_BLOCK)

    c = GLA_CHUNK
    seq_pad = -(-seq // c) * c
    if seq_pad != seq:
        padt = lambda a: jnp.pad(a.reshape(batch, seq, a.shape[-1]),
                                 ((0, 0), (0, seq_pad - seq), (0, 0))).reshape(batch * seq_pad, a.shape[-1])
        qg_p, kg_p, la_p, vg_p = padt(qg), padt(kg), padt(la), padt(vg16)
    else:
        qg_p, kg_p, la_p, vg_p = qg, kg, la, vg16
    if state is None:
        state = jnp.zeros((batch, GLA_HEADS // 2, GLA_HEAD_V, LANES), F32)
    o_g, s_fin = _gla(qg_p, kg_p, la_p, vg_p, state, mats, lv, batch=batch, seq=seq_pad)
    if seq_pad != seq:
        o_g = o_g.reshape(batch, seq_pad, GLA_WIDTH)[:, :seq].reshape(batch * seq, GLA_WIDTH)

    y = _mixout(x1, o_sb, o_g, r, gsb, ggla, w_out, g2, w2g, w2u, w2d, gfin, tm=tm)
    return y, ksb, vsb, _state_from_kernel(s_fin)


def kernel(x_prompt, x_sample, cache_sb_k, cache_sb_v, state_gla, g_ffn1, w_ffn1_gate, w_ffn1_up,
           w_ffn1_down, g_mix, w_in, w_gate_up, b_gate, g_q, g_k, g_sb_out, g_gla_out, w_out,
           g_ffn2, w_ffn2_gate, w_ffn2_up, w_ffn2_down, g_final):
    depth = w_in.shape[0]
    batch, seq, _ = x_prompt.shape
    dec_batch, dec_seq, _ = x_sample.shape
    past = cache_sb_k.shape[2]
    consts = (_group_ones(SB_WIDTH, SB_HEAD_DIM),
              *_suffix_matrices(), *_gla_tables(GLA_CHUNK))

    y_p = x_prompt.reshape(batch * seq, D_MODEL)
    y_s = x_sample.reshape(dec_batch * dec_seq, D_MODEL)
    outs = [[] for _ in range(6)]
    for l in range(depth):
        row = lambda a: a[l].reshape(1, -1)
        w_main = w_in[l][:, :MAIN_WIDTH]
        w_lr = jnp.pad(w_in[l][:, MAIN_WIDTH:], ((0, 0), (0, GATE_PAD - GATE_RANK)))
        lw = (row(g_ffn1), w_ffn1_gate[l].astype(BF16), w_ffn1_up[l].astype(BF16), w_ffn1_down[l].astype(BF16),
              row(g_mix), jnp.concatenate([w_main, w_lr], axis=1).astype(BF16),
              jnp.pad(w_gate_up[l], ((0, GATE_PAD - GATE_RANK), (0, 0))).astype(BF16), row(b_gate),
              jnp.tile(g_q[l], SB_HEADS).reshape(1, -1), jnp.tile(g_k[l], SB_HEADS).reshape(1, -1),
              jnp.tile(g_sb_out[l], SB_HEADS).reshape(1, -1), jnp.tile(g_gla_out[l], GLA_HEADS).reshape(1, -1),
              w_out[l].astype(BF16), row(g_ffn2), w_ffn2_gate[l].astype(BF16), w_ffn2_up[l].astype(BF16),
              w_ffn2_down[l].astype(BF16), row(g_final))
        y_p, k_p, v_p, s_p = _layer(y_p, lw, consts, batch=batch, seq=seq, tm=512)
        cache = (cache_sb_k[l].reshape(dec_batch * past, SB_WIDTH).astype(BF16),
                 cache_sb_v[l].reshape(dec_batch * past, SB_WIDTH).astype(BF16))
        y_s, k_s, v_s, s_s = _layer(y_s, lw, consts, batch=dec_batch, seq=dec_seq, tm=256,
                                    cache=cache, state=_state_to_kernel(state_gla[l]))
        shape_p = (batch, seq, SB_HEADS, SB_HEAD_DIM)
        shape_s = (dec_batch, dec_seq, SB_HEADS, SB_HEAD_DIM)
        for lst, val in zip(outs, (k_p.reshape(shape_p), v_p.reshape(shape_p), s_p,
                                   k_s.reshape(shape_s), v_s.reshape(shape_s), s_s)):
            lst.append(val)
    return (y_p.reshape(batch, seq, D_MODEL), y_s.reshape(dec_batch, dec_seq, D_MODEL),
            *[jnp.stack(o) for o in outs])
```

```python
import functools

import numpy as np
import jax
import jax.numpy as jnp
from jax import lax
from jax.experimental import pallas as pl
from jax.experimental.pallas import tpu as pltpu

F32 = jnp.float32
BF16 = jnp.bfloat16

D_MODEL = 1024
SB_HEADS = 8
SB_HEAD_DIM = 64
SB_WIDTH = SB_HEADS * SB_HEAD_DIM
SB_SCALE = SB_HEAD_DIM ** -0.5
GLA_HEADS = 4
GLA_HEAD_K = 64
GLA_HEAD_V = 128
GLA_KEY_WIDTH = GLA_HEADS * GLA_HEAD_K
GLA_WIDTH = GLA_HEADS * GLA_HEAD_V
GLA_SCALE = GLA_HEAD_K ** -0.5
GATE_RANK = 16
GATE_TAU = 16.0
D_FF = 2816
EPS = 1e-6

LANES = 128
SUBLANES = 8
BF16_SUBLANES = 16
GATE_PAD = LANES
MAIN_WIDTH = 3 * SB_WIDTH + 2 * GLA_KEY_WIDTH + 2 * GLA_WIDTH
IN_PAD = MAIN_WIDTH + GATE_PAD

FF_CHUNK = 256
SB_BLOCK = 128
SB_WINDOW = 3
SB_PAIRS_PER_STEP = 4
GLA_CHUNK = 128
GLA_BATCH_PER_STEP = 2
GLA_DIRECT_MIN = -40.0
SB_STOP = 152.0
LOG2_E = 1.4426950408889634

VMEM_LIMIT = 56 * 1024 * 1024


def _dot(a, b):
    return jnp.dot(a, b, preferred_element_type=F32)


def _dot_nt(a, b):
    return lax.dot_general(a, b, (((1,), (1,)), ((), ())), preferred_element_type=F32)


def _dot_tn(a, b):
    return lax.dot_general(a, b, (((0,), (0,)), ((), ())), preferred_element_type=F32)


def _split_bf16(x):
    hi = x.astype(BF16)
    lo = (x - hi.astype(F32)).astype(BF16)
    return hi, lo


def _dot_hilo(x, m):
    hi, lo = _split_bf16(x)
    return _dot(hi, m) + _dot(lo, m)


def _rms(x, g):
    ms = jnp.mean(x * x, axis=-1, keepdims=True)
    return x * lax.rsqrt(ms + EPS) * g


def _log_sigmoid(x):
    return jnp.minimum(x, 0.0) - jnp.log1p(jnp.exp(-jnp.abs(x)))


def _const_spec(shape):
    nd = len(shape)
    return pl.BlockSpec(shape, lambda *_: (0,) * nd, pipeline_mode=pl.Buffered(1))


def _params(n_grid):
    return pltpu.CompilerParams(dimension_semantics=("arbitrary",) * n_grid,
                                vmem_limit_bytes=VMEM_LIMIT)


def _half_step_ffn(x, g_ref, wg_ref, wu_ref, wd_ref, a_ref):
    h = _rms(x, g_ref[...]).astype(BF16)
    for c in range(D_FF // FF_CHUNK):
        sl = slice(c * FF_CHUNK, (c + 1) * FF_CHUNK)
        gt = _dot(h, wg_ref[:, sl])
        up = _dot(h, wu_ref[:, sl])
        a_ref[:, sl] = (gt * jax.nn.sigmoid(gt) * up).astype(BF16)
    return x + 0.5 * _dot(a_ref[...], wd_ref[...])


def _ffn_kernel(x_ref, g_ref, wg_ref, wu_ref, wd_ref, *refs):
    n_cast = (len(refs) - 2) // 2
    o_ref, a_ref = refs[n_cast], refs[-1]
    o_ref[...] = _half_step_ffn(x_ref[...], g_ref, wg_ref, wu_ref, wd_ref, a_ref)
    for src, dst in zip(refs[:n_cast], refs[n_cast + 1:-1]):
        keep = min(src.shape[1], dst.shape[1]) // LANES * LANES
        dst[:, :keep] = src[:, :keep].astype(BF16)
        if keep < dst.shape[1]:
            dst[:, keep:] = jnp.zeros((dst.shape[0], dst.shape[1] - keep), BF16)
            dst[:, keep:src.shape[1]] = src[:, keep:].astype(BF16)


def _ffn_specs():
    return [_const_spec((1, D_MODEL)), _const_spec((D_MODEL, D_FF)), _const_spec((D_MODEL, D_FF)),
            _const_spec((D_FF, D_MODEL))]


def _cast_block_rows(rows, steps):
    br = next(b for b in range(BF16_SUBLANES, rows + 1, BF16_SUBLANES) if rows % b == 0 and rows // b <= steps)
    return br, rows // br


def _ffn(x, g, wg, wu, wd, *, tm, cast=()):
    n = x.shape[0]
    steps = n // tm
    row = lambda i: (i, 0)
    cast_in, cast_out, cast_shape = [], [], []
    for w, width in cast:
        br, nblk = _cast_block_rows(w.shape[0], steps)
        idx = lambda i, last=nblk - 1: (jnp.minimum(i, last), 0)
        cast_in.append(pl.BlockSpec((br, w.shape[1]), idx))
        cast_out.append(pl.BlockSpec((br, width), idx))
        cast_shape.append(jax.ShapeDtypeStruct((w.shape[0], width), BF16))
    out = pl.pallas_call(
        _ffn_kernel,
        grid=(steps,),
        in_specs=[pl.BlockSpec((tm, D_MODEL), row)] + _ffn_specs() + cast_in,
        out_specs=[pl.BlockSpec((tm, D_MODEL), row)] + cast_out,
        out_shape=[jax.ShapeDtypeStruct((n, D_MODEL), F32)] + cast_shape,
        scratch_shapes=[pltpu.VMEM((tm, D_FF), BF16)],
        compiler_params=_params(1),
        name="ffn1",
    )(x, g, wg, wu, wd, *[w for w, _ in cast])
    return out[0], out[1:]


def _store_heads(ref, y):
    groups = y.shape[0] // SUBLANES
    sub = lax.broadcasted_iota(jnp.int32, (groups, SUBLANES, LANES), 1)
    rows = []
    for j in range(SB_WIDTH // LANES):
        pair = y[:, j * LANES:(j + 1) * LANES].reshape(groups, SUBLANES, LANES)
        rows += [pair, pltpu.roll(pair, SB_HEAD_DIM, axis=2)]
    for dist in (4, 2, 1):
        low = (sub & dist) == 0
        new = list(rows)
        for h in range(SUBLANES):
            if h & dist == 0:
                a, b = rows[h], rows[h + dist]
                new[h] = jnp.where(low, a, pltpu.roll(b, dist, axis=1))
                new[h + dist] = jnp.where(low, pltpu.roll(a, SUBLANES - dist, axis=1), b)
        rows = new
    for t in range(SUBLANES):
        ref[:, t] = rows[t][:, :, :SB_HEAD_DIM]


def _mixin_kernel(x_ref, g_ref, w_ref, wgu_ref, bg_ref, gq_ref, gk_ref, grp_ref,
                  qsb_ref, ksb_ref, ksb16_ref, vsb_ref, vsb16_ref,
                  qg_ref, kg_ref, vg16_ref, r_ref, la_ref):
    h = _rms(x_ref[...], g_ref[...]).astype(BF16)
    grp = grp_ref[...]

    def proj(lo, width):
        return _dot(h, w_ref[:, lo:lo + width])

    def head_norm(y, gain):
        ms = _dot_hilo(y * y, grp) * (1.0 / SB_HEAD_DIM)
        return y * lax.rsqrt(ms + EPS) * gain

    q = head_norm(proj(0, SB_WIDTH), gq_ref[...])
    qsb_ref[...] = (q * SB_SCALE).astype(BF16)
    k = head_norm(proj(SB_WIDTH, SB_WIDTH), gk_ref[...])
    _store_heads(ksb_ref, k)
    ksb16_ref[...] = k.astype(BF16)
    v = proj(2 * SB_WIDTH, SB_WIDTH)
    _store_heads(vsb_ref, v)
    vsb16_ref[...] = v.astype(BF16)
    off = 3 * SB_WIDTH
    qg_ref[...] = proj(off, GLA_KEY_WIDTH) * GLA_SCALE
    kg_ref[...] = proj(off + GLA_KEY_WIDTH, GLA_KEY_WIDTH)
    vg16_ref[...] = proj(off + 2 * GLA_KEY_WIDTH, GLA_WIDTH).astype(BF16)
    r_ref[...] = proj(off + 2 * GLA_KEY_WIDTH + GLA_WIDTH, GLA_WIDTH)
    lr = proj(MAIN_WIDTH, GATE_PAD).astype(BF16)
    gate = _dot(lr, wgu_ref[...]) + bg_ref[...]
    la_ref[...] = _log_sigmoid(gate) * (1.0 / GATE_TAU)


def _mixin(x, g, w, wgu, bg, gq, gk, grp, *, tm):
    n = x.shape[0]
    row = lambda i: (i, 0)
    heads = (SB_HEADS, SB_HEAD_DIM)
    assert SB_HEADS == SUBLANES and tm % SUBLANES == 0
    lead = lambda rows, wd: (rows // SUBLANES, SUBLANES) if len(wd) == 2 else (rows,)
    widths = [((SB_WIDTH,), BF16), (heads, F32), ((SB_WIDTH,), BF16), (heads, F32), ((SB_WIDTH,), BF16),
              ((GLA_KEY_WIDTH,), F32), ((GLA_KEY_WIDTH,), F32), ((GLA_WIDTH,), BF16), ((GLA_WIDTH,), F32),
              ((GLA_KEY_WIDTH,), F32)]
    return pl.pallas_call(
        _mixin_kernel,
        grid=(n // tm,),
        in_specs=[pl.BlockSpec((tm, D_MODEL), row),
                  _const_spec((1, D_MODEL)),
                  _const_spec((D_MODEL, IN_PAD)),
                  _const_spec((GATE_PAD, GLA_KEY_WIDTH)),
                  _const_spec((1, GLA_KEY_WIDTH)),
                  _const_spec((1, SB_WIDTH)),
                  _const_spec((1, SB_WIDTH)),
                  _const_spec((SB_WIDTH, SB_WIDTH))],
        out_specs=[pl.BlockSpec(lead(tm, wd) + wd, lambda i, nd=len(lead(tm, wd) + wd) - 1: (i,) + (0,) * nd)
                   for wd, _ in widths],
        out_shape=[jax.ShapeDtypeStruct(lead(n, wd) + wd, dt) for wd, dt in widths],
        compiler_params=_params(1),
        name="mixer_in",
    )(x, g, w, wgu, bg, gq, gk, grp)


def _neg_abs(x):
    return -jnp.abs(x)


def _sb_log2_terms(z):
    z2 = z * LOG2_E
    t = jnp.log2(1.0 + jnp.exp2(_neg_abs(z2)))
    return jnp.maximum(z2, 0.0) + t, jnp.minimum(z2, 0.0) - t


def _sb_kernel(q_ref, k_ref, v_ref, upair_ref, uo_ref, o_ref, carry_ref, acc_ref, live_ref, *, q_block):
    tq = q_ref.shape[0]
    pairs = q_ref.shape[1] // LANES
    heads = 2 * pairs
    i = pl.program_id(2) if q_block is None else q_block
    lane = lax.broadcasted_iota(jnp.int32, (tq, LANES), 1)
    head_lanes = (lane < SB_HEAD_DIM, lane >= SB_HEAD_DIM)
    rowi = lax.broadcasted_iota(jnp.int32, (heads * tq, SB_BLOCK), 0)
    coli = lax.broadcasted_iota(jnp.int32, (heads * tq, SB_BLOCK), 1)
    causal = coli < (rowi & (tq - 1))

    def pair_lanes(p):
        return slice(p * LANES, (p + 1) * LANES)

    def q_pair(p):
        q2 = q_ref[:, pair_lanes(p)]
        zero = jnp.zeros_like(q2)
        return jnp.concatenate([jnp.where(hl, q2, zero) for hl in head_lanes], axis=0)

    def rows(ref, p, first_block, nblocks):
        start = first_block * SB_BLOCK
        if not isinstance(start, int):
            start = pl.multiple_of(start, SB_BLOCK)
        return ref[pl.ds(start, nblocks * SB_BLOCK), pair_lanes(p)]

    def suffix_sums(x):
        nb = x.shape[1] // SB_BLOCK
        hi, lo = _split_bf16(x)

        def mm(cols, m2):
            return _dot(jnp.concatenate([hi[:, cols], lo[:, cols]], axis=1), m2)

        within, total = [], []
        if nb >= 3:
            for j in range(0, nb - 1, 2):
                y = mm(slice(j * SB_BLOCK, (j + 2) * SB_BLOCK), upair_ref[...])
                within += [y[:, :SB_BLOCK], y[:, SB_BLOCK:]]
                total += [jnp.sum(x[:, b * SB_BLOCK:(b + 1) * SB_BLOCK], axis=1, keepdims=True) for b in (j, j + 1)]
        for j in range(len(within), nb):
            y = mm(slice(j * SB_BLOCK, (j + 1) * SB_BLOCK), uo_ref[...])
            within.append(y[:, :SB_BLOCK])
            total.append(y[:, SB_BLOCK:])
        later = total[nb - 1]
        out = [within[nb - 1]]
        for j in range(nb - 2, -1, -1):
            out.insert(0, within[j] + later)
            later = later + total[j]
        return (out[0] if nb == 1 else jnp.concatenate(out, axis=1)), later

    def sweep(first_block, nblocks, own, carry, valid=None):
        nk = nblocks * SB_BLOCK

        def masked(x):
            if valid is not None:
                return jnp.where(valid, x, 0.0)
            if not own:
                return x
            tail = jnp.where(causal, x[:, nk - SB_BLOCK:], 0.0)
            return tail if nblocks == 1 else jnp.concatenate([x[:, :nk - SB_BLOCK], tail], axis=1)

        z = jnp.concatenate([_dot_nt(q_pair(p), rows(k_ref, p, first_block, nblocks))
                             for p in range(pairs)], axis=0)
        fail, log_beta = _sb_log2_terms(z)
        after, total = suffix_sums(masked(fail))
        if carry is not None:
            after = after + carry
        w = masked(jnp.exp2(log_beta - after)).astype(BF16)
        pv = jnp.concatenate([_dot(w[2 * p * tq:(2 * p + 2) * tq], rows(v_ref, p, first_block, nblocks))
                              for p in range(pairs)], axis=0)
        if carry is None:
            carry_ref[...] = total
            acc_ref[...] = pv
        else:
            carry_ref[...] = carry + total
            acc_ref[...] += pv

    def live():
        return jnp.min(carry_ref[...]) < SB_STOP

    def mark_live():
        live_ref[0] = jnp.where(live(), 1, 0)

    half = SB_BLOCK // 2

    def by_halves(x, f):
        y = [f(s, jnp.concatenate([x[s * half:(s + 1) * half], x[tq + s * half:tq + (s + 1) * half]], axis=0))
             for s in range(2)]
        return jnp.concatenate([y[0][:half], y[1][:half], y[0][half:], y[1][half:]], axis=0)

    def half_windows():
        starts = [pl.multiple_of(i * SB_BLOCK - (3 - s) * half, half) for s in range(2)]
        nk = 2 * SB_BLOCK
        keep = coli < (rowi & (half - 1)) + half

        def masked(x):
            return jnp.concatenate([x[:, :SB_BLOCK], jnp.where(keep, x[:, SB_BLOCK:], 0.0)], axis=1)

        z = jnp.concatenate(
            [by_halves(q_pair(p), lambda s, lhs: _dot_nt(lhs, k_ref[pl.ds(starts[s], nk), pair_lanes(p)]))
             for p in range(pairs)], axis=0)
        fail, log_beta = _sb_log2_terms(z)
        after, total = suffix_sums(masked(fail))
        w = masked(jnp.exp2(log_beta - after)).astype(BF16)
        carry_ref[...] = total
        acc_ref[...] = jnp.concatenate(
            [by_halves(w[2 * p * tq:(2 * p + 2) * tq],
                       lambda s, lhs: _dot(lhs, v_ref[pl.ds(starts[s], nk), pair_lanes(p)]))
             for p in range(pairs)], axis=0)

    def own_block_only():
        sweep(i, 1, True, None)

    def full_window():
        sweep(i - (SB_WINDOW - 1), SB_WINDOW, True, None)

    if q_block is None:
        assert tq == SB_BLOCK
        windowed = i >= 2
        pl.when(windowed)(half_windows)
        pl.when(jnp.logical_not(windowed))(own_block_only)

        mark_live()

        @pl.when(jnp.logical_and(windowed, live_ref[0] != 0))
        def _():
            unseen = jnp.logical_or(coli < half, (rowi & half) != 0)
            sweep(i - 2, 1, False, carry_ref[...], unseen)
            mark_live()

        n_prev = jnp.where(windowed, i - 2, i)
    else:
        assert q_block >= SB_WINDOW - 1
        full_window()
        mark_live()
        n_prev = q_block - (SB_WINDOW - 1)

    def body(st):
        kb, _ = st
        sweep(kb, 1, False, carry_ref[...])
        return kb - 1, live()

    lax.while_loop(lambda st: jnp.logical_and(st[0] >= 0, st[1]), body, (n_prev - 1, live_ref[0] != 0))
    for p in range(pairs):
        o_ref[:, pair_lanes(p)] = jnp.where(head_lanes[0], acc_ref[2 * p * tq:(2 * p + 1) * tq],
                                            acc_ref[(2 * p + 1) * tq:(2 * p + 2) * tq])


def _sb(q16, k16, v16, upair, uo, *, batch, tq, keys, q_block):
    nq = keys // SB_BLOCK if q_block is None else 1
    width = SB_PAIRS_PER_STEP * LANES
    kv_spec = pl.BlockSpec((keys, width), lambda b, p, i: (b, p))
    q_spec = pl.BlockSpec((tq, width), lambda b, p, i: (b * nq + i, p))
    heads = 2 * SB_PAIRS_PER_STEP
    return pl.pallas_call(
        functools.partial(_sb_kernel, q_block=q_block),
        grid=(batch, SB_WIDTH // width, nq),
        in_specs=[q_spec, kv_spec, kv_spec, _const_spec(upair.shape), _const_spec(uo.shape)],
        out_specs=q_spec,
        out_shape=jax.ShapeDtypeStruct((batch * nq * tq, SB_WIDTH), F32),
        scratch_shapes=[pltpu.VMEM((heads * tq, SB_BLOCK), F32),
                        pltpu.VMEM((heads * tq, LANES), F32),
                        pltpu.SMEM((1,), jnp.int32)],
        compiler_params=_params(3),
        name="sb_prompt" if q_block is None else "sb_sample",
    )(q16, k16, v16, upair, uo)


def _gla_levels(c):
    return int(np.log2(c))


def _gla_tables(c):
    nlev = _gla_levels(c)
    t = np.arange(c)[:, None]
    j = np.arange(c)[None, :]
    mats = [(j <= t), (j > t)]
    for lev in range(1, nlev + 1):
        m = 1 << (lev - 1)
        start = (t // (2 * m)) * (2 * m)
        mats.append(((t % (2 * m)) >= m) & (j >= start + m) & (j <= t))
    for lev in range(1, nlev + 1):
        m = 1 << (lev - 1)
        start = (t // (2 * m)) * (2 * m)
        mats.append(((t % (2 * m)) < m) & (j > t) & (j <= start + m - 1))
    mats = np.concatenate([a.astype(np.float32) for a in mats], axis=0)
    x = t ^ j
    level = np.where(j > t, -1, np.where(x == 0, 0, np.floor(np.log2(np.maximum(x, 1))).astype(np.int64) + 1))
    return jnp.asarray(mats, BF16), jnp.asarray(level, jnp.int32)


def _gla_kernel(q_ref, k_ref, la_ref, v_ref, s0_ref, mats_ref, lv_ref, o_ref, sfin_ref, st_ref):
    c = GLA_CHUNK
    nlev = _gla_levels(c)
    streams = q_ref.shape[0]
    step = pl.program_id(1)

    @pl.when(step == 0)
    def _():
        st_ref[...] = s0_ref[...]

    lane = lax.broadcasted_iota(jnp.int32, (c, LANES), 1)
    head_lanes = (lane < GLA_HEAD_K, lane >= GLA_HEAD_K)
    lane_sq = lax.broadcasted_iota(jnp.int32, (GLA_HEAD_V, LANES), 1) < GLA_HEAD_K

    def pick(x, hl):
        return jnp.where(hl, x, jnp.zeros_like(x))

    def stack_heads(x):
        return jnp.concatenate([pick(x, hl) for hl in head_lanes], axis=0)

    def finish(s, p, scores, qe, ke, b_last):
        sl = slice(p * LANES, (p + 1) * LANES)
        st = st_ref[s, p]
        inter = _dot_nt(stack_heads(qe[:, sl]), st.astype(BF16))
        upd = []
        for r in range(2):
            hd = 2 * p + r
            vh = v_ref[s, :, hd * GLA_HEAD_V:(hd + 1) * GLA_HEAD_V]
            o_ref[s, :, hd * GLA_HEAD_V:(hd + 1) * GLA_HEAD_V] = (
                inter[r * c:(r + 1) * c] + _dot(scores[r * c:(r + 1) * c], vh))
            upd.append(_dot_tn(vh, ke[:, sl]))
        st_ref[s, p] = st * jnp.exp(b_last[:, sl]) + jnp.where(lane_sq, upd[0], upd[1])

    la_all = jnp.concatenate([la_ref[s] for s in range(streams)], axis=1)
    b_all = _dot_hilo_left(mats_ref[:c, :], la_all)
    mild = jnp.min(b_all[c - 1:c, :]) >= GLA_DIRECT_MIN

    @pl.when(mild)
    def _():
        rowi = lax.broadcasted_iota(jnp.int32, (2 * c, c), 0) & (c - 1)
        causal = lax.broadcasted_iota(jnp.int32, (2 * c, c), 1) <= rowi
        for s in range(streams):
            b = b_all[:, s * GLA_KEY_WIDTH:(s + 1) * GLA_KEY_WIDTH]
            b_last = b[c - 1:c, :]
            q = q_ref[s]
            k = k_ref[s]
            qe = (q * jnp.exp(b)).astype(BF16)
            kd = (k * jnp.exp(-b)).astype(BF16)
            ke = (k * jnp.exp(b_last - b)).astype(BF16)
            for p in range(GLA_HEADS // 2):
                sl = slice(p * LANES, (p + 1) * LANES)
                scores = jnp.where(causal, _dot_nt(stack_heads(qe[:, sl]), kd[:, sl]), 0.0)
                finish(s, p, scores.astype(BF16), qe, ke, b_last)

    @pl.when(jnp.logical_not(mild))
    def _():
        lv = jnp.concatenate([lv_ref[...], lv_ref[...]], axis=0)
        for s in range(streams):
            q = q_ref[s]
            k = k_ref[s]
            sums = _dot_hilo_left(mats_ref[...], la_ref[s])

            def rows(i):
                return sums[i * c:(i + 1) * c]

            qe = (q * jnp.exp(rows(0))).astype(BF16)
            ke = (k * jnp.exp(rows(1))).astype(BF16)
            q_lev = [q.astype(BF16)] + [(q * jnp.exp(rows(1 + lev))).astype(BF16) for lev in range(1, nlev + 1)]
            k_lev = [k.astype(BF16)] + [(k * jnp.exp(rows(1 + nlev + lev))).astype(BF16)
                                        for lev in range(1, nlev + 1)]
            for p in range(GLA_HEADS // 2):
                sl = slice(p * LANES, (p + 1) * LANES)
                scores = jnp.zeros((2 * c, c), F32)
                for lev in range(nlev + 1):
                    scores = jnp.where(lv == lev, _dot_nt(stack_heads(q_lev[lev][:, sl]), k_lev[lev][:, sl]),
                                       scores)
                finish(s, p, scores.astype(BF16), qe, ke, rows(0)[c - 1:c, :])

    sfin_ref[...] = st_ref[...]


def _dot_hilo_left(m, x):
    hi, lo = _split_bf16(x)
    return _dot(m, hi) + _dot(m, lo)


def _gla(qg, kg, la, vg16, s0, mats, lv, *, batch, seq):
    c = GLA_CHUNK
    g = GLA_BATCH_PER_STEP
    pairs = GLA_HEADS // 2
    blk = lambda width: pl.BlockSpec((g, c, width), lambda b, t: (b, t, 0))
    st_spec = pl.BlockSpec((g, pairs, GLA_HEAD_V, LANES), lambda b, t: (b, 0, 0, 0))
    per_stream = lambda a: a.reshape(batch, seq, a.shape[-1])
    o, s_fin = pl.pallas_call(
        _gla_kernel,
        grid=(batch // g, seq // c),
        in_specs=[blk(GLA_KEY_WIDTH), blk(GLA_KEY_WIDTH), blk(GLA_KEY_WIDTH), blk(GLA_WIDTH), st_spec,
                  _const_spec(mats.shape), _const_spec(lv.shape)],
        out_specs=[blk(GLA_WIDTH), st_spec],
        out_shape=[jax.ShapeDtypeStruct((batch, seq, GLA_WIDTH), F32),
                   jax.ShapeDtypeStruct((batch, pairs, GLA_HEAD_V, LANES), F32)],
        scratch_shapes=[pltpu.VMEM((g, pairs, GLA_HEAD_V, LANES), F32)],
        compiler_params=_params(2),
        name="gla",
    )(per_stream(qg), per_stream(kg), per_stream(la), per_stream(vg16), s0, mats, lv)
    return o.reshape(batch * seq, GLA_WIDTH), s_fin


def _head_norm_lanes(y, head, gain):
    tm, width = y.shape
    low = lax.broadcasted_iota(jnp.int32, (tm, LANES), 1) < head
    cols = []
    for j in range(width // LANES):
        yj = y[:, j * LANES:(j + 1) * LANES]
        sq = yj * yj

        def inv_rms(part):
            return lax.rsqrt(jnp.sum(part, axis=1, keepdims=True) * (1.0 / head) + EPS)

        if head == LANES:
            cols.append(yj * inv_rms(sq))
        else:
            cols.append(yj * jnp.where(low, inv_rms(jnp.where(low, sq, 0.0)), inv_rms(jnp.where(low, 0.0, sq))))
    return jnp.concatenate(cols, axis=1) * gain


def _mixout_kernel(x_ref, osb_ref, og_ref, r_ref, gsb_ref, ggla_ref, w_ref,
                   g_ref, wg_ref, wu_ref, wd_ref, gfin_ref, o_ref, a_ref):
    o_sb = _head_norm_lanes(osb_ref[...], SB_HEAD_DIM, gsb_ref[...])
    r = r_ref[...]
    o_g = _head_norm_lanes(og_ref[...], GLA_HEAD_V, ggla_ref[...]) * (r * jax.nn.sigmoid(r))
    mix = _dot(o_sb.astype(BF16), w_ref[:SB_WIDTH, :]) + _dot(o_g.astype(BF16), w_ref[SB_WIDTH:, :])
    y = _half_step_ffn(x_ref[...] + mix, g_ref, wg_ref, wu_ref, wd_ref, a_ref)
    o_ref[...] = _rms(y, gfin_ref[...])


def _mixout(x, osb, og, r, gsb, ggla, w, g, wg, wu, wd, gfin, *, tm):
    n = x.shape[0]
    row = lambda i: (i, 0)
    return pl.pallas_call(
        _mixout_kernel,
        grid=(n // tm,),
        in_specs=[pl.BlockSpec((tm, D_MODEL), row),
                  pl.BlockSpec((tm, SB_WIDTH), row),
                  pl.BlockSpec((tm, GLA_WIDTH), row),
                  pl.BlockSpec((tm, GLA_WIDTH), row),
                  _const_spec((1, SB_WIDTH)),
                  _const_spec((1, GLA_WIDTH)),
                  _const_spec((D_MODEL, D_MODEL))] + _ffn_specs() + [_const_spec((1, D_MODEL))],
        out_specs=pl.BlockSpec((tm, D_MODEL), row),
        out_shape=jax.ShapeDtypeStruct((n, D_MODEL), F32),
        scratch_shapes=[pltpu.VMEM((tm, D_FF), BF16)],
        compiler_params=_params(1),
        name="mixer_out_ffn2",
    )(x, osb, og, r, gsb, ggla, w, g, wg, wu, wd, gfin)


def _group_ones(width, group):
    idx = np.arange(width) // group
    return jnp.asarray(idx[:, None] == idx[None, :], BF16)


def _suffix_matrices():
    j = np.arange(SB_BLOCK)[:, None]
    s = np.arange(SB_BLOCK)[None, :]
    later = j > s
    zero = np.zeros_like(later)
    pair = np.block([[later, zero], [zero, later]])
    single = np.concatenate([later, np.ones_like(later)], axis=1)
    twice = lambda m: jnp.asarray(np.concatenate([m, m], axis=0), BF16)
    return twice(pair), twice(single)


def _state_to_kernel(s):
    b = s.shape[0]
    s = s.reshape(b, GLA_HEADS // 2, 2, GLA_HEAD_K, GLA_HEAD_V)
    return s.transpose(0, 1, 4, 2, 3).reshape(b, GLA_HEADS // 2, GLA_HEAD_V, LANES)


def _state_from_kernel(s):
    b = s.shape[0]
    s = s.reshape(b, GLA_HEADS // 2, GLA_HEAD_V, 2, GLA_HEAD_K)
    return s.transpose(0, 1, 3, 4, 2).reshape(b, GLA_HEADS, GLA_HEAD_K, GLA_HEAD_V)


def _layer(x1, lw, consts, *, batch, seq, tm, cache=None, state=None):
    (g_mix, w_in, wgu, bg, gq, gk, gsb, ggla, w_out, g2, w2g, w2u, w2d, gfin) = lw
    grp64, upair, uo, mats, lv = consts
    (qsb16, ksb, ksb16, vsb, vsb16, qg, kg, vg16, r, la) = _mixin(x1, g_mix, w_in, wgu, bg, gq, gk, grp64, tm=tm)

    if cache is None:
        o_sb = _sb(qsb16, ksb16, vsb16, upair, uo, batch=batch, tq=SB_BLOCK, keys=seq, q_block=None)
    else:
        past = cache[0].shape[0] // batch
        keys = past + SB_BLOCK

        def with_cache(old, new):
            new = jnp.pad(new.reshape(batch, seq, SB_WIDTH), ((0, 0), (0, SB_BLOCK - seq), (0, 0)))
            return jnp.concatenate([old.reshape(batch, past, SB_WIDTH), new], axis=1).reshape(batch * keys, SB_WIDTH)

        o_sb = _sb(qsb16, with_cache(cache[0], ksb16), with_cache(cache[1], vsb16), upair, uo,
                   batch=batch, tq=seq, keys=keys, q_block=past // SB_BLOCK)

    c = GLA_CHUNK
    seq_pad = -(-seq // c) * c
    if seq_pad != seq:
        padt = lambda a: jnp.pad(a.reshape(batch, seq, a.shape[-1]),
                                 ((0, 0), (0, seq_pad - seq), (0, 0))).reshape(batch * seq_pad, a.shape[-1])
        qg_p, kg_p, la_p, vg_p = padt(qg), padt(kg), padt(la), padt(vg16)
    else:
        qg_p, kg_p, la_p, vg_p = qg, kg, la, vg16
    if state is None:
        state = jnp.zeros((batch, GLA_HEADS // 2, GLA_HEAD_V, LANES), F32)
    o_g, s_fin = _gla(qg_p, kg_p, la_p, vg_p, state, mats, lv, batch=batch, seq=seq_pad)
    if seq_pad != seq:
        o_g = o_g.reshape(batch, seq_pad, GLA_WIDTH)[:, :seq].reshape(batch * seq, GLA_WIDTH)

    y = _mixout(x1, o_sb, o_g, r, gsb, ggla, w_out, g2, w2g, w2u, w2d, gfin, tm=tm)
    return y, ksb, vsb, _state_from_kernel(s_fin)


def kernel(x_prompt, x_sample, cache_sb_k, cache_sb_v, state_gla, g_ffn1, w_ffn1_gate, w_ffn1_up,
           w_ffn1_down, g_mix, w_in, w_gate_up, b_gate, g_q, g_k, g_sb_out, g_gla_out, w_out,
           g_ffn2, w_ffn2_gate, w_ffn2_up, w_ffn2_down, g_final):
    depth = w_in.shape[0]
    batch, seq, _ = x_prompt.shape
    dec_batch, dec_seq, _ = x_sample.shape
    past = cache_sb_k.shape[2]
    consts = (_group_ones(SB_WIDTH, SB_HEAD_DIM),
              *_suffix_matrices(), *_gla_tables(GLA_CHUNK))

    y_p = x_prompt.reshape(batch * seq, D_MODEL)
    y_s = x_sample.reshape(dec_batch * dec_seq, D_MODEL)
    outs = [[] for _ in range(6)]
    for l in range(depth):
        row = lambda a: a[l].reshape(1, -1)
        ffn1 = (row(g_ffn1), w_ffn1_gate[l].astype(BF16), w_ffn1_up[l].astype(BF16), w_ffn1_down[l].astype(BF16))
        x1_p, (w2g, w2u, w2d, w_in16, w_out16) = _ffn(
            y_p, *ffn1, tm=512,
            cast=[(w_ffn2_gate[l], D_FF), (w_ffn2_up[l], D_FF), (w_ffn2_down[l], D_MODEL),
                  (w_in[l], IN_PAD), (w_out[l], D_MODEL)])
        x1_s, _ = _ffn(y_s, *ffn1, tm=256)
        lw = (row(g_mix), w_in16,
              jnp.pad(w_gate_up[l], ((0, GATE_PAD - GATE_RANK), (0, 0))).astype(BF16), row(b_gate),
              jnp.tile(g_q[l], SB_HEADS).reshape(1, -1), jnp.tile(g_k[l], SB_HEADS).reshape(1, -1),
              jnp.tile(g_sb_out[l], SB_HEADS).reshape(1, -1), jnp.tile(g_gla_out[l], GLA_HEADS).reshape(1, -1),
              w_out16, row(g_ffn2), w2g, w2u, w2d, row(g_final))
        y_p, k_p, v_p, s_p = _layer(x1_p, lw, consts, batch=batch, seq=seq, tm=512)
        cache = (cache_sb_k[l].reshape(dec_batch * past, SB_WIDTH).astype(BF16),
                 cache_sb_v[l].reshape(dec_batch * past, SB_WIDTH).astype(BF16))
        y_s, k_s, v_s, s_s = _layer(x1_s, lw, consts, batch=dec_batch, seq=dec_seq, tm=256,
                                    cache=cache, state=_state_to_kernel(state_gla[l]))
        shape_p = (batch, seq, SB_HEADS, SB_HEAD_DIM)
        shape_s = (dec_batch, dec_seq, SB_HEADS, SB_HEAD_DIM)
        for lst, val in zip(outs, (k_p.reshape(shape_p), v_p.reshape(shape_p), s_p,
                                   k_s.reshape(shape_s), v_s.reshape(shape_s), s_s)):
            lst.append(val)
    return (y_p.reshape(batch, seq, D_MODEL), y_s.reshape(dec_batch, dec_seq, D_MODEL),
            *[jnp.stack(o) for o in outs])
```

```python
import functools

import numpy as np
import jax
import jax.numpy as jnp
from jax import lax
from jax.experimental import pallas as pl
from jax.experimental.pallas import tpu as pltpu

F32 = jnp.float32
BF16 = jnp.bfloat16

D_MODEL = 1024
SB_HEADS = 8
SB_HEAD_DIM = 64
SB_WIDTH = SB_HEADS * SB_HEAD_DIM
SB_SCALE = SB_HEAD_DIM ** -0.5
GLA_HEADS = 4
GLA_HEAD_K = 64
GLA_HEAD_V = 128
GLA_KEY_WIDTH = GLA_HEADS * GLA_HEAD_K
GLA_WIDTH = GLA_HEADS * GLA_HEAD_V
GLA_SCALE = GLA_HEAD_K ** -0.5
GATE_RANK = 16
GATE_TAU = 16.0
D_FF = 2816
EPS = 1e-6

LANES = 128
SUBLANES = 8
BF16_SUBLANES = 16
GATE_PAD = LANES
MAIN_WIDTH = 3 * SB_WIDTH + 2 * GLA_KEY_WIDTH + 2 * GLA_WIDTH
IN_PAD = MAIN_WIDTH + GATE_PAD

FF_CHUNK = 256
SB_BLOCK = 128
SB_WINDOW = 3
SB_PAIRS_PER_STEP = 4
GLA_CHUNK = 128
GLA_BATCH_PER_STEP = 2
GLA_DIRECT_MIN = -40.0
SB_STOP = 152.0
LOG2_E = 1.4426950408889634

VMEM_LIMIT = 56 * 1024 * 1024


def _dot(a, b):
    return jnp.dot(a, b, preferred_element_type=F32)


def _dot_nt(a, b):
    return lax.dot_general(a, b, (((1,), (1,)), ((), ())), preferred_element_type=F32)


def _dot_tn(a, b):
    return lax.dot_general(a, b, (((0,), (0,)), ((), ())), preferred_element_type=F32)


def _split_bf16(x):
    hi = x.astype(BF16)
    lo = (x - hi.astype(F32)).astype(BF16)
    return hi, lo


def _dot_hilo(x, m):
    hi, lo = _split_bf16(x)
    return _dot(hi, m) + _dot(lo, m)


def _rms(x, g):
    ms = jnp.mean(x * x, axis=-1, keepdims=True)
    return x * lax.rsqrt(ms + EPS) * g


def _log_sigmoid(x):
    return jnp.minimum(x, 0.0) - jnp.log1p(jnp.exp(-jnp.abs(x)))


def _const_spec(shape):
    nd = len(shape)
    return pl.BlockSpec(shape, lambda *_: (0,) * nd, pipeline_mode=pl.Buffered(1))


def _params(n_grid):
    return pltpu.CompilerParams(dimension_semantics=("arbitrary",) * n_grid,
                                vmem_limit_bytes=VMEM_LIMIT)


def _half_step_ffn(x, g_ref, wg_ref, wu_ref, wd_ref, a_ref):
    h = _rms(x, g_ref[...]).astype(BF16)
    for c in range(D_FF // FF_CHUNK):
        sl = slice(c * FF_CHUNK, (c + 1) * FF_CHUNK)
        gt = _dot(h, wg_ref[:, sl])
        up = _dot(h, wu_ref[:, sl])
        a_ref[:, sl] = (gt * jax.nn.sigmoid(gt) * up).astype(BF16)
    return x + 0.5 * _dot(a_ref[...], wd_ref[...])


def _ffn_kernel(x_ref, g_ref, wg_ref, wu_ref, wd_ref, *refs):
    n_cast = (len(refs) - 2) // 2
    o_ref, a_ref = refs[n_cast], refs[-1]
    o_ref[...] = _half_step_ffn(x_ref[...], g_ref, wg_ref, wu_ref, wd_ref, a_ref)
    for src, dst in zip(refs[:n_cast], refs[n_cast + 1:-1]):
        keep = min(src.shape[1], dst.shape[1]) // LANES * LANES
        dst[:, :keep] = src[:, :keep].astype(BF16)
        if keep < dst.shape[1]:
            dst[:, keep:] = jnp.zeros((dst.shape[0], dst.shape[1] - keep), BF16)
            dst[:, keep:src.shape[1]] = src[:, keep:].astype(BF16)


def _ffn_specs():
    return [_const_spec((1, D_MODEL)), _const_spec((D_MODEL, D_FF)), _const_spec((D_MODEL, D_FF)),
            _const_spec((D_FF, D_MODEL))]


def _cast_block_rows(rows, steps):
    br = next(b for b in range(BF16_SUBLANES, rows + 1, BF16_SUBLANES) if rows % b == 0 and rows // b <= steps)
    return br, rows // br


def _ffn(x, g, wg, wu, wd, *, tm, cast=()):
    n = x.shape[0]
    steps = n // tm
    row = lambda i: (i, 0)
    cast_in, cast_out, cast_shape = [], [], []
    for w, width in cast:
        br, nblk = _cast_block_rows(w.shape[0], steps)
        idx = lambda i, last=nblk - 1: (jnp.minimum(i, last), 0)
        cast_in.append(pl.BlockSpec((br, w.shape[1]), idx))
        cast_out.append(pl.BlockSpec((br, width), idx))
        cast_shape.append(jax.ShapeDtypeStruct((w.shape[0], width), BF16))
    out = pl.pallas_call(
        _ffn_kernel,
        grid=(steps,),
        in_specs=[pl.BlockSpec((tm, D_MODEL), row)] + _ffn_specs() + cast_in,
        out_specs=[pl.BlockSpec((tm, D_MODEL), row)] + cast_out,
        out_shape=[jax.ShapeDtypeStruct((n, D_MODEL), F32)] + cast_shape,
        scratch_shapes=[pltpu.VMEM((tm, D_FF), BF16)],
        compiler_params=_params(1),
        name="ffn1",
    )(x, g, wg, wu, wd, *[w for w, _ in cast])
    return out[0], out[1:]


def _store_heads(ref, y):
    groups = y.shape[0] // SUBLANES
    sub = lax.broadcasted_iota(jnp.int32, (groups, SUBLANES, LANES), 1)
    rows = []
    for j in range(SB_WIDTH // LANES):
        pair = y[:, j * LANES:(j + 1) * LANES].reshape(groups, SUBLANES, LANES)
        rows += [pair, pltpu.roll(pair, SB_HEAD_DIM, axis=2)]
    for dist in (4, 2, 1):
        low = (sub & dist) == 0
        new = list(rows)
        for h in range(SUBLANES):
            if h & dist == 0:
                a, b = rows[h], rows[h + dist]
                new[h] = jnp.where(low, a, pltpu.roll(b, dist, axis=1))
                new[h + dist] = jnp.where(low, pltpu.roll(a, SUBLANES - dist, axis=1), b)
        rows = new
    for t in range(SUBLANES):
        ref[:, t] = rows[t][:, :, :SB_HEAD_DIM]


def _mixin_kernel(x_ref, g_ref, w_ref, wgu_ref, bg_ref, gq_ref, gk_ref, grp_ref,
                  qsb_ref, ksb_ref, ksb16_ref, vsb_ref, vsb16_ref,
                  qg_ref, kg_ref, vg16_ref, r_ref, la_ref):
    h = _rms(x_ref[...], g_ref[...]).astype(BF16)
    grp = grp_ref[...]

    def proj(lo, width):
        return _dot(h, w_ref[:, lo:lo + width])

    def head_norm(y, gain):
        ms = _dot_hilo(y * y, grp) * (1.0 / SB_HEAD_DIM)
        return y * lax.rsqrt(ms + EPS) * gain

    q = head_norm(proj(0, SB_WIDTH), gq_ref[...])
    qsb_ref[...] = (q * SB_SCALE).astype(BF16)
    k = head_norm(proj(SB_WIDTH, SB_WIDTH), gk_ref[...])
    _store_heads(ksb_ref, k)
    ksb16_ref[...] = k.astype(BF16)
    v = proj(2 * SB_WIDTH, SB_WIDTH)
    _store_heads(vsb_ref, v)
    vsb16_ref[...] = v.astype(BF16)
    off = 3 * SB_WIDTH
    qg_ref[...] = proj(off, GLA_KEY_WIDTH) * GLA_SCALE
    kg_ref[...] = proj(off + GLA_KEY_WIDTH, GLA_KEY_WIDTH)
    vg16_ref[...] = proj(off + 2 * GLA_KEY_WIDTH, GLA_WIDTH).astype(BF16)
    r_ref[...] = proj(off + 2 * GLA_KEY_WIDTH + GLA_WIDTH, GLA_WIDTH)
    lr = proj(MAIN_WIDTH, GATE_PAD).astype(BF16)
    gate = _dot(lr, wgu_ref[...]) + bg_ref[...]
    la_ref[...] = _log_sigmoid(gate) * (1.0 / GATE_TAU)


def _mixin(x, g, w, wgu, bg, gq, gk, grp, *, tm):
    n = x.shape[0]
    row = lambda i: (i, 0)
    heads = (SB_HEADS, SB_HEAD_DIM)
    assert SB_HEADS == SUBLANES and tm % SUBLANES == 0
    lead = lambda rows, wd: (rows // SUBLANES, SUBLANES) if len(wd) == 2 else (rows,)
    widths = [((SB_WIDTH,), BF16), (heads, F32), ((SB_WIDTH,), BF16), (heads, F32), ((SB_WIDTH,), BF16),
              ((GLA_KEY_WIDTH,), F32), ((GLA_KEY_WIDTH,), F32), ((GLA_WIDTH,), BF16), ((GLA_WIDTH,), F32),
              ((GLA_KEY_WIDTH,), F32)]
    return pl.pallas_call(
        _mixin_kernel,
        grid=(n // tm,),
        in_specs=[pl.BlockSpec((tm, D_MODEL), row),
                  _const_spec((1, D_MODEL)),
                  _const_spec((D_MODEL, IN_PAD)),
                  _const_spec((GATE_PAD, GLA_KEY_WIDTH)),
                  _const_spec((1, GLA_KEY_WIDTH)),
                  _const_spec((1, SB_WIDTH)),
                  _const_spec((1, SB_WIDTH)),
                  _const_spec((SB_WIDTH, SB_WIDTH))],
        out_specs=[pl.BlockSpec(lead(tm, wd) + wd, lambda i, nd=len(lead(tm, wd) + wd) - 1: (i,) + (0,) * nd)
                   for wd, _ in widths],
        out_shape=[jax.ShapeDtypeStruct(lead(n, wd) + wd, dt) for wd, dt in widths],
        compiler_params=_params(1),
        name="mixer_in",
    )(x, g, w, wgu, bg, gq, gk, grp)


def _neg_abs(x):
    return -jnp.abs(x)


def _sb_log2_terms(z):
    z2 = z * LOG2_E
    t = jnp.log2(1.0 + jnp.exp2(_neg_abs(z2)))
    return jnp.maximum(z2, 0.0) + t, jnp.minimum(z2, 0.0) - t


def _sb_kernel(q_ref, k_ref, v_ref, upair_ref, uo_ref, *refs, q_block):
    if q_block is None:
        o_ref, carry_ref, acc_ref, live_ref = refs
    else:
        kc_hbm, vc_hbm, o_ref, carry_ref, acc_ref, live_ref, kbuf, vbuf, sems = refs
    tq = q_ref.shape[0]
    pairs = q_ref.shape[1] // LANES
    heads = 2 * pairs
    i = pl.program_id(2) if q_block is None else q_block
    lane = lax.broadcasted_iota(jnp.int32, (tq, LANES), 1)
    head_lanes = (lane < SB_HEAD_DIM, lane >= SB_HEAD_DIM)
    rowi = lax.broadcasted_iota(jnp.int32, (heads * tq, SB_BLOCK), 0)
    coli = lax.broadcasted_iota(jnp.int32, (heads * tq, SB_BLOCK), 1)
    causal = coli < (rowi & (tq - 1))

    def pair_lanes(p):
        return slice(p * LANES, (p + 1) * LANES)

    def q_pair(p):
        q2 = q_ref[:, pair_lanes(p)]
        zero = jnp.zeros_like(q2)
        return jnp.concatenate([jnp.where(hl, q2, zero) for hl in head_lanes], axis=0)

    def rows(ref, p, first_block, nblocks):
        start = first_block * SB_BLOCK
        if not isinstance(start, int):
            start = pl.multiple_of(start, SB_BLOCK)
        return ref[pl.ds(start, nblocks * SB_BLOCK), pair_lanes(p)]

    def suffix_sums(x):
        nb = x.shape[1] // SB_BLOCK
        hi, lo = _split_bf16(x)

        def mm(cols, m2):
            return _dot(jnp.concatenate([hi[:, cols], lo[:, cols]], axis=1), m2)

        within, total = [], []
        if nb >= 3:
            for j in range(0, nb - 1, 2):
                y = mm(slice(j * SB_BLOCK, (j + 2) * SB_BLOCK), upair_ref[...])
                within += [y[:, :SB_BLOCK], y[:, SB_BLOCK:]]
                total += [jnp.sum(x[:, b * SB_BLOCK:(b + 1) * SB_BLOCK], axis=1, keepdims=True) for b in (j, j + 1)]
        for j in range(len(within), nb):
            y = mm(slice(j * SB_BLOCK, (j + 1) * SB_BLOCK), uo_ref[...])
            within.append(y[:, :SB_BLOCK])
            total.append(y[:, SB_BLOCK:])
        later = total[nb - 1]
        out = [within[nb - 1]]
        for j in range(nb - 2, -1, -1):
            out.insert(0, within[j] + later)
            later = later + total[j]
        return (out[0] if nb == 1 else jnp.concatenate(out, axis=1)), later

    def resident(first_block, nblocks):
        return (lambda p: rows(k_ref, p, first_block, nblocks), lambda p: rows(v_ref, p, first_block, nblocks))

    def sweep(kv, nblocks, own, carry, valid=None):
        nk = nblocks * SB_BLOCK

        def masked(x):
            if valid is not None:
                return jnp.where(valid, x, 0.0)
            if not own:
                return x
            tail = jnp.where(causal, x[:, nk - SB_BLOCK:], 0.0)
            return tail if nblocks == 1 else jnp.concatenate([x[:, :nk - SB_BLOCK], tail], axis=1)

        z = jnp.concatenate([_dot_nt(q_pair(p), kv[0](p)) for p in range(pairs)], axis=0)
        fail, log_beta = _sb_log2_terms(z)
        after, total = suffix_sums(masked(fail))
        if carry is not None:
            after = after + carry
        w = masked(jnp.exp2(log_beta - after)).astype(BF16)
        pv = jnp.concatenate([_dot(w[2 * p * tq:(2 * p + 2) * tq], kv[1](p))
                              for p in range(pairs)], axis=0)
        if carry is None:
            carry_ref[...] = total
            acc_ref[...] = pv
        else:
            carry_ref[...] = carry + total
            acc_ref[...] += pv

    def live():
        return jnp.min(carry_ref[...]) < SB_STOP

    def mark_live():
        live_ref[0] = jnp.where(live(), 1, 0)

    half = SB_BLOCK // 2

    def by_halves(x, f):
        y = [f(s, jnp.concatenate([x[s * half:(s + 1) * half], x[tq + s * half:tq + (s + 1) * half]], axis=0))
             for s in range(2)]
        return jnp.concatenate([y[0][:half], y[1][:half], y[0][half:], y[1][half:]], axis=0)

    def half_windows():
        starts = [pl.multiple_of(i * SB_BLOCK - (3 - s) * half, half) for s in range(2)]
        nk = 2 * SB_BLOCK
        keep = coli < (rowi & (half - 1)) + half

        def masked(x):
            return jnp.concatenate([x[:, :SB_BLOCK], jnp.where(keep, x[:, SB_BLOCK:], 0.0)], axis=1)

        z = jnp.concatenate(
            [by_halves(q_pair(p), lambda s, lhs: _dot_nt(lhs, k_ref[pl.ds(starts[s], nk), pair_lanes(p)]))
             for p in range(pairs)], axis=0)
        fail, log_beta = _sb_log2_terms(z)
        after, total = suffix_sums(masked(fail))
        w = masked(jnp.exp2(log_beta - after)).astype(BF16)
        carry_ref[...] = total
        acc_ref[...] = jnp.concatenate(
            [by_halves(w[2 * p * tq:(2 * p + 2) * tq],
                       lambda s, lhs: _dot(lhs, v_ref[pl.ds(starts[s], nk), pair_lanes(p)]))
             for p in range(pairs)], axis=0)

    def own_block_only():
        sweep(resident(i, 1), 1, True, None)

    if q_block is None:
        assert tq == SB_BLOCK
        windowed = i >= 2
        pl.when(windowed)(half_windows)
        pl.when(jnp.logical_not(windowed))(own_block_only)

        mark_live()

        @pl.when(jnp.logical_and(windowed, live_ref[0] != 0))
        def _():
            unseen = jnp.logical_or(coli < half, (rowi & half) != 0)
            sweep(resident(i - 2, 1), 1, False, carry_ref[...], unseen)
            mark_live()

        n_prev = jnp.where(windowed, i - 2, i)
        earlier = lambda kb: resident(kb, 1)
    else:
        assert q_block >= SB_WINDOW - 1 and k_ref.shape[0] == SB_WINDOW * SB_BLOCK
        sweep(resident(0, SB_WINDOW), SB_WINDOW, True, None)
        mark_live()
        n_prev = q_block - (SB_WINDOW - 1)

        def earlier(kb):
            src = lambda hbm: hbm.at[pl.program_id(0), pl.ds(pl.multiple_of(kb * SB_BLOCK, SB_BLOCK), SB_BLOCK)]
            copies = [pltpu.make_async_copy(src(hbm), buf, sems.at[n])
                      for n, (hbm, buf) in enumerate(((kc_hbm, kbuf), (vc_hbm, vbuf)))]
            for c in copies:
                c.start()
            for c in copies:
                c.wait()
            slab = lambda buf: (lambda p: jnp.concatenate([buf[:, 2 * p, :], buf[:, 2 * p + 1, :]],
                                                          axis=1).astype(BF16))
            return slab(kbuf), slab(vbuf)

    def body(st):
        kb, _ = st
        sweep(earlier(kb), 1, False, carry_ref[...])
        return kb - 1, live()

    lax.while_loop(lambda st: jnp.logical_and(st[0] >= 0, st[1]), body, (n_prev - 1, live_ref[0] != 0))
    for p in range(pairs):
        o_ref[:, pair_lanes(p)] = jnp.where(head_lanes[0], acc_ref[2 * p * tq:(2 * p + 1) * tq],
                                            acc_ref[(2 * p + 1) * tq:(2 * p + 2) * tq])


def _sb(q16, k16, v16, upair, uo, *, batch, tq, keys, cache=None):
    nq = keys // SB_BLOCK if cache is None else 1
    width = SB_PAIRS_PER_STEP * LANES
    assert width == SB_WIDTH or cache is None
    kv_spec = pl.BlockSpec((keys, width), lambda b, p, i: (b, p))
    q_spec = pl.BlockSpec((tq, width), lambda b, p, i: (b * nq + i, p))
    heads = 2 * SB_PAIRS_PER_STEP
    in_specs = [q_spec, kv_spec, kv_spec, _const_spec(upair.shape), _const_spec(uo.shape)]
    scratch = [pltpu.VMEM((heads * tq, SB_BLOCK), F32), pltpu.VMEM((heads * tq, LANES), F32),
               pltpu.SMEM((1,), jnp.int32)]
    operands = [q16, k16, v16, upair, uo]
    q_block = None
    if cache is not None:
        past = cache[0].shape[1]
        assert past % SB_BLOCK == 0 and keys == SB_WINDOW * SB_BLOCK <= past + SB_BLOCK
        q_block = past // SB_BLOCK
        in_specs += [pl.BlockSpec(memory_space=pl.ANY)] * 2
        scratch += [pltpu.VMEM((SB_BLOCK, SB_HEADS, SB_HEAD_DIM), F32)] * 2 + [pltpu.SemaphoreType.DMA((2,))]
        operands += list(cache)
    return pl.pallas_call(
        functools.partial(_sb_kernel, q_block=q_block),
        grid=(batch, SB_WIDTH // width, nq),
        in_specs=in_specs,
        out_specs=q_spec,
        out_shape=jax.ShapeDtypeStruct((batch * nq * tq, SB_WIDTH), F32),
        scratch_shapes=scratch,
        compiler_params=_params(3),
        name="sb_prompt" if cache is None else "sb_sample",
    )(*operands)


def _gla_levels(c):
    return int(np.log2(c))


def _gla_tables(c):
    nlev = _gla_levels(c)
    t = np.arange(c)[:, None]
    j = np.arange(c)[None, :]
    mats = [(j <= t), (j > t)]
    for lev in range(1, nlev + 1):
        m = 1 << (lev - 1)
        start = (t // (2 * m)) * (2 * m)
        mats.append(((t % (2 * m)) >= m) & (j >= start + m) & (j <= t))
    for lev in range(1, nlev + 1):
        m = 1 << (lev - 1)
        start = (t // (2 * m)) * (2 * m)
        mats.append(((t % (2 * m)) < m) & (j > t) & (j <= start + m - 1))
    mats = np.concatenate([a.astype(np.float32) for a in mats], axis=0)
    x = t ^ j
    level = np.where(j > t, -1, np.where(x == 0, 0, np.floor(np.log2(np.maximum(x, 1))).astype(np.int64) + 1))
    return jnp.asarray(mats, BF16), jnp.asarray(level, jnp.int32)


def _gla_kernel(q_ref, k_ref, la_ref, v_ref, s0_ref, mats_ref, lv_ref, o_ref, sfin_ref, st_ref):
    c = GLA_CHUNK
    nlev = _gla_levels(c)
    streams = q_ref.shape[0]
    step = pl.program_id(1)

    @pl.when(step == 0)
    def _():
        st_ref[...] = s0_ref[...]

    lane = lax.broadcasted_iota(jnp.int32, (c, LANES), 1)
    head_lanes = (lane < GLA_HEAD_K, lane >= GLA_HEAD_K)
    lane_sq = lax.broadcasted_iota(jnp.int32, (GLA_HEAD_V, LANES), 1) < GLA_HEAD_K

    def pick(x, hl):
        return jnp.where(hl, x, jnp.zeros_like(x))

    def stack_heads(x):
        return jnp.concatenate([pick(x, hl) for hl in head_lanes], axis=0)

    def finish(s, p, scores, qe, ke, b_last):
        sl = slice(p * LANES, (p + 1) * LANES)
        st = st_ref[s, p]
        inter = _dot_nt(stack_heads(qe[:, sl]), st.astype(BF16))
        upd = []
        for r in range(2):
            hd = 2 * p + r
            vh = v_ref[s, :, hd * GLA_HEAD_V:(hd + 1) * GLA_HEAD_V]
            o_ref[s, :, hd * GLA_HEAD_V:(hd + 1) * GLA_HEAD_V] = (
                inter[r * c:(r + 1) * c] + _dot(scores[r * c:(r + 1) * c], vh))
            upd.append(_dot_tn(vh, ke[:, sl]))
        st_ref[s, p] = st * jnp.exp(b_last[:, sl]) + jnp.where(lane_sq, upd[0], upd[1])

    la_all = jnp.concatenate([la_ref[s] for s in range(streams)], axis=1)
    b_all = _dot_hilo_left(mats_ref[:c, :], la_all)
    mild = jnp.min(b_all[c - 1:c, :]) >= GLA_DIRECT_MIN

    @pl.when(mild)
    def _():
        rowi = lax.broadcasted_iota(jnp.int32, (2 * c, c), 0) & (c - 1)
        causal = lax.broadcasted_iota(jnp.int32, (2 * c, c), 1) <= rowi
        for s in range(streams):
            b = b_all[:, s * GLA_KEY_WIDTH:(s + 1) * GLA_KEY_WIDTH]
            b_last = b[c - 1:c, :]
            q = q_ref[s]
            k = k_ref[s]
            qe = (q * jnp.exp(b)).astype(BF16)
            kd = (k * jnp.exp(-b)).astype(BF16)
            ke = (k * jnp.exp(b_last - b)).astype(BF16)
            for p in range(GLA_HEADS // 2):
                sl = slice(p * LANES, (p + 1) * LANES)
                scores = jnp.where(causal, _dot_nt(stack_heads(qe[:, sl]), kd[:, sl]), 0.0)
                finish(s, p, scores.astype(BF16), qe, ke, b_last)

    @pl.when(jnp.logical_not(mild))
    def _():
        lv = jnp.concatenate([lv_ref[...], lv_ref[...]], axis=0)
        for s in range(streams):
            q = q_ref[s]
            k = k_ref[s]
            sums = _dot_hilo_left(mats_ref[...], la_ref[s])

            def rows(i):
                return sums[i * c:(i + 1) * c]

            qe = (q * jnp.exp(rows(0))).astype(BF16)
            ke = (k * jnp.exp(rows(1))).astype(BF16)
            q_lev = [q.astype(BF16)] + [(q * jnp.exp(rows(1 + lev))).astype(BF16) for lev in range(1, nlev + 1)]
            k_lev = [k.astype(BF16)] + [(k * jnp.exp(rows(1 + nlev + lev))).astype(BF16)
                                        for lev in range(1, nlev + 1)]
            for p in range(GLA_HEADS // 2):
                sl = slice(p * LANES, (p + 1) * LANES)
                scores = jnp.zeros((2 * c, c), F32)
                for lev in range(nlev + 1):
                    scores = jnp.where(lv == lev, _dot_nt(stack_heads(q_lev[lev][:, sl]), k_lev[lev][:, sl]),
                                       scores)
                finish(s, p, scores.astype(BF16), qe, ke, rows(0)[c - 1:c, :])

    sfin_ref[...] = st_ref[...]


def _dot_hilo_left(m, x):
    hi, lo = _split_bf16(x)
    return _dot(m, hi) + _dot(m, lo)


def _gla(qg, kg, la, vg16, s0, mats, lv, *, batch, seq):
    c = GLA_CHUNK
    g = GLA_BATCH_PER_STEP
    pairs = GLA_HEADS // 2
    blk = lambda width: pl.BlockSpec((g, c, width), lambda b, t: (b, t, 0))
    st_spec = pl.BlockSpec((g, pairs, GLA_HEAD_V, LANES), lambda b, t: (b, 0, 0, 0))
    per_stream = lambda a: a.reshape(batch, seq, a.shape[-1])
    o, s_fin = pl.pallas_call(
        _gla_kernel,
        grid=(batch // g, seq // c),
        in_specs=[blk(GLA_KEY_WIDTH), blk(GLA_KEY_WIDTH), blk(GLA_KEY_WIDTH), blk(GLA_WIDTH), st_spec,
                  _const_spec(mats.shape), _const_spec(lv.shape)],
        out_specs=[blk(GLA_WIDTH), st_spec],
        out_shape=[jax.ShapeDtypeStruct((batch, seq, GLA_WIDTH), F32),
                   jax.ShapeDtypeStruct((batch, pairs, GLA_HEAD_V, LANES), F32)],
        scratch_shapes=[pltpu.VMEM((g, pairs, GLA_HEAD_V, LANES), F32)],
        compiler_params=_params(2),
        name="gla",
    )(per_stream(qg), per_stream(kg), per_stream(la), per_stream(vg16), s0, mats, lv)
    return o.reshape(batch * seq, GLA_WIDTH), s_fin


def _head_norm_lanes(y, head, gain):
    tm, width = y.shape
    low = lax.broadcasted_iota(jnp.int32, (tm, LANES), 1) < head
    cols = []
    for j in range(width // LANES):
        yj = y[:, j * LANES:(j + 1) * LANES]
        sq = yj * yj

        def inv_rms(part):
            return lax.rsqrt(jnp.sum(part, axis=1, keepdims=True) * (1.0 / head) + EPS)

        if head == LANES:
            cols.append(yj * inv_rms(sq))
        else:
            cols.append(yj * jnp.where(low, inv_rms(jnp.where(low, sq, 0.0)), inv_rms(jnp.where(low, 0.0, sq))))
    return jnp.concatenate(cols, axis=1) * gain


def _mixout_kernel(x_ref, osb_ref, og_ref, r_ref, gsb_ref, ggla_ref, w_ref,
                   g_ref, wg_ref, wu_ref, wd_ref, gfin_ref, o_ref, a_ref):
    o_sb = _head_norm_lanes(osb_ref[...], SB_HEAD_DIM, gsb_ref[...])
    r = r_ref[...]
    o_g = _head_norm_lanes(og_ref[...], GLA_HEAD_V, ggla_ref[...]) * (r * jax.nn.sigmoid(r))
    mix = _dot(o_sb.astype(BF16), w_ref[:SB_WIDTH, :]) + _dot(o_g.astype(BF16), w_ref[SB_WIDTH:, :])
    y = _half_step_ffn(x_ref[...] + mix, g_ref, wg_ref, wu_ref, wd_ref, a_ref)
    o_ref[...] = _rms(y, gfin_ref[...])


def _mixout(x, osb, og, r, gsb, ggla, w, g, wg, wu, wd, gfin, *, tm):
    n = x.shape[0]
    row = lambda i: (i, 0)
    return pl.pallas_call(
        _mixout_kernel,
        grid=(n // tm,),
        in_specs=[pl.BlockSpec((tm, D_MODEL), row),
                  pl.BlockSpec((tm, SB_WIDTH), row),
                  pl.BlockSpec((tm, GLA_WIDTH), row),
                  pl.BlockSpec((tm, GLA_WIDTH), row),
                  _const_spec((1, SB_WIDTH)),
                  _const_spec((1, GLA_WIDTH)),
                  _const_spec((D_MODEL, D_MODEL))] + _ffn_specs() + [_const_spec((1, D_MODEL))],
        out_specs=pl.BlockSpec((tm, D_MODEL), row),
        out_shape=jax.ShapeDtypeStruct((n, D_MODEL), F32),
        scratch_shapes=[pltpu.VMEM((tm, D_FF), BF16)],
        compiler_params=_params(1),
        name="mixer_out_ffn2",
    )(x, osb, og, r, gsb, ggla, w, g, wg, wu, wd, gfin)


def _group_ones(width, group):
    idx = np.arange(width) // group
    return jnp.asarray(idx[:, None] == idx[None, :], BF16)


def _suffix_matrices():
    j = np.arange(SB_BLOCK)[:, None]
    s = np.arange(SB_BLOCK)[None, :]
    later = j > s
    zero = np.zeros_like(later)
    pair = np.block([[later, zero], [zero, later]])
    single = np.concatenate([later, np.ones_like(later)], axis=1)
    twice = lambda m: jnp.asarray(np.concatenate([m, m], axis=0), BF16)
    return twice(pair), twice(single)


def _state_to_kernel(s):
    b = s.shape[0]
    s = s.reshape(b, GLA_HEADS // 2, 2, GLA_HEAD_K, GLA_HEAD_V)
    return s.transpose(0, 1, 4, 2, 3).reshape(b, GLA_HEADS // 2, GLA_HEAD_V, LANES)


def _state_from_kernel(s):
    b = s.shape[0]
    s = s.reshape(b, GLA_HEADS // 2, GLA_HEAD_V, 2, GLA_HEAD_K)
    return s.transpose(0, 1, 3, 4, 2).reshape(b, GLA_HEADS, GLA_HEAD_K, GLA_HEAD_V)


def _layer(x1, lw, consts, *, batch, seq, tm, cache=None, state=None):
    (g_mix, w_in, wgu, bg, gq, gk, gsb, ggla, w_out, g2, w2g, w2u, w2d, gfin) = lw
    grp64, upair, uo, mats, lv = consts
    (qsb16, ksb, ksb16, vsb, vsb16, qg, kg, vg16, r, la) = _mixin(x1, g_mix, w_in, wgu, bg, gq, gk, grp64, tm=tm)

    if cache is None:
        o_sb = _sb(qsb16, ksb16, vsb16, upair, uo, batch=batch, tq=SB_BLOCK, keys=seq)
    else:
        recent = (SB_WINDOW - 1) * SB_BLOCK

        def window(old, new):
            old = old[:, old.shape[1] - recent:].reshape(batch, recent, SB_WIDTH).astype(BF16)
            new = jnp.pad(new.reshape(batch, seq, SB_WIDTH), ((0, 0), (0, SB_BLOCK - seq), (0, 0)))
            return jnp.concatenate([old, new], axis=1).reshape(batch * SB_WINDOW * SB_BLOCK, SB_WIDTH)

        o_sb = _sb(qsb16, window(cache[0], ksb16), window(cache[1], vsb16), upair, uo,
                   batch=batch, tq=seq, keys=SB_WINDOW * SB_BLOCK, cache=cache)

    c = GLA_CHUNK
    seq_pad = -(-seq // c) * c
    if seq_pad != seq:
        padt = lambda a: jnp.pad(a.reshape(batch, seq, a.shape[-1]),
                                 ((0, 0), (0, seq_pad - seq), (0, 0))).reshape(batch * seq_pad, a.shape[-1])
        qg_p, kg_p, la_p, vg_p = padt(qg), padt(kg), padt(la), padt(vg16)
    else:
        qg_p, kg_p, la_p, vg_p = qg, kg, la, vg16
    if state is None:
        state = jnp.zeros((batch, GLA_HEADS // 2, GLA_HEAD_V, LANES), F32)
    o_g, s_fin = _gla(qg_p, kg_p, la_p, vg_p, state, mats, lv, batch=batch, seq=seq_pad)
    if seq_pad != seq:
        o_g = o_g.reshape(batch, seq_pad, GLA_WIDTH)[:, :seq].reshape(batch * seq, GLA_WIDTH)

    y = _mixout(x1, o_sb, o_g, r, gsb, ggla, w_out, g2, w2g, w2u, w2d, gfin, tm=tm)
    return y, ksb, vsb, _state_from_kernel(s_fin)


def kernel(x_prompt, x_sample, cache_sb_k, cache_sb_v, state_gla, g_ffn1, w_ffn1_gate, w_ffn1_up,
           w_ffn1_down, g_mix, w_in, w_gate_up, b_gate, g_q, g_k, g_sb_out, g_gla_out, w_out,
           g_ffn2, w_ffn2_gate, w_ffn2_up, w_ffn2_down, g_final):
    depth = w_in.shape[0]
    batch, seq, _ = x_prompt.shape
    dec_batch, dec_seq, _ = x_sample.shape
    consts = (_group_ones(SB_WIDTH, SB_HEAD_DIM),
              *_suffix_matrices(), *_gla_tables(GLA_CHUNK))

    y_p = x_prompt.reshape(batch * seq, D_MODEL)
    y_s = x_sample.reshape(dec_batch * dec_seq, D_MODEL)
    outs = [[] for _ in range(6)]
    for l in range(depth):
        row = lambda a: a[l].reshape(1, -1)
        ffn1 = (row(g_ffn1), w_ffn1_gate[l].astype(BF16), w_ffn1_up[l].astype(BF16), w_ffn1_down[l].astype(BF16))
        x1_p, (w2g, w2u, w2d, w_in16, w_out16) = _ffn(
            y_p, *ffn1, tm=512,
            cast=[(w_ffn2_gate[l], D_FF), (w_ffn2_up[l], D_FF), (w_ffn2_down[l], D_MODEL),
                  (w_in[l], IN_PAD), (w_out[l], D_MODEL)])
        x1_s, _ = _ffn(y_s, *ffn1, tm=256)
        lw = (row(g_mix), w_in16,
              jnp.pad(w_gate_up[l], ((0, GATE_PAD - GATE_RANK), (0, 0))).astype(BF16), row(b_gate),
              jnp.tile(g_q[l], SB_HEADS).reshape(1, -1), jnp.tile(g_k[l], SB_HEADS).reshape(1, -1),
              jnp.tile(g_sb_out[l], SB_HEADS).reshape(1, -1), jnp.tile(g_gla_out[l], GLA_HEADS).reshape(1, -1),
              w_out16, row(g_ffn2), w2g, w2u, w2d, row(g_final))
        y_p, k_p, v_p, s_p = _layer(x1_p, lw, consts, batch=batch, seq=seq, tm=512)
        cache = (cache_sb_k[l], cache_sb_v[l])
        y_s, k_s, v_s, s_s = _layer(x1_s, lw, consts, batch=dec_batch, seq=dec_seq, tm=256,
                                    cache=cache, state=_state_to_kernel(state_gla[l]))
        shape_p = (batch, seq, SB_HEADS, SB_HEAD_DIM)
        shape_s = (dec_batch, dec_seq, SB_HEADS, SB_HEAD_DIM)
        for lst, val in zip(outs, (k_p.reshape(shape_p), v_p.reshape(shape_p), s_p,
                                   k_s.reshape(shape_s), v_s.reshape(shape_s), s_s)):
            lst.append(val)
    return (y_p.reshape(batch, seq, D_MODEL), y_s.reshape(dec_batch, dec_seq, D_MODEL),
            *[jnp.stack(o) for o in outs])
```

```python
import functools

import numpy as np
import jax
import jax.numpy as jnp
from jax import lax
from jax.experimental import pallas as pl
from jax.experimental.pallas import tpu as pltpu

F32 = jnp.float32
BF16 = jnp.bfloat16

D_MODEL = 1024
SB_HEADS = 8
SB_HEAD_DIM = 64
SB_WIDTH = SB_HEADS * SB_HEAD_DIM
SB_SCALE = SB_HEAD_DIM ** -0.5
GLA_HEADS = 4
GLA_HEAD_K = 64
GLA_HEAD_V = 128
GLA_KEY_WIDTH = GLA_HEADS * GLA_HEAD_K
GLA_WIDTH = GLA_HEADS * GLA_HEAD_V
GLA_SCALE = GLA_HEAD_K ** -0.5
GATE_RANK = 16
GATE_TAU = 16.0
D_FF = 2816
EPS = 1e-6

LANES = 128
SUBLANES = 8
BF16_SUBLANES = 16
GATE_PAD = LANES
MAIN_WIDTH = 3 * SB_WIDTH + 2 * GLA_KEY_WIDTH + 2 * GLA_WIDTH
IN_PAD = MAIN_WIDTH + GATE_PAD

FF_CHUNK = 256
SB_BLOCK = 128
SB_WINDOW = 3
SB_PAIRS_PER_STEP = 4
GLA_CHUNK = 128
GLA_BATCH_PER_STEP = 2
GLA_DIRECT_MIN = -40.0
SB_STOP = 152.0
LOG2_E = 1.4426950408889634

VMEM_LIMIT = 56 * 1024 * 1024


def _dot(a, b):
    return jnp.dot(a, b, preferred_element_type=F32)


def _dot_nt(a, b):
    return lax.dot_general(a, b, (((1,), (1,)), ((), ())), preferred_element_type=F32)


def _dot_tn(a, b):
    return lax.dot_general(a, b, (((0,), (0,)), ((), ())), preferred_element_type=F32)


def _split_bf16(x):
    hi = x.astype(BF16)
    lo = (x - hi.astype(F32)).astype(BF16)
    return hi, lo


def _dot_hilo(x, m):
    hi, lo = _split_bf16(x)
    return _dot(hi, m) + _dot(lo, m)


def _rms(x, g):
    ms = jnp.mean(x * x, axis=-1, keepdims=True)
    return x * lax.rsqrt(ms + EPS) * g


def _log_sigmoid(x):
    return jnp.minimum(x, 0.0) - jnp.log1p(jnp.exp(-jnp.abs(x)))


def _const_spec(shape):
    nd = len(shape)
    return pl.BlockSpec(shape, lambda *_: (0,) * nd, pipeline_mode=pl.Buffered(1))


def _params(n_grid):
    return pltpu.CompilerParams(dimension_semantics=("arbitrary",) * n_grid,
                                vmem_limit_bytes=VMEM_LIMIT)


def _half_step_ffn(x, g_ref, wg_ref, wu_ref, wd_ref, a_ref):
    h = _rms(x, g_ref[...]).astype(BF16)
    for c in range(D_FF // FF_CHUNK):
        sl = slice(c * FF_CHUNK, (c + 1) * FF_CHUNK)
        gt = _dot(h, wg_ref[:, sl])
        up = _dot(h, wu_ref[:, sl])
        a_ref[:, sl] = (gt * jax.nn.sigmoid(gt) * up).astype(BF16)
    return x + 0.5 * _dot(a_ref[...], wd_ref[...])


def _ffn_kernel(x_ref, g_ref, wg_ref, wu_ref, wd_ref, *refs):
    n_cast = (len(refs) - 2) // 2
    o_ref, a_ref = refs[n_cast], refs[-1]
    o_ref[...] = _half_step_ffn(x_ref[...], g_ref, wg_ref, wu_ref, wd_ref, a_ref)
    for src, dst in zip(refs[:n_cast], refs[n_cast + 1:-1]):
        keep = min(src.shape[1], dst.shape[1]) // LANES * LANES
        dst[:, :keep] = src[:, :keep].astype(BF16)
        if keep < dst.shape[1]:
            dst[:, keep:] = jnp.zeros((dst.shape[0], dst.shape[1] - keep), BF16)
            dst[:, keep:src.shape[1]] = src[:, keep:].astype(BF16)


def _ffn_specs():
    return [_const_spec((1, D_MODEL)), _const_spec((D_MODEL, D_FF)), _const_spec((D_MODEL, D_FF)),
            _const_spec((D_FF, D_MODEL))]


def _cast_block_rows(rows, steps):
    br = next(b for b in range(BF16_SUBLANES, rows + 1, BF16_SUBLANES) if rows % b == 0 and rows // b <= steps)
    return br, rows // br


def _ffn(x, g, wg, wu, wd, *, tm, cast=()):
    n = x.shape[0]
    steps = n // tm
    row = lambda i: (i, 0)
    cast_in, cast_out, cast_shape = [], [], []
    for w, width in cast:
        br, nblk = _cast_block_rows(w.shape[0], steps)
        idx = lambda i, last=nblk - 1: (jnp.minimum(i, last), 0)
        cast_in.append(pl.BlockSpec((br, w.shape[1]), idx))
        cast_out.append(pl.BlockSpec((br, width), idx))
        cast_shape.append(jax.ShapeDtypeStruct((w.shape[0], width), BF16))
    out = pl.pallas_call(
        _ffn_kernel,
        grid=(steps,),
        in_specs=[pl.BlockSpec((tm, D_MODEL), row)] + _ffn_specs() + cast_in,
        out_specs=[pl.BlockSpec((tm, D_MODEL), row)] + cast_out,
        out_shape=[jax.ShapeDtypeStruct((n, D_MODEL), F32)] + cast_shape,
        scratch_shapes=[pltpu.VMEM((tm, D_FF), BF16)],
        compiler_params=_params(1),
        name="ffn1",
    )(x, g, wg, wu, wd, *[w for w, _ in cast])
    return out[0], out[1:]


def _store_heads(ref, y):
    groups = y.shape[0] // SUBLANES
    sub = lax.broadcasted_iota(jnp.int32, (groups, SUBLANES, LANES), 1)
    rows = []
    for j in range(SB_WIDTH // LANES):
        pair = y[:, j * LANES:(j + 1) * LANES].reshape(groups, SUBLANES, LANES)
        rows += [pair, pltpu.roll(pair, SB_HEAD_DIM, axis=2)]
    for dist in (4, 2, 1):
        low = (sub & dist) == 0
        new = list(rows)
        for h in range(SUBLANES):
            if h & dist == 0:
                a, b = rows[h], rows[h + dist]
                new[h] = jnp.where(low, a, pltpu.roll(b, dist, axis=1))
                new[h + dist] = jnp.where(low, pltpu.roll(a, SUBLANES - dist, axis=1), b)
        rows = new
    for t in range(SUBLANES):
        ref[:, t] = rows[t][:, :, :SB_HEAD_DIM]


def _mixin_kernel(x_ref, g_ref, w_ref, wgu_ref, bg_ref, gq_ref, gk_ref, grp_ref,
                  qsb_ref, ksb_ref, ksb16_ref, vsb_ref, vsb16_ref,
                  qg_ref, kg_ref, vg16_ref, r_ref, la_ref):
    h = _rms(x_ref[...], g_ref[...]).astype(BF16)
    grp = grp_ref[...]

    def proj(lo, width):
        return _dot(h, w_ref[:, lo:lo + width])

    def head_norm(y, gain):
        ms = _dot_hilo(y * y, grp) * (1.0 / SB_HEAD_DIM)
        return y * lax.rsqrt(ms + EPS) * gain

    q = head_norm(proj(0, SB_WIDTH), gq_ref[...])
    qsb_ref[...] = (q * SB_SCALE).astype(BF16)
    k = head_norm(proj(SB_WIDTH, SB_WIDTH), gk_ref[...])
    _store_heads(ksb_ref, k)
    ksb16_ref[...] = k.astype(BF16)
    v = proj(2 * SB_WIDTH, SB_WIDTH)
    _store_heads(vsb_ref, v)
    vsb16_ref[...] = v.astype(BF16)
    off = 3 * SB_WIDTH
    qg_ref[...] = proj(off, GLA_KEY_WIDTH) * GLA_SCALE
    kg_ref[...] = proj(off + GLA_KEY_WIDTH, GLA_KEY_WIDTH)
    vg16_ref[...] = proj(off + 2 * GLA_KEY_WIDTH, GLA_WIDTH).astype(BF16)
    r_ref[...] = proj(off + 2 * GLA_KEY_WIDTH + GLA_WIDTH, GLA_WIDTH)
    lr = proj(MAIN_WIDTH, GATE_PAD).astype(BF16)
    gate = _dot(lr, wgu_ref[...]) + bg_ref[...]
    la_ref[...] = _log_sigmoid(gate) * (1.0 / GATE_TAU)


def _mixin(x, g, w, wgu, bg, gq, gk, grp, *, tm):
    n = x.shape[0]
    row = lambda i: (i, 0)
    heads = (SB_HEADS, SB_HEAD_DIM)
    assert SB_HEADS == SUBLANES and tm % SUBLANES == 0
    lead = lambda rows, wd: (rows // SUBLANES, SUBLANES) if len(wd) == 2 else (rows,)
    widths = [((SB_WIDTH,), BF16), (heads, F32), ((SB_WIDTH,), BF16), (heads, F32), ((SB_WIDTH,), BF16),
              ((GLA_KEY_WIDTH,), F32), ((GLA_KEY_WIDTH,), F32), ((GLA_WIDTH,), BF16), ((GLA_WIDTH,), F32),
              ((GLA_KEY_WIDTH,), F32)]
    return pl.pallas_call(
        _mixin_kernel,
        grid=(n // tm,),
        in_specs=[pl.BlockSpec((tm, D_MODEL), row),
                  _const_spec((1, D_MODEL)),
                  _const_spec((D_MODEL, IN_PAD)),
                  _const_spec((GATE_PAD, GLA_KEY_WIDTH)),
                  _const_spec((1, GLA_KEY_WIDTH)),
                  _const_spec((1, SB_WIDTH)),
                  _const_spec((1, SB_WIDTH)),
                  _const_spec((SB_WIDTH, SB_WIDTH))],
        out_specs=[pl.BlockSpec(lead(tm, wd) + wd, lambda i, nd=len(lead(tm, wd) + wd) - 1: (i,) + (0,) * nd)
                   for wd, _ in widths],
        out_shape=[jax.ShapeDtypeStruct(lead(n, wd) + wd, dt) for wd, dt in widths],
        compiler_params=_params(1),
        name="mixer_in",
    )(x, g, w, wgu, bg, gq, gk, grp)


def _neg_abs(x):
    return -jnp.abs(x)


def _sb_log2_terms(z):
    z2 = z * LOG2_E
    t = jnp.log2(1.0 + jnp.exp2(_neg_abs(z2)))
    return jnp.maximum(z2, 0.0) + t, jnp.minimum(z2, 0.0) - t


def _sb_kernel(q_ref, k_ref, v_ref, upair_ref, uo_ref, *refs, q_block):
    if q_block is None:
        o_ref, carry_ref, acc_ref, live_ref = refs
    else:
        kc_hbm, vc_hbm, o_ref, carry_ref, acc_ref, live_ref, kbuf, vbuf, sems = refs
    tq = q_ref.shape[0]
    pairs = q_ref.shape[1] // LANES
    heads = 2 * pairs
    i = pl.program_id(2) if q_block is None else q_block
    lane = lax.broadcasted_iota(jnp.int32, (tq, LANES), 1)
    head_lanes = (lane < SB_HEAD_DIM, lane >= SB_HEAD_DIM)
    rowi = lax.broadcasted_iota(jnp.int32, (heads * tq, SB_BLOCK), 0)
    coli = lax.broadcasted_iota(jnp.int32, (heads * tq, SB_BLOCK), 1)
    causal = coli < (rowi & (tq - 1))

    def pair_lanes(p):
        return slice(p * LANES, (p + 1) * LANES)

    def q_pair(p):
        q2 = q_ref[:, pair_lanes(p)]
        zero = jnp.zeros_like(q2)
        return jnp.concatenate([jnp.where(hl, q2, zero) for hl in head_lanes], axis=0)

    def rows(ref, p, first_block, nblocks):
        start = first_block * SB_BLOCK
        if not isinstance(start, int):
            start = pl.multiple_of(start, SB_BLOCK)
        return ref[pl.ds(start, nblocks * SB_BLOCK), pair_lanes(p)]

    def suffix_sums(x):
        nb = x.shape[1] // SB_BLOCK
        hi, lo = _split_bf16(x)

        def mm(cols, m2):
            return _dot(jnp.concatenate([hi[:, cols], lo[:, cols]], axis=1), m2)

        within, total = [], []
        if nb >= 3:
            for j in range(0, nb - 1, 2):
                y = mm(slice(j * SB_BLOCK, (j + 2) * SB_BLOCK), upair_ref[...])
                within += [y[:, :SB_BLOCK], y[:, SB_BLOCK:]]
                total += [jnp.sum(x[:, b * SB_BLOCK:(b + 1) * SB_BLOCK], axis=1, keepdims=True) for b in (j, j + 1)]
        for j in range(len(within), nb):
            y = mm(slice(j * SB_BLOCK, (j + 1) * SB_BLOCK), uo_ref[...])
            within.append(y[:, :SB_BLOCK])
            total.append(y[:, SB_BLOCK:])
        later = total[nb - 1]
        out = [within[nb - 1]]
        for j in range(nb - 2, -1, -1):
            out.insert(0, within[j] + later)
            later = later + total[j]
        return (out[0] if nb == 1 else jnp.concatenate(out, axis=1)), later

    def resident(first_block, nblocks):
        return (lambda p: rows(k_ref, p, first_block, nblocks), lambda p: rows(v_ref, p, first_block, nblocks))

    def sweep(kv, nblocks, own, carry, valid=None):
        nk = nblocks * SB_BLOCK

        def masked(x):
            if valid is not None:
                return jnp.where(valid, x, 0.0)
            if not own:
                return x
            tail = jnp.where(causal, x[:, nk - SB_BLOCK:], 0.0)
            return tail if nblocks == 1 else jnp.concatenate([x[:, :nk - SB_BLOCK], tail], axis=1)

        z = jnp.concatenate([_dot_nt(q_pair(p), kv[0](p)) for p in range(pairs)], axis=0)
        fail, log_beta = _sb_log2_terms(z)
        after, total = suffix_sums(masked(fail))
        if carry is not None:
            after = after + carry
        w = masked(jnp.exp2(log_beta - after)).astype(BF16)
        pv = jnp.concatenate([_dot(w[2 * p * tq:(2 * p + 2) * tq], kv[1](p))
                              for p in range(pairs)], axis=0)
        if carry is None:
            carry_ref[...] = total
            acc_ref[...] = pv
        else:
            carry_ref[...] = carry + total
            acc_ref[...] += pv

    def live():
        return jnp.min(carry_ref[...]) < SB_STOP

    def mark_live():
        live_ref[0] = jnp.where(live(), 1, 0)

    half = SB_BLOCK // 2

    def by_halves(x, f):
        y = [f(s, jnp.concatenate([x[s * half:(s + 1) * half], x[tq + s * half:tq + (s + 1) * half]], axis=0))
             for s in range(2)]
        return jnp.concatenate([y[0][:half], y[1][:half], y[0][half:], y[1][half:]], axis=0)

    def half_windows():
        starts = [pl.multiple_of(i * SB_BLOCK - (3 - s) * half, half) for s in range(2)]
        nk = 2 * SB_BLOCK
        keep = coli < (rowi & (half - 1)) + half

        def masked(x):
            return jnp.concatenate([x[:, :SB_BLOCK], jnp.where(keep, x[:, SB_BLOCK:], 0.0)], axis=1)

        z = jnp.concatenate(
            [by_halves(q_pair(p), lambda s, lhs: _dot_nt(lhs, k_ref[pl.ds(starts[s], nk), pair_lanes(p)]))
             for p in range(pairs)], axis=0)
        fail, log_beta = _sb_log2_terms(z)
        after, total = suffix_sums(masked(fail))
        w = masked(jnp.exp2(log_beta - after)).astype(BF16)
        carry_ref[...] = total
        acc_ref[...] = jnp.concatenate(
            [by_halves(w[2 * p * tq:(2 * p + 2) * tq],
                       lambda s, lhs: _dot(lhs, v_ref[pl.ds(starts[s], nk), pair_lanes(p)]))
             for p in range(pairs)], axis=0)

    def own_block_only():
        sweep(resident(i, 1), 1, True, None)

    if q_block is None:
        assert tq == SB_BLOCK
        windowed = i >= 2
        pl.when(windowed)(half_windows)
        pl.when(jnp.logical_not(windowed))(own_block_only)

        mark_live()

        @pl.when(jnp.logical_and(windowed, live_ref[0] != 0))
        def _():
            unseen = jnp.logical_or(coli < half, (rowi & half) != 0)
            sweep(resident(i - 2, 1), 1, False, carry_ref[...], unseen)
            mark_live()

        n_prev = jnp.where(windowed, i - 2, i)
        earlier = lambda kb: resident(kb, 1)
    else:
        assert q_block >= SB_WINDOW - 1 and k_ref.shape[0] == SB_WINDOW * SB_BLOCK
        sweep(resident(0, SB_WINDOW), SB_WINDOW, True, None)
        mark_live()
        n_prev = q_block - (SB_WINDOW - 1)

        def earlier(kb):
            src = lambda hbm: hbm.at[pl.program_id(0), :, :, pl.ds(pl.multiple_of(kb * SB_BLOCK, SB_BLOCK), SB_BLOCK)]
            copies = [pltpu.make_async_copy(src(hbm), buf, sems.at[n])
                      for n, (hbm, buf) in enumerate(((kc_hbm, kbuf), (vc_hbm, vbuf)))]
            for c in copies:
                c.start()
            for c in copies:
                c.wait()
            slab = lambda buf: (lambda p: jnp.concatenate([buf[2 * p].T, buf[2 * p + 1].T], axis=1).astype(BF16))
            return slab(kbuf), slab(vbuf)

    def body(st):
        kb, _ = st
        sweep(earlier(kb), 1, False, carry_ref[...])
        return kb - 1, live()

    lax.while_loop(lambda st: jnp.logical_and(st[0] >= 0, st[1]), body, (n_prev - 1, live_ref[0] != 0))
    for p in range(pairs):
        o_ref[:, pair_lanes(p)] = jnp.where(head_lanes[0], acc_ref[2 * p * tq:(2 * p + 1) * tq],
                                            acc_ref[(2 * p + 1) * tq:(2 * p + 2) * tq])


def _sb(q16, k16, v16, upair, uo, *, batch, tq, keys, cache=None):
    nq = keys // SB_BLOCK if cache is None else 1
    width = SB_PAIRS_PER_STEP * LANES
    assert width == SB_WIDTH or cache is None
    kv_spec = pl.BlockSpec((keys, width), lambda b, p, i: (b, p))
    q_spec = pl.BlockSpec((tq, width), lambda b, p, i: (b * nq + i, p))
    heads = 2 * SB_PAIRS_PER_STEP
    in_specs = [q_spec, kv_spec, kv_spec, _const_spec(upair.shape), _const_spec(uo.shape)]
    scratch = [pltpu.VMEM((heads * tq, SB_BLOCK), F32), pltpu.VMEM((heads * tq, LANES), F32),
               pltpu.SMEM((1,), jnp.int32)]
    operands = [q16, k16, v16, upair, uo]
    q_block = None
    if cache is not None:
        past = cache[0].shape[3]
        assert past % SB_BLOCK == 0 and keys == SB_WINDOW * SB_BLOCK <= past + SB_BLOCK
        q_block = past // SB_BLOCK
        in_specs += [pl.BlockSpec(memory_space=pl.ANY)] * 2
        scratch += [pltpu.VMEM((SB_HEADS, SB_HEAD_DIM, SB_BLOCK), F32)] * 2 + [pltpu.SemaphoreType.DMA((2,))]
        operands += list(cache)
    return pl.pallas_call(
        functools.partial(_sb_kernel, q_block=q_block),
        grid=(batch, SB_WIDTH // width, nq),
        in_specs=in_specs,
        out_specs=q_spec,
        out_shape=jax.ShapeDtypeStruct((batch * nq * tq, SB_WIDTH), F32),
        scratch_shapes=scratch,
        compiler_params=_params(3),
        name="sb_prompt" if cache is None else "sb_sample",
    )(*operands)


def _gla_levels(c):
    return int(np.log2(c))


def _gla_tables(c):
    nlev = _gla_levels(c)
    t = np.arange(c)[:, None]
    j = np.arange(c)[None, :]
    mats = [(j <= t), (j > t)]
    for lev in range(1, nlev + 1):
        m = 1 << (lev - 1)
        start = (t // (2 * m)) * (2 * m)
        mats.append(((t % (2 * m)) >= m) & (j >= start + m) & (j <= t))
    for lev in range(1, nlev + 1):
        m = 1 << (lev - 1)
        start = (t // (2 * m)) * (2 * m)
        mats.append(((t % (2 * m)) < m) & (j > t) & (j <= start + m - 1))
    mats = np.concatenate([a.astype(np.float32) for a in mats], axis=0)
    x = t ^ j
    level = np.where(j > t, -1, np.where(x == 0, 0, np.floor(np.log2(np.maximum(x, 1))).astype(np.int64) + 1))
    return jnp.asarray(mats, BF16), jnp.asarray(level, jnp.int32)


def _gla_kernel(q_ref, k_ref, la_ref, v_ref, s0_ref, mats_ref, lv_ref, o_ref, sfin_ref, st_ref):
    c = GLA_CHUNK
    nlev = _gla_levels(c)
    streams = q_ref.shape[0]
    step = pl.program_id(1)

    @pl.when(step == 0)
    def _():
        st_ref[...] = s0_ref[...]

    lane = lax.broadcasted_iota(jnp.int32, (c, LANES), 1)
    head_lanes = (lane < GLA_HEAD_K, lane >= GLA_HEAD_K)
    lane_sq = lax.broadcasted_iota(jnp.int32, (GLA_HEAD_V, LANES), 1) < GLA_HEAD_K

    def pick(x, hl):
        return jnp.where(hl, x, jnp.zeros_like(x))

    def stack_heads(x):
        return jnp.concatenate([pick(x, hl) for hl in head_lanes], axis=0)

    def finish(s, p, scores, qe, ke, b_last):
        sl = slice(p * LANES, (p + 1) * LANES)
        st = st_ref[s, p]
        inter = _dot_nt(stack_heads(qe[:, sl]), st.astype(BF16))
        upd = []
        for r in range(2):
            hd = 2 * p + r
            vh = v_ref[s, :, hd * GLA_HEAD_V:(hd + 1) * GLA_HEAD_V]
            o_ref[s, :, hd * GLA_HEAD_V:(hd + 1) * GLA_HEAD_V] = (
                inter[r * c:(r + 1) * c] + _dot(scores[r * c:(r + 1) * c], vh))
            upd.append(_dot_tn(vh, ke[:, sl]))
        st_ref[s, p] = st * jnp.exp(b_last[:, sl]) + jnp.where(lane_sq, upd[0], upd[1])

    la_all = jnp.concatenate([la_ref[s] for s in range(streams)], axis=1)
    b_all = _dot_hilo_left(mats_ref[:c, :], la_all)
    mild = jnp.min(b_all[c - 1:c, :]) >= GLA_DIRECT_MIN

    @pl.when(mild)
    def _():
        rowi = lax.broadcasted_iota(jnp.int32, (2 * c, c), 0) & (c - 1)
        causal = lax.broadcasted_iota(jnp.int32, (2 * c, c), 1) <= rowi
        for s in range(streams):
            b = b_all[:, s * GLA_KEY_WIDTH:(s + 1) * GLA_KEY_WIDTH]
            b_last = b[c - 1:c, :]
            q = q_ref[s]
            k = k_ref[s]
            qe = (q * jnp.exp(b)).astype(BF16)
            kd = (k * jnp.exp(-b)).astype(BF16)
            ke = (k * jnp.exp(b_last - b)).astype(BF16)
            for p in range(GLA_HEADS // 2):
                sl = slice(p * LANES, (p + 1) * LANES)
                scores = jnp.where(causal, _dot_nt(stack_heads(qe[:, sl]), kd[:, sl]), 0.0)
                finish(s, p, scores.astype(BF16), qe, ke, b_last)

    @pl.when(jnp.logical_not(mild))
    def _():
        lv = jnp.concatenate([lv_ref[...], lv_ref[...]], axis=0)
        for s in range(streams):
            q = q_ref[s]
            k = k_ref[s]
            sums = _dot_hilo_left(mats_ref[...], la_ref[s])

            def rows(i):
                return sums[i * c:(i + 1) * c]

            qe = (q * jnp.exp(rows(0))).astype(BF16)
            ke = (k * jnp.exp(rows(1))).astype(BF16)
            q_lev = [q.astype(BF16)] + [(q * jnp.exp(rows(1 + lev))).astype(BF16) for lev in range(1, nlev + 1)]
            k_lev = [k.astype(BF16)] + [(k * jnp.exp(rows(1 + nlev + lev))).astype(BF16)
                                        for lev in range(1, nlev + 1)]
            for p in range(GLA_HEADS // 2):
                sl = slice(p * LANES, (p + 1) * LANES)
                scores = jnp.zeros((2 * c, c), F32)
                for lev in range(nlev + 1):
                    scores = jnp.where(lv == lev, _dot_nt(stack_heads(q_lev[lev][:, sl]), k_lev[lev][:, sl]),
                                       scores)
                finish(s, p, scores.astype(BF16), qe, ke, rows(0)[c - 1:c, :])

    sfin_ref[...] = st_ref[...]


def _dot_hilo_left(m, x):
    hi, lo = _split_bf16(x)
    return _dot(m, hi) + _dot(m, lo)


def _gla(qg, kg, la, vg16, s0, mats, lv, *, batch, seq):
    c = GLA_CHUNK
    g = GLA_BATCH_PER_STEP
    pairs = GLA_HEADS // 2
    blk = lambda width: pl.BlockSpec((g, c, width), lambda b, t: (b, t, 0))
    st_spec = pl.BlockSpec((g, pairs, GLA_HEAD_V, LANES), lambda b, t: (b, 0, 0, 0))
    per_stream = lambda a: a.reshape(batch, seq, a.shape[-1])
    o, s_fin = pl.pallas_call(
        _gla_kernel,
        grid=(batch // g, seq // c),
        in_specs=[blk(GLA_KEY_WIDTH), blk(GLA_KEY_WIDTH), blk(GLA_KEY_WIDTH), blk(GLA_WIDTH), st_spec,
                  _const_spec(mats.shape), _const_spec(lv.shape)],
        out_specs=[blk(GLA_WIDTH), st_spec],
        out_shape=[jax.ShapeDtypeStruct((batch, seq, GLA_WIDTH), F32),
                   jax.ShapeDtypeStruct((batch, pairs, GLA_HEAD_V, LANES), F32)],
        scratch_shapes=[pltpu.VMEM((g, pairs, GLA_HEAD_V, LANES), F32)],
        compiler_params=_params(2),
        name="gla",
    )(per_stream(qg), per_stream(kg), per_stream(la), per_stream(vg16), s0, mats, lv)
    return o.reshape(batch * seq, GLA_WIDTH), s_fin


def _head_norm_lanes(y, head, gain):
    tm, width = y.shape
    low = lax.broadcasted_iota(jnp.int32, (tm, LANES), 1) < head
    cols = []
    for j in range(width // LANES):
        yj = y[:, j * LANES:(j + 1) * LANES]
        sq = yj * yj

        def inv_rms(part):
            return lax.rsqrt(jnp.sum(part, axis=1, keepdims=True) * (1.0 / head) + EPS)

        if head == LANES:
            cols.append(yj * inv_rms(sq))
        else:
            cols.append(yj * jnp.where(low, inv_rms(jnp.where(low, sq, 0.0)), inv_rms(jnp.where(low, 0.0, sq))))
    return jnp.concatenate(cols, axis=1) * gain


def _mixout_kernel(x_ref, osb_ref, og_ref, r_ref, gsb_ref, ggla_ref, w_ref,
                   g_ref, wg_ref, wu_ref, wd_ref, gfin_ref, o_ref, a_ref):
    o_sb = _head_norm_lanes(osb_ref[...], SB_HEAD_DIM, gsb_ref[...])
    r = r_ref[...]
    o_g = _head_norm_lanes(og_ref[...], GLA_HEAD_V, ggla_ref[...]) * (r * jax.nn.sigmoid(r))
    mix = _dot(o_sb.astype(BF16), w_ref[:SB_WIDTH, :]) + _dot(o_g.astype(BF16), w_ref[SB_WIDTH:, :])
    y = _half_step_ffn(x_ref[...] + mix, g_ref, wg_ref, wu_ref, wd_ref, a_ref)
    o_ref[...] = _rms(y, gfin_ref[...])


def _mixout(x, osb, og, r, gsb, ggla, w, g, wg, wu, wd, gfin, *, tm):
    n = x.shape[0]
    row = lambda i: (i, 0)
    return pl.pallas_call(
        _mixout_kernel,
        grid=(n // tm,),
        in_specs=[pl.BlockSpec((tm, D_MODEL), row),
                  pl.BlockSpec((tm, SB_WIDTH), row),
                  pl.BlockSpec((tm, GLA_WIDTH), row),
                  pl.BlockSpec((tm, GLA_WIDTH), row),
                  _const_spec((1, SB_WIDTH)),
                  _const_spec((1, GLA_WIDTH)),
                  _const_spec((D_MODEL, D_MODEL))] + _ffn_specs() + [_const_spec((1, D_MODEL))],
        out_specs=pl.BlockSpec((tm, D_MODEL), row),
        out_shape=jax.ShapeDtypeStruct((n, D_MODEL), F32),
        scratch_shapes=[pltpu.VMEM((tm, D_FF), BF16)],
        compiler_params=_params(1),
        name="mixer_out_ffn2",
    )(x, osb, og, r, gsb, ggla, w, g, wg, wu, wd, gfin)


def _group_ones(width, group):
    idx = np.arange(width) // group
    return jnp.asarray(idx[:, None] == idx[None, :], BF16)


def _suffix_matrices():
    j = np.arange(SB_BLOCK)[:, None]
    s = np.arange(SB_BLOCK)[None, :]
    later = j > s
    zero = np.zeros_like(later)
    pair = np.block([[later, zero], [zero, later]])
    single = np.concatenate([later, np.ones_like(later)], axis=1)
    twice = lambda m: jnp.asarray(np.concatenate([m, m], axis=0), BF16)
    return twice(pair), twice(single)


def _state_to_kernel(s):
    b = s.shape[0]
    s = s.reshape(b, GLA_HEADS // 2, 2, GLA_HEAD_K, GLA_HEAD_V)
    return s.transpose(0, 1, 4, 2, 3).reshape(b, GLA_HEADS // 2, GLA_HEAD_V, LANES)


def _state_from_kernel(s):
    b = s.shape[0]
    s = s.reshape(b, GLA_HEADS // 2, GLA_HEAD_V, 2, GLA_HEAD_K)
    return s.transpose(0, 1, 3, 4, 2).reshape(b, GLA_HEADS, GLA_HEAD_K, GLA_HEAD_V)


def _layer(x1, lw, consts, *, batch, seq, tm, cache=None, state=None):
    (g_mix, w_in, wgu, bg, gq, gk, gsb, ggla, w_out, g2, w2g, w2u, w2d, gfin) = lw
    grp64, upair, uo, mats, lv = consts
    (qsb16, ksb, ksb16, vsb, vsb16, qg, kg, vg16, r, la) = _mixin(x1, g_mix, w_in, wgu, bg, gq, gk, grp64, tm=tm)

    if cache is None:
        o_sb = _sb(qsb16, ksb16, vsb16, upair, uo, batch=batch, tq=SB_BLOCK, keys=seq)
    else:
        recent = (SB_WINDOW - 1) * SB_BLOCK

        def window(old, new):
            old = old[:, :, :, old.shape[3] - recent:].transpose(0, 3, 1, 2).reshape(batch, recent, SB_WIDTH)
            old = old.astype(BF16)
            new = jnp.pad(new.reshape(batch, seq, SB_WIDTH), ((0, 0), (0, SB_BLOCK - seq), (0, 0)))
            return jnp.concatenate([old, new], axis=1).reshape(batch * SB_WINDOW * SB_BLOCK, SB_WIDTH)

        o_sb = _sb(qsb16, window(cache[0], ksb16), window(cache[1], vsb16), upair, uo,
                   batch=batch, tq=seq, keys=SB_WINDOW * SB_BLOCK, cache=cache)

    c = GLA_CHUNK
    seq_pad = -(-seq // c) * c
    if seq_pad != seq:
        padt = lambda a: jnp.pad(a.reshape(batch, seq, a.shape[-1]),
                                 ((0, 0), (0, seq_pad - seq), (0, 0))).reshape(batch * seq_pad, a.shape[-1])
        qg_p, kg_p, la_p, vg_p = padt(qg), padt(kg), padt(la), padt(vg16)
    else:
        qg_p, kg_p, la_p, vg_p = qg, kg, la, vg16
    if state is None:
        state = jnp.zeros((batch, GLA_HEADS // 2, GLA_HEAD_V, LANES), F32)
    o_g, s_fin = _gla(qg_p, kg_p, la_p, vg_p, state, mats, lv, batch=batch, seq=seq_pad)
    if seq_pad != seq:
        o_g = o_g.reshape(batch, seq_pad, GLA_WIDTH)[:, :seq].reshape(batch * seq, GLA_WIDTH)

    y = _mixout(x1, o_sb, o_g, r, gsb, ggla, w_out, g2, w2g, w2u, w2d, gfin, tm=tm)
    return y, ksb, vsb, _state_from_kernel(s_fin)


def kernel(x_prompt, x_sample, cache_sb_k, cache_sb_v, state_gla, g_ffn1, w_ffn1_gate, w_ffn1_up,
           w_ffn1_down, g_mix, w_in, w_gate_up, b_gate, g_q, g_k, g_sb_out, g_gla_out, w_out,
           g_ffn2, w_ffn2_gate, w_ffn2_up, w_ffn2_down, g_final):
    depth = w_in.shape[0]
    batch, seq, _ = x_prompt.shape
    dec_batch, dec_seq, _ = x_sample.shape
    consts = (_group_ones(SB_WIDTH, SB_HEAD_DIM),
              *_suffix_matrices(), *_gla_tables(GLA_CHUNK))

    y_p = x_prompt.reshape(batch * seq, D_MODEL)
    y_s = x_sample.reshape(dec_batch * dec_seq, D_MODEL)
    outs = [[] for _ in range(6)]
    for l in range(depth):
        row = lambda a: a[l].reshape(1, -1)
        ffn1 = (row(g_ffn1), w_ffn1_gate[l].astype(BF16), w_ffn1_up[l].astype(BF16), w_ffn1_down[l].astype(BF16))
        x1_p, (w2g, w2u, w2d, w_in16, w_out16) = _ffn(
            y_p, *ffn1, tm=512,
            cast=[(w_ffn2_gate[l], D_FF), (w_ffn2_up[l], D_FF), (w_ffn2_down[l], D_MODEL),
                  (w_in[l], IN_PAD), (w_out[l], D_MODEL)])
        x1_s, _ = _ffn(y_s, *ffn1, tm=256)
        lw = (row(g_mix), w_in16,
              jnp.pad(w_gate_up[l], ((0, GATE_PAD - GATE_RANK), (0, 0))).astype(BF16), row(b_gate),
              jnp.tile(g_q[l], SB_HEADS).reshape(1, -1), jnp.tile(g_k[l], SB_HEADS).reshape(1, -1),
              jnp.tile(g_sb_out[l], SB_HEADS).reshape(1, -1), jnp.tile(g_gla_out[l], GLA_HEADS).reshape(1, -1),
              w_out16, row(g_ffn2), w2g, w2u, w2d, row(g_final))
        y_p, k_p, v_p, s_p = _layer(x1_p, lw, consts, batch=batch, seq=seq, tm=512)
        cache = (cache_sb_k[l].transpose(0, 2, 3, 1), cache_sb_v[l].transpose(0, 2, 3, 1))
        y_s, k_s, v_s, s_s = _layer(x1_s, lw, consts, batch=dec_batch, seq=dec_seq, tm=256,
                                    cache=cache, state=_state_to_kernel(state_gla[l]))
        shape_p = (batch, seq, SB_HEADS, SB_HEAD_DIM)
        shape_s = (dec_batch, dec_seq, SB_HEADS, SB_HEAD_DIM)
        for lst, val in zip(outs, (k_p.reshape(shape_p), v_p.reshape(shape_p), s_p,
                                   k_s.reshape(shape_s), v_s.reshape(shape_s), s_s)):
            lst.append(val)
    return (y_p.reshape(batch, seq, D_MODEL), y_s.reshape(dec_batch, dec_seq, D_MODEL),
            *[jnp.stack(o) for o in outs])
```

```python
import functools

import numpy as np
import jax
import jax.numpy as jnp
from jax import lax
from jax.experimental import pallas as pl
from jax.experimental.pallas import tpu as pltpu

F32 = jnp.float32
BF16 = jnp.bfloat16

D_MODEL = 1024
SB_HEADS = 8
SB_HEAD_DIM = 64
SB_WIDTH = SB_HEADS * SB_HEAD_DIM
SB_SCALE = SB_HEAD_DIM ** -0.5
GLA_HEADS = 4
GLA_HEAD_K = 64
GLA_HEAD_V = 128
GLA_KEY_WIDTH = GLA_HEADS * GLA_HEAD_K
GLA_WIDTH = GLA_HEADS * GLA_HEAD_V
GLA_SCALE = GLA_HEAD_K ** -0.5
GATE_RANK = 16
GATE_TAU = 16.0
D_FF = 2816
EPS = 1e-6

LANES = 128
SUBLANES = 8
BF16_SUBLANES = 16
GATE_PAD = LANES
MAIN_WIDTH = 3 * SB_WIDTH + 2 * GLA_KEY_WIDTH + 2 * GLA_WIDTH
IN_PAD = MAIN_WIDTH + GATE_PAD

FF_CHUNK = 256
SB_BLOCK = 128
SB_WINDOW = 3
SB_PAIRS_PER_STEP = 4
GLA_CHUNK = 128
GLA_BATCH_PER_STEP = 2
GLA_DIRECT_MIN = -40.0
SB_STOP = 152.0
LOG2_E = 1.4426950408889634

VMEM_LIMIT = 56 * 1024 * 1024


def _dot(a, b):
    return jnp.dot(a, b, preferred_element_type=F32)


def _dot_nt(a, b):
    return lax.dot_general(a, b, (((1,), (1,)), ((), ())), preferred_element_type=F32)


def _dot_tn(a, b):
    return lax.dot_general(a, b, (((0,), (0,)), ((), ())), preferred_element_type=F32)


def _split_bf16(x):
    hi = x.astype(BF16)
    lo = (x - hi.astype(F32)).astype(BF16)
    return hi, lo


def _dot_hilo(x, m):
    hi, lo = _split_bf16(x)
    return _dot(hi, m) + _dot(lo, m)


def _rms(x, g):
    ms = jnp.mean(x * x, axis=-1, keepdims=True)
    return x * lax.rsqrt(ms + EPS) * g


def _log_sigmoid(x):
    return jnp.minimum(x, 0.0) - jnp.log1p(jnp.exp(-jnp.abs(x)))


def _const_spec(shape):
    nd = len(shape)
    return pl.BlockSpec(shape, lambda *_: (0,) * nd, pipeline_mode=pl.Buffered(1))


def _params(n_grid):
    return pltpu.CompilerParams(dimension_semantics=("arbitrary",) * n_grid,
                                vmem_limit_bytes=VMEM_LIMIT)


def _half_step_ffn(x, g_ref, wg_ref, wu_ref, wd_ref, a_ref):
    h = _rms(x, g_ref[...]).astype(BF16)
    for c in range(D_FF // FF_CHUNK):
        sl = slice(c * FF_CHUNK, (c + 1) * FF_CHUNK)
        gt = _dot(h, wg_ref[:, sl])
        up = _dot(h, wu_ref[:, sl])
        a_ref[:, sl] = (gt * jax.nn.sigmoid(gt) * up).astype(BF16)
    return x + 0.5 * _dot(a_ref[...], wd_ref[...])


def _ffn_kernel(x_ref, g_ref, wg_ref, wu_ref, wd_ref, *refs):
    n_cast = (len(refs) - 2) // 2
    o_ref, a_ref = refs[n_cast], refs[-1]
    o_ref[...] = _half_step_ffn(x_ref[...], g_ref, wg_ref, wu_ref, wd_ref, a_ref)
    for src, dst in zip(refs[:n_cast], refs[n_cast + 1:-1]):
        keep = min(src.shape[1], dst.shape[1]) // LANES * LANES
        dst[:, :keep] = src[:, :keep].astype(BF16)
        if keep < dst.shape[1]:
            dst[:, keep:] = jnp.zeros((dst.shape[0], dst.shape[1] - keep), BF16)
            dst[:, keep:src.shape[1]] = src[:, keep:].astype(BF16)


def _ffn_specs():
    return [_const_spec((1, D_MODEL)), _const_spec((D_MODEL, D_FF)), _const_spec((D_MODEL, D_FF)),
            _const_spec((D_FF, D_MODEL))]


def _cast_block_rows(rows, steps):
    br = next(b for b in range(BF16_SUBLANES, rows + 1, BF16_SUBLANES) if rows % b == 0 and rows // b <= steps)
    return br, rows // br


def _ffn(x, g, wg, wu, wd, *, tm, cast=()):
    n = x.shape[0]
    steps = n // tm
    row = lambda i: (i, 0)
    cast_in, cast_out, cast_shape = [], [], []
    for w, width in cast:
        br, nblk = _cast_block_rows(w.shape[0], steps)
        idx = lambda i, last=nblk - 1: (jnp.minimum(i, last), 0)
        cast_in.append(pl.BlockSpec((br, w.shape[1]), idx))
        cast_out.append(pl.BlockSpec((br, width), idx))
        cast_shape.append(jax.ShapeDtypeStruct((w.shape[0], width), BF16))
    out = pl.pallas_call(
        _ffn_kernel,
        grid=(steps,),
        in_specs=[pl.BlockSpec((tm, D_MODEL), row)] + _ffn_specs() + cast_in,
        out_specs=[pl.BlockSpec((tm, D_MODEL), row)] + cast_out,
        out_shape=[jax.ShapeDtypeStruct((n, D_MODEL), F32)] + cast_shape,
        scratch_shapes=[pltpu.VMEM((tm, D_FF), BF16)],
        compiler_params=_params(1),
        name="ffn1",
    )(x, g, wg, wu, wd, *[w for w, _ in cast])
    return out[0], out[1:]


def _store_heads(ref, y):
    groups = y.shape[0] // SUBLANES
    sub = lax.broadcasted_iota(jnp.int32, (groups, SUBLANES, LANES), 1)
    rows = []
    for j in range(SB_WIDTH // LANES):
        pair = y[:, j * LANES:(j + 1) * LANES].reshape(groups, SUBLANES, LANES)
        rows += [pair, pltpu.roll(pair, SB_HEAD_DIM, axis=2)]
    for dist in (4, 2, 1):
        low = (sub & dist) == 0
        new = list(rows)
        for h in range(SUBLANES):
            if h & dist == 0:
                a, b = rows[h], rows[h + dist]
                new[h] = jnp.where(low, a, pltpu.roll(b, dist, axis=1))
                new[h + dist] = jnp.where(low, pltpu.roll(a, SUBLANES - dist, axis=1), b)
        rows = new
    for t in range(SUBLANES):
        ref[:, t] = rows[t][:, :, :SB_HEAD_DIM]


def _mixin_kernel(x_ref, g_ref, w_ref, wgu_ref, bg_ref, gq_ref, gk_ref, grp_ref,
                  qsb_ref, ksb_ref, ksb16_ref, vsb_ref, vsb16_ref,
                  qg_ref, kg_ref, vg16_ref, r_ref, la_ref):
    h = _rms(x_ref[...], g_ref[...]).astype(BF16)
    grp = grp_ref[...]

    def proj(lo, width):
        return _dot(h, w_ref[:, lo:lo + width])

    def head_norm(y, gain):
        ms = _dot_hilo(y * y, grp) * (1.0 / SB_HEAD_DIM)
        return y * lax.rsqrt(ms + EPS) * gain

    q = head_norm(proj(0, SB_WIDTH), gq_ref[...])
    qsb_ref[...] = (q * SB_SCALE).astype(BF16)
    k = head_norm(proj(SB_WIDTH, SB_WIDTH), gk_ref[...])
    _store_heads(ksb_ref, k)
    ksb16_ref[...] = k.astype(BF16)
    v = proj(2 * SB_WIDTH, SB_WIDTH)
    _store_heads(vsb_ref, v)
    vsb16_ref[...] = v.astype(BF16)
    off = 3 * SB_WIDTH
    qg_ref[...] = proj(off, GLA_KEY_WIDTH) * GLA_SCALE
    kg_ref[...] = proj(off + GLA_KEY_WIDTH, GLA_KEY_WIDTH)
    vg16_ref[...] = proj(off + 2 * GLA_KEY_WIDTH, GLA_WIDTH).astype(BF16)
    r_ref[...] = proj(off + 2 * GLA_KEY_WIDTH + GLA_WIDTH, GLA_WIDTH)
    lr = proj(MAIN_WIDTH, GATE_PAD).astype(BF16)
    gate = _dot(lr, wgu_ref[...]) + bg_ref[...]
    la_ref[...] = _log_sigmoid(gate) * (1.0 / GATE_TAU)


def _mixin(x, g, w, wgu, bg, gq, gk, grp, *, tm):
    n = x.shape[0]
    row = lambda i: (i, 0)
    heads = (SB_HEADS, SB_HEAD_DIM)
    assert SB_HEADS == SUBLANES and tm % SUBLANES == 0
    lead = lambda rows, wd: (rows // SUBLANES, SUBLANES) if len(wd) == 2 else (rows,)
    widths = [((SB_WIDTH,), BF16), (heads, F32), ((SB_WIDTH,), BF16), (heads, F32), ((SB_WIDTH,), BF16),
              ((GLA_KEY_WIDTH,), F32), ((GLA_KEY_WIDTH,), F32), ((GLA_WIDTH,), BF16), ((GLA_WIDTH,), F32),
              ((GLA_KEY_WIDTH,), F32)]
    return pl.pallas_call(
        _mixin_kernel,
        grid=(n // tm,),
        in_specs=[pl.BlockSpec((tm, D_MODEL), row),
                  _const_spec((1, D_MODEL)),
                  _const_spec((D_MODEL, IN_PAD)),
                  _const_spec((GATE_PAD, GLA_KEY_WIDTH)),
                  _const_spec((1, GLA_KEY_WIDTH)),
                  _const_spec((1, SB_WIDTH)),
                  _const_spec((1, SB_WIDTH)),
                  _const_spec((SB_WIDTH, SB_WIDTH))],
        out_specs=[pl.BlockSpec(lead(tm, wd) + wd, lambda i, nd=len(lead(tm, wd) + wd) - 1: (i,) + (0,) * nd)
                   for wd, _ in widths],
        out_shape=[jax.ShapeDtypeStruct(lead(n, wd) + wd, dt) for wd, dt in widths],
        compiler_params=_params(1),
        name="mixer_in",
    )(x, g, w, wgu, bg, gq, gk, grp)


def _neg_abs(x):
    return -jnp.abs(x)


def _sb_log2_terms(z):
    z2 = z * LOG2_E
    t = jnp.log2(1.0 + jnp.exp2(_neg_abs(z2)))
    return jnp.maximum(z2, 0.0) + t, jnp.minimum(z2, 0.0) - t


def _sb_kernel(q_ref, k_ref, v_ref, upair_ref, uo_ref, *refs, q_block):
    if q_block is None:
        o_ref, carry_ref, acc_ref, live_ref = refs
    else:
        kc_hbm, vc_hbm, o_ref, carry_ref, acc_ref, live_ref, kbuf, vbuf, sems, kwin, vwin = refs
    tq = q_ref.shape[0]
    pairs = q_ref.shape[1] // LANES
    heads = 2 * pairs
    i = pl.program_id(2) if q_block is None else q_block
    lane = lax.broadcasted_iota(jnp.int32, (tq, LANES), 1)
    head_lanes = (lane < SB_HEAD_DIM, lane >= SB_HEAD_DIM)
    rowi = lax.broadcasted_iota(jnp.int32, (heads * tq, SB_BLOCK), 0)
    coli = lax.broadcasted_iota(jnp.int32, (heads * tq, SB_BLOCK), 1)
    causal = coli < (rowi & (tq - 1))

    def pair_lanes(p):
        return slice(p * LANES, (p + 1) * LANES)

    def q_pair(p):
        q2 = q_ref[:, pair_lanes(p)]
        zero = jnp.zeros_like(q2)
        return jnp.concatenate([jnp.where(hl, q2, zero) for hl in head_lanes], axis=0)

    def rows(ref, p, first_block, nblocks):
        start = first_block * SB_BLOCK
        if not isinstance(start, int):
            start = pl.multiple_of(start, SB_BLOCK)
        return ref[pl.ds(start, nblocks * SB_BLOCK), pair_lanes(p)]

    def suffix_sums(x):
        nb = x.shape[1] // SB_BLOCK
        hi, lo = _split_bf16(x)

        def mm(cols, m2):
            return _dot(jnp.concatenate([hi[:, cols], lo[:, cols]], axis=1), m2)

        within, total = [], []
        if nb >= 3:
            for j in range(0, nb - 1, 2):
                y = mm(slice(j * SB_BLOCK, (j + 2) * SB_BLOCK), upair_ref[...])
                within += [y[:, :SB_BLOCK], y[:, SB_BLOCK:]]
                total += [jnp.sum(x[:, b * SB_BLOCK:(b + 1) * SB_BLOCK], axis=1, keepdims=True) for b in (j, j + 1)]
        for j in range(len(within), nb):
            y = mm(slice(j * SB_BLOCK, (j + 1) * SB_BLOCK), uo_ref[...])
            within.append(y[:, :SB_BLOCK])
            total.append(y[:, SB_BLOCK:])
        later = total[nb - 1]
        out = [within[nb - 1]]
        for j in range(nb - 2, -1, -1):
            out.insert(0, within[j] + later)
            later = later + total[j]
        return (out[0] if nb == 1 else jnp.concatenate(out, axis=1)), later

    def resident(first_block, nblocks, kv_refs=(k_ref, v_ref)):
        return tuple((lambda p, ref=ref: rows(ref, p, first_block, nblocks)) for ref in kv_refs)

    def sweep(kv, nblocks, own, carry, valid=None):
        nk = nblocks * SB_BLOCK

        def masked(x):
            if valid is not None:
                return jnp.where(valid, x, 0.0)
            if not own:
                return x
            tail = jnp.where(causal, x[:, nk - SB_BLOCK:], 0.0)
            return tail if nblocks == 1 else jnp.concatenate([x[:, :nk - SB_BLOCK], tail], axis=1)

        z = jnp.concatenate([_dot_nt(q_pair(p), kv[0](p)) for p in range(pairs)], axis=0)
        fail, log_beta = _sb_log2_terms(z)
        after, total = suffix_sums(masked(fail))
        if carry is not None:
            after = after + carry
        w = masked(jnp.exp2(log_beta - after)).astype(BF16)
        pv = jnp.concatenate([_dot(w[2 * p * tq:(2 * p + 2) * tq], kv[1](p))
                              for p in range(pairs)], axis=0)
        if carry is None:
            carry_ref[...] = total
            acc_ref[...] = pv
        else:
            carry_ref[...] = carry + total
            acc_ref[...] += pv

    def live():
        return jnp.min(carry_ref[...]) < SB_STOP

    def mark_live():
        live_ref[0] = jnp.where(live(), 1, 0)

    half = SB_BLOCK // 2

    def by_halves(x, f):
        y = [f(s, jnp.concatenate([x[s * half:(s + 1) * half], x[tq + s * half:tq + (s + 1) * half]], axis=0))
             for s in range(2)]
        return jnp.concatenate([y[0][:half], y[1][:half], y[0][half:], y[1][half:]], axis=0)

    def half_windows():
        starts = [pl.multiple_of(i * SB_BLOCK - (3 - s) * half, half) for s in range(2)]
        nk = 2 * SB_BLOCK
        keep = coli < (rowi & (half - 1)) + half

        def masked(x):
            return jnp.concatenate([x[:, :SB_BLOCK], jnp.where(keep, x[:, SB_BLOCK:], 0.0)], axis=1)

        z = jnp.concatenate(
            [by_halves(q_pair(p), lambda s, lhs: _dot_nt(lhs, k_ref[pl.ds(starts[s], nk), pair_lanes(p)]))
             for p in range(pairs)], axis=0)
        fail, log_beta = _sb_log2_terms(z)
        after, total = suffix_sums(masked(fail))
        w = masked(jnp.exp2(log_beta - after)).astype(BF16)
        carry_ref[...] = total
        acc_ref[...] = jnp.concatenate(
            [by_halves(w[2 * p * tq:(2 * p + 2) * tq],
                       lambda s, lhs: _dot(lhs, v_ref[pl.ds(starts[s], nk), pair_lanes(p)]))
             for p in range(pairs)], axis=0)

    def own_block_only():
        sweep(resident(i, 1), 1, True, None)

    if q_block is None:
        assert tq == SB_BLOCK
        windowed = i >= 2
        pl.when(windowed)(half_windows)
        pl.when(jnp.logical_not(windowed))(own_block_only)

        mark_live()

        @pl.when(jnp.logical_and(windowed, live_ref[0] != 0))
        def _():
            unseen = jnp.logical_or(coli < half, (rowi & half) != 0)
            sweep(resident(i - 2, 1), 1, False, carry_ref[...], unseen)
            mark_live()

        n_prev = jnp.where(windowed, i - 2, i)
        earlier = lambda kb: resident(kb, 1)
    else:
        recent = SB_WINDOW - 1
        assert q_block >= recent and k_ref.shape[0] == tq <= SB_BLOCK

        def earlier(kb):
            src = lambda hbm: hbm.at[pl.program_id(0), :, :, pl.ds(pl.multiple_of(kb * SB_BLOCK, SB_BLOCK), SB_BLOCK)]
            copies = [pltpu.make_async_copy(src(hbm), buf, sems.at[n])
                      for n, (hbm, buf) in enumerate(((kc_hbm, kbuf), (vc_hbm, vbuf)))]
            for c in copies:
                c.start()
            for c in copies:
                c.wait()
            slab = lambda buf: (lambda p: jnp.concatenate([buf[2 * p].T, buf[2 * p + 1].T], axis=1).astype(BF16))
            return slab(kbuf), slab(vbuf)

        for j in range(recent):
            for slab, win in zip(earlier(q_block - recent + j), (kwin, vwin)):
                for p in range(pairs):
                    win[j * SB_BLOCK:(j + 1) * SB_BLOCK, pair_lanes(p)] = slab(p)
        for new, win in ((k_ref, kwin), (v_ref, vwin)):
            win[recent * SB_BLOCK:recent * SB_BLOCK + tq, :] = new[...]
            win[recent * SB_BLOCK + tq:, :] = jnp.zeros((SB_BLOCK - tq, win.shape[1]), BF16)
        sweep(resident(0, SB_WINDOW, (kwin, vwin)), SB_WINDOW, True, None)
        mark_live()
        n_prev = q_block - recent

    def body(st):
        kb, _ = st
        sweep(earlier(kb), 1, False, carry_ref[...])
        return kb - 1, live()

    lax.while_loop(lambda st: jnp.logical_and(st[0] >= 0, st[1]), body, (n_prev - 1, live_ref[0] != 0))
    for p in range(pairs):
        o_ref[:, pair_lanes(p)] = jnp.where(head_lanes[0], acc_ref[2 * p * tq:(2 * p + 1) * tq],
                                            acc_ref[(2 * p + 1) * tq:(2 * p + 2) * tq])


def _sb(q16, k16, v16, upair, uo, *, batch, tq, keys, cache=None):
    nq = keys // SB_BLOCK if cache is None else 1
    width = SB_PAIRS_PER_STEP * LANES
    assert width == SB_WIDTH or cache is None
    kv_spec = pl.BlockSpec((keys, width), lambda b, p, i: (b, p))
    q_spec = pl.BlockSpec((tq, width), lambda b, p, i: (b * nq + i, p))
    heads = 2 * SB_PAIRS_PER_STEP
    in_specs = [q_spec, kv_spec, kv_spec, _const_spec(upair.shape), _const_spec(uo.shape)]
    scratch = [pltpu.VMEM((heads * tq, SB_BLOCK), F32), pltpu.VMEM((heads * tq, LANES), F32),
               pltpu.SMEM((1,), jnp.int32)]
    operands = [q16, k16, v16, upair, uo]
    q_block = None
    if cache is not None:
        past = cache[0].shape[3]
        assert past % SB_BLOCK == 0 and keys == tq and SB_WINDOW * SB_BLOCK <= past + SB_BLOCK
        q_block = past // SB_BLOCK
        in_specs += [pl.BlockSpec(memory_space=pl.ANY)] * 2
        scratch += ([pltpu.VMEM((SB_HEADS, SB_HEAD_DIM, SB_BLOCK), F32)] * 2 + [pltpu.SemaphoreType.DMA((2,))]
                    + [pltpu.VMEM((SB_WINDOW * SB_BLOCK, SB_WIDTH), BF16)] * 2)
        operands += list(cache)
    return pl.pallas_call(
        functools.partial(_sb_kernel, q_block=q_block),
        grid=(batch, SB_WIDTH // width, nq),
        in_specs=in_specs,
        out_specs=q_spec,
        out_shape=jax.ShapeDtypeStruct((batch * nq * tq, SB_WIDTH), F32),
        scratch_shapes=scratch,
        compiler_params=_params(3),
        name="sb_prompt" if cache is None else "sb_sample",
    )(*operands)


def _gla_levels(c):
    return int(np.log2(c))


def _gla_tables(c):
    nlev = _gla_levels(c)
    t = np.arange(c)[:, None]
    j = np.arange(c)[None, :]
    mats = [(j <= t), (j > t)]
    for lev in range(1, nlev + 1):
        m = 1 << (lev - 1)
        start = (t // (2 * m)) * (2 * m)
        mats.append(((t % (2 * m)) >= m) & (j >= start + m) & (j <= t))
    for lev in range(1, nlev + 1):
        m = 1 << (lev - 1)
        start = (t // (2 * m)) * (2 * m)
        mats.append(((t % (2 * m)) < m) & (j > t) & (j <= start + m - 1))
    mats = np.concatenate([a.astype(np.float32) for a in mats], axis=0)
    x = t ^ j
    level = np.where(j > t, -1, np.where(x == 0, 0, np.floor(np.log2(np.maximum(x, 1))).astype(np.int64) + 1))
    return jnp.asarray(mats, BF16), jnp.asarray(level, jnp.int32)


def _gla_kernel(q_ref, k_ref, la_ref, v_ref, s0_ref, mats_ref, lv_ref, o_ref, sfin_ref, st_ref):
    c = GLA_CHUNK
    nlev = _gla_levels(c)
    streams, t_rows = q_ref.shape[:2]
    step = pl.program_id(1)

    def chunk(x):
        return x if t_rows == c else jnp.concatenate([x, jnp.zeros((c - t_rows, x.shape[1]), x.dtype)], axis=0)

    @pl.when(step == 0)
    def _():
        st_ref[...] = s0_ref[...]

    lane = lax.broadcasted_iota(jnp.int32, (c, LANES), 1)
    head_lanes = (lane < GLA_HEAD_K, lane >= GLA_HEAD_K)
    lane_sq = lax.broadcasted_iota(jnp.int32, (GLA_HEAD_V, LANES), 1) < GLA_HEAD_K

    def pick(x, hl):
        return jnp.where(hl, x, jnp.zeros_like(x))

    def stack_heads(x):
        return jnp.concatenate([pick(x, hl) for hl in head_lanes], axis=0)

    def finish(s, p, scores, qe, ke, b_last):
        sl = slice(p * LANES, (p + 1) * LANES)
        st = st_ref[s, p]
        inter = _dot_nt(stack_heads(qe[:, sl]), st.astype(BF16))
        upd = []
        for r in range(2):
            hd = 2 * p + r
            vh = chunk(v_ref[s, :, hd * GLA_HEAD_V:(hd + 1) * GLA_HEAD_V])
            o_ref[s, :, hd * GLA_HEAD_V:(hd + 1) * GLA_HEAD_V] = (
                inter[r * c:(r + 1) * c] + _dot(scores[r * c:(r + 1) * c], vh))[:t_rows]
            upd.append(_dot_tn(vh, ke[:, sl]))
        st_ref[s, p] = st * jnp.exp(b_last[:, sl]) + jnp.where(lane_sq, upd[0], upd[1])

    la_all = jnp.concatenate([chunk(la_ref[s]) for s in range(streams)], axis=1)
    b_all = _dot_hilo_left(mats_ref[:c, :], la_all)
    mild = jnp.min(b_all[c - 1:c, :]) >= GLA_DIRECT_MIN

    @pl.when(mild)
    def _():
        rowi = lax.broadcasted_iota(jnp.int32, (2 * c, c), 0) & (c - 1)
        causal = lax.broadcasted_iota(jnp.int32, (2 * c, c), 1) <= rowi
        for s in range(streams):
            b = b_all[:, s * GLA_KEY_WIDTH:(s + 1) * GLA_KEY_WIDTH]
            b_last = b[c - 1:c, :]
            q = chunk(q_ref[s])
            k = chunk(k_ref[s])
            qe = (q * jnp.exp(b)).astype(BF16)
            kd = (k * jnp.exp(-b)).astype(BF16)
            ke = (k * jnp.exp(b_last - b)).astype(BF16)
            for p in range(GLA_HEADS // 2):
                sl = slice(p * LANES, (p + 1) * LANES)
                scores = jnp.where(causal, _dot_nt(stack_heads(qe[:, sl]), kd[:, sl]), 0.0)
                finish(s, p, scores.astype(BF16), qe, ke, b_last)

    @pl.when(jnp.logical_not(mild))
    def _():
        lv = jnp.concatenate([lv_ref[...], lv_ref[...]], axis=0)
        for s in range(streams):
            q = chunk(q_ref[s])
            k = chunk(k_ref[s])
            sums = _dot_hilo_left(mats_ref[...], chunk(la_ref[s]))

            def rows(i):
                return sums[i * c:(i + 1) * c]

            qe = (q * jnp.exp(rows(0))).astype(BF16)
            ke = (k * jnp.exp(rows(1))).astype(BF16)
            q_lev = [q.astype(BF16)] + [(q * jnp.exp(rows(1 + lev))).astype(BF16) for lev in range(1, nlev + 1)]
            k_lev = [k.astype(BF16)] + [(k * jnp.exp(rows(1 + nlev + lev))).astype(BF16)
                                        for lev in range(1, nlev + 1)]
            for p in range(GLA_HEADS // 2):
                sl = slice(p * LANES, (p + 1) * LANES)
                scores = jnp.zeros((2 * c, c), F32)
                for lev in range(nlev + 1):
                    scores = jnp.where(lv == lev, _dot_nt(stack_heads(q_lev[lev][:, sl]), k_lev[lev][:, sl]),
                                       scores)
                finish(s, p, scores.astype(BF16), qe, ke, rows(0)[c - 1:c, :])

    sfin_ref[...] = st_ref[...]


def _dot_hilo_left(m, x):
    hi, lo = _split_bf16(x)
    return _dot(m, hi) + _dot(m, lo)


def _gla(qg, kg, la, vg16, s0, mats, lv, *, batch, seq):
    c = min(GLA_CHUNK, seq)
    assert seq % c == 0
    g = GLA_BATCH_PER_STEP
    pairs = GLA_HEADS // 2
    blk = lambda width: pl.BlockSpec((g, c, width), lambda b, t: (b, t, 0))
    st_spec = pl.BlockSpec((g, pairs, GLA_HEAD_V, LANES), lambda b, t: (b, 0, 0, 0))
    per_stream = lambda a: a.reshape(batch, seq, a.shape[-1])
    o, s_fin = pl.pallas_call(
        _gla_kernel,
        grid=(batch // g, seq // c),
        in_specs=[blk(GLA_KEY_WIDTH), blk(GLA_KEY_WIDTH), blk(GLA_KEY_WIDTH), blk(GLA_WIDTH), st_spec,
                  _const_spec(mats.shape), _const_spec(lv.shape)],
        out_specs=[blk(GLA_WIDTH), st_spec],
        out_shape=[jax.ShapeDtypeStruct((batch, seq, GLA_WIDTH), F32),
                   jax.ShapeDtypeStruct((batch, pairs, GLA_HEAD_V, LANES), F32)],
        scratch_shapes=[pltpu.VMEM((g, pairs, GLA_HEAD_V, LANES), F32)],
        compiler_params=_params(2),
        name="gla",
    )(per_stream(qg), per_stream(kg), per_stream(la), per_stream(vg16), s0, mats, lv)
    return o.reshape(batch * seq, GLA_WIDTH), s_fin


def _head_norm_lanes(y, head, gain):
    tm, width = y.shape
    low = lax.broadcasted_iota(jnp.int32, (tm, LANES), 1) < head
    cols = []
    for j in range(width // LANES):
        yj = y[:, j * LANES:(j + 1) * LANES]
        sq = yj * yj

        def inv_rms(part):
            return lax.rsqrt(jnp.sum(part, axis=1, keepdims=True) * (1.0 / head) + EPS)

        if head == LANES:
            cols.append(yj * inv_rms(sq))
        else:
            cols.append(yj * jnp.where(low, inv_rms(jnp.where(low, sq, 0.0)), inv_rms(jnp.where(low, 0.0, sq))))
    return jnp.concatenate(cols, axis=1) * gain


def _mixout_kernel(x_ref, osb_ref, og_ref, r_ref, gsb_ref, ggla_ref, w_ref,
                   g_ref, wg_ref, wu_ref, wd_ref, gfin_ref, o_ref, a_ref):
    o_sb = _head_norm_lanes(osb_ref[...], SB_HEAD_DIM, gsb_ref[...])
    r = r_ref[...]
    o_g = _head_norm_lanes(og_ref[...], GLA_HEAD_V, ggla_ref[...]) * (r * jax.nn.sigmoid(r))
    mix = _dot(o_sb.astype(BF16), w_ref[:SB_WIDTH, :]) + _dot(o_g.astype(BF16), w_ref[SB_WIDTH:, :])
    y = _half_step_ffn(x_ref[...] + mix, g_ref, wg_ref, wu_ref, wd_ref, a_ref)
    o_ref[...] = _rms(y, gfin_ref[...])


def _mixout(x, osb, og, r, gsb, ggla, w, g, wg, wu, wd, gfin, *, tm):
    n = x.shape[0]
    row = lambda i: (i, 0)
    return pl.pallas_call(
        _mixout_kernel,
        grid=(n // tm,),
        in_specs=[pl.BlockSpec((tm, D_MODEL), row),
                  pl.BlockSpec((tm, SB_WIDTH), row),
                  pl.BlockSpec((tm, GLA_WIDTH), row),
                  pl.BlockSpec((tm, GLA_WIDTH), row),
                  _const_spec((1, SB_WIDTH)),
                  _const_spec((1, GLA_WIDTH)),
                  _const_spec((D_MODEL, D_MODEL))] + _ffn_specs() + [_const_spec((1, D_MODEL))],
        out_specs=pl.BlockSpec((tm, D_MODEL), row),
        out_shape=jax.ShapeDtypeStruct((n, D_MODEL), F32),
        scratch_shapes=[pltpu.VMEM((tm, D_FF), BF16)],
        compiler_params=_params(1),
        name="mixer_out_ffn2",
    )(x, osb, og, r, gsb, ggla, w, g, wg, wu, wd, gfin)


def _group_ones(width, group):
    idx = np.arange(width) // group
    return jnp.asarray(idx[:, None] == idx[None, :], BF16)


def _suffix_matrices():
    j = np.arange(SB_BLOCK)[:, None]
    s = np.arange(SB_BLOCK)[None, :]
    later = j > s
    zero = np.zeros_like(later)
    pair = np.block([[later, zero], [zero, later]])
    single = np.concatenate([later, np.ones_like(later)], axis=1)
    twice = lambda m: jnp.asarray(np.concatenate([m, m], axis=0), BF16)
    return twice(pair), twice(single)


def _state_to_kernel(s):
    b = s.shape[0]
    s = s.reshape(b, GLA_HEADS // 2, 2, GLA_HEAD_K, GLA_HEAD_V)
    return s.transpose(0, 1, 4, 2, 3).reshape(b, GLA_HEADS // 2, GLA_HEAD_V, LANES)


def _state_from_kernel(s):
    b = s.shape[0]
    s = s.reshape(b, GLA_HEADS // 2, GLA_HEAD_V, 2, GLA_HEAD_K)
    return s.transpose(0, 1, 3, 4, 2).reshape(b, GLA_HEADS, GLA_HEAD_K, GLA_HEAD_V)


def _layer(x1, lw, consts, *, batch, seq, tm, cache=None, state=None):
    (g_mix, w_in, wgu, bg, gq, gk, gsb, ggla, w_out, g2, w2g, w2u, w2d, gfin) = lw
    grp64, upair, uo, mats, lv = consts
    (qsb16, ksb, ksb16, vsb, vsb16, qg, kg, vg16, r, la) = _mixin(x1, g_mix, w_in, wgu, bg, gq, gk, grp64, tm=tm)

    if cache is None:
        o_sb = _sb(qsb16, ksb16, vsb16, upair, uo, batch=batch, tq=SB_BLOCK, keys=seq)
    else:
        o_sb = _sb(qsb16, ksb16, vsb16, upair, uo, batch=batch, tq=seq, keys=seq, cache=cache)

    if state is None:
        state = jnp.zeros((batch, GLA_HEADS // 2, GLA_HEAD_V, LANES), F32)
    o_g, s_fin = _gla(qg, kg, la, vg16, state, mats, lv, batch=batch, seq=seq)

    y = _mixout(x1, o_sb, o_g, r, gsb, ggla, w_out, g2, w2g, w2u, w2d, gfin, tm=tm)
    return y, ksb, vsb, _state_from_kernel(s_fin)


def kernel(x_prompt, x_sample, cache_sb_k, cache_sb_v, state_gla, g_ffn1, w_ffn1_gate, w_ffn1_up,
           w_ffn1_down, g_mix, w_in, w_gate_up, b_gate, g_q, g_k, g_sb_out, g_gla_out, w_out,
           g_ffn2, w_ffn2_gate, w_ffn2_up, w_ffn2_down, g_final):
    depth = w_in.shape[0]
    batch, seq, _ = x_prompt.shape
    dec_batch, dec_seq, _ = x_sample.shape
    consts = (_group_ones(SB_WIDTH, SB_HEAD_DIM),
              *_suffix_matrices(), *_gla_tables(GLA_CHUNK))

    y_p = x_prompt.reshape(batch * seq, D_MODEL)
    y_s = x_sample.reshape(dec_batch * dec_seq, D_MODEL)
    outs = [[] for _ in range(6)]
    for l in range(depth):
        row = lambda a: a[l].reshape(1, -1)
        ffn1 = (row(g_ffn1), w_ffn1_gate[l].astype(BF16), w_ffn1_up[l].astype(BF16), w_ffn1_down[l].astype(BF16))
        x1_p, (w2g, w2u, w2d, w_in16, w_out16) = _ffn(
            y_p, *ffn1, tm=512,
            cast=[(w_ffn2_gate[l], D_FF), (w_ffn2_up[l], D_FF), (w_ffn2_down[l], D_MODEL),
                  (w_in[l], IN_PAD), (w_out[l], D_MODEL)])
        x1_s, _ = _ffn(y_s, *ffn1, tm=256)
        lw = (row(g_mix), w_in16,
              jnp.pad(w_gate_up[l], ((0, GATE_PAD - GATE_RANK), (0, 0))).astype(BF16), row(b_gate),
              jnp.tile(g_q[l], SB_HEADS).reshape(1, -1), jnp.tile(g_k[l], SB_HEADS).reshape(1, -1),
              jnp.tile(g_sb_out[l], SB_HEADS).reshape(1, -1), jnp.tile(g_gla_out[l], GLA_HEADS).reshape(1, -1),
              w_out16, row(g_ffn2), w2g, w2u, w2d, row(g_final))
        y_p, k_p, v_p, s_p = _layer(x1_p, lw, consts, batch=batch, seq=seq, tm=512)
        cache = (cache_sb_k[l].transpose(0, 2, 3, 1), cache_sb_v[l].transpose(0, 2, 3, 1))
        y_s, k_s, v_s, s_s = _layer(x1_s, lw, consts, batch=dec_batch, seq=dec_seq, tm=256,
                                    cache=cache, state=_state_to_kernel(state_gla[l]))
        shape_p = (batch, seq, SB_HEADS, SB_HEAD_DIM)
        shape_s = (dec_batch, dec_seq, SB_HEADS, SB_HEAD_DIM)
        for lst, val in zip(outs, (k_p.reshape(shape_p), v_p.reshape(shape_p), s_p,
                                   k_s.reshape(shape_s), v_s.reshape(shape_s), s_s)):
            lst.append(val)
    return (y_p.reshape(batch, seq, D_MODEL), y_s.reshape(dec_batch, dec_seq, D_MODEL),
            *[jnp.stack(o) for o in outs])
```

```python
import functools

import numpy as np
import jax
import jax.numpy as jnp
from jax import lax
from jax.experimental import pallas as pl
from jax.experimental.pallas import tpu as pltpu

F32 = jnp.float32
BF16 = jnp.bfloat16

D_MODEL = 1024
SB_HEADS = 8
SB_HEAD_DIM = 64
SB_WIDTH = SB_HEADS * SB_HEAD_DIM
SB_SCALE = SB_HEAD_DIM ** -0.5
GLA_HEADS = 4
GLA_HEAD_K = 64
GLA_HEAD_V = 128
GLA_KEY_WIDTH = GLA_HEADS * GLA_HEAD_K
GLA_WIDTH = GLA_HEADS * GLA_HEAD_V
GLA_SCALE = GLA_HEAD_K ** -0.5
GATE_RANK = 16
GATE_TAU = 16.0
D_FF = 2816
EPS = 1e-6

LANES = 128
SUBLANES = 8
BF16_SUBLANES = 16
GATE_PAD = LANES
MAIN_WIDTH = 3 * SB_WIDTH + 2 * GLA_KEY_WIDTH + 2 * GLA_WIDTH
IN_PAD = MAIN_WIDTH + GATE_PAD

FF_CHUNK = 256
SB_BLOCK = 128
SB_WINDOW = 3
SB_PAIRS_PER_STEP = 4
SB_QBLOCKS_PER_STEP = 2
GLA_CHUNK = 128
GLA_BATCH_PER_STEP = 2
GLA_DIRECT_MIN = -40.0
SB_STOP = 152.0
LOG2_E = 1.4426950408889634

VMEM_LIMIT = 56 * 1024 * 1024


def _dot(a, b):
    return jnp.dot(a, b, preferred_element_type=F32)


def _dot_nt(a, b):
    return lax.dot_general(a, b, (((1,), (1,)), ((), ())), preferred_element_type=F32)


def _dot_tn(a, b):
    return lax.dot_general(a, b, (((0,), (0,)), ((), ())), preferred_element_type=F32)


def _split_bf16(x):
    hi = x.astype(BF16)
    lo = (x - hi.astype(F32)).astype(BF16)
    return hi, lo


def _rms(x, g):
    ms = jnp.mean(x * x, axis=-1, keepdims=True)
    return x * lax.rsqrt(ms + EPS) * g


def _log_sigmoid(x):
    return jnp.minimum(x, 0.0) - jnp.log1p(jnp.exp(-jnp.abs(x)))


def _const_spec(shape):
    nd = len(shape)
    return pl.BlockSpec(shape, lambda *_: (0,) * nd, pipeline_mode=pl.Buffered(1))


def _params(n_grid):
    return pltpu.CompilerParams(dimension_semantics=("arbitrary",) * n_grid,
                                vmem_limit_bytes=VMEM_LIMIT)


def _half_step_ffn(x, g_ref, wg_ref, wu_ref, wd_ref, a_ref):
    h = _rms(x, g_ref[...]).astype(BF16)
    for c in range(D_FF // FF_CHUNK):
        sl = slice(c * FF_CHUNK, (c + 1) * FF_CHUNK)
        gt = _dot(h, wg_ref[:, sl])
        up = _dot(h, wu_ref[:, sl])
        a_ref[:, sl] = (gt * jax.nn.sigmoid(gt) * up).astype(BF16)
    return x + 0.5 * _dot(a_ref[...], wd_ref[...])


def _ffn_kernel(x_ref, g_ref, wg_ref, wu_ref, wd_ref, *refs):
    n_cast = (len(refs) - 2) // 2
    o_ref, a_ref = refs[n_cast], refs[-1]
    o_ref[...] = _half_step_ffn(x_ref[...], g_ref, wg_ref, wu_ref, wd_ref, a_ref)
    for src, dst in zip(refs[:n_cast], refs[n_cast + 1:-1]):
        keep = min(src.shape[1], dst.shape[1]) // LANES * LANES
        dst[:, :keep] = src[:, :keep].astype(BF16)
        if keep < dst.shape[1]:
            dst[:, keep:] = jnp.zeros((dst.shape[0], dst.shape[1] - keep), BF16)
            dst[:, keep:src.shape[1]] = src[:, keep:].astype(BF16)


def _ffn_specs():
    return [_const_spec((1, D_MODEL)), _const_spec((D_MODEL, D_FF)), _const_spec((D_MODEL, D_FF)),
            _const_spec((D_FF, D_MODEL))]


def _cast_block_rows(rows, steps):
    br = next(b for b in range(BF16_SUBLANES, rows + 1, BF16_SUBLANES) if rows % b == 0 and rows // b <= steps)
    return br, rows // br


def _ffn(x, g, wg, wu, wd, *, tm, cast=()):
    n = x.shape[0]
    steps = n // tm
    row = lambda i: (i, 0)
    cast_in, cast_out, cast_shape = [], [], []
    for w, width in cast:
        br, nblk = _cast_block_rows(w.shape[0], steps)
        idx = lambda i, last=nblk - 1: (jnp.minimum(i, last), 0)
        cast_in.append(pl.BlockSpec((br, w.shape[1]), idx))
        cast_out.append(pl.BlockSpec((br, width), idx))
        cast_shape.append(jax.ShapeDtypeStruct((w.shape[0], width), BF16))
    out = pl.pallas_call(
        _ffn_kernel,
        grid=(steps,),
        in_specs=[pl.BlockSpec((tm, D_MODEL), row)] + _ffn_specs() + cast_in,
        out_specs=[pl.BlockSpec((tm, D_MODEL), row)] + cast_out,
        out_shape=[jax.ShapeDtypeStruct((n, D_MODEL), F32)] + cast_shape,
        scratch_shapes=[pltpu.VMEM((tm, D_FF), BF16)],
        compiler_params=_params(1),
        name="ffn1",
    )(x, g, wg, wu, wd, *[w for w, _ in cast])
    return out[0], out[1:]


def _head_norm_lanes(y, head, gain):
    tm, width = y.shape
    low = lax.broadcasted_iota(jnp.int32, (tm, LANES), 1) < head
    cols = []
    for j in range(width // LANES):
        yj = y[:, j * LANES:(j + 1) * LANES]
        sq = yj * yj

        def inv_rms(part):
            return lax.rsqrt(jnp.sum(part, axis=1, keepdims=True) * (1.0 / head) + EPS)

        if head == LANES:
            cols.append(yj * inv_rms(sq))
        else:
            cols.append(yj * jnp.where(low, inv_rms(jnp.where(low, sq, 0.0)), inv_rms(jnp.where(low, 0.0, sq))))
    return jnp.concatenate(cols, axis=1) * gain


def _store_heads(ref, y):
    groups = y.shape[0] // SUBLANES
    sub = lax.broadcasted_iota(jnp.int32, (groups, SUBLANES, LANES), 1)
    rows = []
    for j in range(SB_WIDTH // LANES):
        pair = y[:, j * LANES:(j + 1) * LANES].reshape(groups, SUBLANES, LANES)
        rows += [pair, pltpu.roll(pair, SB_HEAD_DIM, axis=2)]
    for dist in (4, 2, 1):
        low = (sub & dist) == 0
        new = list(rows)
        for h in range(SUBLANES):
            if h & dist == 0:
                a, b = rows[h], rows[h + dist]
                new[h] = jnp.where(low, a, pltpu.roll(b, dist, axis=1))
                new[h + dist] = jnp.where(low, pltpu.roll(a, SUBLANES - dist, axis=1), b)
        rows = new
    for t in range(SUBLANES):
        ref[:, t] = rows[t][:, :, :SB_HEAD_DIM]


def _mixin_kernel(x_ref, g_ref, w_ref, wgu_ref, bg_ref, gq_ref, gk_ref,
                  qsb_ref, ksb_ref, ksb16_ref, vsb_ref, vsb16_ref,
                  qg_ref, kg_ref, vg16_ref, r_ref, la_ref):
    h = _rms(x_ref[...], g_ref[...]).astype(BF16)

    def proj(lo, width):
        return _dot(h, w_ref[:, lo:lo + width])

    q = _head_norm_lanes(proj(0, SB_WIDTH), SB_HEAD_DIM, gq_ref[...])
    qsb_ref[...] = (q * SB_SCALE).astype(BF16)
    k = _head_norm_lanes(proj(SB_WIDTH, SB_WIDTH), SB_HEAD_DIM, gk_ref[...])
    _store_heads(ksb_ref, k)
    ksb16_ref[...] = k.astype(BF16)
    v = proj(2 * SB_WIDTH, SB_WIDTH)
    _store_heads(vsb_ref, v)
    vsb16_ref[...] = v.astype(BF16)
    off = 3 * SB_WIDTH
    qg_ref[...] = proj(off, GLA_KEY_WIDTH) * GLA_SCALE
    kg_ref[...] = proj(off + GLA_KEY_WIDTH, GLA_KEY_WIDTH)
    vg16_ref[...] = proj(off + 2 * GLA_KEY_WIDTH, GLA_WIDTH).astype(BF16)
    r_ref[...] = proj(off + 2 * GLA_KEY_WIDTH + GLA_WIDTH, GLA_WIDTH)
    lr = proj(MAIN_WIDTH, GATE_PAD).astype(BF16)
    gate = _dot(lr, wgu_ref[...]) + bg_ref[...]
    la_ref[...] = _log_sigmoid(gate) * (1.0 / GATE_TAU)


def _mixin(x, g, w, wgu, bg, gq, gk, *, tm):
    n = x.shape[0]
    row = lambda i: (i, 0)
    heads = (SB_HEADS, SB_HEAD_DIM)
    assert SB_HEADS == SUBLANES and tm % SUBLANES == 0
    lead = lambda rows, wd: (rows // SUBLANES, SUBLANES) if len(wd) == 2 else (rows,)
    widths = [((SB_WIDTH,), BF16), (heads, F32), ((SB_WIDTH,), BF16), (heads, F32), ((SB_WIDTH,), BF16),
              ((GLA_KEY_WIDTH,), F32), ((GLA_KEY_WIDTH,), F32), ((GLA_WIDTH,), BF16), ((GLA_WIDTH,), F32),
              ((GLA_KEY_WIDTH,), F32)]
    return pl.pallas_call(
        _mixin_kernel,
        grid=(n // tm,),
        in_specs=[pl.BlockSpec((tm, D_MODEL), row),
                  _const_spec((1, D_MODEL)),
                  _const_spec((D_MODEL, IN_PAD)),
                  _const_spec((GATE_PAD, GLA_KEY_WIDTH)),
                  _const_spec((1, GLA_KEY_WIDTH)),
                  _const_spec((1, SB_WIDTH)),
                  _const_spec((1, SB_WIDTH))],
        out_specs=[pl.BlockSpec(lead(tm, wd) + wd, lambda i, nd=len(lead(tm, wd) + wd) - 1: (i,) + (0,) * nd)
                   for wd, _ in widths],
        out_shape=[jax.ShapeDtypeStruct(lead(n, wd) + wd, dt) for wd, dt in widths],
        compiler_params=_params(1),
        name="mixer_in",
    )(x, g, w, wgu, bg, gq, gk)


def _neg_abs(x):
    return -jnp.abs(x)


def _sb_log2_terms(z):
    z2 = z * LOG2_E
    t = jnp.log2(1.0 + jnp.exp2(_neg_abs(z2)))
    return jnp.maximum(z2, 0.0) + t, jnp.minimum(z2, 0.0) - t


def _sb_kernel(q_ref, k_ref, v_ref, upair_ref, uo_ref, *refs, tq, q_block):
    if q_block is None:
        o_ref, carry_ref, acc_ref, live_ref = refs
    else:
        kc_hbm, vc_hbm, o_ref, carry_ref, acc_ref, live_ref, kbuf, vbuf, sems, kwin, vwin = refs
    pairs = q_ref.shape[1] // LANES
    heads = 2 * pairs

    def one_block(i, row0):
        lane = lax.broadcasted_iota(jnp.int32, (tq, LANES), 1)
        head_lanes = (lane < SB_HEAD_DIM, lane >= SB_HEAD_DIM)
        rowi = lax.broadcasted_iota(jnp.int32, (heads * tq, SB_BLOCK), 0)
        coli = lax.broadcasted_iota(jnp.int32, (heads * tq, SB_BLOCK), 1)
        causal = coli < (rowi & (tq - 1))

        def pair_lanes(p):
            return slice(p * LANES, (p + 1) * LANES)

        def q_pair(p):
            q2 = q_ref[row0:row0 + tq, pair_lanes(p)]
            zero = jnp.zeros_like(q2)
            return jnp.concatenate([jnp.where(hl, q2, zero) for hl in head_lanes], axis=0)

        def rows(ref, p, first_block, nblocks):
            start = first_block * SB_BLOCK
            if not isinstance(start, int):
                start = pl.multiple_of(start, SB_BLOCK)
            return ref[pl.ds(start, nblocks * SB_BLOCK), pair_lanes(p)]

        def suffix_sums(x):
            nb = x.shape[1] // SB_BLOCK
            hi, lo = _split_bf16(x)

            def mm(cols, m2):
                return _dot(jnp.concatenate([hi[:, cols], lo[:, cols]], axis=1), m2)

            within, total = [], []
            if nb >= 3:
                for j in range(0, nb - 1, 2):
                    y = mm(slice(j * SB_BLOCK, (j + 2) * SB_BLOCK), upair_ref[...])
                    within += [y[:, :SB_BLOCK], y[:, SB_BLOCK:]]
                    total += [jnp.sum(x[:, b * SB_BLOCK:(b + 1) * SB_BLOCK], axis=1, keepdims=True) for b in (j, j + 1)]
            for j in range(len(within), nb):
                y = mm(slice(j * SB_BLOCK, (j + 1) * SB_BLOCK), uo_ref[...])
                within.append(y[:, :SB_BLOCK])
                total.append(y[:, SB_BLOCK:])
            later = total[nb - 1]
            out = [within[nb - 1]]
            for j in range(nb - 2, -1, -1):
                out.insert(0, within[j] + later)
                later = later + total[j]
            return (out[0] if nb == 1 else jnp.concatenate(out, axis=1)), later

        def resident(first_block, nblocks, kv_refs=(k_ref, v_ref)):
            return tuple((lambda p, ref=ref: rows(ref, p, first_block, nblocks)) for ref in kv_refs)

        def sweep(kv, nblocks, own, carry, valid=None):
            nk = nblocks * SB_BLOCK

            def masked(x):
                if valid is not None:
                    return jnp.where(valid, x, 0.0)
                if not own:
                    return x
                tail = jnp.where(causal, x[:, nk - SB_BLOCK:], 0.0)
                return tail if nblocks == 1 else jnp.concatenate([x[:, :nk - SB_BLOCK], tail], axis=1)

            z = jnp.concatenate([_dot_nt(q_pair(p), kv[0](p)) for p in range(pairs)], axis=0)
            fail, log_beta = _sb_log2_terms(z)
            after, total = suffix_sums(masked(fail))
            if carry is not None:
                after = after + carry
            w = masked(jnp.exp2(log_beta - after)).astype(BF16)
            pv = jnp.concatenate([_dot(w[2 * p * tq:(2 * p + 2) * tq], kv[1](p))
                                  for p in range(pairs)], axis=0)
            if carry is None:
                carry_ref[...] = total
                acc_ref[...] = pv
            else:
                carry_ref[...] = carry + total
                acc_ref[...] += pv

        def live():
            return jnp.min(carry_ref[...]) < SB_STOP

        def mark_live():
            live_ref[0] = jnp.where(live(), 1, 0)

        half = SB_BLOCK // 2

        def by_halves(x, f):
            y = [f(s, jnp.concatenate([x[s * half:(s + 1) * half], x[tq + s * half:tq + (s + 1) * half]], axis=0))
                 for s in range(2)]
            return jnp.concatenate([y[0][:half], y[1][:half], y[0][half:], y[1][half:]], axis=0)

        def half_windows():
            starts = [pl.multiple_of(i * SB_BLOCK - (3 - s) * half, half) for s in range(2)]
            nk = 2 * SB_BLOCK
            keep = coli < (rowi & (half - 1)) + half

            def masked(x):
                return jnp.concatenate([x[:, :SB_BLOCK], jnp.where(keep, x[:, SB_BLOCK:], 0.0)], axis=1)

            z = jnp.concatenate(
                [by_halves(q_pair(p), lambda s, lhs: _dot_nt(lhs, k_ref[pl.ds(starts[s], nk), pair_lanes(p)]))
                 for p in range(pairs)], axis=0)
            fail, log_beta = _sb_log2_terms(z)
            after, total = suffix_sums(masked(fail))
            w = masked(jnp.exp2(log_beta - after)).astype(BF16)
            carry_ref[...] = total
            acc_ref[...] = jnp.concatenate(
                [by_halves(w[2 * p * tq:(2 * p + 2) * tq],
                           lambda s, lhs: _dot(lhs, v_ref[pl.ds(starts[s], nk), pair_lanes(p)]))
                 for p in range(pairs)], axis=0)

        def own_block_only():
            sweep(resident(i, 1), 1, True, None)

        if q_block is None:
            assert tq == SB_BLOCK
            windowed = i >= 2
            pl.when(windowed)(half_windows)
            pl.when(jnp.logical_not(windowed))(own_block_only)

            mark_live()

            @pl.when(jnp.logical_and(windowed, live_ref[0] != 0))
            def _():
                unseen = jnp.logical_or(coli < half, (rowi & half) != 0)
                sweep(resident(i - 2, 1), 1, False, carry_ref[...], unseen)
                mark_live()

            n_prev = jnp.where(windowed, i - 2, i)
            earlier = lambda kb: resident(kb, 1)
        else:
            recent = SB_WINDOW - 1
            assert q_block >= recent and k_ref.shape[0] == tq <= SB_BLOCK

            def fetch(first_block, nblocks):
                nk = nblocks * SB_BLOCK
                start = first_block * SB_BLOCK
                if not isinstance(start, int):
                    start = pl.multiple_of(start, SB_BLOCK)
                copies = [pltpu.make_async_copy(hbm.at[pl.program_id(0), :, :, pl.ds(start, nk)],
                                                buf.at[:, :, pl.ds(0, nk)], sems.at[n])
                          for n, (hbm, buf) in enumerate(((kc_hbm, kbuf), (vc_hbm, vbuf)))]
                for c in copies:
                    c.start()
                for c in copies:
                    c.wait()

                def slabs(buf):
                    piece = lambda hd, j: buf[hd, :, j * SB_BLOCK:(j + 1) * SB_BLOCK].T
                    return lambda j, p: jnp.concatenate([piece(2 * p, j), piece(2 * p + 1, j)], axis=1).astype(BF16)

                return slabs(kbuf), slabs(vbuf)

            def earlier(kb):
                return tuple((lambda p, slab=slab: slab(0, p)) for slab in fetch(kb, 1))

            for slab, win in zip(fetch(q_block - recent, recent), (kwin, vwin)):
                for j in range(recent):
                    for p in range(pairs):
                        win[j * SB_BLOCK:(j + 1) * SB_BLOCK, pair_lanes(p)] = slab(j, p)
            for new, win in ((k_ref, kwin), (v_ref, vwin)):
                win[recent * SB_BLOCK:recent * SB_BLOCK + tq, :] = new[...]
                win[recent * SB_BLOCK + tq:, :] = jnp.zeros((SB_BLOCK - tq, win.shape[1]), BF16)
            sweep(resident(0, SB_WINDOW, (kwin, vwin)), SB_WINDOW, True, None)
            mark_live()
            n_prev = q_block - recent

        def body(st):
            kb, _ = st
            sweep(earlier(kb), 1, False, carry_ref[...])
            return kb - 1, live()

        lax.while_loop(lambda st: jnp.logical_and(st[0] >= 0, st[1]), body, (n_prev - 1, live_ref[0] != 0))
        for p in range(pairs):
            o_ref[row0:row0 + tq, pair_lanes(p)] = jnp.where(head_lanes[0], acc_ref[2 * p * tq:(2 * p + 1) * tq],
                                                              acc_ref[(2 * p + 1) * tq:(2 * p + 2) * tq])

    if q_block is None:
        per_step = q_ref.shape[0] // tq
        for u in range(per_step):
            one_block(pl.program_id(2) * per_step + u, u * tq)
    else:
        one_block(q_block, 0)


def _sb(q16, k16, v16, upair, uo, *, batch, tq, keys, cache=None):
    sub = SB_QBLOCKS_PER_STEP if cache is None else 1
    nq = keys // (sub * SB_BLOCK) if cache is None else 1
    width = SB_PAIRS_PER_STEP * LANES
    assert width == SB_WIDTH or cache is None
    kv_spec = pl.BlockSpec((keys, width), lambda b, p, i: (b, p))
    q_spec = pl.BlockSpec((sub * tq, width), lambda b, p, i: (b * nq + i, p))
    heads = 2 * SB_PAIRS_PER_STEP
    in_specs = [q_spec, kv_spec, kv_spec, _const_spec(upair.shape), _const_spec(uo.shape)]
    scratch = [pltpu.VMEM((heads * tq, SB_BLOCK), F32), pltpu.VMEM((heads * tq, LANES), F32),
               pltpu.SMEM((1,), jnp.int32)]
    operands = [q16, k16, v16, upair, uo]
    q_block = None
    if cache is not None:
        past = cache[0].shape[3]
        assert past % SB_BLOCK == 0 and keys == tq and SB_WINDOW * SB_BLOCK <= past + SB_BLOCK
        q_block = past // SB_BLOCK
        in_specs += [pl.BlockSpec(memory_space=pl.ANY)] * 2
        scratch += ([pltpu.VMEM((SB_HEADS, SB_HEAD_DIM, (SB_WINDOW - 1) * SB_BLOCK), F32)] * 2
                    + [pltpu.SemaphoreType.DMA((2,))]
                    + [pltpu.VMEM((SB_WINDOW * SB_BLOCK, SB_WIDTH), BF16)] * 2)
        operands += list(cache)
    return pl.pallas_call(
        functools.partial(_sb_kernel, tq=tq, q_block=q_block),
        grid=(batch, SB_WIDTH // width, nq),
        in_specs=in_specs,
        out_specs=q_spec,
        out_shape=jax.ShapeDtypeStruct((batch * nq * sub * tq, SB_WIDTH), F32),
        scratch_shapes=scratch,
        compiler_params=_params(3),
        name="sb_prompt" if cache is None else "sb_sample",
    )(*operands)


def _gla_levels(c):
    return int(np.log2(c))


def _gla_tables(c):
    nlev = _gla_levels(c)
    t = np.arange(c)[:, None]
    j = np.arange(c)[None, :]
    mats = [(j <= t), (j > t)]
    for lev in range(1, nlev + 1):
        m = 1 << (lev - 1)
        start = (t // (2 * m)) * (2 * m)
        mats.append(((t % (2 * m)) >= m) & (j >= start + m) & (j <= t))
    for lev in range(1, nlev + 1):
        m = 1 << (lev - 1)
        start = (t // (2 * m)) * (2 * m)
        mats.append(((t % (2 * m)) < m) & (j > t) & (j <= start + m - 1))
    mats = np.concatenate([a.astype(np.float32) for a in mats], axis=0)
    x = t ^ j
    level = np.where(j > t, -1, np.where(x == 0, 0, np.floor(np.log2(np.maximum(x, 1))).astype(np.int64) + 1))
    return jnp.asarray(mats, BF16), jnp.asarray(level, jnp.int32)


def _gla_kernel(q_ref, k_ref, la_ref, v_ref, s0_ref, mats_ref, lv_ref, o_ref, sfin_ref, st_ref):
    c = GLA_CHUNK
    nlev = _gla_levels(c)
    streams, t_rows = q_ref.shape[:2]
    step = pl.program_id(1)

    def chunk(x):
        return x if t_rows == c else jnp.concatenate([x, jnp.zeros((c - t_rows, x.shape[1]), x.dtype)], axis=0)

    @pl.when(step == 0)
    def _():
        st_ref[...] = s0_ref[...]

    lane = lax.broadcasted_iota(jnp.int32, (c, LANES), 1)
    head_lanes = (lane < GLA_HEAD_K, lane >= GLA_HEAD_K)
    lane_sq = lax.broadcasted_iota(jnp.int32, (GLA_HEAD_V, LANES), 1) < GLA_HEAD_K

    def pick(x, hl):
        return jnp.where(hl, x, jnp.zeros_like(x))

    def stack_heads(x):
        return jnp.concatenate([pick(x, hl) for hl in head_lanes], axis=0)

    def finish(s, p, scores, qe, ke, b_last):
        sl = slice(p * LANES, (p + 1) * LANES)
        st = st_ref[s, p]
        inter = _dot_nt(stack_heads(qe[:, sl]), st.astype(BF16))
        upd = []
        for r in range(2):
            hd = 2 * p + r
            vh = chunk(v_ref[s, :, hd * GLA_HEAD_V:(hd + 1) * GLA_HEAD_V])
            o_ref[s, :, hd * GLA_HEAD_V:(hd + 1) * GLA_HEAD_V] = (
                inter[r * c:(r + 1) * c] + _dot(scores[r * c:(r + 1) * c], vh))[:t_rows]
            upd.append(_dot_tn(vh, ke[:, sl]))
        st_ref[s, p] = st * jnp.exp(b_last[:, sl]) + jnp.where(lane_sq, upd[0], upd[1])

    la_all = jnp.concatenate([chunk(la_ref[s]) for s in range(streams)], axis=1)
    b_all = _dot_hilo_left(mats_ref[:c, :], la_all)
    mild = jnp.min(b_all[c - 1:c, :]) >= GLA_DIRECT_MIN

    @pl.when(mild)
    def _():
        rowi = lax.broadcasted_iota(jnp.int32, (2 * c, c), 0) & (c - 1)
        causal = lax.broadcasted_iota(jnp.int32, (2 * c, c), 1) <= rowi
        for s in range(streams):
            b = b_all[:, s * GLA_KEY_WIDTH:(s + 1) * GLA_KEY_WIDTH]
            b_last = b[c - 1:c, :]
            q = chunk(q_ref[s])
            k = chunk(k_ref[s])
            qe = (q * jnp.exp(b)).astype(BF16)
            kd = (k * jnp.exp(-b)).astype(BF16)
            ke = (k * jnp.exp(b_last - b)).astype(BF16)
            for p in range(GLA_HEADS // 2):
                sl = slice(p * LANES, (p + 1) * LANES)
                scores = jnp.where(causal, _dot_nt(stack_heads(qe[:, sl]), kd[:, sl]), 0.0)
                finish(s, p, scores.astype(BF16), qe, ke, b_last)

    @pl.when(jnp.logical_not(mild))
    def _():
        lv = jnp.concatenate([lv_ref[...], lv_ref[...]], axis=0)
        for s in range(streams):
            q = chunk(q_ref[s])
            k = chunk(k_ref[s])
            sums = _dot_hilo_left(mats_ref[...], chunk(la_ref[s]))

            def rows(i):
                return sums[i * c:(i + 1) * c]

            qe = (q * jnp.exp(rows(0))).astype(BF16)
            ke = (k * jnp.exp(rows(1))).astype(BF16)
            q_lev = [q.astype(BF16)] + [(q * jnp.exp(rows(1 + lev))).astype(BF16) for lev in range(1, nlev + 1)]
            k_lev = [k.astype(BF16)] + [(k * jnp.exp(rows(1 + nlev + lev))).astype(BF16)
                                        for lev in range(1, nlev + 1)]
            for p in range(GLA_HEADS // 2):
                sl = slice(p * LANES, (p + 1) * LANES)
                scores = jnp.zeros((2 * c, c), F32)
                for lev in range(nlev + 1):
                    scores = jnp.where(lv == lev, _dot_nt(stack_heads(q_lev[lev][:, sl]), k_lev[lev][:, sl]),
                                       scores)
                finish(s, p, scores.astype(BF16), qe, ke, rows(0)[c - 1:c, :])

    sfin_ref[...] = st_ref[...]


def _dot_hilo_left(m, x):
    hi, lo = _split_bf16(x)
    return _dot(m, hi) + _dot(m, lo)


def _gla(qg, kg, la, vg16, s0, mats, lv, *, batch, seq):
    c = min(GLA_CHUNK, seq)
    assert seq % c == 0
    g = GLA_BATCH_PER_STEP
    pairs = GLA_HEADS // 2
    blk = lambda width: pl.BlockSpec((g, c, width), lambda b, t: (b, t, 0))
    st_spec = pl.BlockSpec((g, pairs, GLA_HEAD_V, LANES), lambda b, t: (b, 0, 0, 0))
    per_stream = lambda a: a.reshape(batch, seq, a.shape[-1])
    o, s_fin = pl.pallas_call(
        _gla_kernel,
        grid=(batch // g, seq // c),
        in_specs=[blk(GLA_KEY_WIDTH), blk(GLA_KEY_WIDTH), blk(GLA_KEY_WIDTH), blk(GLA_WIDTH), st_spec,
                  _const_spec(mats.shape), _const_spec(lv.shape)],
        out_specs=[blk(GLA_WIDTH), st_spec],
        out_shape=[jax.ShapeDtypeStruct((batch, seq, GLA_WIDTH), F32),
                   jax.ShapeDtypeStruct((batch, pairs, GLA_HEAD_V, LANES), F32)],
        scratch_shapes=[pltpu.VMEM((g, pairs, GLA_HEAD_V, LANES), F32)],
        compiler_params=_params(2),
        name="gla",
    )(per_stream(qg), per_stream(kg), per_stream(la), per_stream(vg16), s0, mats, lv)
    return o.reshape(batch * seq, GLA_WIDTH), s_fin


def _mixout_kernel(x_ref, osb_ref, og_ref, r_ref, gsb_ref, ggla_ref, w_ref,
                   g_ref, wg_ref, wu_ref, wd_ref, gfin_ref, o_ref, a_ref):
    o_sb = _head_norm_lanes(osb_ref[...], SB_HEAD_DIM, gsb_ref[...])
    r = r_ref[...]
    o_g = _head_norm_lanes(og_ref[...], GLA_HEAD_V, ggla_ref[...]) * (r * jax.nn.sigmoid(r))
    mix = _dot(o_sb.astype(BF16), w_ref[:SB_WIDTH, :]) + _dot(o_g.astype(BF16), w_ref[SB_WIDTH:, :])
    y = _half_step_ffn(x_ref[...] + mix, g_ref, wg_ref, wu_ref, wd_ref, a_ref)
    o_ref[...] = _rms(y, gfin_ref[...])


def _mixout(x, osb, og, r, gsb, ggla, w, g, wg, wu, wd, gfin, *, tm):
    n = x.shape[0]
    row = lambda i: (i, 0)
    return pl.pallas_call(
        _mixout_kernel,
        grid=(n // tm,),
        in_specs=[pl.BlockSpec((tm, D_MODEL), row),
                  pl.BlockSpec((tm, SB_WIDTH), row),
                  pl.BlockSpec((tm, GLA_WIDTH), row),
                  pl.BlockSpec((tm, GLA_WIDTH), row),
                  _const_spec((1, SB_WIDTH)),
                  _const_spec((1, GLA_WIDTH)),
                  _const_spec((D_MODEL, D_MODEL))] + _ffn_specs() + [_const_spec((1, D_MODEL))],
        out_specs=pl.BlockSpec((tm, D_MODEL), row),
        out_shape=jax.ShapeDtypeStruct((n, D_MODEL), F32),
        scratch_shapes=[pltpu.VMEM((tm, D_FF), BF16)],
        compiler_params=_params(1),
        name="mixer_out_ffn2",
    )(x, osb, og, r, gsb, ggla, w, g, wg, wu, wd, gfin)


def _suffix_matrices():
    j = np.arange(SB_BLOCK)[:, None]
    s = np.arange(SB_BLOCK)[None, :]
    later = j > s
    zero = np.zeros_like(later)
    pair = np.block([[later, zero], [zero, later]])
    single = np.concatenate([later, np.ones_like(later)], axis=1)
    twice = lambda m: jnp.asarray(np.concatenate([m, m], axis=0), BF16)
    return twice(pair), twice(single)


def _state_to_kernel(s):
    b = s.shape[0]
    s = s.reshape(b, GLA_HEADS // 2, 2, GLA_HEAD_K, GLA_HEAD_V)
    return s.transpose(0, 1, 4, 2, 3).reshape(b, GLA_HEADS // 2, GLA_HEAD_V, LANES)


def _state_from_kernel(s):
    b = s.shape[0]
    s = s.reshape(b, GLA_HEADS // 2, GLA_HEAD_V, 2, GLA_HEAD_K)
    return s.transpose(0, 1, 3, 4, 2).reshape(b, GLA_HEADS, GLA_HEAD_K, GLA_HEAD_V)


def _layer(x1, lw, consts, *, batch, seq, tm, cache=None, state=None):
    (g_mix, w_in, wgu, bg, gq, gk, gsb, ggla, w_out, g2, w2g, w2u, w2d, gfin) = lw
    upair, uo, mats, lv = consts
    (qsb16, ksb, ksb16, vsb, vsb16, qg, kg, vg16, r, la) = _mixin(x1, g_mix, w_in, wgu, bg, gq, gk, tm=tm)

    if cache is None:
        o_sb = _sb(qsb16, ksb16, vsb16, upair, uo, batch=batch, tq=SB_BLOCK, keys=seq)
    else:
        o_sb = _sb(qsb16, ksb16, vsb16, upair, uo, batch=batch, tq=seq, keys=seq, cache=cache)

    if state is None:
        state = jnp.zeros((batch, GLA_HEADS // 2, GLA_HEAD_V, LANES), F32)
    o_g, s_fin = _gla(qg, kg, la, vg16, state, mats, lv, batch=batch, seq=seq)

    y = _mixout(x1, o_sb, o_g, r, gsb, ggla, w_out, g2, w2g, w2u, w2d, gfin, tm=tm)
    return y, ksb, vsb, _state_from_kernel(s_fin)


def kernel(x_prompt, x_sample, cache_sb_k, cache_sb_v, state_gla, g_ffn1, w_ffn1_gate, w_ffn1_up,
           w_ffn1_down, g_mix, w_in, w_gate_up, b_gate, g_q, g_k, g_sb_out, g_gla_out, w_out,
           g_ffn2, w_ffn2_gate, w_ffn2_up, w_ffn2_down, g_final):
    depth = w_in.shape[0]
    batch, seq, _ = x_prompt.shape
    dec_batch, dec_seq, _ = x_sample.shape
    consts = (*_suffix_matrices(), *_gla_tables(GLA_CHUNK))

    y_p = x_prompt.reshape(batch * seq, D_MODEL)
    y_s = x_sample.reshape(dec_batch * dec_seq, D_MODEL)
    outs = [[] for _ in range(6)]
    for l in range(depth):
        row = lambda a: a[l].reshape(1, -1)
        ffn1 = (row(g_ffn1), w_ffn1_gate[l].astype(BF16), w_ffn1_up[l].astype(BF16), w_ffn1_down[l].astype(BF16))
        x1_p, (w2g, w2u, w2d, w_in16, w_out16) = _ffn(
            y_p, *ffn1, tm=512,
            cast=[(w_ffn2_gate[l], D_FF), (w_ffn2_up[l], D_FF), (w_ffn2_down[l], D_MODEL),
                  (w_in[l], IN_PAD), (w_out[l], D_MODEL)])
        x1_s, _ = _ffn(y_s, *ffn1, tm=256)
        lw = (row(g_mix), w_in16,
              jnp.pad(w_gate_up[l], ((0, GATE_PAD - GATE_RANK), (0, 0))).astype(BF16), row(b_gate),
              jnp.tile(g_q[l], SB_HEADS).reshape(1, -1), jnp.tile(g_k[l], SB_HEADS).reshape(1, -1),
              jnp.tile(g_sb_out[l], SB_HEADS).reshape(1, -1), jnp.tile(g_gla_out[l], GLA_HEADS).reshape(1, -1),
              w_out16, row(g_ffn2), w2g, w2u, w2d, row(g_final))
        y_p, k_p, v_p, s_p = _layer(x1_p, lw, consts, batch=batch, seq=seq, tm=512)
        cache = (cache_sb_k[l].transpose(0, 2, 3, 1), cache_sb_v[l].transpose(0, 2, 3, 1))
        y_s, k_s, v_s, s_s = _layer(x1_s, lw, consts, batch=dec_batch, seq=dec_seq, tm=256,
                                    cache=cache, state=_state_to_kernel(state_gla[l]))
        shape_p = (batch, seq, SB_HEADS, SB_HEAD_DIM)
        shape_s = (dec_batch, dec_seq, SB_HEADS, SB_HEAD_DIM)
        for lst, val in zip(outs, (k_p.reshape(shape_p), v_p.reshape(shape_p), s_p,
                                   k_s.reshape(shape_s), v_s.reshape(shape_s), s_s)):
            lst.append(val)
    return (y_p.reshape(batch, seq, D_MODEL), y_s.reshape(dec_batch, dec_seq, D_MODEL),
            *[jnp.stack(o) for o in outs])
```

```python
import functools

import numpy as np
import jax
import jax.numpy as jnp
from jax import lax
from jax.experimental import pallas as pl
from jax.experimental.pallas import tpu as pltpu

F32 = jnp.float32
BF16 = jnp.bfloat16

D_MODEL = 1024
SB_HEADS = 8
SB_HEAD_DIM = 64
SB_WIDTH = SB_HEADS * SB_HEAD_DIM
SB_SCALE = SB_HEAD_DIM ** -0.5
GLA_HEADS = 4
GLA_HEAD_K = 64
GLA_HEAD_V = 128
GLA_KEY_WIDTH = GLA_HEADS * GLA_HEAD_K
GLA_WIDTH = GLA_HEADS * GLA_HEAD_V
GLA_SCALE = GLA_HEAD_K ** -0.5
GATE_RANK = 16
GATE_TAU = 16.0
D_FF = 2816
EPS = 1e-6

LANES = 128
SUBLANES = 8
BF16_SUBLANES = 16
GATE_PAD = LANES
MAIN_WIDTH = 3 * SB_WIDTH + 2 * GLA_KEY_WIDTH + 2 * GLA_WIDTH
IN_PAD = MAIN_WIDTH + GATE_PAD

FF_CHUNK = 256
SB_BLOCK = 128
SB_WINDOW = 3
SB_PAIRS_PER_STEP = 4
SB_QBLOCKS_PER_STEP = 2
GLA_CHUNK = 128
GLA_BATCH_PER_STEP = 2
GLA_CHUNKS_PER_STEP = 2
GLA_DIRECT_MIN = -40.0
SB_STOP = 152.0
LOG2_E = 1.4426950408889634

VMEM_LIMIT = 56 * 1024 * 1024


def _dot(a, b):
    return jnp.dot(a, b, preferred_element_type=F32)


def _dot_nt(a, b):
    return lax.dot_general(a, b, (((1,), (1,)), ((), ())), preferred_element_type=F32)


def _dot_tn(a, b):
    return lax.dot_general(a, b, (((0,), (0,)), ((), ())), preferred_element_type=F32)


def _split_bf16(x):
    hi = x.astype(BF16)
    lo = (x - hi.astype(F32)).astype(BF16)
    return hi, lo


def _rms(x, g):
    ms = jnp.mean(x * x, axis=-1, keepdims=True)
    return x * lax.rsqrt(ms + EPS) * g


def _log_sigmoid(x):
    return jnp.minimum(x, 0.0) - jnp.log1p(jnp.exp(-jnp.abs(x)))


def _const_spec(shape):
    nd = len(shape)
    return pl.BlockSpec(shape, lambda *_: (0,) * nd, pipeline_mode=pl.Buffered(1))


def _params(n_grid):
    return pltpu.CompilerParams(dimension_semantics=("arbitrary",) * n_grid,
                                vmem_limit_bytes=VMEM_LIMIT)


def _half_step_ffn(x, g_ref, wg_ref, wu_ref, wd_ref, a_ref):
    h = _rms(x, g_ref[...]).astype(BF16)
    for c in range(D_FF // FF_CHUNK):
        sl = slice(c * FF_CHUNK, (c + 1) * FF_CHUNK)
        gt = _dot(h, wg_ref[:, sl])
        up = _dot(h, wu_ref[:, sl])
        a_ref[:, sl] = (gt * jax.nn.sigmoid(gt) * up).astype(BF16)
    return x + 0.5 * _dot(a_ref[...], wd_ref[...])


def _ffn_kernel(x_ref, g_ref, wg_ref, wu_ref, wd_ref, *refs):
    n_cast = (len(refs) - 2) // 2
    o_ref, a_ref = refs[n_cast], refs[-1]
    o_ref[...] = _half_step_ffn(x_ref[...], g_ref, wg_ref, wu_ref, wd_ref, a_ref)
    for src, dst in zip(refs[:n_cast], refs[n_cast + 1:-1]):
        keep = min(src.shape[1], dst.shape[1]) // LANES * LANES
        dst[:, :keep] = src[:, :keep].astype(BF16)
        if keep < dst.shape[1]:
            dst[:, keep:] = jnp.zeros((dst.shape[0], dst.shape[1] - keep), BF16)
            dst[:, keep:src.shape[1]] = src[:, keep:].astype(BF16)


def _ffn_specs():
    return [_const_spec((1, D_MODEL)), _const_spec((D_MODEL, D_FF)), _const_spec((D_MODEL, D_FF)),
            _const_spec((D_FF, D_MODEL))]


def _cast_block_rows(rows, steps):
    br = next(b for b in range(BF16_SUBLANES, rows + 1, BF16_SUBLANES) if rows % b == 0 and rows // b <= steps)
    return br, rows // br


def _ffn(x, g, wg, wu, wd, *, tm, cast=()):
    n = x.shape[0]
    steps = n // tm
    row = lambda i: (i, 0)
    cast_in, cast_out, cast_shape = [], [], []
    for w, width in cast:
        br, nblk = _cast_block_rows(w.shape[0], steps)
        idx = lambda i, last=nblk - 1: (jnp.minimum(i, last), 0)
        cast_in.append(pl.BlockSpec((br, w.shape[1]), idx))
        cast_out.append(pl.BlockSpec((br, width), idx))
        cast_shape.append(jax.ShapeDtypeStruct((w.shape[0], width), BF16))
    out = pl.pallas_call(
        _ffn_kernel,
        grid=(steps,),
        in_specs=[pl.BlockSpec((tm, D_MODEL), row)] + _ffn_specs() + cast_in,
        out_specs=[pl.BlockSpec((tm, D_MODEL), row)] + cast_out,
        out_shape=[jax.ShapeDtypeStruct((n, D_MODEL), F32)] + cast_shape,
        scratch_shapes=[pltpu.VMEM((tm, D_FF), BF16)],
        compiler_params=_params(1),
        name="ffn1",
    )(x, g, wg, wu, wd, *[w for w, _ in cast])
    return out[0], out[1:]


def _ffn_stream_kernel(x_ref, g_ref, wg_ref, wu_ref, wd_ref, o_ref, wg16_ref, wu16_ref, wd16_ref,
                       h_ref, acc_ref):
    c = pl.program_id(0)

    @pl.when(c == 0)
    def _():
        h_ref[...] = _rms(x_ref[...], g_ref[...]).astype(BF16)
        acc_ref[...] = jnp.zeros_like(acc_ref)

    wg = wg_ref[...].astype(BF16)
    wu = wu_ref[...].astype(BF16)
    wd = wd_ref[...].astype(BF16)
    wg16_ref[...] = wg
    wu16_ref[...] = wu
    wd16_ref[...] = wd
    h = h_ref[...]
    gt = _dot(h, wg)
    acc_ref[...] += _dot((gt * jax.nn.sigmoid(gt) * _dot(h, wu)).astype(BF16), wd)

    @pl.when(c == pl.num_programs(0) - 1)
    def _():
        o_ref[...] = x_ref[...] + 0.5 * acc_ref[...]


def _ffn_stream(x, g, wg, wu, wd):
    n = x.shape[0]
    whole = lambda shape: pl.BlockSpec(shape, lambda c: (0, 0))
    cols = pl.BlockSpec((D_MODEL, FF_CHUNK), lambda c: (0, c))
    rows = pl.BlockSpec((FF_CHUNK, D_MODEL), lambda c: (c, 0))
    out = pl.pallas_call(
        _ffn_stream_kernel,
        grid=(D_FF // FF_CHUNK,),
        in_specs=[whole((n, D_MODEL)), whole((1, D_MODEL)), cols, cols, rows],
        out_specs=[whole((n, D_MODEL)), cols, cols, rows],
        out_shape=[jax.ShapeDtypeStruct((n, D_MODEL), F32),
                   jax.ShapeDtypeStruct((D_MODEL, D_FF), BF16), jax.ShapeDtypeStruct((D_MODEL, D_FF), BF16),
                   jax.ShapeDtypeStruct((D_FF, D_MODEL), BF16)],
        scratch_shapes=[pltpu.VMEM((n, D_MODEL), BF16), pltpu.VMEM((n, D_MODEL), F32)],
        compiler_params=_params(1),
        name="ffn1_stream",
    )(x, g, wg, wu, wd)
    return out[0], out[1:]


def _head_norm_lanes(y, head, gain):
    tm, width = y.shape
    low = lax.broadcasted_iota(jnp.int32, (tm, LANES), 1) < head
    cols = []
    for j in range(width // LANES):
        yj = y[:, j * LANES:(j + 1) * LANES]
        sq = yj * yj

        def inv_rms(part):
            return lax.rsqrt(jnp.sum(part, axis=1, keepdims=True) * (1.0 / head) + EPS)

        if head == LANES:
            cols.append(yj * inv_rms(sq))
        else:
            cols.append(yj * jnp.where(low, inv_rms(jnp.where(low, sq, 0.0)), inv_rms(jnp.where(low, 0.0, sq))))
    return jnp.concatenate(cols, axis=1) * gain


def _store_heads(ref, y):
    groups = y.shape[0] // SUBLANES
    sub = lax.broadcasted_iota(jnp.int32, (groups, SUBLANES, LANES), 1)
    rows = []
    for j in range(SB_WIDTH // LANES):
        pair = y[:, j * LANES:(j + 1) * LANES].reshape(groups, SUBLANES, LANES)
        rows += [pair, pltpu.roll(pair, SB_HEAD_DIM, axis=2)]
    for dist in (4, 2, 1):
        low = (sub & dist) == 0
        new = list(rows)
        for h in range(SUBLANES):
            if h & dist == 0:
                a, b = rows[h], rows[h + dist]
                new[h] = jnp.where(low, a, pltpu.roll(b, dist, axis=1))
                new[h + dist] = jnp.where(low, pltpu.roll(a, SUBLANES - dist, axis=1), b)
        rows = new
    for t in range(SUBLANES):
        ref[:, t] = rows[t][:, :, :SB_HEAD_DIM]


def _mixin_kernel(x_ref, g_ref, w_ref, wgu_ref, bg_ref, gq_ref, gk_ref,
                  qsb_ref, ksb_ref, ksb16_ref, vsb_ref, vsb16_ref,
                  qg_ref, kg_ref, vg16_ref, r_ref, la_ref):
    h = _rms(x_ref[...], g_ref[...]).astype(BF16)

    def proj(lo, width):
        return _dot(h, w_ref[:, lo:lo + width])

    q = _head_norm_lanes(proj(0, SB_WIDTH), SB_HEAD_DIM, gq_ref[...])
    qsb_ref[...] = (q * SB_SCALE).astype(BF16)
    k = _head_norm_lanes(proj(SB_WIDTH, SB_WIDTH), SB_HEAD_DIM, gk_ref[...])
    _store_heads(ksb_ref, k)
    ksb16_ref[...] = k.astype(BF16)
    v = proj(2 * SB_WIDTH, SB_WIDTH)
    _store_heads(vsb_ref, v)
    vsb16_ref[...] = v.astype(BF16)
    off = 3 * SB_WIDTH
    qg_ref[...] = proj(off, GLA_KEY_WIDTH) * GLA_SCALE
    kg_ref[...] = proj(off + GLA_KEY_WIDTH, GLA_KEY_WIDTH)
    vg16_ref[...] = proj(off + 2 * GLA_KEY_WIDTH, GLA_WIDTH).astype(BF16)
    r_ref[...] = proj(off + 2 * GLA_KEY_WIDTH + GLA_WIDTH, GLA_WIDTH)
    lr = proj(MAIN_WIDTH, GATE_PAD).astype(BF16)
    gate = _dot(lr, wgu_ref[...]) + bg_ref[...]
    la_ref[...] = _log_sigmoid(gate) * (1.0 / GATE_TAU)


def _mixin(x, g, w, wgu, bg, gq, gk, *, tm):
    n = x.shape[0]
    row = lambda i: (i, 0)
    heads = (SB_HEADS, SB_HEAD_DIM)
    assert SB_HEADS == SUBLANES and tm % SUBLANES == 0
    lead = lambda rows, wd: (rows // SUBLANES, SUBLANES) if len(wd) == 2 else (rows,)
    widths = [((SB_WIDTH,), BF16), (heads, F32), ((SB_WIDTH,), BF16), (heads, F32), ((SB_WIDTH,), BF16),
              ((GLA_KEY_WIDTH,), F32), ((GLA_KEY_WIDTH,), F32), ((GLA_WIDTH,), BF16), ((GLA_WIDTH,), F32),
              ((GLA_KEY_WIDTH,), F32)]
    return pl.pallas_call(
        _mixin_kernel,
        grid=(n // tm,),
        in_specs=[pl.BlockSpec((tm, D_MODEL), row),
                  _const_spec((1, D_MODEL)),
                  _const_spec((D_MODEL, IN_PAD)),
                  _const_spec((GATE_PAD, GLA_KEY_WIDTH)),
                  _const_spec((1, GLA_KEY_WIDTH)),
                  _const_spec((1, SB_WIDTH)),
                  _const_spec((1, SB_WIDTH))],
        out_specs=[pl.BlockSpec(lead(tm, wd) + wd, lambda i, nd=len(lead(tm, wd) + wd) - 1: (i,) + (0,) * nd)
                   for wd, _ in widths],
        out_shape=[jax.ShapeDtypeStruct(lead(n, wd) + wd, dt) for wd, dt in widths],
        compiler_params=_params(1),
        name="mixer_in",
    )(x, g, w, wgu, bg, gq, gk)


def _neg_abs(x):
    return -jnp.abs(x)


def _sb_log2_terms(z):
    z2 = z * LOG2_E
    t = jnp.log2(1.0 + jnp.exp2(_neg_abs(z2)))
    return jnp.maximum(z2, 0.0) + t, jnp.minimum(z2, 0.0) - t


def _sb_kernel(q_ref, k_ref, v_ref, upair_ref, uo_ref, *refs, tq, q_block):
    if q_block is None:
        o_ref, carry_ref, acc_ref, live_ref = refs
    else:
        kc_hbm, vc_hbm, o_ref, carry_ref, acc_ref, live_ref, kbuf, vbuf, sems, kwin, vwin = refs
    pairs = q_ref.shape[1] // LANES
    heads = 2 * pairs

    def one_block(i, row0):
        lane = lax.broadcasted_iota(jnp.int32, (tq, LANES), 1)
        head_lanes = (lane < SB_HEAD_DIM, lane >= SB_HEAD_DIM)
        rowi = lax.broadcasted_iota(jnp.int32, (heads * tq, SB_BLOCK), 0)
        coli = lax.broadcasted_iota(jnp.int32, (heads * tq, SB_BLOCK), 1)
        causal = coli < (rowi & (tq - 1))

        def pair_lanes(p):
            return slice(p * LANES, (p + 1) * LANES)

        def q_pair(p):
            q2 = q_ref[row0:row0 + tq, pair_lanes(p)]
            zero = jnp.zeros_like(q2)
            return jnp.concatenate([jnp.where(hl, q2, zero) for hl in head_lanes], axis=0)

        def rows(ref, p, first_block, nblocks):
            start = first_block * SB_BLOCK
            if not isinstance(start, int):
                start = pl.multiple_of(start, SB_BLOCK)
            return ref[pl.ds(start, nblocks * SB_BLOCK), pair_lanes(p)]

        def suffix_sums(x):
            nb = x.shape[1] // SB_BLOCK
            hi, lo = _split_bf16(x)

            def mm(cols, m2):
                return _dot(jnp.concatenate([hi[:, cols], lo[:, cols]], axis=1), m2)

            within, total = [], []
            if nb >= 3:
                for j in range(0, nb - 1, 2):
                    y = mm(slice(j * SB_BLOCK, (j + 2) * SB_BLOCK), upair_ref[...])
                    within += [y[:, :SB_BLOCK], y[:, SB_BLOCK:]]
                    total += [jnp.sum(x[:, b * SB_BLOCK:(b + 1) * SB_BLOCK], axis=1, keepdims=True) for b in (j, j + 1)]
            for j in range(len(within), nb):
                y = mm(slice(j * SB_BLOCK, (j + 1) * SB_BLOCK), uo_ref[...])
                within.append(y[:, :SB_BLOCK])
                total.append(y[:, SB_BLOCK:])
            later = total[nb - 1]
            out = [within[nb - 1]]
            for j in range(nb - 2, -1, -1):
                out.insert(0, within[j] + later)
                later = later + total[j]
            return (out[0] if nb == 1 else jnp.concatenate(out, axis=1)), later

        def resident(first_block, nblocks, kv_refs=(k_ref, v_ref)):
            return tuple((lambda p, ref=ref: rows(ref, p, first_block, nblocks)) for ref in kv_refs)

        def sweep(kv, nblocks, own, carry, valid=None):
            nk = nblocks * SB_BLOCK

            def masked(x):
                if valid is not None:
                    return jnp.where(valid, x, 0.0)
                if not own:
                    return x
                tail = jnp.where(causal, x[:, nk - SB_BLOCK:], 0.0)
                return tail if nblocks == 1 else jnp.concatenate([x[:, :nk - SB_BLOCK], tail], axis=1)

            z = jnp.concatenate([_dot_nt(q_pair(p), kv[0](p)) for p in range(pairs)], axis=0)
            fail, log_beta = _sb_log2_terms(z)
            after, total = suffix_sums(masked(fail))
            if carry is not None:
                after = after + carry
            w = masked(jnp.exp2(log_beta - after)).astype(BF16)
            pv = jnp.concatenate([_dot(w[2 * p * tq:(2 * p + 2) * tq], kv[1](p))
                                  for p in range(pairs)], axis=0)
            if carry is None:
                carry_ref[...] = total
                acc_ref[...] = pv
            else:
                carry_ref[...] = carry + total
                acc_ref[...] += pv

        def live():
            return jnp.min(carry_ref[...]) < SB_STOP

        def mark_live():
            live_ref[0] = jnp.where(live(), 1, 0)

        half = SB_BLOCK // 2

        def by_halves(x, f):
            y = [f(s, jnp.concatenate([x[s * half:(s + 1) * half], x[tq + s * half:tq + (s + 1) * half]], axis=0))
                 for s in range(2)]
            return jnp.concatenate([y[0][:half], y[1][:half], y[0][half:], y[1][half:]], axis=0)

        def half_windows():
            starts = [pl.multiple_of(i * SB_BLOCK - (3 - s) * half, half) for s in range(2)]
            nk = 2 * SB_BLOCK
            keep = coli < (rowi & (half - 1)) + half

            def masked(x):
                return jnp.concatenate([x[:, :SB_BLOCK], jnp.where(keep, x[:, SB_BLOCK:], 0.0)], axis=1)

            z = jnp.concatenate(
                [by_halves(q_pair(p), lambda s, lhs: _dot_nt(lhs, k_ref[pl.ds(starts[s], nk), pair_lanes(p)]))
                 for p in range(pairs)], axis=0)
            fail, log_beta = _sb_log2_terms(z)
            after, total = suffix_sums(masked(fail))
            w = masked(jnp.exp2(log_beta - after)).astype(BF16)
            carry_ref[...] = total
            acc_ref[...] = jnp.concatenate(
                [by_halves(w[2 * p * tq:(2 * p + 2) * tq],
                           lambda s, lhs: _dot(lhs, v_ref[pl.ds(starts[s], nk), pair_lanes(p)]))
                 for p in range(pairs)], axis=0)

        def own_block_only():
            sweep(resident(i, 1), 1, True, None)

        if q_block is None:
            assert tq == SB_BLOCK
            windowed = i >= 2
            pl.when(windowed)(half_windows)
            pl.when(jnp.logical_not(windowed))(own_block_only)

            mark_live()

            @pl.when(jnp.logical_and(windowed, live_ref[0] != 0))
            def _():
                unseen = jnp.logical_or(coli < half, (rowi & half) != 0)
                sweep(resident(i - 2, 1), 1, False, carry_ref[...], unseen)
                mark_live()

            n_prev = jnp.where(windowed, i - 2, i)
            earlier = lambda kb: resident(kb, 1)
        else:
            recent = SB_WINDOW - 1
            assert q_block >= recent and k_ref.shape[0] == tq <= SB_BLOCK

            def fetch(first_block, nblocks):
                nk = nblocks * SB_BLOCK
                start = first_block * SB_BLOCK
                if not isinstance(start, int):
                    start = pl.multiple_of(start, SB_BLOCK)
                copies = [pltpu.make_async_copy(hbm.at[pl.program_id(0), :, :, pl.ds(start, nk)],
                                                buf.at[:, :, pl.ds(0, nk)], sems.at[n])
                          for n, (hbm, buf) in enumerate(((kc_hbm, kbuf), (vc_hbm, vbuf)))]
                for c in copies:
                    c.start()
                for c in copies:
                    c.wait()

                def slabs(buf):
                    piece = lambda hd, j: buf[hd, :, j * SB_BLOCK:(j + 1) * SB_BLOCK].T
                    return lambda j, p: jnp.concatenate([piece(2 * p, j), piece(2 * p + 1, j)], axis=1).astype(BF16)

                return slabs(kbuf), slabs(vbuf)

            def earlier(kb):
                return tuple((lambda p, slab=slab: slab(0, p)) for slab in fetch(kb, 1))

            for slab, win in zip(fetch(q_block - recent, recent), (kwin, vwin)):
                for j in range(recent):
                    for p in range(pairs):
                        win[j * SB_BLOCK:(j + 1) * SB_BLOCK, pair_lanes(p)] = slab(j, p)
            for new, win in ((k_ref, kwin), (v_ref, vwin)):
                win[recent * SB_BLOCK:recent * SB_BLOCK + tq, :] = new[...]
                win[recent * SB_BLOCK + tq:, :] = jnp.zeros((SB_BLOCK - tq, win.shape[1]), BF16)
            sweep(resident(0, SB_WINDOW, (kwin, vwin)), SB_WINDOW, True, None)
            mark_live()
            n_prev = q_block - recent

        def body(st):
            kb, _ = st
            sweep(earlier(kb), 1, False, carry_ref[...])
            return kb - 1, live()

        lax.while_loop(lambda st: jnp.logical_and(st[0] >= 0, st[1]), body, (n_prev - 1, live_ref[0] != 0))
        for p in range(pairs):
            o_ref[row0:row0 + tq, pair_lanes(p)] = jnp.where(head_lanes[0], acc_ref[2 * p * tq:(2 * p + 1) * tq],
                                                              acc_ref[(2 * p + 1) * tq:(2 * p + 2) * tq])

    if q_block is None:
        per_step = q_ref.shape[0] // tq
        for u in range(per_step):
            one_block(pl.program_id(2) * per_step + u, u * tq)
    else:
        one_block(q_block, 0)


def _sb(q16, k16, v16, upair, uo, *, batch, tq, keys, cache=None):
    sub = SB_QBLOCKS_PER_STEP if cache is None else 1
    nq = keys // (sub * SB_BLOCK) if cache is None else 1
    width = SB_PAIRS_PER_STEP * LANES
    assert width == SB_WIDTH or cache is None
    kv_spec = pl.BlockSpec((keys, width), lambda b, p, i: (b, p))
    q_spec = pl.BlockSpec((sub * tq, width), lambda b, p, i: (b * nq + i, p))
    heads = 2 * SB_PAIRS_PER_STEP
    in_specs = [q_spec, kv_spec, kv_spec, _const_spec(upair.shape), _const_spec(uo.shape)]
    scratch = [pltpu.VMEM((heads * tq, SB_BLOCK), F32), pltpu.VMEM((heads * tq, LANES), F32),
               pltpu.SMEM((1,), jnp.int32)]
    operands = [q16, k16, v16, upair, uo]
    q_block = None
    if cache is not None:
        past = cache[0].shape[3]
        assert past % SB_BLOCK == 0 and keys == tq and SB_WINDOW * SB_BLOCK <= past + SB_BLOCK
        q_block = past // SB_BLOCK
        in_specs += [pl.BlockSpec(memory_space=pl.ANY)] * 2
        scratch += ([pltpu.VMEM((SB_HEADS, SB_HEAD_DIM, (SB_WINDOW - 1) * SB_BLOCK), F32)] * 2
                    + [pltpu.SemaphoreType.DMA((2,))]
                    + [pltpu.VMEM((SB_WINDOW * SB_BLOCK, SB_WIDTH), BF16)] * 2)
        operands += list(cache)
    return pl.pallas_call(
        functools.partial(_sb_kernel, tq=tq, q_block=q_block),
        grid=(batch, SB_WIDTH // width, nq),
        in_specs=in_specs,
        out_specs=q_spec,
        out_shape=jax.ShapeDtypeStruct((batch * nq * sub * tq, SB_WIDTH), F32),
        scratch_shapes=scratch,
        compiler_params=_params(3),
        name="sb_prompt" if cache is None else "sb_sample",
    )(*operands)


def _gla_levels(c):
    return int(np.log2(c))


def _gla_tables(c):
    nlev = _gla_levels(c)
    t = np.arange(c)[:, None]
    j = np.arange(c)[None, :]
    mats = [(j <= t), (j > t)]
    for lev in range(1, nlev + 1):
        m = 1 << (lev - 1)
        start = (t // (2 * m)) * (2 * m)
        mats.append(((t % (2 * m)) >= m) & (j >= start + m) & (j <= t))
    for lev in range(1, nlev + 1):
        m = 1 << (lev - 1)
        start = (t // (2 * m)) * (2 * m)
        mats.append(((t % (2 * m)) < m) & (j > t) & (j <= start + m - 1))
    mats = np.concatenate([a.astype(np.float32) for a in mats], axis=0)
    x = t ^ j
    level = np.where(j > t, -1, np.where(x == 0, 0, np.floor(np.log2(np.maximum(x, 1))).astype(np.int64) + 1))
    return jnp.asarray(mats, BF16), jnp.asarray(level, jnp.int32)


def _gla_kernel(q_ref, k_ref, la_ref, v_ref, s0_ref, mats_ref, lv_ref, o_ref, sfin_ref, st_ref):
    c = GLA_CHUNK
    nlev = _gla_levels(c)
    streams, t_rows = q_ref.shape[:2]

    @pl.when(pl.program_id(1) == 0)
    def _():
        st_ref[...] = s0_ref[...]

    lane = lax.broadcasted_iota(jnp.int32, (c, LANES), 1)
    head_lanes = (lane < GLA_HEAD_K, lane >= GLA_HEAD_K)
    lane_sq = lax.broadcasted_iota(jnp.int32, (GLA_HEAD_V, LANES), 1) < GLA_HEAD_K

    def pick(x, hl):
        return jnp.where(hl, x, jnp.zeros_like(x))

    def stack_heads(x):
        return jnp.concatenate([pick(x, hl) for hl in head_lanes], axis=0)

    def one_chunk(rows):
        n_rows = rows.stop - rows.start

        def chunk(x):
            return x if n_rows == c else jnp.concatenate([x, jnp.zeros((c - n_rows, x.shape[1]), x.dtype)], axis=0)

        def finish(s, p, scores, qe, ke, b_last):
            sl = slice(p * LANES, (p + 1) * LANES)
            st = st_ref[s, p]
            inter = _dot_nt(stack_heads(qe[:, sl]), st.astype(BF16))
            upd = []
            for r in range(2):
                hd = 2 * p + r
                vh = chunk(v_ref[s, rows, hd * GLA_HEAD_V:(hd + 1) * GLA_HEAD_V])
                o_ref[s, rows, hd * GLA_HEAD_V:(hd + 1) * GLA_HEAD_V] = (
                    inter[r * c:(r + 1) * c] + _dot(scores[r * c:(r + 1) * c], vh))[:n_rows]
                upd.append(_dot_tn(vh, ke[:, sl]))
            st_ref[s, p] = st * jnp.exp(b_last[:, sl]) + jnp.where(lane_sq, upd[0], upd[1])

        la_all = jnp.concatenate([chunk(la_ref[s, rows, :]) for s in range(streams)], axis=1)
        b_all = _dot_hilo_left(mats_ref[:c, :], la_all)
        mild = jnp.min(b_all[c - 1:c, :]) >= GLA_DIRECT_MIN

        @pl.when(mild)
        def _():
            rowi = lax.broadcasted_iota(jnp.int32, (2 * c, c), 0) & (c - 1)
            causal = lax.broadcasted_iota(jnp.int32, (2 * c, c), 1) <= rowi
            for s in range(streams):
                b = b_all[:, s * GLA_KEY_WIDTH:(s + 1) * GLA_KEY_WIDTH]
                b_last = b[c - 1:c, :]
                q = chunk(q_ref[s, rows, :])
                k = chunk(k_ref[s, rows, :])
                qe = (q * jnp.exp(b)).astype(BF16)
                kd = (k * jnp.exp(-b)).astype(BF16)
                ke = (k * jnp.exp(b_last - b)).astype(BF16)
                for p in range(GLA_HEADS // 2):
                    sl = slice(p * LANES, (p + 1) * LANES)
                    scores = jnp.where(causal, _dot_nt(stack_heads(qe[:, sl]), kd[:, sl]), 0.0)
                    finish(s, p, scores.astype(BF16), qe, ke, b_last)

        @pl.when(jnp.logical_not(mild))
        def _():
            lv = jnp.concatenate([lv_ref[...], lv_ref[...]], axis=0)
            for s in range(streams):
                q = chunk(q_ref[s, rows, :])
                k = chunk(k_ref[s, rows, :])
                sums = _dot_hilo_left(mats_ref[...], chunk(la_ref[s, rows, :]))

                def level_rows(i):
                    return sums[i * c:(i + 1) * c]

                qe = (q * jnp.exp(level_rows(0))).astype(BF16)
                ke = (k * jnp.exp(level_rows(1))).astype(BF16)
                q_lev = [q.astype(BF16)] + [(q * jnp.exp(level_rows(1 + lev))).astype(BF16)
                                            for lev in range(1, nlev + 1)]
                k_lev = [k.astype(BF16)] + [(k * jnp.exp(level_rows(1 + nlev + lev))).astype(BF16)
                                            for lev in range(1, nlev + 1)]
                for p in range(GLA_HEADS // 2):
                    sl = slice(p * LANES, (p + 1) * LANES)
                    scores = jnp.zeros((2 * c, c), F32)
                    for lev in range(nlev + 1):
                        scores = jnp.where(lv == lev, _dot_nt(stack_heads(q_lev[lev][:, sl]), k_lev[lev][:, sl]),
                                           scores)
                    finish(s, p, scores.astype(BF16), qe, ke, level_rows(0)[c - 1:c, :])

    if t_rows < c:
        one_chunk(slice(0, t_rows))
    else:
        for u in range(t_rows // c):
            one_chunk(slice(u * c, (u + 1) * c))
    sfin_ref[...] = st_ref[...]


def _dot_hilo_left(m, x):
    hi, lo = _split_bf16(x)
    return _dot(m, hi) + _dot(m, lo)


def _gla(qg, kg, la, vg16, s0, mats, lv, *, batch, seq):
    c = min(GLA_CHUNK * GLA_CHUNKS_PER_STEP, seq)
    assert seq % c == 0 and (c < GLA_CHUNK or c % GLA_CHUNK == 0)
    g = GLA_BATCH_PER_STEP
    pairs = GLA_HEADS // 2
    blk = lambda width: pl.BlockSpec((g, c, width), lambda b, t: (b, t, 0))
    st_spec = pl.BlockSpec((g, pairs, GLA_HEAD_V, LANES), lambda b, t: (b, 0, 0, 0))
    per_stream = lambda a: a.reshape(batch, seq, a.shape[-1])
    o, s_fin = pl.pallas_call(
        _gla_kernel,
        grid=(batch // g, seq // c),
        in_specs=[blk(GLA_KEY_WIDTH), blk(GLA_KEY_WIDTH), blk(GLA_KEY_WIDTH), blk(GLA_WIDTH), st_spec,
                  _const_spec(mats.shape), _const_spec(lv.shape)],
        out_specs=[blk(GLA_WIDTH), st_spec],
        out_shape=[jax.ShapeDtypeStruct((batch, seq, GLA_WIDTH), F32),
                   jax.ShapeDtypeStruct((batch, pairs, GLA_HEAD_V, LANES), F32)],
        scratch_shapes=[pltpu.VMEM((g, pairs, GLA_HEAD_V, LANES), F32)],
        compiler_params=_params(2),
        name="gla",
    )(per_stream(qg), per_stream(kg), per_stream(la), per_stream(vg16), s0, mats, lv)
    return o.reshape(batch * seq, GLA_WIDTH), s_fin


def _mixout_kernel(x_ref, osb_ref, og_ref, r_ref, gsb_ref, ggla_ref, w_ref,
                   g_ref, wg_ref, wu_ref, wd_ref, gfin_ref, o_ref, a_ref):
    o_sb = _head_norm_lanes(osb_ref[...], SB_HEAD_DIM, gsb_ref[...])
    r = r_ref[...]
    o_g = _head_norm_lanes(og_ref[...], GLA_HEAD_V, ggla_ref[...]) * (r * jax.nn.sigmoid(r))
    mix = _dot(o_sb.astype(BF16), w_ref[:SB_WIDTH, :]) + _dot(o_g.astype(BF16), w_ref[SB_WIDTH:, :])
    y = _half_step_ffn(x_ref[...] + mix, g_ref, wg_ref, wu_ref, wd_ref, a_ref)
    o_ref[...] = _rms(y, gfin_ref[...])


def _mixout(x, osb, og, r, gsb, ggla, w, g, wg, wu, wd, gfin, *, tm):
    n = x.shape[0]
    row = lambda i: (i, 0)
    return pl.pallas_call(
        _mixout_kernel,
        grid=(n // tm,),
        in_specs=[pl.BlockSpec((tm, D_MODEL), row),
                  pl.BlockSpec((tm, SB_WIDTH), row),
                  pl.BlockSpec((tm, GLA_WIDTH), row),
                  pl.BlockSpec((tm, GLA_WIDTH), row),
                  _const_spec((1, SB_WIDTH)),
                  _const_spec((1, GLA_WIDTH)),
                  _const_spec((D_MODEL, D_MODEL))] + _ffn_specs() + [_const_spec((1, D_MODEL))],
        out_specs=pl.BlockSpec((tm, D_MODEL), row),
        out_shape=jax.ShapeDtypeStruct((n, D_MODEL), F32),
        scratch_shapes=[pltpu.VMEM((tm, D_FF), BF16)],
        compiler_params=_params(1),
        name="mixer_out_ffn2",
    )(x, osb, og, r, gsb, ggla, w, g, wg, wu, wd, gfin)


def _suffix_matrices():
    j = np.arange(SB_BLOCK)[:, None]
    s = np.arange(SB_BLOCK)[None, :]
    later = j > s
    zero = np.zeros_like(later)
    pair = np.block([[later, zero], [zero, later]])
    single = np.concatenate([later, np.ones_like(later)], axis=1)
    twice = lambda m: jnp.asarray(np.concatenate([m, m], axis=0), BF16)
    return twice(pair), twice(single)


def _state_to_kernel(s):
    b = s.shape[0]
    s = s.reshape(b, GLA_HEADS // 2, 2, GLA_HEAD_K, GLA_HEAD_V)
    return s.transpose(0, 1, 4, 2, 3).reshape(b, GLA_HEADS // 2, GLA_HEAD_V, LANES)


def _state_from_kernel(s):
    b = s.shape[0]
    s = s.reshape(b, GLA_HEADS // 2, GLA_HEAD_V, 2, GLA_HEAD_K)
    return s.transpose(0, 1, 3, 4, 2).reshape(b, GLA_HEADS, GLA_HEAD_K, GLA_HEAD_V)


def _layer(x1, lw, consts, *, batch, seq, tm, cache=None, state=None):
    (g_mix, w_in, wgu, bg, gq, gk, gsb, ggla, w_out, g2, w2g, w2u, w2d, gfin) = lw
    upair, uo, mats, lv = consts
    (qsb16, ksb, ksb16, vsb, vsb16, qg, kg, vg16, r, la) = _mixin(x1, g_mix, w_in, wgu, bg, gq, gk, tm=tm)

    if cache is None:
        o_sb = _sb(qsb16, ksb16, vsb16, upair, uo, batch=batch, tq=SB_BLOCK, keys=seq)
    else:
        o_sb = _sb(qsb16, ksb16, vsb16, upair, uo, batch=batch, tq=seq, keys=seq, cache=cache)

    if state is None:
        state = jnp.zeros((batch, GLA_HEADS // 2, GLA_HEAD_V, LANES), F32)
    o_g, s_fin = _gla(qg, kg, la, vg16, state, mats, lv, batch=batch, seq=seq)

    y = _mixout(x1, o_sb, o_g, r, gsb, ggla, w_out, g2, w2g, w2u, w2d, gfin, tm=tm)
    return y, ksb, vsb, _state_from_kernel(s_fin)


def kernel(x_prompt, x_sample, cache_sb_k, cache_sb_v, state_gla, g_ffn1, w_ffn1_gate, w_ffn1_up,
           w_ffn1_down, g_mix, w_in, w_gate_up, b_gate, g_q, g_k, g_sb_out, g_gla_out, w_out,
           g_ffn2, w_ffn2_gate, w_ffn2_up, w_ffn2_down, g_final):
    depth = w_in.shape[0]
    batch, seq, _ = x_prompt.shape
    dec_batch, dec_seq, _ = x_sample.shape
    consts = (*_suffix_matrices(), *_gla_tables(GLA_CHUNK))

    y_p = x_prompt.reshape(batch * seq, D_MODEL)
    y_s = x_sample.reshape(dec_batch * dec_seq, D_MODEL)
    outs = [[] for _ in range(6)]
    for l in range(depth):
        row = lambda a: a[l].reshape(1, -1)
        x1_s, ffn1 = _ffn_stream(y_s, row(g_ffn1), w_ffn1_gate[l], w_ffn1_up[l], w_ffn1_down[l])
        x1_p, (w2g, w2u, w2d, w_in16, w_out16) = _ffn(
            y_p, row(g_ffn1), *ffn1, tm=512,
            cast=[(w_ffn2_gate[l], D_FF), (w_ffn2_up[l], D_FF), (w_ffn2_down[l], D_MODEL),
                  (w_in[l], IN_PAD), (w_out[l], D_MODEL)])
        lw = (row(g_mix), w_in16,
              jnp.pad(w_gate_up[l], ((0, GATE_PAD - GATE_RANK), (0, 0))).astype(BF16), row(b_gate),
              jnp.tile(g_q[l], SB_HEADS).reshape(1, -1), jnp.tile(g_k[l], SB_HEADS).reshape(1, -1),
              jnp.tile(g_sb_out[l], SB_HEADS).reshape(1, -1), jnp.tile(g_gla_out[l], GLA_HEADS).reshape(1, -1),
              w_out16, row(g_ffn2), w2g, w2u, w2d, row(g_final))
        y_p, k_p, v_p, s_p = _layer(x1_p, lw, consts, batch=batch, seq=seq, tm=512)
        cache = (cache_sb_k[l].transpose(0, 2, 3, 1), cache_sb_v[l].transpose(0, 2, 3, 1))
        y_s, k_s, v_s, s_s = _layer(x1_s, lw, consts, batch=dec_batch, seq=dec_seq, tm=256,
                                    cache=cache, state=_state_to_kernel(state_gla[l]))
        shape_p = (batch, seq, SB_HEADS, SB_HEAD_DIM)
        shape_s = (dec_batch, dec_seq, SB_HEADS, SB_HEAD_DIM)
        for lst, val in zip(outs, (k_p.reshape(shape_p), v_p.reshape(shape_p), s_p,
                                   k_s.reshape(shape_s), v_s.reshape(shape_s), s_s)):
            lst.append(val)
    return (y_p.reshape(batch, seq, D_MODEL), y_s.reshape(dec_batch, dec_seq, D_MODEL),
            *[jnp.stack(o) for o in outs])
```

```python
import functools

import numpy as np
import jax
import jax.numpy as jnp
from jax import lax
from jax.experimental import pallas as pl
from jax.experimental.pallas import tpu as pltpu

F32 = jnp.float32
BF16 = jnp.bfloat16

D_MODEL = 1024
SB_HEADS = 8
SB_HEAD_DIM = 64
SB_WIDTH = SB_HEADS * SB_HEAD_DIM
SB_SCALE = SB_HEAD_DIM ** -0.5
GLA_HEADS = 4
GLA_HEAD_K = 64
GLA_HEAD_V = 128
GLA_KEY_WIDTH = GLA_HEADS * GLA_HEAD_K
GLA_WIDTH = GLA_HEADS * GLA_HEAD_V
GLA_SCALE = GLA_HEAD_K ** -0.5
GATE_RANK = 16
GATE_TAU = 16.0
D_FF = 2816
EPS = 1e-6

LANES = 128
SUBLANES = 8
BF16_SUBLANES = 16
GATE_PAD = LANES
MAIN_WIDTH = 3 * SB_WIDTH + 2 * GLA_KEY_WIDTH + 2 * GLA_WIDTH
IN_PAD = MAIN_WIDTH + GATE_PAD

FF_CHUNK = 256
FFN1_ROWS = 512
SB_BLOCK = 128
SB_WINDOW = 3
SB_PAIRS_PER_STEP = 4
SB_QBLOCKS_PER_STEP = 2
GLA_CHUNK = 128
GLA_BATCH_PER_STEP = 2
GLA_CHUNKS_PER_STEP = 2
GLA_DIRECT_MIN = -40.0
SB_STOP = 152.0
LOG2_E = 1.4426950408889634

VMEM_LIMIT = 56 * 1024 * 1024


def _dot(a, b):
    return jnp.dot(a, b, preferred_element_type=F32)


def _dot_nt(a, b):
    return lax.dot_general(a, b, (((1,), (1,)), ((), ())), preferred_element_type=F32)


def _dot_tn(a, b):
    return lax.dot_general(a, b, (((0,), (0,)), ((), ())), preferred_element_type=F32)


def _split_bf16(x):
    hi = x.astype(BF16)
    lo = (x - hi.astype(F32)).astype(BF16)
    return hi, lo


def _rms(x, g):
    ms = jnp.mean(x * x, axis=-1, keepdims=True)
    return x * lax.rsqrt(ms + EPS) * g


def _log_sigmoid(x):
    return jnp.minimum(x, 0.0) - jnp.log1p(jnp.exp(-jnp.abs(x)))


def _const_spec(shape):
    nd = len(shape)
    return pl.BlockSpec(shape, lambda *_: (0,) * nd, pipeline_mode=pl.Buffered(1))


def _params(n_grid):
    return pltpu.CompilerParams(dimension_semantics=("arbitrary",) * n_grid,
                                vmem_limit_bytes=VMEM_LIMIT)


def _half_step_ffn(x, g_ref, wg_ref, wu_ref, wd_ref, a_ref):
    h = _rms(x, g_ref[...]).astype(BF16)
    for c in range(D_FF // FF_CHUNK):
        sl = slice(c * FF_CHUNK, (c + 1) * FF_CHUNK)
        gt = _dot(h, wg_ref[:, sl])
        up = _dot(h, wu_ref[:, sl])
        a_ref[:, sl] = (gt * jax.nn.sigmoid(gt) * up).astype(BF16)
    return x + 0.5 * _dot(a_ref[...], wd_ref[...])


def _ffn_kernel(x_ref, g_ref, wg_ref, wu_ref, wd_ref, *refs, transposed):
    n_cast = (len(refs) - 2) // 2
    o_ref, a_ref = refs[n_cast], refs[-1]
    o_ref[...] = _half_step_ffn(x_ref[...], g_ref, wg_ref, wu_ref, wd_ref, a_ref)
    for src, dst, tr in zip(refs[:n_cast], refs[n_cast + 1:-1], transposed):
        if tr is None:
            dst[...] = src[...].astype(BF16)
        else:
            rows, blocks = tr
            first = jnp.minimum(pl.program_id(0), blocks - 1) * src.shape[0]
            inside = lax.broadcasted_iota(jnp.int32, src.shape, 0) < rows - first
            dst[...] = jnp.where(inside, src[...], 0.0).T.astype(BF16)


def _ffn_specs():
    return [_const_spec((1, D_MODEL)), _const_spec((D_MODEL, D_FF)), _const_spec((D_MODEL, D_FF)),
            _const_spec((D_FF, D_MODEL))]


def _cast_block_rows(rows, steps):
    br = next(b for b in range(BF16_SUBLANES, rows + 1, BF16_SUBLANES) if rows % b == 0 and rows // b <= steps)
    return br, rows // br


def _ffn(x, g, wg, wu, wd, *, tm, cast=()):
    n = x.shape[0]
    steps = n // tm
    row = lambda i: (i, 0)
    cast_in, cast_out, cast_shape, transposed = [], [], [], []
    for w in cast:
        if isinstance(w, tuple):
            (w, width), nblk = w, -(-w[1] // LANES)
            assert width % LANES == 0 and nblk <= steps and w.shape[0] <= width
            cast_in.append(pl.BlockSpec((LANES, w.shape[1]), lambda i, last=nblk - 1: (jnp.minimum(i, last), 0)))
            cast_out.append(pl.BlockSpec((w.shape[1], LANES), lambda i, last=nblk - 1: (0, jnp.minimum(i, last))))
            cast_shape.append(jax.ShapeDtypeStruct((w.shape[1], width), BF16))
            transposed.append((w.shape[0], nblk))
        else:
            br, nblk = _cast_block_rows(w.shape[0], steps)
            idx = lambda i, last=nblk - 1: (jnp.minimum(i, last), 0)
            cast_in.append(pl.BlockSpec((br, w.shape[1]), idx))
            cast_out.append(pl.BlockSpec((br, w.shape[1]), idx))
            cast_shape.append(jax.ShapeDtypeStruct(w.shape, BF16))
            transposed.append(None)
    cast = [w[0] if isinstance(w, tuple) else w for w in cast]
    out = pl.pallas_call(
        functools.partial(_ffn_kernel, transposed=tuple(transposed)),
        grid=(steps,),
        in_specs=[pl.BlockSpec((tm, D_MODEL), row)] + _ffn_specs() + cast_in,
        out_specs=[pl.BlockSpec((tm, D_MODEL), row)] + cast_out,
        out_shape=[jax.ShapeDtypeStruct((n, D_MODEL), F32)] + cast_shape,
        scratch_shapes=[pltpu.VMEM((tm, D_FF), BF16)],
        compiler_params=_params(1),
        name="ffn1",
    )(x, g, wg, wu, wd, *cast)
    return out[0], out[1:]


def _ffn_stream_kernel(x_ref, g_ref, wg_ref, wu_ref, wd_ref, o_ref, wg16_ref, wu16_ref, wd16_ref,
                       h_ref, acc_ref):
    c = pl.program_id(0)

    @pl.when(c == 0)
    def _():
        h_ref[...] = _rms(x_ref[...], g_ref[...]).astype(BF16)
        acc_ref[...] = jnp.zeros_like(acc_ref)

    wg = wg_ref[...].astype(BF16)
    wu = wu_ref[...].astype(BF16)
    wd = wd_ref[...].astype(BF16)
    wg16_ref[...] = wg
    wu16_ref[...] = wu
    wd16_ref[...] = wd
    h = h_ref[...]
    gt = _dot(h, wg)
    acc_ref[...] += _dot((gt * jax.nn.sigmoid(gt) * _dot(h, wu)).astype(BF16), wd)

    @pl.when(c == pl.num_programs(0) - 1)
    def _():
        o_ref[...] = x_ref[...] + 0.5 * acc_ref[...]


def _ffn_stream(x, g, wg, wu, wd):
    n = x.shape[0]
    whole = lambda shape: pl.BlockSpec(shape, lambda c: (0, 0))
    cols = pl.BlockSpec((D_MODEL, FF_CHUNK), lambda c: (0, c))
    rows = pl.BlockSpec((FF_CHUNK, D_MODEL), lambda c: (c, 0))
    out = pl.pallas_call(
        _ffn_stream_kernel,
        grid=(D_FF // FF_CHUNK,),
        in_specs=[whole((n, D_MODEL)), whole((1, D_MODEL)), cols, cols, rows],
        out_specs=[whole((n, D_MODEL)), cols, cols, rows],
        out_shape=[jax.ShapeDtypeStruct((n, D_MODEL), F32),
                   jax.ShapeDtypeStruct((D_MODEL, D_FF), BF16), jax.ShapeDtypeStruct((D_MODEL, D_FF), BF16),
                   jax.ShapeDtypeStruct((D_FF, D_MODEL), BF16)],
        scratch_shapes=[pltpu.VMEM((n, D_MODEL), BF16), pltpu.VMEM((n, D_MODEL), F32)],
        compiler_params=_params(1),
        name="ffn1_stream",
    )(x, g, wg, wu, wd)
    return out[0], out[1:]


def _head_norm_lanes(y, head, gain):
    tm, width = y.shape
    low = lax.broadcasted_iota(jnp.int32, (tm, LANES), 1) < head
    cols = []
    for j in range(width // LANES):
        yj = y[:, j * LANES:(j + 1) * LANES]
        sq = yj * yj

        def inv_rms(part):
            return lax.rsqrt(jnp.sum(part, axis=1, keepdims=True) * (1.0 / head) + EPS)

        if head == LANES:
            cols.append(yj * inv_rms(sq))
        else:
            cols.append(yj * jnp.where(low, inv_rms(jnp.where(low, sq, 0.0)), inv_rms(jnp.where(low, 0.0, sq))))
    return jnp.concatenate(cols, axis=1) * gain


def _store_heads(ref, y):
    groups = y.shape[0] // SUBLANES
    sub = lax.broadcasted_iota(jnp.int32, (groups, SUBLANES, LANES), 1)
    rows = []
    for j in range(SB_WIDTH // LANES):
        pair = y[:, j * LANES:(j + 1) * LANES].reshape(groups, SUBLANES, LANES)
        rows += [pair, pltpu.roll(pair, SB_HEAD_DIM, axis=2)]
    for dist in (4, 2, 1):
        low = (sub & dist) == 0
        new = list(rows)
        for h in range(SUBLANES):
            if h & dist == 0:
                a, b = rows[h], rows[h + dist]
                new[h] = jnp.where(low, a, pltpu.roll(b, dist, axis=1))
                new[h + dist] = jnp.where(low, pltpu.roll(a, SUBLANES - dist, axis=1), b)
        rows = new
    for t in range(SUBLANES):
        ref[:, t] = rows[t][:, :, :SB_HEAD_DIM]


def _mixin_kernel(x_ref, g_ref, w_ref, wgu_ref, bg_ref, gq_ref, gk_ref,
                  qsb_ref, ksb_ref, ksb16_ref, vsb_ref, vsb16_ref,
                  qg_ref, kg_ref, vg16_ref, r_ref, la_ref):
    h = _rms(x_ref[...], g_ref[...]).astype(BF16)

    def proj(lo, width):
        return _dot(h, w_ref[:, lo:lo + width])

    q = _head_norm_lanes(proj(0, SB_WIDTH), SB_HEAD_DIM, gq_ref[...])
    qsb_ref[...] = (q * SB_SCALE).astype(BF16)
    k = _head_norm_lanes(proj(SB_WIDTH, SB_WIDTH), SB_HEAD_DIM, gk_ref[...])
    _store_heads(ksb_ref, k)
    ksb16_ref[...] = k.astype(BF16)
    v = proj(2 * SB_WIDTH, SB_WIDTH)
    _store_heads(vsb_ref, v)
    vsb16_ref[...] = v.astype(BF16)
    off = 3 * SB_WIDTH
    qg_ref[...] = proj(off, GLA_KEY_WIDTH) * GLA_SCALE
    kg_ref[...] = proj(off + GLA_KEY_WIDTH, GLA_KEY_WIDTH)
    vg16_ref[...] = proj(off + 2 * GLA_KEY_WIDTH, GLA_WIDTH).astype(BF16)
    r_ref[...] = proj(off + 2 * GLA_KEY_WIDTH + GLA_WIDTH, GLA_WIDTH)
    lr = proj(MAIN_WIDTH, GATE_PAD).astype(BF16)
    gate = _dot(lr, wgu_ref[...]) + bg_ref[...]
    la_ref[...] = _log_sigmoid(gate) * (1.0 / GATE_TAU)


def _mixin(x, g, w, wgu, bg, gq, gk, *, tm):
    n = x.shape[0]
    row = lambda i: (i, 0)
    heads = (SB_HEADS, SB_HEAD_DIM)
    assert SB_HEADS == SUBLANES and tm % SUBLANES == 0
    lead = lambda rows, wd: (rows // SUBLANES, SUBLANES) if len(wd) == 2 else (rows,)
    widths = [((SB_WIDTH,), BF16), (heads, F32), ((SB_WIDTH,), BF16), (heads, F32), ((SB_WIDTH,), BF16),
              ((GLA_KEY_WIDTH,), F32), ((GLA_KEY_WIDTH,), F32), ((GLA_WIDTH,), BF16), ((GLA_WIDTH,), F32),
              ((GLA_KEY_WIDTH,), F32)]
    return pl.pallas_call(
        _mixin_kernel,
        grid=(n // tm,),
        in_specs=[pl.BlockSpec((tm, D_MODEL), row),
                  _const_spec((1, D_MODEL)),
                  _const_spec((D_MODEL, IN_PAD)),
                  _const_spec((GATE_PAD, GLA_KEY_WIDTH)),
                  _const_spec((1, GLA_KEY_WIDTH)),
                  _const_spec((1, SB_WIDTH)),
                  _const_spec((1, SB_WIDTH))],
        out_specs=[pl.BlockSpec(lead(tm, wd) + wd, lambda i, nd=len(lead(tm, wd) + wd) - 1: (i,) + (0,) * nd)
                   for wd, _ in widths],
        out_shape=[jax.ShapeDtypeStruct(lead(n, wd) + wd, dt) for wd, dt in widths],
        compiler_params=_params(1),
        name="mixer_in",
    )(x, g, w, wgu, bg, gq, gk)


def _neg_abs(x):
    return -jnp.abs(x)


def _sb_log2_terms(z):
    z2 = z * LOG2_E
    t = jnp.log2(1.0 + jnp.exp2(_neg_abs(z2)))
    return jnp.maximum(z2, 0.0) + t, jnp.minimum(z2, 0.0) - t


def _sb_kernel(q_ref, k_ref, v_ref, upair_ref, uo_ref, *refs, tq, q_block):
    if q_block is None:
        o_ref, carry_ref, acc_ref, live_ref = refs
    else:
        kc_hbm, vc_hbm, o_ref, carry_ref, acc_ref, live_ref, kbuf, vbuf, sems, kwin, vwin = refs
    pairs = q_ref.shape[1] // LANES
    heads = 2 * pairs

    def one_block(i, row0):
        lane = lax.broadcasted_iota(jnp.int32, (tq, LANES), 1)
        head_lanes = (lane < SB_HEAD_DIM, lane >= SB_HEAD_DIM)
        rowi = lax.broadcasted_iota(jnp.int32, (heads * tq, SB_BLOCK), 0)
        coli = lax.broadcasted_iota(jnp.int32, (heads * tq, SB_BLOCK), 1)
        causal = coli < (rowi & (tq - 1))

        def pair_lanes(p):
            return slice(p * LANES, (p + 1) * LANES)

        def q_pair(p):
            q2 = q_ref[row0:row0 + tq, pair_lanes(p)]
            zero = jnp.zeros_like(q2)
            return jnp.concatenate([jnp.where(hl, q2, zero) for hl in head_lanes], axis=0)

        def rows(ref, p, first_block, nblocks):
            start = first_block * SB_BLOCK
            if not isinstance(start, int):
                start = pl.multiple_of(start, SB_BLOCK)
            return ref[pl.ds(start, nblocks * SB_BLOCK), pair_lanes(p)]

        def suffix_sums(x):
            nb = x.shape[1] // SB_BLOCK
            hi, lo = _split_bf16(x)

            def mm(cols, m2):
                return _dot(jnp.concatenate([hi[:, cols], lo[:, cols]], axis=1), m2)

            within, total = [], []
            if nb >= 3:
                for j in range(0, nb - 1, 2):
                    y = mm(slice(j * SB_BLOCK, (j + 2) * SB_BLOCK), upair_ref[...])
                    within += [y[:, :SB_BLOCK], y[:, SB_BLOCK:]]
                    total += [jnp.sum(x[:, b * SB_BLOCK:(b + 1) * SB_BLOCK], axis=1, keepdims=True) for b in (j, j + 1)]
            for j in range(len(within), nb):
                y = mm(slice(j * SB_BLOCK, (j + 1) * SB_BLOCK), uo_ref[...])
                within.append(y[:, :SB_BLOCK])
                total.append(y[:, SB_BLOCK:])
            later = total[nb - 1]
            out = [within[nb - 1]]
            for j in range(nb - 2, -1, -1):
                out.insert(0, within[j] + later)
                later = later + total[j]
            return (out[0] if nb == 1 else jnp.concatenate(out, axis=1)), later

        def resident(first_block, nblocks, kv_refs=(k_ref, v_ref)):
            return tuple((lambda p, ref=ref: rows(ref, p, first_block, nblocks)) for ref in kv_refs)

        def sweep(kv, nblocks, own, carry, valid=None):
            nk = nblocks * SB_BLOCK

            def masked(x):
                if valid is not None:
                    return jnp.where(valid, x, 0.0)
                if not own:
                    return x
                tail = jnp.where(causal, x[:, nk - SB_BLOCK:], 0.0)
                return tail if nblocks == 1 else jnp.concatenate([x[:, :nk - SB_BLOCK], tail], axis=1)

            z = jnp.concatenate([_dot_nt(q_pair(p), kv[0](p)) for p in range(pairs)], axis=0)
            fail, log_beta = _sb_log2_terms(z)
            after, total = suffix_sums(masked(fail))
            if carry is not None:
                after = after + carry
            w = masked(jnp.exp2(log_beta - after)).astype(BF16)
            pv = jnp.concatenate([_dot(w[2 * p * tq:(2 * p + 2) * tq], kv[1](p))
                                  for p in range(pairs)], axis=0)
            if carry is None:
                carry_ref[...] = total
                acc_ref[...] = pv
            else:
                carry_ref[...] = carry + total
                acc_ref[...] += pv

        def live():
            return jnp.min(carry_ref[...]) < SB_STOP

        def mark_live():
            live_ref[0] = jnp.where(live(), 1, 0)

        half = SB_BLOCK // 2

        def by_halves(x, f):
            y = [f(s, jnp.concatenate([x[s * half:(s + 1) * half], x[tq + s * half:tq + (s + 1) * half]], axis=0))
                 for s in range(2)]
            return jnp.concatenate([y[0][:half], y[1][:half], y[0][half:], y[1][half:]], axis=0)

        def half_windows():
            starts = [pl.multiple_of(i * SB_BLOCK - (3 - s) * half, half) for s in range(2)]
            nk = 2 * SB_BLOCK
            keep = coli < (rowi & (half - 1)) + half

            def masked(x):
                return jnp.concatenate([x[:, :SB_BLOCK], jnp.where(keep, x[:, SB_BLOCK:], 0.0)], axis=1)

            z = jnp.concatenate(
                [by_halves(q_pair(p), lambda s, lhs: _dot_nt(lhs, k_ref[pl.ds(starts[s], nk), pair_lanes(p)]))
                 for p in range(pairs)], axis=0)
            fail, log_beta = _sb_log2_terms(z)
            after, total = suffix_sums(masked(fail))
            w = masked(jnp.exp2(log_beta - after)).astype(BF16)
            carry_ref[...] = total
            acc_ref[...] = jnp.concatenate(
                [by_halves(w[2 * p * tq:(2 * p + 2) * tq],
                           lambda s, lhs: _dot(lhs, v_ref[pl.ds(starts[s], nk), pair_lanes(p)]))
                 for p in range(pairs)], axis=0)

        def own_block_only():
            sweep(resident(i, 1), 1, True, None)

        if q_block is None:
            assert tq == SB_BLOCK
            windowed = i >= 2
            pl.when(windowed)(half_windows)
            pl.when(jnp.logical_not(windowed))(own_block_only)

            mark_live()

            @pl.when(jnp.logical_and(windowed, live_ref[0] != 0))
            def _():
                unseen = jnp.logical_or(coli < half, (rowi & half) != 0)
                sweep(resident(i - 2, 1), 1, False, carry_ref[...], unseen)
                mark_live()

            n_prev = jnp.where(windowed, i - 2, i)
            earlier = lambda kb: resident(kb, 1)
        else:
            recent = SB_WINDOW - 1
            assert q_block >= recent and k_ref.shape[0] == tq <= SB_BLOCK

            def fetch(first_block, nblocks):
                nk = nblocks * SB_BLOCK
                start = first_block * SB_BLOCK
                if not isinstance(start, int):
                    start = pl.multiple_of(start, SB_BLOCK)
                copies = [pltpu.make_async_copy(hbm.at[pl.program_id(0), :, :, pl.ds(start, nk)],
                                                buf.at[:, :, pl.ds(0, nk)], sems.at[n])
                          for n, (hbm, buf) in enumerate(((kc_hbm, kbuf), (vc_hbm, vbuf)))]
                for c in copies:
                    c.start()
                for c in copies:
                    c.wait()

                def slabs(buf):
                    piece = lambda hd, j: buf[hd, :, j * SB_BLOCK:(j + 1) * SB_BLOCK].T
                    return lambda j, p: jnp.concatenate([piece(2 * p, j), piece(2 * p + 1, j)], axis=1).astype(BF16)

                return slabs(kbuf), slabs(vbuf)

            def earlier(kb):
                return tuple((lambda p, slab=slab: slab(0, p)) for slab in fetch(kb, 1))

            for slab, win in zip(fetch(q_block - recent, recent), (kwin, vwin)):
                for j in range(recent):
                    for p in range(pairs):
                        win[j * SB_BLOCK:(j + 1) * SB_BLOCK, pair_lanes(p)] = slab(j, p)
            for new, win in ((k_ref, kwin), (v_ref, vwin)):
                win[recent * SB_BLOCK:recent * SB_BLOCK + tq, :] = new[...]
                win[recent * SB_BLOCK + tq:, :] = jnp.zeros((SB_BLOCK - tq, win.shape[1]), BF16)
            sweep(resident(0, SB_WINDOW, (kwin, vwin)), SB_WINDOW, True, None)
            mark_live()
            n_prev = q_block - recent

        def body(st):
            kb, _ = st
            sweep(earlier(kb), 1, False, carry_ref[...])
            return kb - 1, live()

        lax.while_loop(lambda st: jnp.logical_and(st[0] >= 0, st[1]), body, (n_prev - 1, live_ref[0] != 0))
        for p in range(pairs):
            o_ref[row0:row0 + tq, pair_lanes(p)] = jnp.where(head_lanes[0], acc_ref[2 * p * tq:(2 * p + 1) * tq],
                                                              acc_ref[(2 * p + 1) * tq:(2 * p + 2) * tq])

    if q_block is None:
        per_step = q_ref.shape[0] // tq
        for u in range(per_step):
            one_block(pl.program_id(2) * per_step + u, u * tq)
    else:
        one_block(q_block, 0)


def _sb(q16, k16, v16, upair, uo, *, batch, tq, keys, cache=None):
    sub = SB_QBLOCKS_PER_STEP if cache is None else 1
    nq = keys // (sub * SB_BLOCK) if cache is None else 1
    width = SB_PAIRS_PER_STEP * LANES
    assert width == SB_WIDTH or cache is None
    kv_spec = pl.BlockSpec((keys, width), lambda b, p, i: (b, p))
    q_spec = pl.BlockSpec((sub * tq, width), lambda b, p, i: (b * nq + i, p))
    heads = 2 * SB_PAIRS_PER_STEP
    in_specs = [q_spec, kv_spec, kv_spec, _const_spec(upair.shape), _const_spec(uo.shape)]
    scratch = [pltpu.VMEM((heads * tq, SB_BLOCK), F32), pltpu.VMEM((heads * tq, LANES), F32),
               pltpu.SMEM((1,), jnp.int32)]
    operands = [q16, k16, v16, upair, uo]
    q_block = None
    if cache is not None:
        past = cache[0].shape[3]
        assert past % SB_BLOCK == 0 and keys == tq and SB_WINDOW * SB_BLOCK <= past + SB_BLOCK
        q_block = past // SB_BLOCK
        in_specs += [pl.BlockSpec(memory_space=pl.ANY)] * 2
        scratch += ([pltpu.VMEM((SB_HEADS, SB_HEAD_DIM, (SB_WINDOW - 1) * SB_BLOCK), F32)] * 2
                    + [pltpu.SemaphoreType.DMA((2,))]
                    + [pltpu.VMEM((SB_WINDOW * SB_BLOCK, SB_WIDTH), BF16)] * 2)
        operands += list(cache)
    return pl.pallas_call(
        functools.partial(_sb_kernel, tq=tq, q_block=q_block),
        grid=(batch, SB_WIDTH // width, nq),
        in_specs=in_specs,
        out_specs=q_spec,
        out_shape=jax.ShapeDtypeStruct((batch * nq * sub * tq, SB_WIDTH), F32),
        scratch_shapes=scratch,
        compiler_params=_params(3),
        name="sb_prompt" if cache is None else "sb_sample",
    )(*operands)


def _gla_levels(c):
    return int(np.log2(c))


def _gla_tables(c):
    nlev = _gla_levels(c)
    t = np.arange(c)[:, None]
    j = np.arange(c)[None, :]
    mats = [(j <= t), (j > t)]
    for lev in range(1, nlev + 1):
        m = 1 << (lev - 1)
        start = (t // (2 * m)) * (2 * m)
        mats.append(((t % (2 * m)) >= m) & (j >= start + m) & (j <= t))
    for lev in range(1, nlev + 1):
        m = 1 << (lev - 1)
        start = (t // (2 * m)) * (2 * m)
        mats.append(((t % (2 * m)) < m) & (j > t) & (j <= start + m - 1))
    mats = np.concatenate([a.astype(np.float32) for a in mats], axis=0)
    x = t ^ j
    level = np.where(j > t, -1, np.where(x == 0, 0, np.floor(np.log2(np.maximum(x, 1))).astype(np.int64) + 1))
    return jnp.asarray(mats, BF16), jnp.asarray(level, jnp.int32)


def _gla_kernel(q_ref, k_ref, la_ref, v_ref, s0_ref, mats_ref, lv_ref, o_ref, sfin_ref, st_ref):
    c = GLA_CHUNK
    nlev = _gla_levels(c)
    streams, t_rows = q_ref.shape[:2]

    @pl.when(pl.program_id(1) == 0)
    def _():
        st_ref[...] = s0_ref[...]

    lane = lax.broadcasted_iota(jnp.int32, (c, LANES), 1)
    head_lanes = (lane < GLA_HEAD_K, lane >= GLA_HEAD_K)
    lane_sq = lax.broadcasted_iota(jnp.int32, (GLA_HEAD_V, LANES), 1) < GLA_HEAD_K

    def pick(x, hl):
        return jnp.where(hl, x, jnp.zeros_like(x))

    def stack_heads(x):
        return jnp.concatenate([pick(x, hl) for hl in head_lanes], axis=0)

    def one_chunk(rows):
        n_rows = rows.stop - rows.start

        def chunk(x):
            return x if n_rows == c else jnp.concatenate([x, jnp.zeros((c - n_rows, x.shape[1]), x.dtype)], axis=0)

        def finish(s, p, scores, qe, ke, b_last):
            sl = slice(p * LANES, (p + 1) * LANES)
            st = st_ref[s, p]
            inter = _dot_nt(stack_heads(qe[:, sl]), st.astype(BF16))
            upd = []
            for r in range(2):
                hd = 2 * p + r
                vh = chunk(v_ref[s, rows, hd * GLA_HEAD_V:(hd + 1) * GLA_HEAD_V])
                o_ref[s, rows, hd * GLA_HEAD_V:(hd + 1) * GLA_HEAD_V] = (
                    inter[r * c:(r + 1) * c] + _dot(scores[r * c:(r + 1) * c], vh))[:n_rows]
                upd.append(_dot_tn(vh, ke[:, sl]))
            st_ref[s, p] = st * jnp.exp(b_last[:, sl]) + jnp.where(lane_sq, upd[0], upd[1])

        la_all = jnp.concatenate([chunk(la_ref[s, rows, :]) for s in range(streams)], axis=1)
        b_all = _dot_hilo_left(mats_ref[:c, :], la_all)
        mild = jnp.min(b_all[c - 1:c, :]) >= GLA_DIRECT_MIN

        @pl.when(mild)
        def _():
            rowi = lax.broadcasted_iota(jnp.int32, (2 * c, c), 0) & (c - 1)
            causal = lax.broadcasted_iota(jnp.int32, (2 * c, c), 1) <= rowi
            for s in range(streams):
                b = b_all[:, s * GLA_KEY_WIDTH:(s + 1) * GLA_KEY_WIDTH]
                b_last = b[c - 1:c, :]
                q = chunk(q_ref[s, rows, :])
                k = chunk(k_ref[s, rows, :])
                qe = (q * jnp.exp(b)).astype(BF16)
                kd = (k * jnp.exp(-b)).astype(BF16)
                ke = (k * jnp.exp(b_last - b)).astype(BF16)
                for p in range(GLA_HEADS // 2):
                    sl = slice(p * LANES, (p + 1) * LANES)
                    scores = jnp.where(causal, _dot_nt(stack_heads(qe[:, sl]), kd[:, sl]), 0.0)
                    finish(s, p, scores.astype(BF16), qe, ke, b_last)

        @pl.when(jnp.logical_not(mild))
        def _():
            lv = jnp.concatenate([lv_ref[...], lv_ref[...]], axis=0)
            for s in range(streams):
                q = chunk(q_ref[s, rows, :])
                k = chunk(k_ref[s, rows, :])
                sums = _dot_hilo_left(mats_ref[...], chunk(la_ref[s, rows, :]))

                def level_rows(i):
                    return sums[i * c:(i + 1) * c]

                qe = (q * jnp.exp(level_rows(0))).astype(BF16)
                ke = (k * jnp.exp(level_rows(1))).astype(BF16)
                q_lev = [q.astype(BF16)] + [(q * jnp.exp(level_rows(1 + lev))).astype(BF16)
                                            for lev in range(1, nlev + 1)]
                k_lev = [k.astype(BF16)] + [(k * jnp.exp(level_rows(1 + nlev + lev))).astype(BF16)
                                            for lev in range(1, nlev + 1)]
                for p in range(GLA_HEADS // 2):
                    sl = slice(p * LANES, (p + 1) * LANES)
                    scores = jnp.zeros((2 * c, c), F32)
                    for lev in range(nlev + 1):
                        scores = jnp.where(lv == lev, _dot_nt(stack_heads(q_lev[lev][:, sl]), k_lev[lev][:, sl]),
                                           scores)
                    finish(s, p, scores.astype(BF16), qe, ke, level_rows(0)[c - 1:c, :])

    if t_rows < c:
        one_chunk(slice(0, t_rows))
    else:
        for u in range(t_rows // c):
            one_chunk(slice(u * c, (u + 1) * c))
    sfin_ref[...] = st_ref[...]


def _dot_hilo_left(m, x):
    hi, lo = _split_bf16(x)
    return _dot(m, hi) + _dot(m, lo)


def _gla(qg, kg, la, vg16, s0, mats, lv, *, batch, seq):
    c = min(GLA_CHUNK * GLA_CHUNKS_PER_STEP, seq)
    assert seq % c == 0 and (c < GLA_CHUNK or c % GLA_CHUNK == 0)
    g = GLA_BATCH_PER_STEP
    pairs = GLA_HEADS // 2
    blk = lambda width: pl.BlockSpec((g, c, width), lambda b, t: (b, t, 0))
    st_spec = pl.BlockSpec((g, pairs, GLA_HEAD_V, LANES), lambda b, t: (b, 0, 0, 0))
    per_stream = lambda a: a.reshape(batch, seq, a.shape[-1])
    o, s_fin = pl.pallas_call(
        _gla_kernel,
        grid=(batch // g, seq // c),
        in_specs=[blk(GLA_KEY_WIDTH), blk(GLA_KEY_WIDTH), blk(GLA_KEY_WIDTH), blk(GLA_WIDTH), st_spec,
                  _const_spec(mats.shape), _const_spec(lv.shape)],
        out_specs=[blk(GLA_WIDTH), st_spec],
        out_shape=[jax.ShapeDtypeStruct((batch, seq, GLA_WIDTH), F32),
                   jax.ShapeDtypeStruct((batch, pairs, GLA_HEAD_V, LANES), F32)],
        scratch_shapes=[pltpu.VMEM((g, pairs, GLA_HEAD_V, LANES), F32)],
        compiler_params=_params(2),
        name="gla",
    )(per_stream(qg), per_stream(kg), per_stream(la), per_stream(vg16), s0, mats, lv)
    return o.reshape(batch * seq, GLA_WIDTH), s_fin


def _mixout_kernel(x_ref, osb_ref, og_ref, r_ref, gsb_ref, ggla_ref, w_ref,
                   g_ref, wg_ref, wu_ref, wd_ref, gfin_ref, o_ref, a_ref):
    o_sb = _head_norm_lanes(osb_ref[...], SB_HEAD_DIM, gsb_ref[...])
    r = r_ref[...]
    o_g = _head_norm_lanes(og_ref[...], GLA_HEAD_V, ggla_ref[...]) * (r * jax.nn.sigmoid(r))
    mix = _dot(o_sb.astype(BF16), w_ref[:SB_WIDTH, :]) + _dot(o_g.astype(BF16), w_ref[SB_WIDTH:, :])
    y = _half_step_ffn(x_ref[...] + mix, g_ref, wg_ref, wu_ref, wd_ref, a_ref)
    o_ref[...] = _rms(y, gfin_ref[...])


def _mixout(x, osb, og, r, gsb, ggla, w, g, wg, wu, wd, gfin, *, tm):
    n = x.shape[0]
    row = lambda i: (i, 0)
    return pl.pallas_call(
        _mixout_kernel,
        grid=(n // tm,),
        in_specs=[pl.BlockSpec((tm, D_MODEL), row),
                  pl.BlockSpec((tm, SB_WIDTH), row),
                  pl.BlockSpec((tm, GLA_WIDTH), row),
                  pl.BlockSpec((tm, GLA_WIDTH), row),
                  _const_spec((1, SB_WIDTH)),
                  _const_spec((1, GLA_WIDTH)),
                  _const_spec((D_MODEL, D_MODEL))] + _ffn_specs() + [_const_spec((1, D_MODEL))],
        out_specs=pl.BlockSpec((tm, D_MODEL), row),
        out_shape=jax.ShapeDtypeStruct((n, D_MODEL), F32),
        scratch_shapes=[pltpu.VMEM((tm, D_FF), BF16)],
        compiler_params=_params(1),
        name="mixer_out_ffn2",
    )(x, osb, og, r, gsb, ggla, w, g, wg, wu, wd, gfin)


def _suffix_matrices():
    j = np.arange(SB_BLOCK)[:, None]
    s = np.arange(SB_BLOCK)[None, :]
    later = j > s
    zero = np.zeros_like(later)
    pair = np.block([[later, zero], [zero, later]])
    single = np.concatenate([later, np.ones_like(later)], axis=1)
    twice = lambda m: jnp.asarray(np.concatenate([m, m], axis=0), BF16)
    return twice(pair), twice(single)


def _state_to_kernel(s):
    b = s.shape[0]
    s = s.reshape(b, GLA_HEADS // 2, 2, GLA_HEAD_K, GLA_HEAD_V)
    return s.transpose(0, 1, 4, 2, 3).reshape(b, GLA_HEADS // 2, GLA_HEAD_V, LANES)


def _state_from_kernel(s):
    b = s.shape[0]
    s = s.reshape(b, GLA_HEADS // 2, GLA_HEAD_V, 2, GLA_HEAD_K)
    return s.transpose(0, 1, 3, 4, 2).reshape(b, GLA_HEADS, GLA_HEAD_K, GLA_HEAD_V)


def _layer(x1, lw, consts, *, batch, seq, tm, cache=None, state=None):
    (g_mix, w_in, wgu, bg, gq, gk, gsb, ggla, w_out, g2, w2g, w2u, w2d, gfin) = lw
    upair, uo, mats, lv = consts
    (qsb16, ksb, ksb16, vsb, vsb16, qg, kg, vg16, r, la) = _mixin(x1, g_mix, w_in, wgu, bg, gq, gk, tm=tm)

    if cache is None:
        o_sb = _sb(qsb16, ksb16, vsb16, upair, uo, batch=batch, tq=SB_BLOCK, keys=seq)
    else:
        o_sb = _sb(qsb16, ksb16, vsb16, upair, uo, batch=batch, tq=seq, keys=seq, cache=cache)

    if state is None:
        state = jnp.zeros((batch, GLA_HEADS // 2, GLA_HEAD_V, LANES), F32)
    o_g, s_fin = _gla(qg, kg, la, vg16, state, mats, lv, batch=batch, seq=seq)

    y = _mixout(x1, o_sb, o_g, r, gsb, ggla, w_out, g2, w2g, w2u, w2d, gfin, tm=tm)
    return y, ksb, vsb, _state_from_kernel(s_fin)


def kernel(x_prompt, x_sample, cache_sb_k, cache_sb_v, state_gla, g_ffn1, w_ffn1_gate, w_ffn1_up,
           w_ffn1_down, g_mix, w_in, w_gate_up, b_gate, g_q, g_k, g_sb_out, g_gla_out, w_out,
           g_ffn2, w_ffn2_gate, w_ffn2_up, w_ffn2_down, g_final):
    depth = w_in.shape[0]
    batch, seq, _ = x_prompt.shape
    dec_batch, dec_seq, _ = x_sample.shape
    consts = (*_suffix_matrices(), *_gla_tables(GLA_CHUNK))

    y_p = x_prompt.reshape(batch * seq, D_MODEL)
    y_s = x_sample.reshape(dec_batch * dec_seq, D_MODEL)
    outs = [[] for _ in range(6)]
    for l in range(depth):
        row = lambda a: a[l].reshape(1, -1)
        x1_s, ffn1 = _ffn_stream(y_s, row(g_ffn1), w_ffn1_gate[l], w_ffn1_up[l], w_ffn1_down[l])
        x1_p, (w2g, w2u, w2d, w_in16, w_out16) = _ffn(
            y_p, row(g_ffn1), *ffn1, tm=FFN1_ROWS,
            cast=[w_ffn2_gate[l], w_ffn2_up[l], w_ffn2_down[l], (w_in[l].T, IN_PAD), w_out[l]])
        lw = (row(g_mix), w_in16,
              jnp.pad(w_gate_up[l], ((0, GATE_PAD - GATE_RANK), (0, 0))).astype(BF16), row(b_gate),
              jnp.tile(g_q[l], SB_HEADS).reshape(1, -1), jnp.tile(g_k[l], SB_HEADS).reshape(1, -1),
              jnp.tile(g_sb_out[l], SB_HEADS).reshape(1, -1), jnp.tile(g_gla_out[l], GLA_HEADS).reshape(1, -1),
              w_out16, row(g_ffn2), w2g, w2u, w2d, row(g_final))
        y_p, k_p, v_p, s_p = _layer(x1_p, lw, consts, batch=batch, seq=seq, tm=512)
        cache = (cache_sb_k[l].transpose(0, 2, 3, 1), cache_sb_v[l].transpose(0, 2, 3, 1))
        y_s, k_s, v_s, s_s = _layer(x1_s, lw, consts, batch=dec_batch, seq=dec_seq, tm=256,
                                    cache=cache, state=_state_to_kernel(state_gla[l]))
        shape_p = (batch, seq, SB_HEADS, SB_HEAD_DIM)
        shape_s = (dec_batch, dec_seq, SB_HEADS, SB_HEAD_DIM)
        for lst, val in zip(outs, (k_p.reshape(shape_p), v_p.reshape(shape_p), s_p,
                                   k_s.reshape(shape_s), v_s.reshape(shape_s), s_s)):
            lst.append(val)
    return (y_p.reshape(batch, seq, D_MODEL), y_s.reshape(dec_batch, dec_seq, D_MODEL),
            *[jnp.stack(o) for o in outs])
```

```python
import functools

import numpy as np
import jax
import jax.numpy as jnp
from jax import lax
from jax.experimental import pallas as pl
from jax.experimental.pallas import tpu as pltpu

F32 = jnp.float32
BF16 = jnp.bfloat16

D_MODEL = 1024
SB_HEADS = 8
SB_HEAD_DIM = 64
SB_WIDTH = SB_HEADS * SB_HEAD_DIM
SB_SCALE = SB_HEAD_DIM ** -0.5
GLA_HEADS = 4
GLA_HEAD_K = 64
GLA_HEAD_V = 128
GLA_KEY_WIDTH = GLA_HEADS * GLA_HEAD_K
GLA_WIDTH = GLA_HEADS * GLA_HEAD_V
GLA_SCALE = GLA_HEAD_K ** -0.5
GATE_RANK = 16
GATE_TAU = 16.0
D_FF = 2816
EPS = 1e-6

LANES = 128
SUBLANES = 8
BF16_SUBLANES = 16
GATE_PAD = LANES
MAIN_WIDTH = 3 * SB_WIDTH + 2 * GLA_KEY_WIDTH + 2 * GLA_WIDTH
IN_PAD = MAIN_WIDTH + GATE_PAD

FF_CHUNK = 256
FFN1_ROWS = 1024
FF_STREAM_CHUNK = 256
SB_BLOCK = 128
SB_WINDOW = 3
SB_PAIRS_PER_STEP = 4
SB_QBLOCKS_PER_STEP = 2
GLA_CHUNK = 128
GLA_BATCH_PER_STEP = 2
GLA_CHUNKS_PER_STEP = 2
GLA_DIRECT_MIN = -40.0
SB_STOP = 152.0
LOG2_E = 1.4426950408889634

VMEM_LIMIT = 56 * 1024 * 1024


def _dot(a, b):
    return jnp.dot(a, b, preferred_element_type=F32)


def _dot_nt(a, b):
    return lax.dot_general(a, b, (((1,), (1,)), ((), ())), preferred_element_type=F32)


def _dot_tn(a, b):
    return lax.dot_general(a, b, (((0,), (0,)), ((), ())), preferred_element_type=F32)


def _split_bf16(x):
    hi = x.astype(BF16)
    lo = (x - hi.astype(F32)).astype(BF16)
    return hi, lo


def _rms(x, g):
    ms = jnp.mean(x * x, axis=-1, keepdims=True)
    return x * lax.rsqrt(ms + EPS) * g


def _log_sigmoid(x):
    return jnp.minimum(x, 0.0) - jnp.log1p(jnp.exp(-jnp.abs(x)))


def _const_spec(shape):
    nd = len(shape)
    return pl.BlockSpec(shape, lambda *_: (0,) * nd, pipeline_mode=pl.Buffered(1))


def _params(n_grid):
    return pltpu.CompilerParams(dimension_semantics=("arbitrary",) * n_grid,
                                vmem_limit_bytes=VMEM_LIMIT)


def _half_step_ffn(x, g_ref, wg_ref, wu_ref, wd_ref, a_ref):
    h = _rms(x, g_ref[...]).astype(BF16)
    for c in range(D_FF // FF_CHUNK):
        sl = slice(c * FF_CHUNK, (c + 1) * FF_CHUNK)
        gt = _dot(h, wg_ref[:, sl])
        up = _dot(h, wu_ref[:, sl])
        a_ref[:, sl] = (gt * jax.nn.sigmoid(gt) * up).astype(BF16)
    return x + 0.5 * _dot(a_ref[...], wd_ref[...])


def _ffn_kernel(x_ref, g_ref, wg_ref, wu_ref, wd_ref, *refs, transposed):
    n_cast = (len(refs) - 2) // 2
    o_ref, a_ref = refs[n_cast], refs[-1]
    o_ref[...] = _half_step_ffn(x_ref[...], g_ref, wg_ref, wu_ref, wd_ref, a_ref)
    for src, dst, tr in zip(refs[:n_cast], refs[n_cast + 1:-1], transposed):
        if tr is None:
            dst[...] = src[...].astype(BF16)
        else:
            rows, blocks = tr
            first = jnp.minimum(pl.program_id(0), blocks - 1) * src.shape[0]
            inside = lax.broadcasted_iota(jnp.int32, src.shape, 0) < rows - first
            dst[...] = jnp.where(inside, src[...], 0.0).T.astype(BF16)


def _ffn_specs():
    return [_const_spec((1, D_MODEL)), _const_spec((D_MODEL, D_FF)), _const_spec((D_MODEL, D_FF)),
            _const_spec((D_FF, D_MODEL))]


def _cast_block_rows(rows, steps):
    br = next(b for b in range(BF16_SUBLANES, rows + 1, BF16_SUBLANES) if rows % b == 0 and rows // b <= steps)
    return br, rows // br


def _ffn(x, g, wg, wu, wd, *, tm, cast=()):
    n = x.shape[0]
    steps = n // tm
    row = lambda i: (i, 0)
    cast_in, cast_out, cast_shape, transposed = [], [], [], []
    for w in cast:
        if isinstance(w, tuple):
            w, width = w
            tb = LANES * -(-width // (LANES * steps))
            nblk = -(-width // tb)
            width = nblk * tb
            assert nblk <= steps and w.shape[0] <= width
            cast_in.append(pl.BlockSpec((tb, w.shape[1]), lambda i, last=nblk - 1: (jnp.minimum(i, last), 0)))
            cast_out.append(pl.BlockSpec((w.shape[1], tb), lambda i, last=nblk - 1: (0, jnp.minimum(i, last))))
            cast_shape.append(jax.ShapeDtypeStruct((w.shape[1], width), BF16))
            transposed.append((w.shape[0], nblk))
        else:
            br, nblk = _cast_block_rows(w.shape[0], steps)
            idx = lambda i, last=nblk - 1: (jnp.minimum(i, last), 0)
            cast_in.append(pl.BlockSpec((br, w.shape[1]), idx))
            cast_out.append(pl.BlockSpec((br, w.shape[1]), idx))
            cast_shape.append(jax.ShapeDtypeStruct(w.shape, BF16))
            transposed.append(None)
    cast = [w[0] if isinstance(w, tuple) else w for w in cast]
    out = pl.pallas_call(
        functools.partial(_ffn_kernel, transposed=tuple(transposed)),
        grid=(steps,),
        in_specs=[pl.BlockSpec((tm, D_MODEL), row)] + _ffn_specs() + cast_in,
        out_specs=[pl.BlockSpec((tm, D_MODEL), row)] + cast_out,
        out_shape=[jax.ShapeDtypeStruct((n, D_MODEL), F32)] + cast_shape,
        scratch_shapes=[pltpu.VMEM((tm, D_FF), BF16)],
        compiler_params=_params(1),
        name="ffn1",
    )(x, g, wg, wu, wd, *cast)
    return out[0], out[1:]


def _ffn_stream_kernel(x_ref, g_ref, wg_ref, wu_ref, wd_ref, o_ref, wg16_ref, wu16_ref, wd16_ref,
                       h_ref, acc_ref):
    c = pl.program_id(0)

    @pl.when(c == 0)
    def _():
        h_ref[...] = _rms(x_ref[...], g_ref[...]).astype(BF16)
        acc_ref[...] = jnp.zeros_like(acc_ref)

    wg = wg_ref[...].astype(BF16)
    wu = wu_ref[...].astype(BF16)
    wd = wd_ref[...].astype(BF16)
    wg16_ref[...] = wg
    wu16_ref[...] = wu
    wd16_ref[...] = wd
    h = h_ref[...]
    gt = _dot(h, wg)
    acc_ref[...] += _dot((gt * jax.nn.sigmoid(gt) * _dot(h, wu)).astype(BF16), wd)

    @pl.when(c == pl.num_programs(0) - 1)
    def _():
        o_ref[...] = x_ref[...] + 0.5 * acc_ref[...]


def _ffn_stream(x, g, wg, wu, wd):
    n = x.shape[0]
    whole = lambda shape: pl.BlockSpec(shape, lambda c: (0, 0))
    cols = pl.BlockSpec((D_MODEL, FF_STREAM_CHUNK), lambda c: (0, c))
    rows = pl.BlockSpec((FF_STREAM_CHUNK, D_MODEL), lambda c: (c, 0))
    out = pl.pallas_call(
        _ffn_stream_kernel,
        grid=(D_FF // FF_STREAM_CHUNK,),
        in_specs=[whole((n, D_MODEL)), whole((1, D_MODEL)), cols, cols, rows],
        out_specs=[whole((n, D_MODEL)), cols, cols, rows],
        out_shape=[jax.ShapeDtypeStruct((n, D_MODEL), F32),
                   jax.ShapeDtypeStruct((D_MODEL, D_FF), BF16), jax.ShapeDtypeStruct((D_MODEL, D_FF), BF16),
                   jax.ShapeDtypeStruct((D_FF, D_MODEL), BF16)],
        scratch_shapes=[pltpu.VMEM((n, D_MODEL), BF16), pltpu.VMEM((n, D_MODEL), F32)],
        compiler_params=_params(1),
        name="ffn1_stream",
    )(x, g, wg, wu, wd)
    return out[0], out[1:]


def _head_norm_lanes(y, head, gain):
    tm, width = y.shape
    low = lax.broadcasted_iota(jnp.int32, (tm, LANES), 1) < head
    cols = []
    for j in range(width // LANES):
        yj = y[:, j * LANES:(j + 1) * LANES]
        sq = yj * yj

        def inv_rms(part):
            return lax.rsqrt(jnp.sum(part, axis=1, keepdims=True) * (1.0 / head) + EPS)

        if head == LANES:
            cols.append(yj * inv_rms(sq))
        else:
            cols.append(yj * jnp.where(low, inv_rms(jnp.where(low, sq, 0.0)), inv_rms(jnp.where(low, 0.0, sq))))
    return jnp.concatenate(cols, axis=1) * gain


def _store_heads(ref, y):
    groups = y.shape[0] // SUBLANES
    sub = lax.broadcasted_iota(jnp.int32, (groups, SUBLANES, LANES), 1)
    rows = []
    for j in range(SB_WIDTH // LANES):
        pair = y[:, j * LANES:(j + 1) * LANES].reshape(groups, SUBLANES, LANES)
        rows += [pair, pltpu.roll(pair, SB_HEAD_DIM, axis=2)]
    for dist in (4, 2, 1):
        low = (sub & dist) == 0
        new = list(rows)
        for h in range(SUBLANES):
            if h & dist == 0:
                a, b = rows[h], rows[h + dist]
                new[h] = jnp.where(low, a, pltpu.roll(b, dist, axis=1))
                new[h + dist] = jnp.where(low, pltpu.roll(a, SUBLANES - dist, axis=1), b)
        rows = new
    for t in range(SUBLANES):
        ref[:, t] = rows[t][:, :, :SB_HEAD_DIM]


def _mixin_kernel(x_ref, g_ref, w_ref, wgu_ref, bg_ref, gq_ref, gk_ref,
                  qsb_ref, ksb_ref, ksb16_ref, vsb_ref, vsb16_ref,
                  qg_ref, kg_ref, vg16_ref, r_ref, la_ref):
    h = _rms(x_ref[...], g_ref[...]).astype(BF16)

    def proj(lo, width):
        return _dot(h, w_ref[:, lo:lo + width])

    q = _head_norm_lanes(proj(0, SB_WIDTH), SB_HEAD_DIM, gq_ref[...])
    qsb_ref[...] = (q * SB_SCALE).astype(BF16)
    k = _head_norm_lanes(proj(SB_WIDTH, SB_WIDTH), SB_HEAD_DIM, gk_ref[...])
    _store_heads(ksb_ref, k)
    ksb16_ref[...] = k.astype(BF16)
    v = proj(2 * SB_WIDTH, SB_WIDTH)
    _store_heads(vsb_ref, v)
    vsb16_ref[...] = v.astype(BF16)
    off = 3 * SB_WIDTH
    qg_ref[...] = proj(off, GLA_KEY_WIDTH) * GLA_SCALE
    kg_ref[...] = proj(off + GLA_KEY_WIDTH, GLA_KEY_WIDTH)
    vg16_ref[...] = proj(off + 2 * GLA_KEY_WIDTH, GLA_WIDTH).astype(BF16)
    r_ref[...] = proj(off + 2 * GLA_KEY_WIDTH + GLA_WIDTH, GLA_WIDTH)
    lr = proj(MAIN_WIDTH, GATE_PAD).astype(BF16)
    gate = _dot(lr, wgu_ref[...]) + bg_ref[...]
    la_ref[...] = _log_sigmoid(gate) * (1.0 / GATE_TAU)


def _mixin(x, g, w, wgu, bg, gq, gk, *, tm):
    n = x.shape[0]
    row = lambda i: (i, 0)
    heads = (SB_HEADS, SB_HEAD_DIM)
    assert SB_HEADS == SUBLANES and tm % SUBLANES == 0
    lead = lambda rows, wd: (rows // SUBLANES, SUBLANES) if len(wd) == 2 else (rows,)
    widths = [((SB_WIDTH,), BF16), (heads, F32), ((SB_WIDTH,), BF16), (heads, F32), ((SB_WIDTH,), BF16),
              ((GLA_KEY_WIDTH,), F32), ((GLA_KEY_WIDTH,), F32), ((GLA_WIDTH,), BF16), ((GLA_WIDTH,), F32),
              ((GLA_KEY_WIDTH,), F32)]
    return pl.pallas_call(
        _mixin_kernel,
        grid=(n // tm,),
        in_specs=[pl.BlockSpec((tm, D_MODEL), row),
                  _const_spec((1, D_MODEL)),
                  _const_spec(w.shape),
                  _const_spec((GATE_PAD, GLA_KEY_WIDTH)),
                  _const_spec((1, GLA_KEY_WIDTH)),
                  _const_spec((1, SB_WIDTH)),
                  _const_spec((1, SB_WIDTH))],
        out_specs=[pl.BlockSpec(lead(tm, wd) + wd, lambda i, nd=len(lead(tm, wd) + wd) - 1: (i,) + (0,) * nd)
                   for wd, _ in widths],
        out_shape=[jax.ShapeDtypeStruct(lead(n, wd) + wd, dt) for wd, dt in widths],
        compiler_params=_params(1),
        name="mixer_in",
    )(x, g, w, wgu, bg, gq, gk)


def _neg_abs(x):
    return -jnp.abs(x)


def _sb_log2_terms(z):
    z2 = z * LOG2_E
    t = jnp.log2(1.0 + jnp.exp2(_neg_abs(z2)))
    return jnp.maximum(z2, 0.0) + t, jnp.minimum(z2, 0.0) - t


def _sb_kernel(q_ref, k_ref, v_ref, upair_ref, uo_ref, *refs, tq, q_block):
    if q_block is None:
        o_ref, carry_ref, acc_ref, live_ref = refs
    else:
        kc_hbm, vc_hbm, o_ref, carry_ref, acc_ref, live_ref, kbuf, vbuf, sems, kwin, vwin = refs
    pairs = q_ref.shape[1] // LANES
    heads = 2 * pairs

    def one_block(i, row0):
        lane = lax.broadcasted_iota(jnp.int32, (tq, LANES), 1)
        head_lanes = (lane < SB_HEAD_DIM, lane >= SB_HEAD_DIM)
        rowi = lax.broadcasted_iota(jnp.int32, (heads * tq, SB_BLOCK), 0)
        coli = lax.broadcasted_iota(jnp.int32, (heads * tq, SB_BLOCK), 1)
        causal = coli < (rowi & (tq - 1))

        def pair_lanes(p):
            return slice(p * LANES, (p + 1) * LANES)

        def q_pair(p):
            q2 = q_ref[row0:row0 + tq, pair_lanes(p)]
            zero = jnp.zeros_like(q2)
            return jnp.concatenate([jnp.where(hl, q2, zero) for hl in head_lanes], axis=0)

        def rows(ref, p, first_block, nblocks):
            start = first_block * SB_BLOCK
            if not isinstance(start, int):
                start = pl.multiple_of(start, SB_BLOCK)
            return ref[pl.ds(start, nblocks * SB_BLOCK), pair_lanes(p)]

        def suffix_sums(x):
            nb = x.shape[1] // SB_BLOCK
            hi, lo = _split_bf16(x)

            def mm(cols, m2):
                return _dot(jnp.concatenate([hi[:, cols], lo[:, cols]], axis=1), m2)

            within, total = [], []
            if nb >= 3:
                for j in range(0, nb - 1, 2):
                    y = mm(slice(j * SB_BLOCK, (j + 2) * SB_BLOCK), upair_ref[...])
                    within += [y[:, :SB_BLOCK], y[:, SB_BLOCK:]]
                    total += [jnp.sum(x[:, b * SB_BLOCK:(b + 1) * SB_BLOCK], axis=1, keepdims=True) for b in (j, j + 1)]
            for j in range(len(within), nb):
                y = mm(slice(j * SB_BLOCK, (j + 1) * SB_BLOCK), uo_ref[...])
                within.append(y[:, :SB_BLOCK])
                total.append(y[:, SB_BLOCK:])
            later = total[nb - 1]
            out = [within[nb - 1]]
            for j in range(nb - 2, -1, -1):
                out.insert(0, within[j] + later)
                later = later + total[j]
            return (out[0] if nb == 1 else jnp.concatenate(out, axis=1)), later

        def resident(first_block, nblocks, kv_refs=(k_ref, v_ref)):
            return tuple((lambda p, ref=ref: rows(ref, p, first_block, nblocks)) for ref in kv_refs)

        def sweep(kv, nblocks, own, carry, valid=None):
            nk = nblocks * SB_BLOCK

            def masked(x):
                if valid is not None:
                    return jnp.where(valid, x, 0.0)
                if not own:
                    return x
                tail = jnp.where(causal, x[:, nk - SB_BLOCK:], 0.0)
                return tail if nblocks == 1 else jnp.concatenate([x[:, :nk - SB_BLOCK], tail], axis=1)

            z = jnp.concatenate([_dot_nt(q_pair(p), kv[0](p)) for p in range(pairs)], axis=0)
            fail, log_beta = _sb_log2_terms(z)
            after, total = suffix_sums(masked(fail))
            if carry is not None:
                after = after + carry
            w = masked(jnp.exp2(log_beta - after)).astype(BF16)
            pv = jnp.concatenate([_dot(w[2 * p * tq:(2 * p + 2) * tq], kv[1](p))
                                  for p in range(pairs)], axis=0)
            if carry is None:
                carry_ref[...] = total
                acc_ref[...] = pv
            else:
                carry_ref[...] = carry + total
                acc_ref[...] += pv

        def live():
            return jnp.min(carry_ref[...]) < SB_STOP

        def mark_live():
            live_ref[0] = jnp.where(live(), 1, 0)

        half = SB_BLOCK // 2

        def by_halves(x, f):
            y = [f(s, jnp.concatenate([x[s * half:(s + 1) * half], x[tq + s * half:tq + (s + 1) * half]], axis=0))
                 for s in range(2)]
            return jnp.concatenate([y[0][:half], y[1][:half], y[0][half:], y[1][half:]], axis=0)

        def half_windows():
            starts = [pl.multiple_of(i * SB_BLOCK - (3 - s) * half, half) for s in range(2)]
            nk = 2 * SB_BLOCK
            keep = coli < (rowi & (half - 1)) + half

            def masked(x):
                return jnp.concatenate([x[:, :SB_BLOCK], jnp.where(keep, x[:, SB_BLOCK:], 0.0)], axis=1)

            z = jnp.concatenate(
                [by_halves(q_pair(p), lambda s, lhs: _dot_nt(lhs, k_ref[pl.ds(starts[s], nk), pair_lanes(p)]))
                 for p in range(pairs)], axis=0)
            fail, log_beta = _sb_log2_terms(z)
            after, total = suffix_sums(masked(fail))
            w = masked(jnp.exp2(log_beta - after)).astype(BF16)
            carry_ref[...] = total
            acc_ref[...] = jnp.concatenate(
                [by_halves(w[2 * p * tq:(2 * p + 2) * tq],
                           lambda s, lhs: _dot(lhs, v_ref[pl.ds(starts[s], nk), pair_lanes(p)]))
                 for p in range(pairs)], axis=0)

        def own_block_only():
            sweep(resident(i, 1), 1, True, None)

        if q_block is None:
            assert tq == SB_BLOCK
            windowed = i >= 2
            pl.when(windowed)(half_windows)
            pl.when(jnp.logical_not(windowed))(own_block_only)

            mark_live()

            @pl.when(jnp.logical_and(windowed, live_ref[0] != 0))
            def _():
                unseen = jnp.logical_or(coli < half, (rowi & half) != 0)
                sweep(resident(i - 2, 1), 1, False, carry_ref[...], unseen)
                mark_live()

            n_prev = jnp.where(windowed, i - 2, i)
            earlier = lambda kb: resident(kb, 1)
        else:
            recent = SB_WINDOW - 1
            assert q_block >= recent and k_ref.shape[0] == tq <= SB_BLOCK

            def fetch(first_block, nblocks):
                nk = nblocks * SB_BLOCK
                start = first_block * SB_BLOCK
                if not isinstance(start, int):
                    start = pl.multiple_of(start, SB_BLOCK)
                copies = [pltpu.make_async_copy(hbm.at[pl.program_id(0), :, :, pl.ds(start, nk)],
                                                buf.at[:, :, pl.ds(0, nk)], sems.at[n])
                          for n, (hbm, buf) in enumerate(((kc_hbm, kbuf), (vc_hbm, vbuf)))]
                for c in copies:
                    c.start()
                for c in copies:
                    c.wait()

                def slabs(buf):
                    piece = lambda hd, j: buf[hd, :, j * SB_BLOCK:(j + 1) * SB_BLOCK].T
                    return lambda j, p: jnp.concatenate([piece(2 * p, j), piece(2 * p + 1, j)], axis=1).astype(BF16)

                return slabs(kbuf), slabs(vbuf)

            def earlier(kb):
                return tuple((lambda p, slab=slab: slab(0, p)) for slab in fetch(kb, 1))

            for slab, win in zip(fetch(q_block - recent, recent), (kwin, vwin)):
                for j in range(recent):
                    for p in range(pairs):
                        win[j * SB_BLOCK:(j + 1) * SB_BLOCK, pair_lanes(p)] = slab(j, p)
            for new, win in ((k_ref, kwin), (v_ref, vwin)):
                win[recent * SB_BLOCK:recent * SB_BLOCK + tq, :] = new[...]
                win[recent * SB_BLOCK + tq:, :] = jnp.zeros((SB_BLOCK - tq, win.shape[1]), BF16)
            sweep(resident(0, SB_WINDOW, (kwin, vwin)), SB_WINDOW, True, None)
            mark_live()
            n_prev = q_block - recent

        def body(st):
            kb, _ = st
            sweep(earlier(kb), 1, False, carry_ref[...])
            return kb - 1, live()

        lax.while_loop(lambda st: jnp.logical_and(st[0] >= 0, st[1]), body, (n_prev - 1, live_ref[0] != 0))
        for p in range(pairs):
            o_ref[row0:row0 + tq, pair_lanes(p)] = jnp.where(head_lanes[0], acc_ref[2 * p * tq:(2 * p + 1) * tq],
                                                              acc_ref[(2 * p + 1) * tq:(2 * p + 2) * tq])

    if q_block is None:
        per_step = q_ref.shape[0] // tq
        for u in range(per_step):
            one_block(pl.program_id(2) * per_step + u, u * tq)
    else:
        one_block(q_block, 0)


def _sb(q16, k16, v16, upair, uo, *, batch, tq, keys, cache=None):
    sub = SB_QBLOCKS_PER_STEP if cache is None else 1
    nq = keys // (sub * SB_BLOCK) if cache is None else 1
    width = SB_PAIRS_PER_STEP * LANES
    assert width == SB_WIDTH or cache is None
    kv_spec = pl.BlockSpec((keys, width), lambda b, p, i: (b, p))
    q_spec = pl.BlockSpec((sub * tq, width), lambda b, p, i: (b * nq + i, p))
    heads = 2 * SB_PAIRS_PER_STEP
    in_specs = [q_spec, kv_spec, kv_spec, _const_spec(upair.shape), _const_spec(uo.shape)]
    scratch = [pltpu.VMEM((heads * tq, SB_BLOCK), F32), pltpu.VMEM((heads * tq, LANES), F32),
               pltpu.SMEM((1,), jnp.int32)]
    operands = [q16, k16, v16, upair, uo]
    q_block = None
    if cache is not None:
        past = cache[0].shape[3]
        assert past % SB_BLOCK == 0 and keys == tq and SB_WINDOW * SB_BLOCK <= past + SB_BLOCK
        q_block = past // SB_BLOCK
        in_specs += [pl.BlockSpec(memory_space=pl.ANY)] * 2
        scratch += ([pltpu.VMEM((SB_HEADS, SB_HEAD_DIM, (SB_WINDOW - 1) * SB_BLOCK), F32)] * 2
                    + [pltpu.SemaphoreType.DMA((2,))]
                    + [pltpu.VMEM((SB_WINDOW * SB_BLOCK, SB_WIDTH), BF16)] * 2)
        operands += list(cache)
    return pl.pallas_call(
        functools.partial(_sb_kernel, tq=tq, q_block=q_block),
        grid=(batch, SB_WIDTH // width, nq),
        in_specs=in_specs,
        out_specs=q_spec,
        out_shape=jax.ShapeDtypeStruct((batch * nq * sub * tq, SB_WIDTH), F32),
        scratch_shapes=scratch,
        compiler_params=_params(3),
        name="sb_prompt" if cache is None else "sb_sample",
    )(*operands)


def _gla_levels(c):
    return int(np.log2(c))


def _gla_tables(c):
    nlev = _gla_levels(c)
    t = np.arange(c)[:, None]
    j = np.arange(c)[None, :]
    mats = [(j <= t), (j > t)]
    for lev in range(1, nlev + 1):
        m = 1 << (lev - 1)
        start = (t // (2 * m)) * (2 * m)
        mats.append(((t % (2 * m)) >= m) & (j >= start + m) & (j <= t))
    for lev in range(1, nlev + 1):
        m = 1 << (lev - 1)
        start = (t // (2 * m)) * (2 * m)
        mats.append(((t % (2 * m)) < m) & (j > t) & (j <= start + m - 1))
    mats = np.concatenate([a.astype(np.float32) for a in mats], axis=0)
    x = t ^ j
    level = np.where(j > t, -1, np.where(x == 0, 0, np.floor(np.log2(np.maximum(x, 1))).astype(np.int64) + 1))
    return jnp.asarray(mats, BF16), jnp.asarray(level, jnp.int32)


def _gla_kernel(q_ref, k_ref, la_ref, v_ref, s0_ref, mats_ref, lv_ref, o_ref, sfin_ref, st_ref):
    c = GLA_CHUNK
    nlev = _gla_levels(c)
    streams, t_rows = q_ref.shape[:2]

    @pl.when(pl.program_id(1) == 0)
    def _():
        st_ref[...] = s0_ref[...]

    lane = lax.broadcasted_iota(jnp.int32, (c, LANES), 1)
    head_lanes = (lane < GLA_HEAD_K, lane >= GLA_HEAD_K)
    lane_sq = lax.broadcasted_iota(jnp.int32, (GLA_HEAD_V, LANES), 1) < GLA_HEAD_K

    def pick(x, hl):
        return jnp.where(hl, x, jnp.zeros_like(x))

    def stack_heads(x):
        return jnp.concatenate([pick(x, hl) for hl in head_lanes], axis=0)

    def one_chunk(rows):
        n_rows = rows.stop - rows.start

        def chunk(x):
            return x if n_rows == c else jnp.concatenate([x, jnp.zeros((c - n_rows, x.shape[1]), x.dtype)], axis=0)

        def finish(s, p, scores, qe, ke, b_last):
            sl = slice(p * LANES, (p + 1) * LANES)
            st = st_ref[s, p]
            inter = _dot_nt(stack_heads(qe[:, sl]), st.astype(BF16))
            upd = []
            for r in range(2):
                hd = 2 * p + r
                vh = chunk(v_ref[s, rows, hd * GLA_HEAD_V:(hd + 1) * GLA_HEAD_V])
                o_ref[s, rows, hd * GLA_HEAD_V:(hd + 1) * GLA_HEAD_V] = (
                    inter[r * c:(r + 1) * c] + _dot(scores[r * c:(r + 1) * c], vh))[:n_rows]
                upd.append(_dot_tn(vh, ke[:, sl]))
            st_ref[s, p] = st * jnp.exp(b_last[:, sl]) + jnp.where(lane_sq, upd[0], upd[1])

        la_all = jnp.concatenate([chunk(la_ref[s, rows, :]) for s in range(streams)], axis=1)
        b_all = _dot_hilo_left(mats_ref[:c, :], la_all)
        mild = jnp.min(b_all[c - 1:c, :]) >= GLA_DIRECT_MIN

        @pl.when(mild)
        def _():
            rowi = lax.broadcasted_iota(jnp.int32, (2 * c, c), 0) & (c - 1)
            causal = lax.broadcasted_iota(jnp.int32, (2 * c, c), 1) <= rowi
            for s in range(streams):
                b = b_all[:, s * GLA_KEY_WIDTH:(s + 1) * GLA_KEY_WIDTH]
                b_last = b[c - 1:c, :]
                q = chunk(q_ref[s, rows, :])
                k = chunk(k_ref[s, rows, :])
                qe = (q * jnp.exp(b)).astype(BF16)
                kd = (k * jnp.exp(-b)).astype(BF16)
                ke = (k * jnp.exp(b_last - b)).astype(BF16)
                for p in range(GLA_HEADS // 2):
                    sl = slice(p * LANES, (p + 1) * LANES)
                    scores = jnp.where(causal, _dot_nt(stack_heads(qe[:, sl]), kd[:, sl]), 0.0)
                    finish(s, p, scores.astype(BF16), qe, ke, b_last)

        @pl.when(jnp.logical_not(mild))
        def _():
            lv = jnp.concatenate([lv_ref[...], lv_ref[...]], axis=0)
            for s in range(streams):
                q = chunk(q_ref[s, rows, :])
                k = chunk(k_ref[s, rows, :])
                sums = _dot_hilo_left(mats_ref[...], chunk(la_ref[s, rows, :]))

                def level_rows(i):
                    return sums[i * c:(i + 1) * c]

                qe = (q * jnp.exp(level_rows(0))).astype(BF16)
                ke = (k * jnp.exp(level_rows(1))).astype(BF16)
                q_lev = [q.astype(BF16)] + [(q * jnp.exp(level_rows(1 + lev))).astype(BF16)
                                            for lev in range(1, nlev + 1)]
                k_lev = [k.astype(BF16)] + [(k * jnp.exp(level_rows(1 + nlev + lev))).astype(BF16)
                                            for lev in range(1, nlev + 1)]
                for p in range(GLA_HEADS // 2):
                    sl = slice(p * LANES, (p + 1) * LANES)
                    scores = jnp.zeros((2 * c, c), F32)
                    for lev in range(nlev + 1):
                        scores = jnp.where(lv == lev, _dot_nt(stack_heads(q_lev[lev][:, sl]), k_lev[lev][:, sl]),
                                           scores)
                    finish(s, p, scores.astype(BF16), qe, ke, level_rows(0)[c - 1:c, :])

    if t_rows < c:
        one_chunk(slice(0, t_rows))
    else:
        for u in range(t_rows // c):
            one_chunk(slice(u * c, (u + 1) * c))
    sfin_ref[...] = st_ref[...]


def _dot_hilo_left(m, x):
    hi, lo = _split_bf16(x)
    return _dot(m, hi) + _dot(m, lo)


def _gla(qg, kg, la, vg16, s0, mats, lv, *, batch, seq):
    c = min(GLA_CHUNK * GLA_CHUNKS_PER_STEP, seq)
    assert seq % c == 0 and (c < GLA_CHUNK or c % GLA_CHUNK == 0)
    g = GLA_BATCH_PER_STEP
    pairs = GLA_HEADS // 2
    blk = lambda width: pl.BlockSpec((g, c, width), lambda b, t: (b, t, 0))
    st_spec = pl.BlockSpec((g, pairs, GLA_HEAD_V, LANES), lambda b, t: (b, 0, 0, 0))
    per_stream = lambda a: a.reshape(batch, seq, a.shape[-1])
    o, s_fin = pl.pallas_call(
        _gla_kernel,
        grid=(batch // g, seq // c),
        in_specs=[blk(GLA_KEY_WIDTH), blk(GLA_KEY_WIDTH), blk(GLA_KEY_WIDTH), blk(GLA_WIDTH), st_spec,
                  _const_spec(mats.shape), _const_spec(lv.shape)],
        out_specs=[blk(GLA_WIDTH), st_spec],
        out_shape=[jax.ShapeDtypeStruct((batch, seq, GLA_WIDTH), F32),
                   jax.ShapeDtypeStruct((batch, pairs, GLA_HEAD_V, LANES), F32)],
        scratch_shapes=[pltpu.VMEM((g, pairs, GLA_HEAD_V, LANES), F32)],
        compiler_params=_params(2),
        name="gla",
    )(per_stream(qg), per_stream(kg), per_stream(la), per_stream(vg16), s0, mats, lv)
    return o.reshape(batch * seq, GLA_WIDTH), s_fin


def _mixout_kernel(x_ref, osb_ref, og_ref, r_ref, gsb_ref, ggla_ref, w_ref,
                   g_ref, wg_ref, wu_ref, wd_ref, gfin_ref, o_ref, a_ref):
    o_sb = _head_norm_lanes(osb_ref[...], SB_HEAD_DIM, gsb_ref[...])
    r = r_ref[...]
    o_g = _head_norm_lanes(og_ref[...], GLA_HEAD_V, ggla_ref[...]) * (r * jax.nn.sigmoid(r))
    mix = _dot(o_sb.astype(BF16), w_ref[:SB_WIDTH, :]) + _dot(o_g.astype(BF16), w_ref[SB_WIDTH:, :])
    y = _half_step_ffn(x_ref[...] + mix, g_ref, wg_ref, wu_ref, wd_ref, a_ref)
    o_ref[...] = _rms(y, gfin_ref[...])


def _mixout(x, osb, og, r, gsb, ggla, w, g, wg, wu, wd, gfin, *, tm):
    n = x.shape[0]
    row = lambda i: (i, 0)
    return pl.pallas_call(
        _mixout_kernel,
        grid=(n // tm,),
        in_specs=[pl.BlockSpec((tm, D_MODEL), row),
                  pl.BlockSpec((tm, SB_WIDTH), row),
                  pl.BlockSpec((tm, GLA_WIDTH), row),
                  pl.BlockSpec((tm, GLA_WIDTH), row),
                  _const_spec((1, SB_WIDTH)),
                  _const_spec((1, GLA_WIDTH)),
                  _const_spec((D_MODEL, D_MODEL))] + _ffn_specs() + [_const_spec((1, D_MODEL))],
        out_specs=pl.BlockSpec((tm, D_MODEL), row),
        out_shape=jax.ShapeDtypeStruct((n, D_MODEL), F32),
        scratch_shapes=[pltpu.VMEM((tm, D_FF), BF16)],
        compiler_params=_params(1),
        name="mixer_out_ffn2",
    )(x, osb, og, r, gsb, ggla, w, g, wg, wu, wd, gfin)


def _suffix_matrices():
    j = np.arange(SB_BLOCK)[:, None]
    s = np.arange(SB_BLOCK)[None, :]
    later = j > s
    zero = np.zeros_like(later)
    pair = np.block([[later, zero], [zero, later]])
    single = np.concatenate([later, np.ones_like(later)], axis=1)
    twice = lambda m: jnp.asarray(np.concatenate([m, m], axis=0), BF16)
    return twice(pair), twice(single)


def _state_to_kernel(s):
    b = s.shape[0]
    s = s.reshape(b, GLA_HEADS // 2, 2, GLA_HEAD_K, GLA_HEAD_V)
    return s.transpose(0, 1, 4, 2, 3).reshape(b, GLA_HEADS // 2, GLA_HEAD_V, LANES)


def _state_from_kernel(s):
    b = s.shape[0]
    s = s.reshape(b, GLA_HEADS // 2, GLA_HEAD_V, 2, GLA_HEAD_K)
    return s.transpose(0, 1, 3, 4, 2).reshape(b, GLA_HEADS, GLA_HEAD_K, GLA_HEAD_V)


def _layer(x1, lw, consts, *, batch, seq, tm, cache=None, state=None):
    (g_mix, w_in, wgu, bg, gq, gk, gsb, ggla, w_out, g2, w2g, w2u, w2d, gfin) = lw
    upair, uo, mats, lv = consts
    (qsb16, ksb, ksb16, vsb, vsb16, qg, kg, vg16, r, la) = _mixin(x1, g_mix, w_in, wgu, bg, gq, gk, tm=tm)

    if cache is None:
        o_sb = _sb(qsb16, ksb16, vsb16, upair, uo, batch=batch, tq=SB_BLOCK, keys=seq)
    else:
        o_sb = _sb(qsb16, ksb16, vsb16, upair, uo, batch=batch, tq=seq, keys=seq, cache=cache)

    if state is None:
        state = jnp.zeros((batch, GLA_HEADS // 2, GLA_HEAD_V, LANES), F32)
    o_g, s_fin = _gla(qg, kg, la, vg16, state, mats, lv, batch=batch, seq=seq)

    y = _mixout(x1, o_sb, o_g, r, gsb, ggla, w_out, g2, w2g, w2u, w2d, gfin, tm=tm)
    return y, ksb, vsb, _state_from_kernel(s_fin)


def kernel(x_prompt, x_sample, cache_sb_k, cache_sb_v, state_gla, g_ffn1, w_ffn1_gate, w_ffn1_up,
           w_ffn1_down, g_mix, w_in, w_gate_up, b_gate, g_q, g_k, g_sb_out, g_gla_out, w_out,
           g_ffn2, w_ffn2_gate, w_ffn2_up, w_ffn2_down, g_final):
    depth = w_in.shape[0]
    batch, seq, _ = x_prompt.shape
    dec_batch, dec_seq, _ = x_sample.shape
    consts = (*_suffix_matrices(), *_gla_tables(GLA_CHUNK))

    y_p = x_prompt.reshape(batch * seq, D_MODEL)
    y_s = x_sample.reshape(dec_batch * dec_seq, D_MODEL)
    outs = [[] for _ in range(6)]
    for l in range(depth):
        row = lambda a: a[l].reshape(1, -1)
        x1_s, ffn1 = _ffn_stream(y_s, row(g_ffn1), w_ffn1_gate[l], w_ffn1_up[l], w_ffn1_down[l])
        x1_p, (w2g, w2u, w2d, w_in16, w_out16) = _ffn(
            y_p, row(g_ffn1), *ffn1, tm=FFN1_ROWS,
            cast=[w_ffn2_gate[l], w_ffn2_up[l], w_ffn2_down[l], (w_in[l].T, IN_PAD), w_out[l]])
        lw = (row(g_mix), w_in16,
              jnp.pad(w_gate_up[l], ((0, GATE_PAD - GATE_RANK), (0, 0))).astype(BF16), row(b_gate),
              jnp.tile(g_q[l], SB_HEADS).reshape(1, -1), jnp.tile(g_k[l], SB_HEADS).reshape(1, -1),
              jnp.tile(g_sb_out[l], SB_HEADS).reshape(1, -1), jnp.tile(g_gla_out[l], GLA_HEADS).reshape(1, -1),
              w_out16, row(g_ffn2), w2g, w2u, w2d, row(g_final))
        y_p, k_p, v_p, s_p = _layer(x1_p, lw, consts, batch=batch, seq=seq, tm=512)
        cache = (cache_sb_k[l].transpose(0, 2, 3, 1), cache_sb_v[l].transpose(0, 2, 3, 1))
        y_s, k_s, v_s, s_s = _layer(x1_s, lw, consts, batch=dec_batch, seq=dec_seq, tm=256,
                                    cache=cache, state=_state_to_kernel(state_gla[l]))
        shape_p = (batch, seq, SB_HEADS, SB_HEAD_DIM)
        shape_s = (dec_batch, dec_seq, SB_HEADS, SB_HEAD_DIM)
        for lst, val in zip(outs, (k_p.reshape(shape_p), v_p.reshape(shape_p), s_p,
                                   k_s.reshape(shape_s), v_s.reshape(shape_s), s_s)):
            lst.append(val)
    return (y_p.reshape(batch, seq, D_MODEL), y_s.reshape(dec_batch, dec_seq, D_MODEL),
            *[jnp.stack(o) for o in outs])
```

```python
import functools

import numpy as np
import jax
import jax.numpy as jnp
from jax import lax
from jax.experimental import pallas as pl
from jax.experimental.pallas import tpu as pltpu

F32 = jnp.float32
BF16 = jnp.bfloat16

D_MODEL = 1024
SB_HEADS = 8
SB_HEAD_DIM = 64
SB_WIDTH = SB_HEADS * SB_HEAD_DIM
SB_SCALE = SB_HEAD_DIM ** -0.5
GLA_HEADS = 4
GLA_HEAD_K = 64
GLA_HEAD_V = 128
GLA_KEY_WIDTH = GLA_HEADS * GLA_HEAD_K
GLA_WIDTH = GLA_HEADS * GLA_HEAD_V
GLA_SCALE = GLA_HEAD_K ** -0.5
GATE_RANK = 16
GATE_TAU = 16.0
D_FF = 2816
EPS = 1e-6

LANES = 128
SUBLANES = 8
BF16_SUBLANES = 16
GATE_PAD = LANES
MAIN_WIDTH = 3 * SB_WIDTH + 2 * GLA_KEY_WIDTH + 2 * GLA_WIDTH
IN_PAD = MAIN_WIDTH + GATE_PAD

FF_CHUNK = 256
FFN1_ROWS = 1024
FF_STREAM_CHUNK = 256
SB_BLOCK = 128
SB_WINDOW = 3
SB_PAIRS_PER_STEP = 4
SB_QBLOCKS_PER_STEP = 4
HALF_WINDOW = 2
GLA_CHUNK = 128
GLA_BATCH_PER_STEP = 2
GLA_CHUNKS_PER_STEP = 2
GLA_DIRECT_MIN = -40.0
SB_STOP = 152.0
LOG2_E = 1.4426950408889634

VMEM_LIMIT = 56 * 1024 * 1024


def _dot(a, b):
    return jnp.dot(a, b, preferred_element_type=F32)


def _dot_nt(a, b):
    return lax.dot_general(a, b, (((1,), (1,)), ((), ())), preferred_element_type=F32)


def _dot_tn(a, b):
    return lax.dot_general(a, b, (((0,), (0,)), ((), ())), preferred_element_type=F32)


def _split_bf16(x):
    hi = x.astype(BF16)
    lo = (x - hi.astype(F32)).astype(BF16)
    return hi, lo


def _rms(x, g):
    ms = jnp.mean(x * x, axis=-1, keepdims=True)
    return x * lax.rsqrt(ms + EPS) * g


def _log_sigmoid(x):
    return jnp.minimum(x, 0.0) - jnp.log1p(jnp.exp(-jnp.abs(x)))


def _const_spec(shape):
    nd = len(shape)
    return pl.BlockSpec(shape, lambda *_: (0,) * nd, pipeline_mode=pl.Buffered(1))


def _params(n_grid):
    return pltpu.CompilerParams(dimension_semantics=("arbitrary",) * n_grid,
                                vmem_limit_bytes=VMEM_LIMIT)


def _half_step_ffn(x, g_ref, wg_ref, wu_ref, wd_ref, a_ref):
    h = _rms(x, g_ref[...]).astype(BF16)
    for c in range(D_FF // FF_CHUNK):
        sl = slice(c * FF_CHUNK, (c + 1) * FF_CHUNK)
        gt = _dot(h, wg_ref[:, sl])
        up = _dot(h, wu_ref[:, sl])
        a_ref[:, sl] = (gt * jax.nn.sigmoid(gt) * up).astype(BF16)
    return x + 0.5 * _dot(a_ref[...], wd_ref[...])


def _ffn_kernel(x_ref, g_ref, wg_ref, wu_ref, wd_ref, *refs, transposed):
    n_cast = (len(refs) - 2) // 2
    o_ref, a_ref = refs[n_cast], refs[-1]
    o_ref[...] = _half_step_ffn(x_ref[...], g_ref, wg_ref, wu_ref, wd_ref, a_ref)
    for src, dst, tr in zip(refs[:n_cast], refs[n_cast + 1:-1], transposed):
        if tr is None:
            dst[...] = src[...].astype(BF16)
        else:
            rows, blocks = tr
            first = jnp.minimum(pl.program_id(0), blocks - 1) * src.shape[0]
            inside = lax.broadcasted_iota(jnp.int32, src.shape, 0) < rows - first
            dst[...] = jnp.where(inside, src[...], 0.0).T.astype(BF16)


def _ffn_specs():
    return [_const_spec((1, D_MODEL)), _const_spec((D_MODEL, D_FF)), _const_spec((D_MODEL, D_FF)),
            _const_spec((D_FF, D_MODEL))]


def _cast_block_rows(rows, steps):
    br = next(b for b in range(BF16_SUBLANES, rows + 1, BF16_SUBLANES) if rows % b == 0 and rows // b <= steps)
    return br, rows // br


def _ffn(x, g, wg, wu, wd, *, tm, cast=()):
    n = x.shape[0]
    steps = n // tm
    row = lambda i: (i, 0)
    cast_in, cast_out, cast_shape, transposed = [], [], [], []
    for w in cast:
        if isinstance(w, tuple):
            w, width = w
            tb = LANES * -(-width // (LANES * steps))
            nblk = -(-width // tb)
            width = nblk * tb
            assert nblk <= steps and w.shape[0] <= width
            cast_in.append(pl.BlockSpec((tb, w.shape[1]), lambda i, last=nblk - 1: (jnp.minimum(i, last), 0)))
            cast_out.append(pl.BlockSpec((w.shape[1], tb), lambda i, last=nblk - 1: (0, jnp.minimum(i, last))))
            cast_shape.append(jax.ShapeDtypeStruct((w.shape[1], width), BF16))
            transposed.append((w.shape[0], nblk))
        else:
            br, nblk = _cast_block_rows(w.shape[0], steps)
            idx = lambda i, last=nblk - 1: (jnp.minimum(i, last), 0)
            cast_in.append(pl.BlockSpec((br, w.shape[1]), idx))
            cast_out.append(pl.BlockSpec((br, w.shape[1]), idx))
            cast_shape.append(jax.ShapeDtypeStruct(w.shape, BF16))
            transposed.append(None)
    cast = [w[0] if isinstance(w, tuple) else w for w in cast]
    out = pl.pallas_call(
        functools.partial(_ffn_kernel, transposed=tuple(transposed)),
        grid=(steps,),
        in_specs=[pl.BlockSpec((tm, D_MODEL), row)] + _ffn_specs() + cast_in,
        out_specs=[pl.BlockSpec((tm, D_MODEL), row)] + cast_out,
        out_shape=[jax.ShapeDtypeStruct((n, D_MODEL), F32)] + cast_shape,
        scratch_shapes=[pltpu.VMEM((tm, D_FF), BF16)],
        compiler_params=_params(1),
        name="ffn1",
    )(x, g, wg, wu, wd, *cast)
    return out[0], out[1:]


def _ffn_stream_kernel(x_ref, g_ref, wg_ref, wu_ref, wd_ref, o_ref, wg16_ref, wu16_ref, wd16_ref,
                       h_ref, acc_ref):
    c = pl.program_id(0)

    @pl.when(c == 0)
    def _():
        h_ref[...] = _rms(x_ref[...], g_ref[...]).astype(BF16)
        acc_ref[...] = jnp.zeros_like(acc_ref)

    wg = wg_ref[...].astype(BF16)
    wu = wu_ref[...].astype(BF16)
    wd = wd_ref[...].astype(BF16)
    wg16_ref[...] = wg
    wu16_ref[...] = wu
    wd16_ref[...] = wd
    h = h_ref[...]
    gt = _dot(h, wg)
    acc_ref[...] += _dot((gt * jax.nn.sigmoid(gt) * _dot(h, wu)).astype(BF16), wd)

    @pl.when(c == pl.num_programs(0) - 1)
    def _():
        o_ref[...] = x_ref[...] + 0.5 * acc_ref[...]


def _ffn_stream(x, g, wg, wu, wd):
    n = x.shape[0]
    whole = lambda shape: pl.BlockSpec(shape, lambda c: (0, 0))
    cols = pl.BlockSpec((D_MODEL, FF_STREAM_CHUNK), lambda c: (0, c))
    rows = pl.BlockSpec((FF_STREAM_CHUNK, D_MODEL), lambda c: (c, 0))
    out = pl.pallas_call(
        _ffn_stream_kernel,
        grid=(D_FF // FF_STREAM_CHUNK,),
        in_specs=[whole((n, D_MODEL)), whole((1, D_MODEL)), cols, cols, rows],
        out_specs=[whole((n, D_MODEL)), cols, cols, rows],
        out_shape=[jax.ShapeDtypeStruct((n, D_MODEL), F32),
                   jax.ShapeDtypeStruct((D_MODEL, D_FF), BF16), jax.ShapeDtypeStruct((D_MODEL, D_FF), BF16),
                   jax.ShapeDtypeStruct((D_FF, D_MODEL), BF16)],
        scratch_shapes=[pltpu.VMEM((n, D_MODEL), BF16), pltpu.VMEM((n, D_MODEL), F32)],
        compiler_params=_params(1),
        name="ffn1_stream",
    )(x, g, wg, wu, wd)
    return out[0], out[1:]


def _head_norm_lanes(y, head, gain):
    tm, width = y.shape
    low = lax.broadcasted_iota(jnp.int32, (tm, LANES), 1) < head
    cols = []
    for j in range(width // LANES):
        yj = y[:, j * LANES:(j + 1) * LANES]
        sq = yj * yj

        def inv_rms(part):
            return lax.rsqrt(jnp.sum(part, axis=1, keepdims=True) * (1.0 / head) + EPS)

        if head == LANES:
            cols.append(yj * inv_rms(sq))
        else:
            cols.append(yj * jnp.where(low, inv_rms(jnp.where(low, sq, 0.0)), inv_rms(jnp.where(low, 0.0, sq))))
    return jnp.concatenate(cols, axis=1) * gain


def _store_heads(ref, y):
    groups = y.shape[0] // SUBLANES
    sub = lax.broadcasted_iota(jnp.int32, (groups, SUBLANES, LANES), 1)
    rows = []
    for j in range(SB_WIDTH // LANES):
        pair = y[:, j * LANES:(j + 1) * LANES].reshape(groups, SUBLANES, LANES)
        rows += [pair, pltpu.roll(pair, SB_HEAD_DIM, axis=2)]
    for dist in (4, 2, 1):
        low = (sub & dist) == 0
        new = list(rows)
        for h in range(SUBLANES):
            if h & dist == 0:
                a, b = rows[h], rows[h + dist]
                new[h] = jnp.where(low, a, pltpu.roll(b, dist, axis=1))
                new[h + dist] = jnp.where(low, pltpu.roll(a, SUBLANES - dist, axis=1), b)
        rows = new
    for t in range(SUBLANES):
        ref[:, t] = rows[t][:, :, :SB_HEAD_DIM]


def _mixin_kernel(x_ref, g_ref, w_ref, wgu_ref, bg_ref, gq_ref, gk_ref,
                  qsb_ref, ksb_ref, ksb16_ref, vsb_ref, vsb16_ref,
                  qg_ref, kg_ref, vg16_ref, r_ref, la_ref):
    h = _rms(x_ref[...], g_ref[...]).astype(BF16)

    def proj(lo, width):
        return _dot(h, w_ref[:, lo:lo + width])

    q = _head_norm_lanes(proj(0, SB_WIDTH), SB_HEAD_DIM, gq_ref[...])
    qsb_ref[...] = (q * SB_SCALE).astype(BF16)
    k = _head_norm_lanes(proj(SB_WIDTH, SB_WIDTH), SB_HEAD_DIM, gk_ref[...])
    _store_heads(ksb_ref, k)
    ksb16_ref[...] = k.astype(BF16)
    v = proj(2 * SB_WIDTH, SB_WIDTH)
    _store_heads(vsb_ref, v)
    vsb16_ref[...] = v.astype(BF16)
    off = 3 * SB_WIDTH
    qg_ref[...] = proj(off, GLA_KEY_WIDTH) * GLA_SCALE
    kg_ref[...] = proj(off + GLA_KEY_WIDTH, GLA_KEY_WIDTH)
    vg16_ref[...] = proj(off + 2 * GLA_KEY_WIDTH, GLA_WIDTH).astype(BF16)
    r_ref[...] = proj(off + 2 * GLA_KEY_WIDTH + GLA_WIDTH, GLA_WIDTH)
    lr = proj(MAIN_WIDTH, GATE_PAD).astype(BF16)
    gate = _dot(lr, wgu_ref[...]) + bg_ref[...]
    la_ref[...] = _log_sigmoid(gate) * (1.0 / GATE_TAU)


def _mixin(x, g, w, wgu, bg, gq, gk, *, tm):
    n = x.shape[0]
    row = lambda i: (i, 0)
    heads = (SB_HEADS, SB_HEAD_DIM)
    assert SB_HEADS == SUBLANES and tm % SUBLANES == 0
    lead = lambda rows, wd: (rows // SUBLANES, SUBLANES) if len(wd) == 2 else (rows,)
    widths = [((SB_WIDTH,), BF16), (heads, F32), ((SB_WIDTH,), BF16), (heads, F32), ((SB_WIDTH,), BF16),
              ((GLA_KEY_WIDTH,), F32), ((GLA_KEY_WIDTH,), F32), ((GLA_WIDTH,), BF16), ((GLA_WIDTH,), F32),
              ((GLA_KEY_WIDTH,), F32)]
    return pl.pallas_call(
        _mixin_kernel,
        grid=(n // tm,),
        in_specs=[pl.BlockSpec((tm, D_MODEL), row),
                  _const_spec((1, D_MODEL)),
                  _const_spec(w.shape),
                  _const_spec((GATE_PAD, GLA_KEY_WIDTH)),
                  _const_spec((1, GLA_KEY_WIDTH)),
                  _const_spec((1, SB_WIDTH)),
                  _const_spec((1, SB_WIDTH))],
        out_specs=[pl.BlockSpec(lead(tm, wd) + wd, lambda i, nd=len(lead(tm, wd) + wd) - 1: (i,) + (0,) * nd)
                   for wd, _ in widths],
        out_shape=[jax.ShapeDtypeStruct(lead(n, wd) + wd, dt) for wd, dt in widths],
        compiler_params=_params(1),
        name="mixer_in",
    )(x, g, w, wgu, bg, gq, gk)


def _neg_abs(x):
    return -jnp.abs(x)


def _sb_log2_terms(z):
    z2 = z * LOG2_E
    t = jnp.log2(1.0 + jnp.exp2(_neg_abs(z2)))
    return jnp.maximum(z2, 0.0) + t, jnp.minimum(z2, 0.0) - t


def _sb_kernel(q_ref, k_ref, v_ref, upair_ref, uo_ref, *refs, tq, q_block):
    if q_block is None:
        o_ref, carry_ref, acc_ref, live_ref = refs
    else:
        kc_hbm, vc_hbm, o_ref, carry_ref, acc_ref, live_ref, kbuf, vbuf, sems, kwin, vwin = refs
    pairs = q_ref.shape[1] // LANES
    heads = 2 * pairs

    def one_block(i, row0):
        lane = lax.broadcasted_iota(jnp.int32, (tq, LANES), 1)
        head_lanes = (lane < SB_HEAD_DIM, lane >= SB_HEAD_DIM)
        rowi = lax.broadcasted_iota(jnp.int32, (heads * tq, SB_BLOCK), 0)
        coli = lax.broadcasted_iota(jnp.int32, (heads * tq, SB_BLOCK), 1)
        causal = coli < (rowi & (tq - 1))

        def pair_lanes(p):
            return slice(p * LANES, (p + 1) * LANES)

        def q_pair(p):
            q2 = q_ref[row0:row0 + tq, pair_lanes(p)]
            zero = jnp.zeros_like(q2)
            return jnp.concatenate([jnp.where(hl, q2, zero) for hl in head_lanes], axis=0)

        def rows(ref, p, first_block, nblocks):
            start = first_block * SB_BLOCK
            if not isinstance(start, int):
                start = pl.multiple_of(start, SB_BLOCK)
            return ref[pl.ds(start, nblocks * SB_BLOCK), pair_lanes(p)]

        def suffix_sums(x):
            nb = x.shape[1] // SB_BLOCK
            hi, lo = _split_bf16(x)

            def mm(cols, m2):
                return _dot(jnp.concatenate([hi[:, cols], lo[:, cols]], axis=1), m2)

            within, total = [], []
            if nb >= 3:
                for j in range(0, nb - 1, 2):
                    y = mm(slice(j * SB_BLOCK, (j + 2) * SB_BLOCK), upair_ref[...])
                    within += [y[:, :SB_BLOCK], y[:, SB_BLOCK:]]
                    total += [jnp.sum(x[:, b * SB_BLOCK:(b + 1) * SB_BLOCK], axis=1, keepdims=True) for b in (j, j + 1)]
            for j in range(len(within), nb):
                y = mm(slice(j * SB_BLOCK, (j + 1) * SB_BLOCK), uo_ref[...])
                within.append(y[:, :SB_BLOCK])
                total.append(y[:, SB_BLOCK:])
            later = total[nb - 1]
            out = [within[nb - 1]]
            for j in range(nb - 2, -1, -1):
                out.insert(0, within[j] + later)
                later = later + total[j]
            return (out[0] if nb == 1 else jnp.concatenate(out, axis=1)), later

        def resident(first_block, nblocks, kv_refs=(k_ref, v_ref)):
            return tuple((lambda p, ref=ref: rows(ref, p, first_block, nblocks)) for ref in kv_refs)

        def sweep(kv, nblocks, own, carry, valid=None):
            nk = nblocks * SB_BLOCK

            def masked(x):
                if valid is not None:
                    return jnp.where(valid, x, 0.0)
                if not own:
                    return x
                tail = jnp.where(causal, x[:, nk - SB_BLOCK:], 0.0)
                return tail if nblocks == 1 else jnp.concatenate([x[:, :nk - SB_BLOCK], tail], axis=1)

            z = jnp.concatenate([_dot_nt(q_pair(p), kv[0](p)) for p in range(pairs)], axis=0)
            fail, log_beta = _sb_log2_terms(z)
            after, total = suffix_sums(masked(fail))
            if carry is not None:
                after = after + carry
            w = masked(jnp.exp2(log_beta - after)).astype(BF16)
            pv = jnp.concatenate([_dot(w[2 * p * tq:(2 * p + 2) * tq], kv[1](p))
                                  for p in range(pairs)], axis=0)
            if carry is None:
                carry_ref[...] = total
                acc_ref[...] = pv
            else:
                carry_ref[...] = carry + total
                acc_ref[...] += pv

        def live():
            return jnp.min(carry_ref[...]) < SB_STOP

        def mark_live():
            live_ref[0] = jnp.where(live(), 1, 0)

        half = SB_BLOCK // 2
        assert HALF_WINDOW == 2

        def by_halves(x, f):
            y = [f(s, jnp.concatenate([x[s * half:(s + 1) * half], x[tq + s * half:tq + (s + 1) * half]], axis=0))
                 for s in range(2)]
            return jnp.concatenate([y[0][:half], y[1][:half], y[0][half:], y[1][half:]], axis=0)

        def half_windows():
            nk = HALF_WINDOW * SB_BLOCK
            starts = [pl.multiple_of(i * SB_BLOCK + (s + 1) * half - nk, half) for s in range(2)]
            keep = coli < (rowi & (half - 1)) + half

            def masked(x):
                return jnp.concatenate([x[:, :SB_BLOCK], jnp.where(keep, x[:, SB_BLOCK:], 0.0)], axis=1)

            z = jnp.concatenate(
                [by_halves(q_pair(p), lambda s, lhs: _dot_nt(lhs, k_ref[pl.ds(starts[s], nk), pair_lanes(p)]))
                 for p in range(pairs)], axis=0)
            fail, log_beta = _sb_log2_terms(z)
            after, total = suffix_sums(masked(fail))
            w = masked(jnp.exp2(log_beta - after)).astype(BF16)
            carry_ref[...] = total
            acc_ref[...] = jnp.concatenate(
                [by_halves(w[2 * p * tq:(2 * p + 2) * tq],
                           lambda s, lhs: _dot(lhs, v_ref[pl.ds(starts[s], nk), pair_lanes(p)]))
                 for p in range(pairs)], axis=0)

        def own_block_only():
            sweep(resident(i, 1), 1, True, None)

        if q_block is None:
            assert tq == SB_BLOCK
            windowed = i >= HALF_WINDOW
            pl.when(windowed)(half_windows)
            pl.when(jnp.logical_not(windowed))(own_block_only)

            mark_live()

            @pl.when(jnp.logical_and(windowed, live_ref[0] != 0))
            def _():
                unseen = jnp.logical_or(coli < half, (rowi & half) != 0)
                sweep(resident(i - HALF_WINDOW, 1), 1, False, carry_ref[...], unseen)
                mark_live()

            n_prev = jnp.where(windowed, i - HALF_WINDOW, i)
            earlier = lambda kb: resident(kb, 1)
        else:
            recent = SB_WINDOW - 1
            assert q_block >= recent and k_ref.shape[0] == tq <= SB_BLOCK

            def fetch(first_block, nblocks):
                nk = nblocks * SB_BLOCK
                start = first_block * SB_BLOCK
                if not isinstance(start, int):
                    start = pl.multiple_of(start, SB_BLOCK)
                copies = [pltpu.make_async_copy(hbm.at[pl.program_id(0), :, :, pl.ds(start, nk)],
                                                buf.at[:, :, pl.ds(0, nk)], sems.at[n])
                          for n, (hbm, buf) in enumerate(((kc_hbm, kbuf), (vc_hbm, vbuf)))]
                for c in copies:
                    c.start()
                for c in copies:
                    c.wait()

                def slabs(buf):
                    piece = lambda hd, j: buf[hd, :, j * SB_BLOCK:(j + 1) * SB_BLOCK].T
                    return lambda j, p: jnp.concatenate([piece(2 * p, j), piece(2 * p + 1, j)], axis=1).astype(BF16)

                return slabs(kbuf), slabs(vbuf)

            def earlier(kb):
                return tuple((lambda p, slab=slab: slab(0, p)) for slab in fetch(kb, 1))

            for slab, win in zip(fetch(q_block - recent, recent), (kwin, vwin)):
                for j in range(recent):
                    for p in range(pairs):
                        win[j * SB_BLOCK:(j + 1) * SB_BLOCK, pair_lanes(p)] = slab(j, p)
            for new, win in ((k_ref, kwin), (v_ref, vwin)):
                win[recent * SB_BLOCK:recent * SB_BLOCK + tq, :] = new[...]
                win[recent * SB_BLOCK + tq:, :] = jnp.zeros((SB_BLOCK - tq, win.shape[1]), BF16)
            sweep(resident(0, SB_WINDOW, (kwin, vwin)), SB_WINDOW, True, None)
            mark_live()
            n_prev = q_block - recent

        def body(st):
            kb, _ = st
            sweep(earlier(kb), 1, False, carry_ref[...])
            return kb - 1, live()

        lax.while_loop(lambda st: jnp.logical_and(st[0] >= 0, st[1]), body, (n_prev - 1, live_ref[0] != 0))
        for p in range(pairs):
            o_ref[row0:row0 + tq, pair_lanes(p)] = jnp.where(head_lanes[0], acc_ref[2 * p * tq:(2 * p + 1) * tq],
                                                              acc_ref[(2 * p + 1) * tq:(2 * p + 2) * tq])

    if q_block is None:
        per_step = q_ref.shape[0] // tq
        for u in range(per_step):
            one_block(pl.program_id(2) * per_step + u, u * tq)
    else:
        one_block(q_block, 0)


def _sb(q16, k16, v16, upair, uo, *, batch, tq, keys, cache=None):
    sub = SB_QBLOCKS_PER_STEP if cache is None else 1
    nq = keys // (sub * SB_BLOCK) if cache is None else 1
    width = SB_PAIRS_PER_STEP * LANES
    assert width == SB_WIDTH or cache is None
    kv_spec = pl.BlockSpec((keys, width), lambda b, p, i: (b, p))
    q_spec = pl.BlockSpec((sub * tq, width), lambda b, p, i: (b * nq + i, p))
    heads = 2 * SB_PAIRS_PER_STEP
    in_specs = [q_spec, kv_spec, kv_spec, _const_spec(upair.shape), _const_spec(uo.shape)]
    scratch = [pltpu.VMEM((heads * tq, SB_BLOCK), F32), pltpu.VMEM((heads * tq, LANES), F32),
               pltpu.SMEM((1,), jnp.int32)]
    operands = [q16, k16, v16, upair, uo]
    q_block = None
    if cache is not None:
        past = cache[0].shape[3]
        assert past % SB_BLOCK == 0 and keys == tq and SB_WINDOW * SB_BLOCK <= past + SB_BLOCK
        q_block = past // SB_BLOCK
        in_specs += [pl.BlockSpec(memory_space=pl.ANY)] * 2
        scratch += ([pltpu.VMEM((SB_HEADS, SB_HEAD_DIM, (SB_WINDOW - 1) * SB_BLOCK), F32)] * 2
                    + [pltpu.SemaphoreType.DMA((2,))]
                    + [pltpu.VMEM((SB_WINDOW * SB_BLOCK, SB_WIDTH), BF16)] * 2)
        operands += list(cache)
    return pl.pallas_call(
        functools.partial(_sb_kernel, tq=tq, q_block=q_block),
        grid=(batch, SB_WIDTH // width, nq),
        in_specs=in_specs,
        out_specs=q_spec,
        out_shape=jax.ShapeDtypeStruct((batch * nq * sub * tq, SB_WIDTH), F32),
        scratch_shapes=scratch,
        compiler_params=_params(3),
        name="sb_prompt" if cache is None else "sb_sample",
    )(*operands)


def _gla_levels(c):
    return int(np.log2(c))


def _gla_tables(c):
    nlev = _gla_levels(c)
    t = np.arange(c)[:, None]
    j = np.arange(c)[None, :]
    mats = [(j <= t), (j > t)]
    for lev in range(1, nlev + 1):
        m = 1 << (lev - 1)
        start = (t // (2 * m)) * (2 * m)
        mats.append(((t % (2 * m)) >= m) & (j >= start + m) & (j <= t))
    for lev in range(1, nlev + 1):
        m = 1 << (lev - 1)
        start = (t // (2 * m)) * (2 * m)
        mats.append(((t % (2 * m)) < m) & (j > t) & (j <= start + m - 1))
    mats = np.concatenate([a.astype(np.float32) for a in mats], axis=0)
    x = t ^ j
    level = np.where(j > t, -1, np.where(x == 0, 0, np.floor(np.log2(np.maximum(x, 1))).astype(np.int64) + 1))
    return jnp.asarray(mats, BF16), jnp.asarray(level, jnp.int32)


def _gla_kernel(q_ref, k_ref, la_ref, v_ref, s0_ref, mats_ref, lv_ref, o_ref, sfin_ref, st_ref):
    c = GLA_CHUNK
    nlev = _gla_levels(c)
    streams, t_rows = q_ref.shape[:2]

    @pl.when(pl.program_id(1) == 0)
    def _():
        st_ref[...] = s0_ref[...]

    lane = lax.broadcasted_iota(jnp.int32, (c, LANES), 1)
    head_lanes = (lane < GLA_HEAD_K, lane >= GLA_HEAD_K)
    lane_sq = lax.broadcasted_iota(jnp.int32, (GLA_HEAD_V, LANES), 1) < GLA_HEAD_K

    def pick(x, hl):
        return jnp.where(hl, x, jnp.zeros_like(x))

    def stack_heads(x):
        return jnp.concatenate([pick(x, hl) for hl in head_lanes], axis=0)

    def one_chunk(rows):
        n_rows = rows.stop - rows.start

        def chunk(x):
            return x if n_rows == c else jnp.concatenate([x, jnp.zeros((c - n_rows, x.shape[1]), x.dtype)], axis=0)

        def finish(s, p, scores, qe, ke, b_last):
            sl = slice(p * LANES, (p + 1) * LANES)
            st = st_ref[s, p]
            inter = _dot_nt(stack_heads(qe[:, sl]), st.astype(BF16))
            upd = []
            for r in range(2):
                hd = 2 * p + r
                vh = chunk(v_ref[s, rows, hd * GLA_HEAD_V:(hd + 1) * GLA_HEAD_V])
                o_ref[s, rows, hd * GLA_HEAD_V:(hd + 1) * GLA_HEAD_V] = (
                    inter[r * c:(r + 1) * c] + _dot(scores[r * c:(r + 1) * c], vh))[:n_rows]
                upd.append(_dot_tn(vh, ke[:, sl]))
            st_ref[s, p] = st * jnp.exp(b_last[:, sl]) + jnp.where(lane_sq, upd[0], upd[1])

        la_all = jnp.concatenate([chunk(la_ref[s, rows, :]) for s in range(streams)], axis=1)
        b_all = _dot_hilo_left(mats_ref[:c, :], la_all)
        mild = jnp.min(b_all[c - 1:c, :]) >= GLA_DIRECT_MIN

        @pl.when(mild)
        def _():
            rowi = lax.broadcasted_iota(jnp.int32, (2 * c, c), 0) & (c - 1)
            causal = lax.broadcasted_iota(jnp.int32, (2 * c, c), 1) <= rowi
            for s in range(streams):
                b = b_all[:, s * GLA_KEY_WIDTH:(s + 1) * GLA_KEY_WIDTH]
                b_last = b[c - 1:c, :]
                q = chunk(q_ref[s, rows, :])
                k = chunk(k_ref[s, rows, :])
                qe = (q * jnp.exp(b)).astype(BF16)
                kd = (k * jnp.exp(-b)).astype(BF16)
                ke = (k * jnp.exp(b_last - b)).astype(BF16)
                for p in range(GLA_HEADS // 2):
                    sl = slice(p * LANES, (p + 1) * LANES)
                    scores = jnp.where(causal, _dot_nt(stack_heads(qe[:, sl]), kd[:, sl]), 0.0)
                    finish(s, p, scores.astype(BF16), qe, ke, b_last)

        @pl.when(jnp.logical_not(mild))
        def _():
            lv = jnp.concatenate([lv_ref[...], lv_ref[...]], axis=0)
            for s in range(streams):
                q = chunk(q_ref[s, rows, :])
                k = chunk(k_ref[s, rows, :])
                sums = _dot_hilo_left(mats_ref[...], chunk(la_ref[s, rows, :]))

                def level_rows(i):
                    return sums[i * c:(i + 1) * c]

                qe = (q * jnp.exp(level_rows(0))).astype(BF16)
                ke = (k * jnp.exp(level_rows(1))).astype(BF16)
                q_lev = [q.astype(BF16)] + [(q * jnp.exp(level_rows(1 + lev))).astype(BF16)
                                            for lev in range(1, nlev + 1)]
                k_lev = [k.astype(BF16)] + [(k * jnp.exp(level_rows(1 + nlev + lev))).astype(BF16)
                                            for lev in range(1, nlev + 1)]
                for p in range(GLA_HEADS // 2):
                    sl = slice(p * LANES, (p + 1) * LANES)
                    scores = jnp.zeros((2 * c, c), F32)
                    for lev in range(nlev + 1):
                        scores = jnp.where(lv == lev, _dot_nt(stack_heads(q_lev[lev][:, sl]), k_lev[lev][:, sl]),
                                           scores)
                    finish(s, p, scores.astype(BF16), qe, ke, level_rows(0)[c - 1:c, :])

    if t_rows < c:
        one_chunk(slice(0, t_rows))
    else:
        for u in range(t_rows // c):
            one_chunk(slice(u * c, (u + 1) * c))
    sfin_ref[...] = st_ref[...]


def _dot_hilo_left(m, x):
    hi, lo = _split_bf16(x)
    return _dot(m, hi) + _dot(m, lo)


def _gla(qg, kg, la, vg16, s0, mats, lv, *, batch, seq):
    c = min(GLA_CHUNK * GLA_CHUNKS_PER_STEP, seq)
    assert seq % c == 0 and (c < GLA_CHUNK or c % GLA_CHUNK == 0)
    g = GLA_BATCH_PER_STEP
    pairs = GLA_HEADS // 2
    blk = lambda width: pl.BlockSpec((g, c, width), lambda b, t: (b, t, 0))
    st_spec = pl.BlockSpec((g, pairs, GLA_HEAD_V, LANES), lambda b, t: (b, 0, 0, 0))
    per_stream = lambda a: a.reshape(batch, seq, a.shape[-1])
    o, s_fin = pl.pallas_call(
        _gla_kernel,
        grid=(batch // g, seq // c),
        in_specs=[blk(GLA_KEY_WIDTH), blk(GLA_KEY_WIDTH), blk(GLA_KEY_WIDTH), blk(GLA_WIDTH), st_spec,
                  _const_spec(mats.shape), _const_spec(lv.shape)],
        out_specs=[blk(GLA_WIDTH), st_spec],
        out_shape=[jax.ShapeDtypeStruct((batch, seq, GLA_WIDTH), F32),
                   jax.ShapeDtypeStruct((batch, pairs, GLA_HEAD_V, LANES), F32)],
        scratch_shapes=[pltpu.VMEM((g, pairs, GLA_HEAD_V, LANES), F32)],
        compiler_params=_params(2),
        name="gla",
    )(per_stream(qg), per_stream(kg), per_stream(la), per_stream(vg16), s0, mats, lv)
    return o.reshape(batch * seq, GLA_WIDTH), s_fin


def _mixout_kernel(x_ref, osb_ref, og_ref, r_ref, gsb_ref, ggla_ref, w_ref,
                   g_ref, wg_ref, wu_ref, wd_ref, gfin_ref, o_ref, a_ref):
    o_sb = _head_norm_lanes(osb_ref[...], SB_HEAD_DIM, gsb_ref[...])
    r = r_ref[...]
    o_g = _head_norm_lanes(og_ref[...], GLA_HEAD_V, ggla_ref[...]) * (r * jax.nn.sigmoid(r))
    mix = _dot(o_sb.astype(BF16), w_ref[:SB_WIDTH, :]) + _dot(o_g.astype(BF16), w_ref[SB_WIDTH:, :])
    y = _half_step_ffn(x_ref[...] + mix, g_ref, wg_ref, wu_ref, wd_ref, a_ref)
    o_ref[...] = _rms(y, gfin_ref[...])


def _mixout(x, osb, og, r, gsb, ggla, w, g, wg, wu, wd, gfin, *, tm):
    n = x.shape[0]
    row = lambda i: (i, 0)
    return pl.pallas_call(
        _mixout_kernel,
        grid=(n // tm,),
        in_specs=[pl.BlockSpec((tm, D_MODEL), row),
                  pl.BlockSpec((tm, SB_WIDTH), row),
                  pl.BlockSpec((tm, GLA_WIDTH), row),
                  pl.BlockSpec((tm, GLA_WIDTH), row),
                  _const_spec((1, SB_WIDTH)),
                  _const_spec((1, GLA_WIDTH)),
                  _const_spec((D_MODEL, D_MODEL))] + _ffn_specs() + [_const_spec((1, D_MODEL))],
        out_specs=pl.BlockSpec((tm, D_MODEL), row),
        out_shape=jax.ShapeDtypeStruct((n, D_MODEL), F32),
        scratch_shapes=[pltpu.VMEM((tm, D_FF), BF16)],
        compiler_params=_params(1),
        name="mixer_out_ffn2",
    )(x, osb, og, r, gsb, ggla, w, g, wg, wu, wd, gfin)


def _suffix_matrices():
    j = np.arange(SB_BLOCK)[:, None]
    s = np.arange(SB_BLOCK)[None, :]
    later = j > s
    zero = np.zeros_like(later)
    pair = np.block([[later, zero], [zero, later]])
    single = np.concatenate([later, np.ones_like(later)], axis=1)
    twice = lambda m: jnp.asarray(np.concatenate([m, m], axis=0), BF16)
    return twice(pair), twice(single)


def _state_to_kernel(s):
    b = s.shape[0]
    s = s.reshape(b, GLA_HEADS // 2, 2, GLA_HEAD_K, GLA_HEAD_V)
    return s.transpose(0, 1, 4, 2, 3).reshape(b, GLA_HEADS // 2, GLA_HEAD_V, LANES)


def _state_from_kernel(s):
    b = s.shape[0]
    s = s.reshape(b, GLA_HEADS // 2, GLA_HEAD_V, 2, GLA_HEAD_K)
    return s.transpose(0, 1, 3, 4, 2).reshape(b, GLA_HEADS, GLA_HEAD_K, GLA_HEAD_V)


def _layer(x1, lw, consts, *, batch, seq, tm, cache=None, state=None):
    (g_mix, w_in, wgu, bg, gq, gk, gsb, ggla, w_out, g2, w2g, w2u, w2d, gfin) = lw
    upair, uo, mats, lv = consts
    (qsb16, ksb, ksb16, vsb, vsb16, qg, kg, vg16, r, la) = _mixin(x1, g_mix, w_in, wgu, bg, gq, gk, tm=tm)

    if cache is None:
        o_sb = _sb(qsb16, ksb16, vsb16, upair, uo, batch=batch, tq=SB_BLOCK, keys=seq)
    else:
        o_sb = _sb(qsb16, ksb16, vsb16, upair, uo, batch=batch, tq=seq, keys=seq, cache=cache)

    if state is None:
        state = jnp.zeros((batch, GLA_HEADS // 2, GLA_HEAD_V, LANES), F32)
    o_g, s_fin = _gla(qg, kg, la, vg16, state, mats, lv, batch=batch, seq=seq)

    y = _mixout(x1, o_sb, o_g, r, gsb, ggla, w_out, g2, w2g, w2u, w2d, gfin, tm=tm)
    return y, ksb, vsb, _state_from_kernel(s_fin)


def kernel(x_prompt, x_sample, cache_sb_k, cache_sb_v, state_gla, g_ffn1, w_ffn1_gate, w_ffn1_up,
           w_ffn1_down, g_mix, w_in, w_gate_up, b_gate, g_q, g_k, g_sb_out, g_gla_out, w_out,
           g_ffn2, w_ffn2_gate, w_ffn2_up, w_ffn2_down, g_final):
    depth = w_in.shape[0]
    batch, seq, _ = x_prompt.shape
    dec_batch, dec_seq, _ = x_sample.shape
    consts = (*_suffix_matrices(), *_gla_tables(GLA_CHUNK))

    y_p = x_prompt.reshape(batch * seq, D_MODEL)
    y_s = x_sample.reshape(dec_batch * dec_seq, D_MODEL)
    outs = [[] for _ in range(6)]
    for l in range(depth):
        row = lambda a: a[l].reshape(1, -1)
        x1_s, ffn1 = _ffn_stream(y_s, row(g_ffn1), w_ffn1_gate[l], w_ffn1_up[l], w_ffn1_down[l])
        x1_p, (w2g, w2u, w2d, w_in16, w_out16) = _ffn(
            y_p, row(g_ffn1), *ffn1, tm=FFN1_ROWS,
            cast=[w_ffn2_gate[l], w_ffn2_up[l], w_ffn2_down[l], (w_in[l].T, IN_PAD), w_out[l]])
        lw = (row(g_mix), w_in16,
              jnp.pad(w_gate_up[l], ((0, GATE_PAD - GATE_RANK), (0, 0))).astype(BF16), row(b_gate),
              jnp.tile(g_q[l], SB_HEADS).reshape(1, -1), jnp.tile(g_k[l], SB_HEADS).reshape(1, -1),
              jnp.tile(g_sb_out[l], SB_HEADS).reshape(1, -1), jnp.tile(g_gla_out[l], GLA_HEADS).reshape(1, -1),
              w_out16, row(g_ffn2), w2g, w2u, w2d, row(g_final))
        y_p, k_p, v_p, s_p = _layer(x1_p, lw, consts, batch=batch, seq=seq, tm=512)
        cache = (cache_sb_k[l].transpose(0, 2, 3, 1), cache_sb_v[l].transpose(0, 2, 3, 1))
        y_s, k_s, v_s, s_s = _layer(x1_s, lw, consts, batch=dec_batch, seq=dec_seq, tm=256,
                                    cache=cache, state=_state_to_kernel(state_gla[l]))
        shape_p = (batch, seq, SB_HEADS, SB_HEAD_DIM)
        shape_s = (dec_batch, dec_seq, SB_HEADS, SB_HEAD_DIM)
        for lst, val in zip(outs, (k_p.reshape(shape_p), v_p.reshape(shape_p), s_p,
                                   k_s.reshape(shape_s), v_s.reshape(shape_s), s_s)):
            lst.append(val)
    return (y_p.reshape(batch, seq, D_MODEL), y_s.reshape(dec_batch, dec_seq, D_MODEL),
            *[jnp.stack(o) for o in outs])
```

```python
import functools

import numpy as np
import jax
import jax.numpy as jnp
from jax import lax
from jax.experimental import pallas as pl
from jax.experimental.pallas import tpu as pltpu

F32 = jnp.float32
BF16 = jnp.bfloat16

D_MODEL = 1024
SB_HEADS = 8
SB_HEAD_DIM = 64
SB_WIDTH = SB_HEADS * SB_HEAD_DIM
SB_SCALE = SB_HEAD_DIM ** -0.5
GLA_HEADS = 4
GLA_HEAD_K = 64
GLA_HEAD_V = 128
GLA_KEY_WIDTH = GLA_HEADS * GLA_HEAD_K
GLA_WIDTH = GLA_HEADS * GLA_HEAD_V
GLA_SCALE = GLA_HEAD_K ** -0.5
GATE_RANK = 16
GATE_TAU = 16.0
D_FF = 2816
EPS = 1e-6

LANES = 128
SUBLANES = 8
BF16_SUBLANES = 16
GATE_PAD = LANES
MAIN_WIDTH = 3 * SB_WIDTH + 2 * GLA_KEY_WIDTH + 2 * GLA_WIDTH
IN_PAD = MAIN_WIDTH + GATE_PAD

FF_CHUNK = 256
FFN1_ROWS = 1024
FF_STREAM_CHUNK = 256
SB_BLOCK = 128
SB_WINDOW = 3
SB_PAIRS_PER_STEP = 4
SB_QBLOCKS_PER_STEP = 4
HALF_WINDOW = 2
GLA_CHUNK = 128
GLA_BATCH_PER_STEP = 2
GLA_CHUNKS_PER_STEP = 2
GLA_DIRECT_MIN = -40.0
SB_STOP = 152.0
LOG2_E = 1.4426950408889634

VMEM_LIMIT = 56 * 1024 * 1024


def _dot(a, b):
    return jnp.dot(a, b, preferred_element_type=F32)


def _dot_nt(a, b):
    return lax.dot_general(a, b, (((1,), (1,)), ((), ())), preferred_element_type=F32)


def _dot_tn(a, b):
    return lax.dot_general(a, b, (((0,), (0,)), ((), ())), preferred_element_type=F32)


def _split_bf16(x):
    hi = x.astype(BF16)
    lo = (x - hi.astype(F32)).astype(BF16)
    return hi, lo


def _rms(x, g):
    ms = jnp.mean(x * x, axis=-1, keepdims=True)
    return x * lax.rsqrt(ms + EPS) * g


def _log_sigmoid(x):
    return jnp.minimum(x, 0.0) - jnp.log1p(jnp.exp(-jnp.abs(x)))


def _const_spec(shape):
    nd = len(shape)
    return pl.BlockSpec(shape, lambda *_: (0,) * nd, pipeline_mode=pl.Buffered(1))


def _params(n_grid):
    return pltpu.CompilerParams(dimension_semantics=("arbitrary",) * n_grid,
                                vmem_limit_bytes=VMEM_LIMIT)


def _half_step_ffn(x, g_ref, wg_ref, wu_ref, wd_ref, a_ref):
    h = _rms(x, g_ref[...]).astype(BF16)
    for c in range(D_FF // FF_CHUNK):
        sl = slice(c * FF_CHUNK, (c + 1) * FF_CHUNK)
        gt = _dot(h, wg_ref[:, sl])
        up = _dot(h, wu_ref[:, sl])
        a_ref[:, sl] = (gt * jax.nn.sigmoid(gt) * up).astype(BF16)
    return x + 0.5 * _dot(a_ref[...], wd_ref[...])


def _ffn_kernel(x_ref, g_ref, wg_ref, wu_ref, wd_ref, *refs, transposed):
    n_cast = (len(refs) - 2) // 2
    o_ref, a_ref = refs[n_cast], refs[-1]
    o_ref[...] = _half_step_ffn(x_ref[...], g_ref, wg_ref, wu_ref, wd_ref, a_ref)
    for src, dst, tr in zip(refs[:n_cast], refs[n_cast + 1:-1], transposed):
        if tr is None:
            dst[...] = src[...].astype(BF16)
        else:
            rows, blocks = tr
            first = jnp.minimum(pl.program_id(0), blocks - 1) * src.shape[0]
            inside = lax.broadcasted_iota(jnp.int32, src.shape, 0) < rows - first
            dst[...] = jnp.where(inside, src[...], 0.0).T.astype(BF16)


def _ffn_specs():
    return [_const_spec((1, D_MODEL)), _const_spec((D_MODEL, D_FF)), _const_spec((D_MODEL, D_FF)),
            _const_spec((D_FF, D_MODEL))]


def _cast_block_rows(rows, steps):
    br = next(b for b in range(BF16_SUBLANES, rows + 1, BF16_SUBLANES) if rows % b == 0 and rows // b <= steps)
    return br, rows // br


def _ffn(x, g, wg, wu, wd, *, tm, cast=()):
    n = x.shape[0]
    steps = n // tm
    row = lambda i: (i, 0)
    cast_in, cast_out, cast_shape, transposed = [], [], [], []
    for w in cast:
        if isinstance(w, tuple):
            w, width = w
            tb = LANES * -(-width // (LANES * steps))
            nblk = -(-width // tb)
            width = nblk * tb
            assert nblk <= steps and w.shape[0] <= width
            cast_in.append(pl.BlockSpec((tb, w.shape[1]), lambda i, last=nblk - 1: (jnp.minimum(i, last), 0)))
            cast_out.append(pl.BlockSpec((w.shape[1], tb), lambda i, last=nblk - 1: (0, jnp.minimum(i, last))))
            cast_shape.append(jax.ShapeDtypeStruct((w.shape[1], width), BF16))
            transposed.append((w.shape[0], nblk))
        else:
            br, nblk = _cast_block_rows(w.shape[0], steps)
            idx = lambda i, last=nblk - 1: (jnp.minimum(i, last), 0)
            cast_in.append(pl.BlockSpec((br, w.shape[1]), idx))
            cast_out.append(pl.BlockSpec((br, w.shape[1]), idx))
            cast_shape.append(jax.ShapeDtypeStruct(w.shape, BF16))
            transposed.append(None)
    cast = [w[0] if isinstance(w, tuple) else w for w in cast]
    out = pl.pallas_call(
        functools.partial(_ffn_kernel, transposed=tuple(transposed)),
        grid=(steps,),
        in_specs=[pl.BlockSpec((tm, D_MODEL), row)] + _ffn_specs() + cast_in,
        out_specs=[pl.BlockSpec((tm, D_MODEL), row)] + cast_out,
        out_shape=[jax.ShapeDtypeStruct((n, D_MODEL), F32)] + cast_shape,
        scratch_shapes=[pltpu.VMEM((tm, D_FF), BF16)],
        compiler_params=_params(1),
        name="ffn1",
    )(x, g, wg, wu, wd, *cast)
    return out[0], out[1:]


def _ffn_stream_kernel(x_ref, g_ref, wg_ref, wu_ref, wd_ref, o_ref, wg16_ref, wu16_ref, wd16_ref,
                       h_ref, acc_ref):
    c = pl.program_id(0)

    @pl.when(c == 0)
    def _():
        h_ref[...] = _rms(x_ref[...], g_ref[...]).astype(BF16)
        acc_ref[...] = jnp.zeros_like(acc_ref)

    wg = wg_ref[...].astype(BF16)
    wu = wu_ref[...].astype(BF16)
    wd = wd_ref[...].astype(BF16)
    wg16_ref[...] = wg
    wu16_ref[...] = wu
    wd16_ref[...] = wd
    h = h_ref[...]
    gt = _dot(h, wg)
    acc_ref[...] += _dot((gt * jax.nn.sigmoid(gt) * _dot(h, wu)).astype(BF16), wd)

    @pl.when(c == pl.num_programs(0) - 1)
    def _():
        o_ref[...] = x_ref[...] + 0.5 * acc_ref[...]


def _ffn_stream(x, g, wg, wu, wd):
    n = x.shape[0]
    whole = lambda shape: pl.BlockSpec(shape, lambda c: (0, 0))
    cols = pl.BlockSpec((D_MODEL, FF_STREAM_CHUNK), lambda c: (0, c))
    rows = pl.BlockSpec((FF_STREAM_CHUNK, D_MODEL), lambda c: (c, 0))
    out = pl.pallas_call(
        _ffn_stream_kernel,
        grid=(D_FF // FF_STREAM_CHUNK,),
        in_specs=[whole((n, D_MODEL)), whole((1, D_MODEL)), cols, cols, rows],
        out_specs=[whole((n, D_MODEL)), cols, cols, rows],
        out_shape=[jax.ShapeDtypeStruct((n, D_MODEL), F32),
                   jax.ShapeDtypeStruct((D_MODEL, D_FF), BF16), jax.ShapeDtypeStruct((D_MODEL, D_FF), BF16),
                   jax.ShapeDtypeStruct((D_FF, D_MODEL), BF16)],
        scratch_shapes=[pltpu.VMEM((n, D_MODEL), BF16), pltpu.VMEM((n, D_MODEL), F32)],
        compiler_params=_params(1),
        name="ffn1_stream",
    )(x, g, wg, wu, wd)
    return out[0], out[1:]


def _head_norm_lanes(y, head, gain):
    tm, width = y.shape
    low = lax.broadcasted_iota(jnp.int32, (tm, LANES), 1) < head
    cols = []
    for j in range(width // LANES):
        yj = y[:, j * LANES:(j + 1) * LANES]
        sq = yj * yj

        def inv_rms(part):
            return lax.rsqrt(jnp.sum(part, axis=1, keepdims=True) * (1.0 / head) + EPS)

        if head == LANES:
            cols.append(yj * inv_rms(sq))
        else:
            cols.append(yj * jnp.where(low, inv_rms(jnp.where(low, sq, 0.0)), inv_rms(jnp.where(low, 0.0, sq))))
    return jnp.concatenate(cols, axis=1) * gain


def _store_heads(ref, y):
    groups = y.shape[0] // SUBLANES
    sub = lax.broadcasted_iota(jnp.int32, (groups, SUBLANES, LANES), 1)
    rows = []
    for j in range(SB_WIDTH // LANES):
        pair = y[:, j * LANES:(j + 1) * LANES].reshape(groups, SUBLANES, LANES)
        rows += [pair, pltpu.roll(pair, SB_HEAD_DIM, axis=2)]
    for dist in (4, 2, 1):
        low = (sub & dist) == 0
        new = list(rows)
        for h in range(SUBLANES):
            if h & dist == 0:
                a, b = rows[h], rows[h + dist]
                new[h] = jnp.where(low, a, pltpu.roll(b, dist, axis=1))
                new[h + dist] = jnp.where(low, pltpu.roll(a, SUBLANES - dist, axis=1), b)
        rows = new
    for t in range(SUBLANES):
        ref[:, t] = rows[t][:, :, :SB_HEAD_DIM]


def _mixin_kernel(x_ref, g_ref, w_ref, wgu_ref, bg_ref, gq_ref, gk_ref,
                  qsb_ref, ksb_ref, ksb16_ref, vsb_ref, vsb16_ref,
                  qg_ref, kg_ref, vg16_ref, r_ref, la_ref):
    h = _rms(x_ref[...], g_ref[...]).astype(BF16)

    def proj(lo, width):
        return _dot(h, w_ref[:, lo:lo + width])

    q = _head_norm_lanes(proj(0, SB_WIDTH), SB_HEAD_DIM, gq_ref[...])
    qsb_ref[...] = (q * SB_SCALE).astype(BF16)
    k = _head_norm_lanes(proj(SB_WIDTH, SB_WIDTH), SB_HEAD_DIM, gk_ref[...])
    _store_heads(ksb_ref, k)
    ksb16_ref[...] = k.astype(BF16)
    v = proj(2 * SB_WIDTH, SB_WIDTH)
    _store_heads(vsb_ref, v)
    vsb16_ref[...] = v.astype(BF16)
    off = 3 * SB_WIDTH
    qg_ref[...] = proj(off, GLA_KEY_WIDTH) * GLA_SCALE
    kg_ref[...] = proj(off + GLA_KEY_WIDTH, GLA_KEY_WIDTH)
    vg16_ref[...] = proj(off + 2 * GLA_KEY_WIDTH, GLA_WIDTH).astype(BF16)
    r_ref[...] = proj(off + 2 * GLA_KEY_WIDTH + GLA_WIDTH, GLA_WIDTH)
    lr = proj(MAIN_WIDTH, GATE_PAD).astype(BF16)
    gate = _dot(lr, wgu_ref[...]) + bg_ref[...]
    la_ref[...] = _log_sigmoid(gate) * (1.0 / GATE_TAU)


def _mixin(x, g, w, wgu, bg, gq, gk, *, tm):
    n = x.shape[0]
    row = lambda i: (i, 0)
    heads = (SB_HEADS, SB_HEAD_DIM)
    assert SB_HEADS == SUBLANES and tm % SUBLANES == 0
    lead = lambda rows, wd: (rows // SUBLANES, SUBLANES) if len(wd) == 2 else (rows,)
    widths = [((SB_WIDTH,), BF16), (heads, F32), ((SB_WIDTH,), BF16), (heads, F32), ((SB_WIDTH,), BF16),
              ((GLA_KEY_WIDTH,), F32), ((GLA_KEY_WIDTH,), F32), ((GLA_WIDTH,), BF16), ((GLA_WIDTH,), F32),
              ((GLA_KEY_WIDTH,), F32)]
    return pl.pallas_call(
        _mixin_kernel,
        grid=(n // tm,),
        in_specs=[pl.BlockSpec((tm, D_MODEL), row),
                  _const_spec((1, D_MODEL)),
                  _const_spec(w.shape),
                  _const_spec((GATE_PAD, GLA_KEY_WIDTH)),
                  _const_spec((1, GLA_KEY_WIDTH)),
                  _const_spec((1, SB_WIDTH)),
                  _const_spec((1, SB_WIDTH))],
        out_specs=[pl.BlockSpec(lead(tm, wd) + wd, lambda i, nd=len(lead(tm, wd) + wd) - 1: (i,) + (0,) * nd)
                   for wd, _ in widths],
        out_shape=[jax.ShapeDtypeStruct(lead(n, wd) + wd, dt) for wd, dt in widths],
        compiler_params=_params(1),
        name="mixer_in",
    )(x, g, w, wgu, bg, gq, gk)


def _neg_abs(x):
    return -jnp.abs(x)


def _sb_log2_terms(z):
    z2 = z * LOG2_E
    t = jnp.log2(1.0 + jnp.exp2(_neg_abs(z2)))
    return jnp.maximum(z2, 0.0) + t, jnp.minimum(z2, 0.0) - t


def _sb_kernel(q_ref, k_ref, v_ref, upair_ref, uo_ref, *refs, tq, q_block):
    if q_block is None:
        o_ref, carry_all, acc_all, live_ref = refs
    else:
        kc_hbm, vc_hbm, o_ref, carry_all, acc_all, live_ref, kbuf, vbuf, sems, kwin, vwin = refs
    pairs = q_ref.shape[1] // LANES
    heads = 2 * pairs

    def one_block(i, row0, u, phase):
        carry_ref, acc_ref = carry_all.at[u], acc_all.at[u]
        lane = lax.broadcasted_iota(jnp.int32, (tq, LANES), 1)
        head_lanes = (lane < SB_HEAD_DIM, lane >= SB_HEAD_DIM)
        rowi = lax.broadcasted_iota(jnp.int32, (heads * tq, SB_BLOCK), 0)
        coli = lax.broadcasted_iota(jnp.int32, (heads * tq, SB_BLOCK), 1)
        causal = coli < (rowi & (tq - 1))

        def pair_lanes(p):
            return slice(p * LANES, (p + 1) * LANES)

        def q_pair(p):
            q2 = q_ref[row0:row0 + tq, pair_lanes(p)]
            zero = jnp.zeros_like(q2)
            return jnp.concatenate([jnp.where(hl, q2, zero) for hl in head_lanes], axis=0)

        def rows(ref, p, first_block, nblocks):
            start = first_block * SB_BLOCK
            if not isinstance(start, int):
                start = pl.multiple_of(start, SB_BLOCK)
            return ref[pl.ds(start, nblocks * SB_BLOCK), pair_lanes(p)]

        def suffix_sums(x):
            nb = x.shape[1] // SB_BLOCK
            hi, lo = _split_bf16(x)

            def mm(cols, m2):
                return _dot(jnp.concatenate([hi[:, cols], lo[:, cols]], axis=1), m2)

            within, total = [], []
            if nb >= 3:
                for j in range(0, nb - 1, 2):
                    y = mm(slice(j * SB_BLOCK, (j + 2) * SB_BLOCK), upair_ref[...])
                    within += [y[:, :SB_BLOCK], y[:, SB_BLOCK:]]
                    total += [jnp.sum(x[:, b * SB_BLOCK:(b + 1) * SB_BLOCK], axis=1, keepdims=True) for b in (j, j + 1)]
            for j in range(len(within), nb):
                y = mm(slice(j * SB_BLOCK, (j + 1) * SB_BLOCK), uo_ref[...])
                within.append(y[:, :SB_BLOCK])
                total.append(y[:, SB_BLOCK:])
            later = total[nb - 1]
            out = [within[nb - 1]]
            for j in range(nb - 2, -1, -1):
                out.insert(0, within[j] + later)
                later = later + total[j]
            return (out[0] if nb == 1 else jnp.concatenate(out, axis=1)), later

        def resident(first_block, nblocks, kv_refs=(k_ref, v_ref)):
            return tuple((lambda p, ref=ref: rows(ref, p, first_block, nblocks)) for ref in kv_refs)

        def sweep(kv, nblocks, own, carry, valid=None):
            nk = nblocks * SB_BLOCK

            def masked(x):
                if valid is not None:
                    return jnp.where(valid, x, 0.0)
                if not own:
                    return x
                tail = jnp.where(causal, x[:, nk - SB_BLOCK:], 0.0)
                return tail if nblocks == 1 else jnp.concatenate([x[:, :nk - SB_BLOCK], tail], axis=1)

            z = jnp.concatenate([_dot_nt(q_pair(p), kv[0](p)) for p in range(pairs)], axis=0)
            fail, log_beta = _sb_log2_terms(z)
            after, total = suffix_sums(masked(fail))
            if carry is not None:
                after = after + carry
            w = masked(jnp.exp2(log_beta - after)).astype(BF16)
            pv = jnp.concatenate([_dot(w[2 * p * tq:(2 * p + 2) * tq], kv[1](p))
                                  for p in range(pairs)], axis=0)
            if carry is None:
                carry_ref[...] = total
                acc_ref[...] = pv
            else:
                carry_ref[...] = carry + total
                acc_ref[...] += pv

        def live():
            return jnp.min(carry_ref[...]) < SB_STOP

        def mark_live():
            live_ref[u] = jnp.where(live(), 1, 0)

        half = SB_BLOCK // 2
        assert HALF_WINDOW == 2

        def by_halves(x, f):
            y = [f(s, jnp.concatenate([x[s * half:(s + 1) * half], x[tq + s * half:tq + (s + 1) * half]], axis=0))
                 for s in range(2)]
            return jnp.concatenate([y[0][:half], y[1][:half], y[0][half:], y[1][half:]], axis=0)

        def half_windows():
            nk = HALF_WINDOW * SB_BLOCK
            starts = [pl.multiple_of(i * SB_BLOCK + (s + 1) * half - nk, half) for s in range(2)]
            keep = coli < (rowi & (half - 1)) + half

            def masked(x):
                return jnp.concatenate([x[:, :SB_BLOCK], jnp.where(keep, x[:, SB_BLOCK:], 0.0)], axis=1)

            z = jnp.concatenate(
                [by_halves(q_pair(p), lambda s, lhs: _dot_nt(lhs, k_ref[pl.ds(starts[s], nk), pair_lanes(p)]))
                 for p in range(pairs)], axis=0)
            fail, log_beta = _sb_log2_terms(z)
            after, total = suffix_sums(masked(fail))
            w = masked(jnp.exp2(log_beta - after)).astype(BF16)
            carry_ref[...] = total
            acc_ref[...] = jnp.concatenate(
                [by_halves(w[2 * p * tq:(2 * p + 2) * tq],
                           lambda s, lhs: _dot(lhs, v_ref[pl.ds(starts[s], nk), pair_lanes(p)]))
                 for p in range(pairs)], axis=0)

        def own_block_only():
            sweep(resident(i, 1), 1, True, None)

        if q_block is None:
            assert tq == SB_BLOCK
            windowed = i >= HALF_WINDOW
            if phase == "window":
                half_windows()
                return
            if phase == "first":
                pl.when(windowed)(half_windows)
                pl.when(jnp.logical_not(windowed))(own_block_only)
                return

            mark_live()

            @pl.when(jnp.logical_and(windowed, live_ref[u] != 0))
            def _():
                unseen = jnp.logical_or(coli < half, (rowi & half) != 0)
                sweep(resident(i - HALF_WINDOW, 1), 1, False, carry_ref[...], unseen)
                mark_live()

            n_prev = jnp.where(windowed, i - HALF_WINDOW, i)
            earlier = lambda kb: resident(kb, 1)
        else:
            recent = SB_WINDOW - 1
            assert q_block >= recent and k_ref.shape[0] == tq <= SB_BLOCK

            def fetch(first_block, nblocks):
                nk = nblocks * SB_BLOCK
                start = first_block * SB_BLOCK
                if not isinstance(start, int):
                    start = pl.multiple_of(start, SB_BLOCK)
                copies = [pltpu.make_async_copy(hbm.at[pl.program_id(0), :, :, pl.ds(start, nk)],
                                                buf.at[:, :, pl.ds(0, nk)], sems.at[n])
                          for n, (hbm, buf) in enumerate(((kc_hbm, kbuf), (vc_hbm, vbuf)))]
                for c in copies:
                    c.start()
                for c in copies:
                    c.wait()

                def slabs(buf):
                    piece = lambda hd, j: buf[hd, :, j * SB_BLOCK:(j + 1) * SB_BLOCK].T
                    return lambda j, p: jnp.concatenate([piece(2 * p, j), piece(2 * p + 1, j)], axis=1).astype(BF16)

                return slabs(kbuf), slabs(vbuf)

            def earlier(kb):
                return tuple((lambda p, slab=slab: slab(0, p)) for slab in fetch(kb, 1))

            for slab, win in zip(fetch(q_block - recent, recent), (kwin, vwin)):
                for j in range(recent):
                    for p in range(pairs):
                        win[j * SB_BLOCK:(j + 1) * SB_BLOCK, pair_lanes(p)] = slab(j, p)
            for new, win in ((k_ref, kwin), (v_ref, vwin)):
                win[recent * SB_BLOCK:recent * SB_BLOCK + tq, :] = new[...]
                win[recent * SB_BLOCK + tq:, :] = jnp.zeros((SB_BLOCK - tq, win.shape[1]), BF16)
            sweep(resident(0, SB_WINDOW, (kwin, vwin)), SB_WINDOW, True, None)
            mark_live()
            n_prev = q_block - recent

        def body(st):
            kb, _ = st
            sweep(earlier(kb), 1, False, carry_ref[...])
            return kb - 1, live()

        lax.while_loop(lambda st: jnp.logical_and(st[0] >= 0, st[1]), body, (n_prev - 1, live_ref[u] != 0))
        for p in range(pairs):
            o_ref[row0:row0 + tq, pair_lanes(p)] = jnp.where(head_lanes[0], acc_ref[2 * p * tq:(2 * p + 1) * tq],
                                                              acc_ref[(2 * p + 1) * tq:(2 * p + 2) * tq])

    if q_block is None:
        per_step = q_ref.shape[0] // tq
        blocks = [(pl.program_id(2) * per_step + u, u * tq, u) for u in range(per_step)]
        all_windowed = blocks[0][0] >= HALF_WINDOW

        @pl.when(all_windowed)
        def _():
            for blk in blocks:
                one_block(*blk, "window")

        @pl.when(jnp.logical_not(all_windowed))
        def _():
            for blk in blocks:
                one_block(*blk, "first")

        for blk in blocks:
            one_block(*blk, "rest")
    else:
        one_block(q_block, 0, 0, "rest")


def _sb(q16, k16, v16, upair, uo, *, batch, tq, keys, cache=None):
    sub = SB_QBLOCKS_PER_STEP if cache is None else 1
    nq = keys // (sub * SB_BLOCK) if cache is None else 1
    width = SB_PAIRS_PER_STEP * LANES
    assert width == SB_WIDTH or cache is None
    kv_spec = pl.BlockSpec((keys, width), lambda b, p, i: (b, p))
    q_spec = pl.BlockSpec((sub * tq, width), lambda b, p, i: (b * nq + i, p))
    heads = 2 * SB_PAIRS_PER_STEP
    in_specs = [q_spec, kv_spec, kv_spec, _const_spec(upair.shape), _const_spec(uo.shape)]
    scratch = [pltpu.VMEM((sub, heads * tq, SB_BLOCK), F32), pltpu.VMEM((sub, heads * tq, LANES), F32),
               pltpu.SMEM((sub,), jnp.int32)]
    operands = [q16, k16, v16, upair, uo]
    q_block = None
    if cache is not None:
        past = cache[0].shape[3]
        assert past % SB_BLOCK == 0 and keys == tq and SB_WINDOW * SB_BLOCK <= past + SB_BLOCK
        q_block = past // SB_BLOCK
        in_specs += [pl.BlockSpec(memory_space=pl.ANY)] * 2
        scratch += ([pltpu.VMEM((SB_HEADS, SB_HEAD_DIM, (SB_WINDOW - 1) * SB_BLOCK), F32)] * 2
                    + [pltpu.SemaphoreType.DMA((2,))]
                    + [pltpu.VMEM((SB_WINDOW * SB_BLOCK, SB_WIDTH), BF16)] * 2)
        operands += list(cache)
    return pl.pallas_call(
        functools.partial(_sb_kernel, tq=tq, q_block=q_block),
        grid=(batch, SB_WIDTH // width, nq),
        in_specs=in_specs,
        out_specs=q_spec,
        out_shape=jax.ShapeDtypeStruct((batch * nq * sub * tq, SB_WIDTH), F32),
        scratch_shapes=scratch,
        compiler_params=_params(3),
        name="sb_prompt" if cache is None else "sb_sample",
    )(*operands)


def _gla_levels(c):
    return int(np.log2(c))


def _gla_tables(c):
    nlev = _gla_levels(c)
    t = np.arange(c)[:, None]
    j = np.arange(c)[None, :]
    mats = [(j <= t), (j > t)]
    for lev in range(1, nlev + 1):
        m = 1 << (lev - 1)
        start = (t // (2 * m)) * (2 * m)
        mats.append(((t % (2 * m)) >= m) & (j >= start + m) & (j <= t))
    for lev in range(1, nlev + 1):
        m = 1 << (lev - 1)
        start = (t // (2 * m)) * (2 * m)
        mats.append(((t % (2 * m)) < m) & (j > t) & (j <= start + m - 1))
    mats = np.concatenate([a.astype(np.float32) for a in mats], axis=0)
    x = t ^ j
    level = np.where(j > t, -1, np.where(x == 0, 0, np.floor(np.log2(np.maximum(x, 1))).astype(np.int64) + 1))
    return jnp.asarray(mats, BF16), jnp.asarray(level, jnp.int32)


def _gla_kernel(q_ref, k_ref, la_ref, v_ref, s0_ref, mats_ref, lv_ref, o_ref, sfin_ref, st_ref):
    c = GLA_CHUNK
    nlev = _gla_levels(c)
    streams, t_rows = q_ref.shape[:2]

    @pl.when(pl.program_id(1) == 0)
    def _():
        st_ref[...] = s0_ref[...]

    lane = lax.broadcasted_iota(jnp.int32, (c, LANES), 1)
    head_lanes = (lane < GLA_HEAD_K, lane >= GLA_HEAD_K)
    lane_sq = lax.broadcasted_iota(jnp.int32, (GLA_HEAD_V, LANES), 1) < GLA_HEAD_K

    def pick(x, hl):
        return jnp.where(hl, x, jnp.zeros_like(x))

    def stack_heads(x):
        return jnp.concatenate([pick(x, hl) for hl in head_lanes], axis=0)

    def one_chunk(rows):
        n_rows = rows.stop - rows.start

        def chunk(x):
            return x if n_rows == c else jnp.concatenate([x, jnp.zeros((c - n_rows, x.shape[1]), x.dtype)], axis=0)

        def finish(s, p, scores, qe, ke, b_last):
            sl = slice(p * LANES, (p + 1) * LANES)
            st = st_ref[s, p]
            inter = _dot_nt(stack_heads(qe[:, sl]), st.astype(BF16))
            upd = []
            for r in range(2):
                hd = 2 * p + r
                vh = chunk(v_ref[s, rows, hd * GLA_HEAD_V:(hd + 1) * GLA_HEAD_V])
                o_ref[s, rows, hd * GLA_HEAD_V:(hd + 1) * GLA_HEAD_V] = (
                    inter[r * c:(r + 1) * c] + _dot(scores[r * c:(r + 1) * c], vh))[:n_rows]
                upd.append(_dot_tn(vh, ke[:, sl]))
            st_ref[s, p] = st * jnp.exp(b_last[:, sl]) + jnp.where(lane_sq, upd[0], upd[1])

        la_all = jnp.concatenate([chunk(la_ref[s, rows, :]) for s in range(streams)], axis=1)
        b_all = _dot_hilo_left(mats_ref[:c, :], la_all)
        mild = jnp.min(b_all[c - 1:c, :]) >= GLA_DIRECT_MIN

        @pl.when(mild)
        def _():
            rowi = lax.broadcasted_iota(jnp.int32, (2 * c, c), 0) & (c - 1)
            causal = lax.broadcasted_iota(jnp.int32, (2 * c, c), 1) <= rowi
            for s in range(streams):
                b = b_all[:, s * GLA_KEY_WIDTH:(s + 1) * GLA_KEY_WIDTH]
                b_last = b[c - 1:c, :]
                q = chunk(q_ref[s, rows, :])
                k = chunk(k_ref[s, rows, :])
                qe = (q * jnp.exp(b)).astype(BF16)
                kd = (k * jnp.exp(-b)).astype(BF16)
                ke = (k * jnp.exp(b_last - b)).astype(BF16)
                for p in range(GLA_HEADS // 2):
                    sl = slice(p * LANES, (p + 1) * LANES)
                    scores = jnp.where(causal, _dot_nt(stack_heads(qe[:, sl]), kd[:, sl]), 0.0)
                    finish(s, p, scores.astype(BF16), qe, ke, b_last)

        @pl.when(jnp.logical_not(mild))
        def _():
            lv = jnp.concatenate([lv_ref[...], lv_ref[...]], axis=0)
            for s in range(streams):
                q = chunk(q_ref[s, rows, :])
                k = chunk(k_ref[s, rows, :])
                sums = _dot_hilo_left(mats_ref[...], chunk(la_ref[s, rows, :]))

                def level_rows(i):
                    return sums[i * c:(i + 1) * c]

                qe = (q * jnp.exp(level_rows(0))).astype(BF16)
                ke = (k * jnp.exp(level_rows(1))).astype(BF16)
                q_lev = [q.astype(BF16)] + [(q * jnp.exp(level_rows(1 + lev))).astype(BF16)
                                            for lev in range(1, nlev + 1)]
                k_lev = [k.astype(BF16)] + [(k * jnp.exp(level_rows(1 + nlev + lev))).astype(BF16)
                                            for lev in range(1, nlev + 1)]
                for p in range(GLA_HEADS // 2):
                    sl = slice(p * LANES, (p + 1) * LANES)
                    scores = jnp.zeros((2 * c, c), F32)
                    for lev in range(nlev + 1):
                        scores = jnp.where(lv == lev, _dot_nt(stack_heads(q_lev[lev][:, sl]), k_lev[lev][:, sl]),
                                           scores)
                    finish(s, p, scores.astype(BF16), qe, ke, level_rows(0)[c - 1:c, :])

    if t_rows < c:
        one_chunk(slice(0, t_rows))
    else:
        for u in range(t_rows // c):
            one_chunk(slice(u * c, (u + 1) * c))
    sfin_ref[...] = st_ref[...]


def _dot_hilo_left(m, x):
    hi, lo = _split_bf16(x)
    return _dot(m, hi) + _dot(m, lo)


def _gla(qg, kg, la, vg16, s0, mats, lv, *, batch, seq):
    c = min(GLA_CHUNK * GLA_CHUNKS_PER_STEP, seq)
    assert seq % c == 0 and (c < GLA_CHUNK or c % GLA_CHUNK == 0)
    g = GLA_BATCH_PER_STEP
    pairs = GLA_HEADS // 2
    blk = lambda width: pl.BlockSpec((g, c, width), lambda b, t: (b, t, 0))
    st_spec = pl.BlockSpec((g, pairs, GLA_HEAD_V, LANES), lambda b, t: (b, 0, 0, 0))
    per_stream = lambda a: a.reshape(batch, seq, a.shape[-1])
    o, s_fin = pl.pallas_call(
        _gla_kernel,
        grid=(batch // g, seq // c),
        in_specs=[blk(GLA_KEY_WIDTH), blk(GLA_KEY_WIDTH), blk(GLA_KEY_WIDTH), blk(GLA_WIDTH), st_spec,
                  _const_spec(mats.shape), _const_spec(lv.shape)],
        out_specs=[blk(GLA_WIDTH), st_spec],
        out_shape=[jax.ShapeDtypeStruct((batch, seq, GLA_WIDTH), F32),
                   jax.ShapeDtypeStruct((batch, pairs, GLA_HEAD_V, LANES), F32)],
        scratch_shapes=[pltpu.VMEM((g, pairs, GLA_HEAD_V, LANES), F32)],
        compiler_params=_params(2),
        name="gla",
    )(per_stream(qg), per_stream(kg), per_stream(la), per_stream(vg16), s0, mats, lv)
    return o.reshape(batch * seq, GLA_WIDTH), s_fin


def _mixout_kernel(x_ref, osb_ref, og_ref, r_ref, gsb_ref, ggla_ref, w_ref,
                   g_ref, wg_ref, wu_ref, wd_ref, gfin_ref, o_ref, a_ref):
    o_sb = _head_norm_lanes(osb_ref[...], SB_HEAD_DIM, gsb_ref[...])
    r = r_ref[...]
    o_g = _head_norm_lanes(og_ref[...], GLA_HEAD_V, ggla_ref[...]) * (r * jax.nn.sigmoid(r))
    mix = _dot(o_sb.astype(BF16), w_ref[:SB_WIDTH, :]) + _dot(o_g.astype(BF16), w_ref[SB_WIDTH:, :])
    y = _half_step_ffn(x_ref[...] + mix, g_ref, wg_ref, wu_ref, wd_ref, a_ref)
    o_ref[...] = _rms(y, gfin_ref[...])


def _mixout(x, osb, og, r, gsb, ggla, w, g, wg, wu, wd, gfin, *, tm):
    n = x.shape[0]
    row = lambda i: (i, 0)
    return pl.pallas_call(
        _mixout_kernel,
        grid=(n // tm,),
        in_specs=[pl.BlockSpec((tm, D_MODEL), row),
                  pl.BlockSpec((tm, SB_WIDTH), row),
                  pl.BlockSpec((tm, GLA_WIDTH), row),
                  pl.BlockSpec((tm, GLA_WIDTH), row),
                  _const_spec((1, SB_WIDTH)),
                  _const_spec((1, GLA_WIDTH)),
                  _const_spec((D_MODEL, D_MODEL))] + _ffn_specs() + [_const_spec((1, D_MODEL))],
        out_specs=pl.BlockSpec((tm, D_MODEL), row),
        out_shape=jax.ShapeDtypeStruct((n, D_MODEL), F32),
        scratch_shapes=[pltpu.VMEM((tm, D_FF), BF16)],
        compiler_params=_params(1),
        name="mixer_out_ffn2",
    )(x, osb, og, r, gsb, ggla, w, g, wg, wu, wd, gfin)


def _suffix_matrices():
    j = np.arange(SB_BLOCK)[:, None]
    s = np.arange(SB_BLOCK)[None, :]
    later = j > s
    zero = np.zeros_like(later)
    pair = np.block([[later, zero], [zero, later]])
    single = np.concatenate([later, np.ones_like(later)], axis=1)
    twice = lambda m: jnp.asarray(np.concatenate([m, m], axis=0), BF16)
    return twice(pair), twice(single)


def _state_to_kernel(s):
    b = s.shape[0]
    s = s.reshape(b, GLA_HEADS // 2, 2, GLA_HEAD_K, GLA_HEAD_V)
    return s.transpose(0, 1, 4, 2, 3).reshape(b, GLA_HEADS // 2, GLA_HEAD_V, LANES)


def _state_from_kernel(s):
    b = s.shape[0]
    s = s.reshape(b, GLA_HEADS // 2, GLA_HEAD_V, 2, GLA_HEAD_K)
    return s.transpose(0, 1, 3, 4, 2).reshape(b, GLA_HEADS, GLA_HEAD_K, GLA_HEAD_V)


def _layer(x1, lw, consts, *, batch, seq, tm, cache=None, state=None):
    (g_mix, w_in, wgu, bg, gq, gk, gsb, ggla, w_out, g2, w2g, w2u, w2d, gfin) = lw
    upair, uo, mats, lv = consts
    (qsb16, ksb, ksb16, vsb, vsb16, qg, kg, vg16, r, la) = _mixin(x1, g_mix, w_in, wgu, bg, gq, gk, tm=tm)

    if cache is None:
        o_sb = _sb(qsb16, ksb16, vsb16, upair, uo, batch=batch, tq=SB_BLOCK, keys=seq)
    else:
        o_sb = _sb(qsb16, ksb16, vsb16, upair, uo, batch=batch, tq=seq, keys=seq, cache=cache)

    if state is None:
        state = jnp.zeros((batch, GLA_HEADS // 2, GLA_HEAD_V, LANES), F32)
    o_g, s_fin = _gla(qg, kg, la, vg16, state, mats, lv, batch=batch, seq=seq)

    y = _mixout(x1, o_sb, o_g, r, gsb, ggla, w_out, g2, w2g, w2u, w2d, gfin, tm=tm)
    return y, ksb, vsb, _state_from_kernel(s_fin)


def kernel(x_prompt, x_sample, cache_sb_k, cache_sb_v, state_gla, g_ffn1, w_ffn1_gate, w_ffn1_up,
           w_ffn1_down, g_mix, w_in, w_gate_up, b_gate, g_q, g_k, g_sb_out, g_gla_out, w_out,
           g_ffn2, w_ffn2_gate, w_ffn2_up, w_ffn2_down, g_final):
    depth = w_in.shape[0]
    batch, seq, _ = x_prompt.shape
    dec_batch, dec_seq, _ = x_sample.shape
    consts = (*_suffix_matrices(), *_gla_tables(GLA_CHUNK))

    y_p = x_prompt.reshape(batch * seq, D_MODEL)
    y_s = x_sample.reshape(dec_batch * dec_seq, D_MODEL)
    outs = [[] for _ in range(6)]
    for l in range(depth):
        row = lambda a: a[l].reshape(1, -1)
        x1_s, ffn1 = _ffn_stream(y_s, row(g_ffn1), w_ffn1_gate[l], w_ffn1_up[l], w_ffn1_down[l])
        x1_p, (w2g, w2u, w2d, w_in16, w_out16) = _ffn(
            y_p, row(g_ffn1), *ffn1, tm=FFN1_ROWS,
            cast=[w_ffn2_gate[l], w_ffn2_up[l], w_ffn2_down[l], (w_in[l].T, IN_PAD), w_out[l]])
        lw = (row(g_mix), w_in16,
              jnp.pad(w_gate_up[l], ((0, GATE_PAD - GATE_RANK), (0, 0))).astype(BF16), row(b_gate),
              jnp.tile(g_q[l], SB_HEADS).reshape(1, -1), jnp.tile(g_k[l], SB_HEADS).reshape(1, -1),
              jnp.tile(g_sb_out[l], SB_HEADS).reshape(1, -1), jnp.tile(g_gla_out[l], GLA_HEADS).reshape(1, -1),
              w_out16, row(g_ffn2), w2g, w2u, w2d, row(g_final))
        y_p, k_p, v_p, s_p = _layer(x1_p, lw, consts, batch=batch, seq=seq, tm=512)
        cache = (cache_sb_k[l].transpose(0, 2, 3, 1), cache_sb_v[l].transpose(0, 2, 3, 1))
        y_s, k_s, v_s, s_s = _layer(x1_s, lw, consts, batch=dec_batch, seq=dec_seq, tm=256,
                                    cache=cache, state=_state_to_kernel(state_gla[l]))
        shape_p = (batch, seq, SB_HEADS, SB_HEAD_DIM)
        shape_s = (dec_batch, dec_seq, SB_HEADS, SB_HEAD_DIM)
        for lst, val in zip(outs, (k_p.reshape(shape_p), v_p.reshape(shape_p), s_p,
                                   k_s.reshape(shape_s), v_s.reshape(shape_s), s_s)):
            lst.append(val)
    return (y_p.reshape(batch, seq, D_MODEL), y_s.reshape(dec_batch, dec_seq, D_MODEL),
            *[jnp.stack(o) for o in outs])
```

```python
import functools

import numpy as np
import jax
import jax.numpy as jnp
from jax import lax
from jax.experimental import pallas as pl
from jax.experimental.pallas import tpu as pltpu

F32 = jnp.float32
BF16 = jnp.bfloat16

D_MODEL = 1024
SB_HEADS = 8
SB_HEAD_DIM = 64
SB_WIDTH = SB_HEADS * SB_HEAD_DIM
SB_SCALE = SB_HEAD_DIM ** -0.5
GLA_HEADS = 4
GLA_HEAD_K = 64
GLA_HEAD_V = 128
GLA_KEY_WIDTH = GLA_HEADS * GLA_HEAD_K
GLA_WIDTH = GLA_HEADS * GLA_HEAD_V
GLA_SCALE = GLA_HEAD_K ** -0.5
GATE_RANK = 16
GATE_TAU = 16.0
D_FF = 2816
EPS = 1e-6

LANES = 128
SUBLANES = 8
BF16_SUBLANES = 16
GATE_PAD = LANES
MAIN_WIDTH = 3 * SB_WIDTH + 2 * GLA_KEY_WIDTH + 2 * GLA_WIDTH
IN_PAD = MAIN_WIDTH + GATE_PAD

FF_CHUNK = 256
FFN1_ROWS = 1024
FF_STREAM_CHUNK = 256
SB_BLOCK = 128
SB_WINDOW = 3
SB_PAIRS_PER_STEP = 4
SB_QBLOCKS_PER_STEP = 4
HALF_WINDOW = 2
GLA_CHUNK = 128
GLA_BATCH_PER_STEP = 2
GLA_CHUNKS_PER_STEP = 2
GLA_DIRECT_MIN = -40.0
SB_STOP = 152.0
LOG2_E = 1.4426950408889634

VMEM_LIMIT = 56 * 1024 * 1024


def _dot(a, b):
    return jnp.dot(a, b, preferred_element_type=F32)


def _dot_nt(a, b):
    return lax.dot_general(a, b, (((1,), (1,)), ((), ())), preferred_element_type=F32)


def _dot_tn(a, b):
    return lax.dot_general(a, b, (((0,), (0,)), ((), ())), preferred_element_type=F32)


def _split_bf16(x):
    hi = x.astype(BF16)
    lo = (x - hi.astype(F32)).astype(BF16)
    return hi, lo


def _rms(x, g):
    ms = jnp.mean(x * x, axis=-1, keepdims=True)
    return x * lax.rsqrt(ms + EPS) * g


def _log_sigmoid(x):
    return jnp.minimum(x, 0.0) - jnp.log1p(jnp.exp(-jnp.abs(x)))


def _const_spec(shape):
    nd = len(shape)
    return pl.BlockSpec(shape, lambda *_: (0,) * nd, pipeline_mode=pl.Buffered(1))


def _params(n_grid):
    return pltpu.CompilerParams(dimension_semantics=("arbitrary",) * n_grid,
                                vmem_limit_bytes=VMEM_LIMIT)


def _half_step_ffn(x, g_ref, wg_ref, wu_ref, wd_ref, a_ref):
    h = _rms(x, g_ref[...]).astype(BF16)
    for c in range(D_FF // FF_CHUNK):
        sl = slice(c * FF_CHUNK, (c + 1) * FF_CHUNK)
        gt = _dot(h, wg_ref[:, sl])
        up = _dot(h, wu_ref[:, sl])
        a_ref[:, sl] = (gt * jax.nn.sigmoid(gt) * up).astype(BF16)
    return x + 0.5 * _dot(a_ref[...], wd_ref[...])


def _ffn_kernel(x_ref, g_ref, wg_ref, wu_ref, wd_ref, *refs, transposed):
    n_cast = (len(refs) - 2) // 2
    o_ref, a_ref = refs[n_cast], refs[-1]
    o_ref[...] = _half_step_ffn(x_ref[...], g_ref, wg_ref, wu_ref, wd_ref, a_ref)
    for src, dst, tr in zip(refs[:n_cast], refs[n_cast + 1:-1], transposed):
        if tr is None:
            dst[...] = src[...].astype(BF16)
        else:
            rows, blocks = tr
            first = jnp.minimum(pl.program_id(0), blocks - 1) * src.shape[0]
            inside = lax.broadcasted_iota(jnp.int32, src.shape, 0) < rows - first
            dst[...] = jnp.where(inside, src[...], 0.0).T.astype(BF16)


def _ffn_specs():
    return [_const_spec((1, D_MODEL)), _const_spec((D_MODEL, D_FF)), _const_spec((D_MODEL, D_FF)),
            _const_spec((D_FF, D_MODEL))]


def _cast_block_rows(rows, steps):
    br = next(b for b in range(BF16_SUBLANES, rows + 1, BF16_SUBLANES) if rows % b == 0 and rows // b <= steps)
    return br, rows // br


def _ffn(x, g, wg, wu, wd, *, tm, cast=()):
    n = x.shape[0]
    steps = n // tm
    row = lambda i: (i, 0)
    cast_in, cast_out, cast_shape, transposed = [], [], [], []
    for w in cast:
        if isinstance(w, tuple):
            w, width = w
            tb = LANES * -(-width // (LANES * steps))
            nblk = -(-width // tb)
            width = nblk * tb
            assert nblk <= steps and w.shape[0] <= width
            cast_in.append(pl.BlockSpec((tb, w.shape[1]), lambda i, last=nblk - 1: (jnp.minimum(i, last), 0)))
            cast_out.append(pl.BlockSpec((w.shape[1], tb), lambda i, last=nblk - 1: (0, jnp.minimum(i, last))))
            cast_shape.append(jax.ShapeDtypeStruct((w.shape[1], width), BF16))
            transposed.append((w.shape[0], nblk))
        else:
            br, nblk = _cast_block_rows(w.shape[0], steps)
            idx = lambda i, last=nblk - 1: (jnp.minimum(i, last), 0)
            cast_in.append(pl.BlockSpec((br, w.shape[1]), idx))
            cast_out.append(pl.BlockSpec((br, w.shape[1]), idx))
            cast_shape.append(jax.ShapeDtypeStruct(w.shape, BF16))
            transposed.append(None)
    cast = [w[0] if isinstance(w, tuple) else w for w in cast]
    out = pl.pallas_call(
        functools.partial(_ffn_kernel, transposed=tuple(transposed)),
        grid=(steps,),
        in_specs=[pl.BlockSpec((tm, D_MODEL), row)] + _ffn_specs() + cast_in,
        out_specs=[pl.BlockSpec((tm, D_MODEL), row)] + cast_out,
        out_shape=[jax.ShapeDtypeStruct((n, D_MODEL), F32)] + cast_shape,
        scratch_shapes=[pltpu.VMEM((tm, D_FF), BF16)],
        compiler_params=_params(1),
        name="ffn1",
    )(x, g, wg, wu, wd, *cast)
    return out[0], out[1:]


def _ffn_stream_kernel(x_ref, g_ref, wg_ref, wu_ref, wd_ref, o_ref, wg16_ref, wu16_ref, wd16_ref,
                       h_ref, acc_ref):
    c = pl.program_id(0)

    @pl.when(c == 0)
    def _():
        h_ref[...] = _rms(x_ref[...], g_ref[...]).astype(BF16)
        acc_ref[...] = jnp.zeros_like(acc_ref)

    wg = wg_ref[...].astype(BF16)
    wu = wu_ref[...].astype(BF16)
    wd = wd_ref[...].astype(BF16)
    wg16_ref[...] = wg
    wu16_ref[...] = wu
    wd16_ref[...] = wd
    h = h_ref[...]
    gt = _dot(h, wg)
    acc_ref[...] += _dot((gt * jax.nn.sigmoid(gt) * _dot(h, wu)).astype(BF16), wd)

    @pl.when(c == pl.num_programs(0) - 1)
    def _():
        o_ref[...] = x_ref[...] + 0.5 * acc_ref[...]


def _ffn_stream(x, g, wg, wu, wd):
    n = x.shape[0]
    whole = lambda shape: pl.BlockSpec(shape, lambda c: (0, 0))
    cols = pl.BlockSpec((D_MODEL, FF_STREAM_CHUNK), lambda c: (0, c))
    rows = pl.BlockSpec((FF_STREAM_CHUNK, D_MODEL), lambda c: (c, 0))
    out = pl.pallas_call(
        _ffn_stream_kernel,
        grid=(D_FF // FF_STREAM_CHUNK,),
        in_specs=[whole((n, D_MODEL)), whole((1, D_MODEL)), cols, cols, rows],
        out_specs=[whole((n, D_MODEL)), cols, cols, rows],
        out_shape=[jax.ShapeDtypeStruct((n, D_MODEL), F32),
                   jax.ShapeDtypeStruct((D_MODEL, D_FF), BF16), jax.ShapeDtypeStruct((D_MODEL, D_FF), BF16),
                   jax.ShapeDtypeStruct((D_FF, D_MODEL), BF16)],
        scratch_shapes=[pltpu.VMEM((n, D_MODEL), BF16), pltpu.VMEM((n, D_MODEL), F32)],
        compiler_params=_params(1),
        name="ffn1_stream",
    )(x, g, wg, wu, wd)
    return out[0], out[1:]


def _head_norm_lanes(y, head, gain):
    tm, width = y.shape
    low = lax.broadcasted_iota(jnp.int32, (tm, LANES), 1) < head
    cols = []
    for j in range(width // LANES):
        yj = y[:, j * LANES:(j + 1) * LANES]
        sq = yj * yj

        def inv_rms(part):
            return lax.rsqrt(jnp.sum(part, axis=1, keepdims=True) * (1.0 / head) + EPS)

        if head == LANES:
            cols.append(yj * inv_rms(sq))
        else:
            cols.append(yj * jnp.where(low, inv_rms(jnp.where(low, sq, 0.0)), inv_rms(jnp.where(low, 0.0, sq))))
    return jnp.concatenate(cols, axis=1) * gain


def _store_heads(ref, y):
    groups = y.shape[0] // SUBLANES
    sub = lax.broadcasted_iota(jnp.int32, (groups, SUBLANES, LANES), 1)
    rows = []
    for j in range(SB_WIDTH // LANES):
        pair = y[:, j * LANES:(j + 1) * LANES].reshape(groups, SUBLANES, LANES)
        rows += [pair, pltpu.roll(pair, SB_HEAD_DIM, axis=2)]
    for dist in (4, 2, 1):
        low = (sub & dist) == 0
        new = list(rows)
        for h in range(SUBLANES):
            if h & dist == 0:
                a, b = rows[h], rows[h + dist]
                new[h] = jnp.where(low, a, pltpu.roll(b, dist, axis=1))
                new[h + dist] = jnp.where(low, pltpu.roll(a, SUBLANES - dist, axis=1), b)
        rows = new
    for t in range(SUBLANES):
        ref[:, t] = rows[t][:, :, :SB_HEAD_DIM]


def _mixin_kernel(x_ref, g_ref, w_ref, wgu_ref, bg_ref, gq_ref, gk_ref,
                  qsb_ref, ksb_ref, ksb16_ref, vsb_ref, vsb16_ref,
                  qg_ref, kg_ref, vg16_ref, r_ref, la_ref):
    h = _rms(x_ref[...], g_ref[...]).astype(BF16)

    def proj(lo, width):
        return _dot(h, w_ref[:, lo:lo + width])

    q = _head_norm_lanes(proj(0, SB_WIDTH), SB_HEAD_DIM, gq_ref[...])
    qsb_ref[...] = (q * SB_SCALE).astype(BF16)
    k = _head_norm_lanes(proj(SB_WIDTH, SB_WIDTH), SB_HEAD_DIM, gk_ref[...])
    _store_heads(ksb_ref, k)
    ksb16_ref[...] = k.astype(BF16)
    v = proj(2 * SB_WIDTH, SB_WIDTH)
    _store_heads(vsb_ref, v)
    vsb16_ref[...] = v.astype(BF16)
    off = 3 * SB_WIDTH
    qg_ref[...] = proj(off, GLA_KEY_WIDTH) * GLA_SCALE
    kg_ref[...] = proj(off + GLA_KEY_WIDTH, GLA_KEY_WIDTH)
    vg16_ref[...] = proj(off + 2 * GLA_KEY_WIDTH, GLA_WIDTH).astype(BF16)
    r_ref[...] = proj(off + 2 * GLA_KEY_WIDTH + GLA_WIDTH, GLA_WIDTH)
    lr = proj(MAIN_WIDTH, GATE_PAD).astype(BF16)
    gate = _dot(lr, wgu_ref[...]) + bg_ref[...]
    la_ref[...] = _log_sigmoid(gate) * (1.0 / GATE_TAU)


def _mixin(x, g, w, wgu, bg, gq, gk, *, tm):
    n = x.shape[0]
    row = lambda i: (i, 0)
    heads = (SB_HEADS, SB_HEAD_DIM)
    assert SB_HEADS == SUBLANES and tm % SUBLANES == 0
    lead = lambda rows, wd: (rows // SUBLANES, SUBLANES) if len(wd) == 2 else (rows,)
    widths = [((SB_WIDTH,), BF16), (heads, F32), ((SB_WIDTH,), BF16), (heads, F32), ((SB_WIDTH,), BF16),
              ((GLA_KEY_WIDTH,), F32), ((GLA_KEY_WIDTH,), F32), ((GLA_WIDTH,), BF16), ((GLA_WIDTH,), F32),
              ((GLA_KEY_WIDTH,), F32)]
    return pl.pallas_call(
        _mixin_kernel,
        grid=(n // tm,),
        in_specs=[pl.BlockSpec((tm, D_MODEL), row),
                  _const_spec((1, D_MODEL)),
                  _const_spec(w.shape),
                  _const_spec((GATE_PAD, GLA_KEY_WIDTH)),
                  _const_spec((1, GLA_KEY_WIDTH)),
                  _const_spec((1, SB_WIDTH)),
                  _const_spec((1, SB_WIDTH))],
        out_specs=[pl.BlockSpec(lead(tm, wd) + wd, lambda i, nd=len(lead(tm, wd) + wd) - 1: (i,) + (0,) * nd)
                   for wd, _ in widths],
        out_shape=[jax.ShapeDtypeStruct(lead(n, wd) + wd, dt) for wd, dt in widths],
        compiler_params=_params(1),
        name="mixer_in",
    )(x, g, w, wgu, bg, gq, gk)


def _neg_abs(x):
    return -jnp.abs(x)


def _sb_log2_terms(z):
    z2 = z * LOG2_E
    t = jnp.log2(1.0 + jnp.exp2(_neg_abs(z2)))
    return jnp.maximum(z2, 0.0) + t, jnp.minimum(z2, 0.0) - t


def _sb_kernel(q_ref, k_ref, v_ref, upair_ref, uo_ref, *refs, tq, q_block):
    if q_block is None:
        o_ref, carry_all, acc_all, live_ref = refs
    else:
        kc_hbm, vc_hbm, o_ref, carry_all, acc_all, live_ref, kbuf, vbuf, sems, kwin, vwin = refs
    pairs = q_ref.shape[1] // LANES
    heads = 2 * pairs

    def one_block(i, row0, u, phase):
        carry_ref, acc_ref = carry_all.at[u], acc_all.at[u]
        lane = lax.broadcasted_iota(jnp.int32, (tq, LANES), 1)
        head_lanes = (lane < SB_HEAD_DIM, lane >= SB_HEAD_DIM)
        rowi = lax.broadcasted_iota(jnp.int32, (heads * tq, SB_BLOCK), 0)
        coli = lax.broadcasted_iota(jnp.int32, (heads * tq, SB_BLOCK), 1)
        causal = coli < (rowi & (tq - 1))

        def pair_lanes(p):
            return slice(p * LANES, (p + 1) * LANES)

        def q_pair(p):
            q2 = q_ref[row0:row0 + tq, pair_lanes(p)]
            zero = jnp.zeros_like(q2)
            return jnp.concatenate([jnp.where(hl, q2, zero) for hl in head_lanes], axis=0)

        def rows(ref, p, first_block, nblocks):
            start = first_block * SB_BLOCK
            if not isinstance(start, int):
                start = pl.multiple_of(start, SB_BLOCK)
            return ref[pl.ds(start, nblocks * SB_BLOCK), pair_lanes(p)]

        def suffix_sums(x):
            nb = x.shape[1] // SB_BLOCK
            hi, lo = _split_bf16(x)

            def mm(cols, m2):
                return _dot(jnp.concatenate([hi[:, cols], lo[:, cols]], axis=1), m2)

            within, total = [], []
            if nb >= 3:
                for j in range(0, nb - 1, 2):
                    y = mm(slice(j * SB_BLOCK, (j + 2) * SB_BLOCK), upair_ref[...])
                    within += [y[:, :SB_BLOCK], y[:, SB_BLOCK:]]
                    total += [jnp.sum(x[:, b * SB_BLOCK:(b + 1) * SB_BLOCK], axis=1, keepdims=True) for b in (j, j + 1)]
            for j in range(len(within), nb):
                y = mm(slice(j * SB_BLOCK, (j + 1) * SB_BLOCK), uo_ref[...])
                within.append(y[:, :SB_BLOCK])
                total.append(y[:, SB_BLOCK:])
            later = total[nb - 1]
            out = [within[nb - 1]]
            for j in range(nb - 2, -1, -1):
                out.insert(0, within[j] + later)
                later = later + total[j]
            return (out[0] if nb == 1 else jnp.concatenate(out, axis=1)), later

        def resident(first_block, nblocks, kv_refs=(k_ref, v_ref)):
            return tuple((lambda p, ref=ref: rows(ref, p, first_block, nblocks)) for ref in kv_refs)

        def sweep(kv, nblocks, own, carry, valid=None):
            nk = nblocks * SB_BLOCK

            def masked(x):
                if valid is not None:
                    return jnp.where(valid, x, 0.0)
                if not own:
                    return x
                tail = jnp.where(causal, x[:, nk - SB_BLOCK:], 0.0)
                return tail if nblocks == 1 else jnp.concatenate([x[:, :nk - SB_BLOCK], tail], axis=1)

            z = jnp.concatenate([_dot_nt(q_pair(p), kv[0](p)) for p in range(pairs)], axis=0)
            fail, log_beta = _sb_log2_terms(z)
            after, total = suffix_sums(masked(fail))
            if carry is not None:
                after = after + carry
            w = masked(jnp.exp2(log_beta - after)).astype(BF16)
            pv = jnp.concatenate([_dot(w[2 * p * tq:(2 * p + 2) * tq], kv[1](p))
                                  for p in range(pairs)], axis=0)
            if carry is None:
                carry_ref[...] = total
                acc_ref[...] = pv
            else:
                carry_ref[...] = carry + total
                acc_ref[...] += pv

        def live():
            return jnp.min(carry_ref[...]) < SB_STOP

        def mark_live():
            live_ref[u] = jnp.where(live(), 1, 0)

        half = SB_BLOCK // 2
        assert HALF_WINDOW == 2

        def by_halves(x, f):
            y = [f(s, jnp.concatenate([x[s * half:(s + 1) * half], x[tq + s * half:tq + (s + 1) * half]], axis=0))
                 for s in range(2)]
            return jnp.concatenate([y[0][:half], y[1][:half], y[0][half:], y[1][half:]], axis=0)

        def half_windows():
            nk = HALF_WINDOW * SB_BLOCK
            starts = [pl.multiple_of(i * SB_BLOCK + (s + 1) * half - nk, half) for s in range(2)]
            keep = coli < (rowi & (half - 1)) + half

            def masked(x):
                return jnp.concatenate([x[:, :SB_BLOCK], jnp.where(keep, x[:, SB_BLOCK:], 0.0)], axis=1)

            z = jnp.concatenate(
                [by_halves(q_pair(p), lambda s, lhs: _dot_nt(lhs, k_ref[pl.ds(starts[s], nk), pair_lanes(p)]))
                 for p in range(pairs)], axis=0)
            fail, log_beta = _sb_log2_terms(z)
            after, total = suffix_sums(masked(fail))
            w = masked(jnp.exp2(log_beta - after)).astype(BF16)
            carry_ref[...] = total
            acc_ref[...] = jnp.concatenate(
                [by_halves(w[2 * p * tq:(2 * p + 2) * tq],
                           lambda s, lhs: _dot(lhs, v_ref[pl.ds(starts[s], nk), pair_lanes(p)]))
                 for p in range(pairs)], axis=0)

        def own_block_only():
            sweep(resident(i, 1), 1, True, None)

        if q_block is None:
            assert tq == SB_BLOCK
            windowed = i >= HALF_WINDOW
            if phase == "window":
                half_windows()
                return
            if phase == "first":
                pl.when(windowed)(half_windows)
                pl.when(jnp.logical_not(windowed))(own_block_only)
                return

            mark_live()

            @pl.when(jnp.logical_and(windowed, live_ref[u] != 0))
            def _():
                unseen = jnp.logical_or(coli < half, (rowi & half) != 0)
                sweep(resident(i - HALF_WINDOW, 1), 1, False, carry_ref[...], unseen)
                mark_live()

            n_prev = jnp.where(windowed, i - HALF_WINDOW, i)
            earlier = lambda kb: resident(kb, 1)
        else:
            recent = SB_WINDOW - 1
            assert q_block >= recent and k_ref.shape[0] == tq <= SB_BLOCK

            def fetch(first_block, nblocks):
                nk = nblocks * SB_BLOCK
                start = first_block * SB_BLOCK
                if not isinstance(start, int):
                    start = pl.multiple_of(start, SB_BLOCK)
                copies = [pltpu.make_async_copy(hbm.at[pl.program_id(0), :, :, pl.ds(start, nk)],
                                                buf.at[:, :, pl.ds(0, nk)], sems.at[n])
                          for n, (hbm, buf) in enumerate(((kc_hbm, kbuf), (vc_hbm, vbuf)))]
                for c in copies:
                    c.start()
                for c in copies:
                    c.wait()

                def slabs(buf):
                    piece = lambda hd, j: buf[hd, :, j * SB_BLOCK:(j + 1) * SB_BLOCK].T
                    return lambda j, p: jnp.concatenate([piece(2 * p, j), piece(2 * p + 1, j)], axis=1).astype(BF16)

                return slabs(kbuf), slabs(vbuf)

            def earlier(kb):
                return tuple((lambda p, slab=slab: slab(0, p)) for slab in fetch(kb, 1))

            for slab, win in zip(fetch(q_block - recent, recent), (kwin, vwin)):
                for j in range(recent):
                    for p in range(pairs):
                        win[j * SB_BLOCK:(j + 1) * SB_BLOCK, pair_lanes(p)] = slab(j, p)
            for new, win in ((k_ref, kwin), (v_ref, vwin)):
                win[recent * SB_BLOCK:recent * SB_BLOCK + tq, :] = new[...]
                win[recent * SB_BLOCK + tq:, :] = jnp.zeros((SB_BLOCK - tq, win.shape[1]), BF16)
            sweep(resident(0, SB_WINDOW, (kwin, vwin)), SB_WINDOW, True, None)
            mark_live()
            n_prev = q_block - recent

        def body(st):
            kb, _ = st
            sweep(earlier(kb), 1, False, carry_ref[...])
            return kb - 1, live()

        lax.while_loop(lambda st: jnp.logical_and(st[0] >= 0, st[1]), body, (n_prev - 1, live_ref[u] != 0))
        for p in range(pairs):
            o_ref[row0:row0 + tq, pair_lanes(p)] = jnp.where(head_lanes[0], acc_ref[2 * p * tq:(2 * p + 1) * tq],
                                                              acc_ref[(2 * p + 1) * tq:(2 * p + 2) * tq])

    if q_block is None:
        per_step = q_ref.shape[0] // tq
        blocks = [(pl.program_id(2) * per_step + u, u * tq, u) for u in range(per_step)]
        all_windowed = blocks[0][0] >= HALF_WINDOW

        @pl.when(all_windowed)
        def _():
            for blk in blocks:
                one_block(*blk, "window")

        @pl.when(jnp.logical_not(all_windowed))
        def _():
            for blk in blocks:
                one_block(*blk, "first")

        for blk in blocks:
            one_block(*blk, "rest")
    else:
        one_block(q_block, 0, 0, "rest")


def _sb(q16, k16, v16, upair, uo, *, batch, tq, keys, cache=None):
    sub = SB_QBLOCKS_PER_STEP if cache is None else 1
    nq = keys // (sub * SB_BLOCK) if cache is None else 1
    width = SB_PAIRS_PER_STEP * LANES
    assert width == SB_WIDTH or cache is None
    kv_spec = pl.BlockSpec((keys, width), lambda b, p, i: (b, p))
    q_spec = pl.BlockSpec((sub * tq, width), lambda b, p, i: (b * nq + i, p))
    heads = 2 * SB_PAIRS_PER_STEP
    in_specs = [q_spec, kv_spec, kv_spec, _const_spec(upair.shape), _const_spec(uo.shape)]
    scratch = [pltpu.VMEM((sub, heads * tq, SB_BLOCK), F32), pltpu.VMEM((sub, heads * tq, LANES), F32),
               pltpu.SMEM((sub,), jnp.int32)]
    operands = [q16, k16, v16, upair, uo]
    q_block = None
    if cache is not None:
        past = cache[0].shape[3]
        assert past % SB_BLOCK == 0 and keys == tq and SB_WINDOW * SB_BLOCK <= past + SB_BLOCK
        q_block = past // SB_BLOCK
        in_specs += [pl.BlockSpec(memory_space=pl.ANY)] * 2
        scratch += ([pltpu.VMEM((SB_HEADS, SB_HEAD_DIM, (SB_WINDOW - 1) * SB_BLOCK), F32)] * 2
                    + [pltpu.SemaphoreType.DMA((2,))]
                    + [pltpu.VMEM((SB_WINDOW * SB_BLOCK, SB_WIDTH), BF16)] * 2)
        operands += list(cache)
    return pl.pallas_call(
        functools.partial(_sb_kernel, tq=tq, q_block=q_block),
        grid=(batch, SB_WIDTH // width, nq),
        in_specs=in_specs,
        out_specs=q_spec,
        out_shape=jax.ShapeDtypeStruct((batch * nq * sub * tq, SB_WIDTH), F32),
        scratch_shapes=scratch,
        compiler_params=_params(3),
        name="sb_prompt" if cache is None else "sb_sample",
    )(*operands)


def _gla_levels(c):
    return int(np.log2(c))


def _gla_tables(c):
    nlev = _gla_levels(c)
    t = np.arange(c)[:, None]
    j = np.arange(c)[None, :]
    mats = [(j <= t), (j > t)]
    for lev in range(1, nlev + 1):
        m = 1 << (lev - 1)
        start = (t // (2 * m)) * (2 * m)
        mats.append(((t % (2 * m)) >= m) & (j >= start + m) & (j <= t))
    for lev in range(1, nlev + 1):
        m = 1 << (lev - 1)
        start = (t // (2 * m)) * (2 * m)
        mats.append(((t % (2 * m)) < m) & (j > t) & (j <= start + m - 1))
    mats = np.concatenate([a.astype(np.float32) for a in mats], axis=0)
    x = t ^ j
    level = np.where(j > t, -1, np.where(x == 0, 0, np.floor(np.log2(np.maximum(x, 1))).astype(np.int64) + 1))
    return jnp.asarray(mats, BF16), jnp.asarray(level, jnp.int32)


def _gla_kernel(q_ref, k_ref, la_ref, v_ref, s0_ref, mats_ref, lv_ref, o_ref, sfin_ref, st_ref):
    c = GLA_CHUNK
    nlev = _gla_levels(c)
    streams, t_rows = q_ref.shape[:2]

    @pl.when(pl.program_id(1) == 0)
    def _():
        st_ref[...] = s0_ref[...]

    lane = lax.broadcasted_iota(jnp.int32, (c, LANES), 1)
    head_lanes = (lane < GLA_HEAD_K, lane >= GLA_HEAD_K)
    lane_sq = lax.broadcasted_iota(jnp.int32, (GLA_HEAD_V, LANES), 1) < GLA_HEAD_K

    def pick(x, hl):
        return jnp.where(hl, x, jnp.zeros_like(x))

    def stack_heads(x):
        return jnp.concatenate([pick(x, hl) for hl in head_lanes], axis=0)

    def one_chunk(rows):
        n_rows = rows.stop - rows.start

        def chunk(x):
            return x if n_rows == c else jnp.concatenate([x, jnp.zeros((c - n_rows, x.shape[1]), x.dtype)], axis=0)

        def finish(s, p, scores, qe, ke, b_last):
            sl = slice(p * LANES, (p + 1) * LANES)
            st = st_ref[s, p]
            inter = _dot_nt(stack_heads(qe[:, sl]), st.astype(BF16))
            upd = []
            for r in range(2):
                hd = 2 * p + r
                vh = chunk(v_ref[s, rows, hd * GLA_HEAD_V:(hd + 1) * GLA_HEAD_V])
                o_ref[s, rows, hd * GLA_HEAD_V:(hd + 1) * GLA_HEAD_V] = (
                    inter[r * c:(r + 1) * c] + _dot(scores[r * c:(r + 1) * c], vh))[:n_rows]
                upd.append(_dot_tn(vh, ke[:, sl]))
            st_ref[s, p] = st * jnp.exp(b_last[:, sl]) + jnp.where(lane_sq, upd[0], upd[1])

        la_all = jnp.concatenate([chunk(la_ref[s, rows, :]) for s in range(streams)], axis=1)
        b_all = _dot_hilo_left(mats_ref[:c, :], la_all)
        mild = jnp.min(b_all[c - 1:c, :]) >= GLA_DIRECT_MIN

        def direct():
            rowi = lax.broadcasted_iota(jnp.int32, (2 * c, c), 0) & (c - 1)
            causal = lax.broadcasted_iota(jnp.int32, (2 * c, c), 1) <= rowi
            for s in range(streams):
                b = b_all[:, s * GLA_KEY_WIDTH:(s + 1) * GLA_KEY_WIDTH]
                b_last = b[c - 1:c, :]
                q = chunk(q_ref[s, rows, :])
                k = chunk(k_ref[s, rows, :])
                qe = (q * jnp.exp(b)).astype(BF16)
                kd = (k * jnp.exp(-b)).astype(BF16)
                ke = (k * jnp.exp(b_last - b)).astype(BF16)
                for p in range(GLA_HEADS // 2):
                    sl = slice(p * LANES, (p + 1) * LANES)
                    scores = jnp.where(causal, _dot_nt(stack_heads(qe[:, sl]), kd[:, sl]), 0.0)
                    finish(s, p, scores.astype(BF16), qe, ke, b_last)

        def by_levels():
            lv = jnp.concatenate([lv_ref[...], lv_ref[...]], axis=0)
            for s in range(streams):
                q = chunk(q_ref[s, rows, :])
                k = chunk(k_ref[s, rows, :])
                sums = _dot_hilo_left(mats_ref[...], chunk(la_ref[s, rows, :]))

                def level_rows(i):
                    return sums[i * c:(i + 1) * c]

                qe = (q * jnp.exp(level_rows(0))).astype(BF16)
                ke = (k * jnp.exp(level_rows(1))).astype(BF16)
                q_lev = [q.astype(BF16)] + [(q * jnp.exp(level_rows(1 + lev))).astype(BF16)
                                            for lev in range(1, nlev + 1)]
                k_lev = [k.astype(BF16)] + [(k * jnp.exp(level_rows(1 + nlev + lev))).astype(BF16)
                                            for lev in range(1, nlev + 1)]
                for p in range(GLA_HEADS // 2):
                    sl = slice(p * LANES, (p + 1) * LANES)
                    scores = jnp.zeros((2 * c, c), F32)
                    for lev in range(nlev + 1):
                        scores = jnp.where(lv == lev, _dot_nt(stack_heads(q_lev[lev][:, sl]), k_lev[lev][:, sl]),
                                           scores)
                    finish(s, p, scores.astype(BF16), qe, ke, level_rows(0)[c - 1:c, :])

        return mild, direct, by_levels

    n_chunks = max(1, t_rows // c)
    parts = [one_chunk(slice(u * c, min((u + 1) * c, t_rows))) for u in range(n_chunks)]
    all_mild = functools.reduce(jnp.logical_and, [mild for mild, _, _ in parts])

    @pl.when(all_mild)
    def _():
        for _, direct, _ in parts:
            direct()

    @pl.when(jnp.logical_not(all_mild))
    def _():
        for mild, direct, by_levels in parts:
            pl.when(mild)(direct)
            pl.when(jnp.logical_not(mild))(by_levels)

    sfin_ref[...] = st_ref[...]


def _dot_hilo_left(m, x):
    hi, lo = _split_bf16(x)
    return _dot(m, hi) + _dot(m, lo)


def _gla(qg, kg, la, vg16, s0, mats, lv, *, batch, seq):
    c = min(GLA_CHUNK * GLA_CHUNKS_PER_STEP, seq)
    assert seq % c == 0 and (c < GLA_CHUNK or c % GLA_CHUNK == 0)
    g = GLA_BATCH_PER_STEP
    pairs = GLA_HEADS // 2
    blk = lambda width: pl.BlockSpec((g, c, width), lambda b, t: (b, t, 0))
    st_spec = pl.BlockSpec((g, pairs, GLA_HEAD_V, LANES), lambda b, t: (b, 0, 0, 0))
    per_stream = lambda a: a.reshape(batch, seq, a.shape[-1])
    o, s_fin = pl.pallas_call(
        _gla_kernel,
        grid=(batch // g, seq // c),
        in_specs=[blk(GLA_KEY_WIDTH), blk(GLA_KEY_WIDTH), blk(GLA_KEY_WIDTH), blk(GLA_WIDTH), st_spec,
                  _const_spec(mats.shape), _const_spec(lv.shape)],
        out_specs=[blk(GLA_WIDTH), st_spec],
        out_shape=[jax.ShapeDtypeStruct((batch, seq, GLA_WIDTH), F32),
                   jax.ShapeDtypeStruct((batch, pairs, GLA_HEAD_V, LANES), F32)],
        scratch_shapes=[pltpu.VMEM((g, pairs, GLA_HEAD_V, LANES), F32)],
        compiler_params=_params(2),
        name="gla",
    )(per_stream(qg), per_stream(kg), per_stream(la), per_stream(vg16), s0, mats, lv)
    return o.reshape(batch * seq, GLA_WIDTH), s_fin


def _mixout_kernel(x_ref, osb_ref, og_ref, r_ref, gsb_ref, ggla_ref, w_ref,
                   g_ref, wg_ref, wu_ref, wd_ref, gfin_ref, o_ref, a_ref):
    o_sb = _head_norm_lanes(osb_ref[...], SB_HEAD_DIM, gsb_ref[...])
    r = r_ref[...]
    o_g = _head_norm_lanes(og_ref[...], GLA_HEAD_V, ggla_ref[...]) * (r * jax.nn.sigmoid(r))
    mix = _dot(o_sb.astype(BF16), w_ref[:SB_WIDTH, :]) + _dot(o_g.astype(BF16), w_ref[SB_WIDTH:, :])
    y = _half_step_ffn(x_ref[...] + mix, g_ref, wg_ref, wu_ref, wd_ref, a_ref)
    o_ref[...] = _rms(y, gfin_ref[...])


def _mixout(x, osb, og, r, gsb, ggla, w, g, wg, wu, wd, gfin, *, tm):
    n = x.shape[0]
    row = lambda i: (i, 0)
    return pl.pallas_call(
        _mixout_kernel,
        grid=(n // tm,),
        in_specs=[pl.BlockSpec((tm, D_MODEL), row),
                  pl.BlockSpec((tm, SB_WIDTH), row),
                  pl.BlockSpec((tm, GLA_WIDTH), row),
                  pl.BlockSpec((tm, GLA_WIDTH), row),
                  _const_spec((1, SB_WIDTH)),
                  _const_spec((1, GLA_WIDTH)),
                  _const_spec((D_MODEL, D_MODEL))] + _ffn_specs() + [_const_spec((1, D_MODEL))],
        out_specs=pl.BlockSpec((tm, D_MODEL), row),
        out_shape=jax.ShapeDtypeStruct((n, D_MODEL), F32),
        scratch_shapes=[pltpu.VMEM((tm, D_FF), BF16)],
        compiler_params=_params(1),
        name="mixer_out_ffn2",
    )(x, osb, og, r, gsb, ggla, w, g, wg, wu, wd, gfin)


def _suffix_matrices():
    j = np.arange(SB_BLOCK)[:, None]
    s = np.arange(SB_BLOCK)[None, :]
    later = j > s
    zero = np.zeros_like(later)
    pair = np.block([[later, zero], [zero, later]])
    single = np.concatenate([later, np.ones_like(later)], axis=1)
    twice = lambda m: jnp.asarray(np.concatenate([m, m], axis=0), BF16)
    return twice(pair), twice(single)


def _state_to_kernel(s):
    b = s.shape[0]
    s = s.reshape(b, GLA_HEADS // 2, 2, GLA_HEAD_K, GLA_HEAD_V)
    return s.transpose(0, 1, 4, 2, 3).reshape(b, GLA_HEADS // 2, GLA_HEAD_V, LANES)


def _state_from_kernel(s):
    b = s.shape[0]
    s = s.reshape(b, GLA_HEADS // 2, GLA_HEAD_V, 2, GLA_HEAD_K)
    return s.transpose(0, 1, 3, 4, 2).reshape(b, GLA_HEADS, GLA_HEAD_K, GLA_HEAD_V)


def _layer(x1, lw, consts, *, batch, seq, tm, cache=None, state=None):
    (g_mix, w_in, wgu, bg, gq, gk, gsb, ggla, w_out, g2, w2g, w2u, w2d, gfin) = lw
    upair, uo, mats, lv = consts
    (qsb16, ksb, ksb16, vsb, vsb16, qg, kg, vg16, r, la) = _mixin(x1, g_mix, w_in, wgu, bg, gq, gk, tm=tm)

    if cache is None:
        o_sb = _sb(qsb16, ksb16, vsb16, upair, uo, batch=batch, tq=SB_BLOCK, keys=seq)
    else:
        o_sb = _sb(qsb16, ksb16, vsb16, upair, uo, batch=batch, tq=seq, keys=seq, cache=cache)

    if state is None:
        state = jnp.zeros((batch, GLA_HEADS // 2, GLA_HEAD_V, LANES), F32)
    o_g, s_fin = _gla(qg, kg, la, vg16, state, mats, lv, batch=batch, seq=seq)

    y = _mixout(x1, o_sb, o_g, r, gsb, ggla, w_out, g2, w2g, w2u, w2d, gfin, tm=tm)
    return y, ksb, vsb, _state_from_kernel(s_fin)


def kernel(x_prompt, x_sample, cache_sb_k, cache_sb_v, state_gla, g_ffn1, w_ffn1_gate, w_ffn1_up,
           w_ffn1_down, g_mix, w_in, w_gate_up, b_gate, g_q, g_k, g_sb_out, g_gla_out, w_out,
           g_ffn2, w_ffn2_gate, w_ffn2_up, w_ffn2_down, g_final):
    depth = w_in.shape[0]
    batch, seq, _ = x_prompt.shape
    dec_batch, dec_seq, _ = x_sample.shape
    consts = (*_suffix_matrices(), *_gla_tables(GLA_CHUNK))

    y_p = x_prompt.reshape(batch * seq, D_MODEL)
    y_s = x_sample.reshape(dec_batch * dec_seq, D_MODEL)
    outs = [[] for _ in range(6)]
    for l in range(depth):
        row = lambda a: a[l].reshape(1, -1)
        x1_s, ffn1 = _ffn_stream(y_s, row(g_ffn1), w_ffn1_gate[l], w_ffn1_up[l], w_ffn1_down[l])
        x1_p, (w2g, w2u, w2d, w_in16, w_out16) = _ffn(
            y_p, row(g_ffn1), *ffn1, tm=FFN1_ROWS,
            cast=[w_ffn2_gate[l], w_ffn2_up[l], w_ffn2_down[l], (w_in[l].T, IN_PAD), w_out[l]])
        lw = (row(g_mix), w_in16,
              jnp.pad(w_gate_up[l], ((0, GATE_PAD - GATE_RANK), (0, 0))).astype(BF16), row(b_gate),
              jnp.tile(g_q[l], SB_HEADS).reshape(1, -1), jnp.tile(g_k[l], SB_HEADS).reshape(1, -1),
              jnp.tile(g_sb_out[l], SB_HEADS).reshape(1, -1), jnp.tile(g_gla_out[l], GLA_HEADS).reshape(1, -1),
              w_out16, row(g_ffn2), w2g, w2u, w2d, row(g_final))
        y_p, k_p, v_p, s_p = _layer(x1_p, lw, consts, batch=batch, seq=seq, tm=512)
        cache = (cache_sb_k[l].transpose(0, 2, 3, 1), cache_sb_v[l].transpose(0, 2, 3, 1))
        y_s, k_s, v_s, s_s = _layer(x1_s, lw, consts, batch=dec_batch, seq=dec_seq, tm=256,
                                    cache=cache, state=_state_to_kernel(state_gla[l]))
        shape_p = (batch, seq, SB_HEADS, SB_HEAD_DIM)
        shape_s = (dec_batch, dec_seq, SB_HEADS, SB_HEAD_DIM)
        for lst, val in zip(outs, (k_p.reshape(shape_p), v_p.reshape(shape_p), s_p,
                                   k_s.reshape(shape_s), v_s.reshape(shape_s), s_s)):
            lst.append(val)
    return (y_p.reshape(batch, seq, D_MODEL), y_s.reshape(dec_batch, dec_seq, D_MODEL),
            *[jnp.stack(o) for o in outs])
```

```python
import functools

import numpy as np
import jax
import jax.numpy as jnp
from jax import lax
from jax.experimental import pallas as pl
from jax.experimental.pallas import tpu as pltpu

F32 = jnp.float32
BF16 = jnp.bfloat16

D_MODEL = 1024
SB_HEADS = 8
SB_HEAD_DIM = 64
SB_WIDTH = SB_HEADS * SB_HEAD_DIM
SB_SCALE = SB_HEAD_DIM ** -0.5
GLA_HEADS = 4
GLA_HEAD_K = 64
GLA_HEAD_V = 128
GLA_KEY_WIDTH = GLA_HEADS * GLA_HEAD_K
GLA_WIDTH = GLA_HEADS * GLA_HEAD_V
GLA_SCALE = GLA_HEAD_K ** -0.5
GATE_RANK = 16
GATE_TAU = 16.0
D_FF = 2816
EPS = 1e-6

LANES = 128
SUBLANES = 8
BF16_SUBLANES = 16
GATE_PAD = LANES
MAIN_WIDTH = 3 * SB_WIDTH + 2 * GLA_KEY_WIDTH + 2 * GLA_WIDTH
IN_PAD = MAIN_WIDTH + GATE_PAD

FF_CHUNK = 256
FFN1_ROWS = 1024
FF_STREAM_CHUNK = 256
SB_BLOCK = 128
SB_WINDOW = 3
SB_PAIRS_PER_STEP = 4
SB_QBLOCKS_PER_STEP = 4
HALF_WINDOW = 2
GLA_CHUNK = 128
GLA_BATCH_PER_STEP = 2
GLA_CHUNKS_PER_STEP = 4
GLA_DIRECT_MIN = -40.0
SB_STOP = 152.0
LOG2_E = 1.4426950408889634

VMEM_LIMIT = 56 * 1024 * 1024


def _dot(a, b):
    return jnp.dot(a, b, preferred_element_type=F32)


def _dot_nt(a, b):
    return lax.dot_general(a, b, (((1,), (1,)), ((), ())), preferred_element_type=F32)


def _dot_tn(a, b):
    return lax.dot_general(a, b, (((0,), (0,)), ((), ())), preferred_element_type=F32)


def _split_bf16(x):
    hi = x.astype(BF16)
    lo = (x - hi.astype(F32)).astype(BF16)
    return hi, lo


def _rms(x, g):
    ms = jnp.mean(x * x, axis=-1, keepdims=True)
    return x * lax.rsqrt(ms + EPS) * g


def _log_sigmoid(x):
    return jnp.minimum(x, 0.0) - jnp.log1p(jnp.exp(-jnp.abs(x)))


def _const_spec(shape):
    nd = len(shape)
    return pl.BlockSpec(shape, lambda *_: (0,) * nd, pipeline_mode=pl.Buffered(1))


def _params(n_grid):
    return pltpu.CompilerParams(dimension_semantics=("arbitrary",) * n_grid,
                                vmem_limit_bytes=VMEM_LIMIT)


def _half_step_ffn(x, g_ref, wg_ref, wu_ref, wd_ref, a_ref):
    h = _rms(x, g_ref[...]).astype(BF16)
    for c in range(D_FF // FF_CHUNK):
        sl = slice(c * FF_CHUNK, (c + 1) * FF_CHUNK)
        gt = _dot(h, wg_ref[:, sl])
        up = _dot(h, wu_ref[:, sl])
        a_ref[:, sl] = (gt * jax.nn.sigmoid(gt) * up).astype(BF16)
    return x + 0.5 * _dot(a_ref[...], wd_ref[...])


def _ffn_kernel(x_ref, g_ref, wg_ref, wu_ref, wd_ref, *refs, transposed):
    n_cast = (len(refs) - 2) // 2
    o_ref, a_ref = refs[n_cast], refs[-1]
    o_ref[...] = _half_step_ffn(x_ref[...], g_ref, wg_ref, wu_ref, wd_ref, a_ref)
    for src, dst, tr in zip(refs[:n_cast], refs[n_cast + 1:-1], transposed):
        if tr is None:
            dst[...] = src[...].astype(BF16)
        else:
            rows, blocks = tr
            first = jnp.minimum(pl.program_id(0), blocks - 1) * src.shape[0]
            inside = lax.broadcasted_iota(jnp.int32, src.shape, 0) < rows - first
            dst[...] = jnp.where(inside, src[...], 0.0).T.astype(BF16)


def _ffn_specs():
    return [_const_spec((1, D_MODEL)), _const_spec((D_MODEL, D_FF)), _const_spec((D_MODEL, D_FF)),
            _const_spec((D_FF, D_MODEL))]


def _cast_block_rows(rows, steps):
    br = next(b for b in range(BF16_SUBLANES, rows + 1, BF16_SUBLANES) if rows % b == 0 and rows // b <= steps)
    return br, rows // br


def _ffn(x, g, wg, wu, wd, *, tm, cast=()):
    n = x.shape[0]
    steps = n // tm
    row = lambda i: (i, 0)
    cast_in, cast_out, cast_shape, transposed = [], [], [], []
    for w in cast:
        if isinstance(w, tuple):
            w, width = w
            tb = LANES * -(-width // (LANES * steps))
            nblk = -(-width // tb)
            width = nblk * tb
            assert nblk <= steps and w.shape[0] <= width
            cast_in.append(pl.BlockSpec((tb, w.shape[1]), lambda i, last=nblk - 1: (jnp.minimum(i, last), 0)))
            cast_out.append(pl.BlockSpec((w.shape[1], tb), lambda i, last=nblk - 1: (0, jnp.minimum(i, last))))
            cast_shape.append(jax.ShapeDtypeStruct((w.shape[1], width), BF16))
            transposed.append((w.shape[0], nblk))
        else:
            br, nblk = _cast_block_rows(w.shape[0], steps)
            idx = lambda i, last=nblk - 1: (jnp.minimum(i, last), 0)
            cast_in.append(pl.BlockSpec((br, w.shape[1]), idx))
            cast_out.append(pl.BlockSpec((br, w.shape[1]), idx))
            cast_shape.append(jax.ShapeDtypeStruct(w.shape, BF16))
            transposed.append(None)
    cast = [w[0] if isinstance(w, tuple) else w for w in cast]
    out = pl.pallas_call(
        functools.partial(_ffn_kernel, transposed=tuple(transposed)),
        grid=(steps,),
        in_specs=[pl.BlockSpec((tm, D_MODEL), row)] + _ffn_specs() + cast_in,
        out_specs=[pl.BlockSpec((tm, D_MODEL), row)] + cast_out,
        out_shape=[jax.ShapeDtypeStruct((n, D_MODEL), F32)] + cast_shape,
        scratch_shapes=[pltpu.VMEM((tm, D_FF), BF16)],
        compiler_params=_params(1),
        name="ffn1",
    )(x, g, wg, wu, wd, *cast)
    return out[0], out[1:]


def _ffn_stream_kernel(x_ref, g_ref, wg_ref, wu_ref, wd_ref, o_ref, wg16_ref, wu16_ref, wd16_ref,
                       h_ref, acc_ref):
    c = pl.program_id(0)

    @pl.when(c == 0)
    def _():
        h_ref[...] = _rms(x_ref[...], g_ref[...]).astype(BF16)
        acc_ref[...] = jnp.zeros_like(acc_ref)

    wg = wg_ref[...].astype(BF16)
    wu = wu_ref[...].astype(BF16)
    wd = wd_ref[...].astype(BF16)
    wg16_ref[...] = wg
    wu16_ref[...] = wu
    wd16_ref[...] = wd
    h = h_ref[...]
    gt = _dot(h, wg)
    acc_ref[...] += _dot((gt * jax.nn.sigmoid(gt) * _dot(h, wu)).astype(BF16), wd)

    @pl.when(c == pl.num_programs(0) - 1)
    def _():
        o_ref[...] = x_ref[...] + 0.5 * acc_ref[...]


def _ffn_stream(x, g, wg, wu, wd):
    n = x.shape[0]
    whole = lambda shape: pl.BlockSpec(shape, lambda c: (0, 0))
    cols = pl.BlockSpec((D_MODEL, FF_STREAM_CHUNK), lambda c: (0, c))
    rows = pl.BlockSpec((FF_STREAM_CHUNK, D_MODEL), lambda c: (c, 0))
    out = pl.pallas_call(
        _ffn_stream_kernel,
        grid=(D_FF // FF_STREAM_CHUNK,),
        in_specs=[whole((n, D_MODEL)), whole((1, D_MODEL)), cols, cols, rows],
        out_specs=[whole((n, D_MODEL)), cols, cols, rows],
        out_shape=[jax.ShapeDtypeStruct((n, D_MODEL), F32),
                   jax.ShapeDtypeStruct((D_MODEL, D_FF), BF16), jax.ShapeDtypeStruct((D_MODEL, D_FF), BF16),
                   jax.ShapeDtypeStruct((D_FF, D_MODEL), BF16)],
        scratch_shapes=[pltpu.VMEM((n, D_MODEL), BF16), pltpu.VMEM((n, D_MODEL), F32)],
        compiler_params=_params(1),
        name="ffn1_stream",
    )(x, g, wg, wu, wd)
    return out[0], out[1:]


def _head_norm_lanes(y, head, gain):
    tm, width = y.shape
    low = lax.broadcasted_iota(jnp.int32, (tm, LANES), 1) < head
    cols = []
    for j in range(width // LANES):
        yj = y[:, j * LANES:(j + 1) * LANES]
        sq = yj * yj

        def inv_rms(part):
            return lax.rsqrt(jnp.sum(part, axis=1, keepdims=True) * (1.0 / head) + EPS)

        if head == LANES:
            cols.append(yj * inv_rms(sq))
        else:
            cols.append(yj * jnp.where(low, inv_rms(jnp.where(low, sq, 0.0)), inv_rms(jnp.where(low, 0.0, sq))))
    return jnp.concatenate(cols, axis=1) * gain


def _store_heads(ref, y):
    groups = y.shape[0] // SUBLANES
    sub = lax.broadcasted_iota(jnp.int32, (groups, SUBLANES, LANES), 1)
    rows = []
    for j in range(SB_WIDTH // LANES):
        pair = y[:, j * LANES:(j + 1) * LANES].reshape(groups, SUBLANES, LANES)
        rows += [pair, pltpu.roll(pair, SB_HEAD_DIM, axis=2)]
    for dist in (4, 2, 1):
        low = (sub & dist) == 0
        new = list(rows)
        for h in range(SUBLANES):
            if h & dist == 0:
                a, b = rows[h], rows[h + dist]
                new[h] = jnp.where(low, a, pltpu.roll(b, dist, axis=1))
                new[h + dist] = jnp.where(low, pltpu.roll(a, SUBLANES - dist, axis=1), b)
        rows = new
    for t in range(SUBLANES):
        ref[:, t] = rows[t][:, :, :SB_HEAD_DIM]


def _mixin_kernel(x_ref, g_ref, w_ref, wgu_ref, bg_ref, gq_ref, gk_ref,
                  qsb_ref, ksb_ref, ksb16_ref, vsb_ref, vsb16_ref,
                  qg_ref, kg_ref, vg16_ref, r_ref, la_ref):
    h = _rms(x_ref[...], g_ref[...]).astype(BF16)

    def proj(lo, width):
        return _dot(h, w_ref[:, lo:lo + width])

    q = _head_norm_lanes(proj(0, SB_WIDTH), SB_HEAD_DIM, gq_ref[...])
    qsb_ref[...] = (q * SB_SCALE).astype(BF16)
    k = _head_norm_lanes(proj(SB_WIDTH, SB_WIDTH), SB_HEAD_DIM, gk_ref[...])
    _store_heads(ksb_ref, k)
    ksb16_ref[...] = k.astype(BF16)
    v = proj(2 * SB_WIDTH, SB_WIDTH)
    _store_heads(vsb_ref, v)
    vsb16_ref[...] = v.astype(BF16)
    off = 3 * SB_WIDTH
    qg_ref[...] = proj(off, GLA_KEY_WIDTH) * GLA_SCALE
    kg_ref[...] = proj(off + GLA_KEY_WIDTH, GLA_KEY_WIDTH)
    vg16_ref[...] = proj(off + 2 * GLA_KEY_WIDTH, GLA_WIDTH).astype(BF16)
    r_ref[...] = proj(off + 2 * GLA_KEY_WIDTH + GLA_WIDTH, GLA_WIDTH)
    lr = proj(MAIN_WIDTH, GATE_PAD).astype(BF16)
    gate = _dot(lr, wgu_ref[...]) + bg_ref[...]
    la_ref[...] = _log_sigmoid(gate) * (1.0 / GATE_TAU)


def _mixin(x, g, w, wgu, bg, gq, gk, *, tm):
    n = x.shape[0]
    row = lambda i: (i, 0)
    heads = (SB_HEADS, SB_HEAD_DIM)
    assert SB_HEADS == SUBLANES and tm % SUBLANES == 0
    lead = lambda rows, wd: (rows // SUBLANES, SUBLANES) if len(wd) == 2 else (rows,)
    widths = [((SB_WIDTH,), BF16), (heads, F32), ((SB_WIDTH,), BF16), (heads, F32), ((SB_WIDTH,), BF16),
              ((GLA_KEY_WIDTH,), F32), ((GLA_KEY_WIDTH,), F32), ((GLA_WIDTH,), BF16), ((GLA_WIDTH,), F32),
              ((GLA_KEY_WIDTH,), F32)]
    return pl.pallas_call(
        _mixin_kernel,
        grid=(n // tm,),
        in_specs=[pl.BlockSpec((tm, D_MODEL), row),
                  _const_spec((1, D_MODEL)),
                  _const_spec(w.shape),
                  _const_spec((GATE_PAD, GLA_KEY_WIDTH)),
                  _const_spec((1, GLA_KEY_WIDTH)),
                  _const_spec((1, SB_WIDTH)),
                  _const_spec((1, SB_WIDTH))],
        out_specs=[pl.BlockSpec(lead(tm, wd) + wd, lambda i, nd=len(lead(tm, wd) + wd) - 1: (i,) + (0,) * nd)
                   for wd, _ in widths],
        out_shape=[jax.ShapeDtypeStruct(lead(n, wd) + wd, dt) for wd, dt in widths],
        compiler_params=_params(1),
        name="mixer_in",
    )(x, g, w, wgu, bg, gq, gk)


def _neg_abs(x):
    return -jnp.abs(x)


def _sb_log2_terms(z):
    z2 = z * LOG2_E
    t = jnp.log2(1.0 + jnp.exp2(_neg_abs(z2)))
    return jnp.maximum(z2, 0.0) + t, jnp.minimum(z2, 0.0) - t


def _sb_kernel(q_ref, k_ref, v_ref, upair_ref, uo_ref, *refs, tq, q_block):
    if q_block is None:
        o_ref, carry_all, acc_all, live_ref = refs
    else:
        kc_hbm, vc_hbm, o_ref, carry_all, acc_all, live_ref, kbuf, vbuf, sems, kwin, vwin = refs
    pairs = q_ref.shape[1] // LANES
    heads = 2 * pairs

    def one_block(i, row0, u, phase):
        carry_ref, acc_ref = carry_all.at[u], acc_all.at[u]
        lane = lax.broadcasted_iota(jnp.int32, (tq, LANES), 1)
        head_lanes = (lane < SB_HEAD_DIM, lane >= SB_HEAD_DIM)
        rowi = lax.broadcasted_iota(jnp.int32, (heads * tq, SB_BLOCK), 0)
        coli = lax.broadcasted_iota(jnp.int32, (heads * tq, SB_BLOCK), 1)
        causal = coli < (rowi & (tq - 1))

        def pair_lanes(p):
            return slice(p * LANES, (p + 1) * LANES)

        def q_pair(p):
            q2 = q_ref[row0:row0 + tq, pair_lanes(p)]
            zero = jnp.zeros_like(q2)
            return jnp.concatenate([jnp.where(hl, q2, zero) for hl in head_lanes], axis=0)

        def rows(ref, p, first_block, nblocks):
            start = first_block * SB_BLOCK
            if not isinstance(start, int):
                start = pl.multiple_of(start, SB_BLOCK)
            return ref[pl.ds(start, nblocks * SB_BLOCK), pair_lanes(p)]

        def suffix_sums(x):
            nb = x.shape[1] // SB_BLOCK
            hi, lo = _split_bf16(x)

            def mm(cols, m2):
                return _dot(jnp.concatenate([hi[:, cols], lo[:, cols]], axis=1), m2)

            within, total = [], []
            if nb >= 3:
                for j in range(0, nb - 1, 2):
                    y = mm(slice(j * SB_BLOCK, (j + 2) * SB_BLOCK), upair_ref[...])
                    within += [y[:, :SB_BLOCK], y[:, SB_BLOCK:]]
                    total += [jnp.sum(x[:, b * SB_BLOCK:(b + 1) * SB_BLOCK], axis=1, keepdims=True) for b in (j, j + 1)]
            for j in range(len(within), nb):
                y = mm(slice(j * SB_BLOCK, (j + 1) * SB_BLOCK), uo_ref[...])
                within.append(y[:, :SB_BLOCK])
                total.append(y[:, SB_BLOCK:])
            later = total[nb - 1]
            out = [within[nb - 1]]
            for j in range(nb - 2, -1, -1):
                out.insert(0, within[j] + later)
                later = later + total[j]
            return (out[0] if nb == 1 else jnp.concatenate(out, axis=1)), later

        def resident(first_block, nblocks, kv_refs=(k_ref, v_ref)):
            return tuple((lambda p, ref=ref: rows(ref, p, first_block, nblocks)) for ref in kv_refs)

        def sweep(kv, nblocks, own, carry, valid=None):
            nk = nblocks * SB_BLOCK

            def masked(x):
                if valid is not None:
                    return jnp.where(valid, x, 0.0)
                if not own:
                    return x
                tail = jnp.where(causal, x[:, nk - SB_BLOCK:], 0.0)
                return tail if nblocks == 1 else jnp.concatenate([x[:, :nk - SB_BLOCK], tail], axis=1)

            z = jnp.concatenate([_dot_nt(q_pair(p), kv[0](p)) for p in range(pairs)], axis=0)
            fail, log_beta = _sb_log2_terms(z)
            after, total = suffix_sums(masked(fail))
            if carry is not None:
                after = after + carry
            w = masked(jnp.exp2(log_beta - after)).astype(BF16)
            pv = jnp.concatenate([_dot(w[2 * p * tq:(2 * p + 2) * tq], kv[1](p))
                                  for p in range(pairs)], axis=0)
            if carry is None:
                carry_ref[...] = total
                acc_ref[...] = pv
            else:
                carry_ref[...] = carry + total
                acc_ref[...] += pv

        def live():
            return jnp.min(carry_ref[...]) < SB_STOP

        def mark_live():
            live_ref[u] = jnp.where(live(), 1, 0)

        half = SB_BLOCK // 2
        assert HALF_WINDOW == 2

        def by_halves(x, f):
            y = [f(s, jnp.concatenate([x[s * half:(s + 1) * half], x[tq + s * half:tq + (s + 1) * half]], axis=0))
                 for s in range(2)]
            return jnp.concatenate([y[0][:half], y[1][:half], y[0][half:], y[1][half:]], axis=0)

        def half_windows():
            nk = HALF_WINDOW * SB_BLOCK
            starts = [pl.multiple_of(i * SB_BLOCK + (s + 1) * half - nk, half) for s in range(2)]
            keep = coli < (rowi & (half - 1)) + half

            def masked(x):
                return jnp.concatenate([x[:, :SB_BLOCK], jnp.where(keep, x[:, SB_BLOCK:], 0.0)], axis=1)

            z = jnp.concatenate(
                [by_halves(q_pair(p), lambda s, lhs: _dot_nt(lhs, k_ref[pl.ds(starts[s], nk), pair_lanes(p)]))
                 for p in range(pairs)], axis=0)
            fail, log_beta = _sb_log2_terms(z)
            after, total = suffix_sums(masked(fail))
            w = masked(jnp.exp2(log_beta - after)).astype(BF16)
            carry_ref[...] = total
            acc_ref[...] = jnp.concatenate(
                [by_halves(w[2 * p * tq:(2 * p + 2) * tq],
                           lambda s, lhs: _dot(lhs, v_ref[pl.ds(starts[s], nk), pair_lanes(p)]))
                 for p in range(pairs)], axis=0)

        def own_block_only():
            sweep(resident(i, 1), 1, True, None)

        if q_block is None:
            assert tq == SB_BLOCK
            windowed = i >= HALF_WINDOW
            if phase == "window":
                half_windows()
                return
            if phase == "first":
                pl.when(windowed)(half_windows)
                pl.when(jnp.logical_not(windowed))(own_block_only)
                return

            mark_live()

            @pl.when(jnp.logical_and(windowed, live_ref[u] != 0))
            def _():
                unseen = jnp.logical_or(coli < half, (rowi & half) != 0)
                sweep(resident(i - HALF_WINDOW, 1), 1, False, carry_ref[...], unseen)
                mark_live()

            n_prev = jnp.where(windowed, i - HALF_WINDOW, i)
            earlier = lambda kb: resident(kb, 1)
        else:
            recent = SB_WINDOW - 1
            assert q_block >= recent and k_ref.shape[0] == tq <= SB_BLOCK

            def fetch(first_block, nblocks):
                nk = nblocks * SB_BLOCK
                start = first_block * SB_BLOCK
                if not isinstance(start, int):
                    start = pl.multiple_of(start, SB_BLOCK)
                copies = [pltpu.make_async_copy(hbm.at[pl.program_id(0), :, :, pl.ds(start, nk)],
                                                buf.at[:, :, pl.ds(0, nk)], sems.at[n])
                          for n, (hbm, buf) in enumerate(((kc_hbm, kbuf), (vc_hbm, vbuf)))]
                for c in copies:
                    c.start()
                for c in copies:
                    c.wait()

                def slabs(buf):
                    piece = lambda hd, j: buf[hd, :, j * SB_BLOCK:(j + 1) * SB_BLOCK].T
                    return lambda j, p: jnp.concatenate([piece(2 * p, j), piece(2 * p + 1, j)], axis=1).astype(BF16)

                return slabs(kbuf), slabs(vbuf)

            def earlier(kb):
                return tuple((lambda p, slab=slab: slab(0, p)) for slab in fetch(kb, 1))

            for slab, win in zip(fetch(q_block - recent, recent), (kwin, vwin)):
                for j in range(recent):
                    for p in range(pairs):
                        win[j * SB_BLOCK:(j + 1) * SB_BLOCK, pair_lanes(p)] = slab(j, p)
            for new, win in ((k_ref, kwin), (v_ref, vwin)):
                win[recent * SB_BLOCK:recent * SB_BLOCK + tq, :] = new[...]
                win[recent * SB_BLOCK + tq:, :] = jnp.zeros((SB_BLOCK - tq, win.shape[1]), BF16)
            sweep(resident(0, SB_WINDOW, (kwin, vwin)), SB_WINDOW, True, None)
            mark_live()
            n_prev = q_block - recent

        def body(st):
            kb, _ = st
            sweep(earlier(kb), 1, False, carry_ref[...])
            return kb - 1, live()

        lax.while_loop(lambda st: jnp.logical_and(st[0] >= 0, st[1]), body, (n_prev - 1, live_ref[u] != 0))
        for p in range(pairs):
            o_ref[row0:row0 + tq, pair_lanes(p)] = jnp.where(head_lanes[0], acc_ref[2 * p * tq:(2 * p + 1) * tq],
                                                              acc_ref[(2 * p + 1) * tq:(2 * p + 2) * tq])

    if q_block is None:
        per_step = q_ref.shape[0] // tq
        blocks = [(pl.program_id(2) * per_step + u, u * tq, u) for u in range(per_step)]
        all_windowed = blocks[0][0] >= HALF_WINDOW

        @pl.when(all_windowed)
        def _():
            for blk in blocks:
                one_block(*blk, "window")

        @pl.when(jnp.logical_not(all_windowed))
        def _():
            for blk in blocks:
                one_block(*blk, "first")

        for blk in blocks:
            one_block(*blk, "rest")
    else:
        one_block(q_block, 0, 0, "rest")


def _sb(q16, k16, v16, upair, uo, *, batch, tq, keys, cache=None):
    sub = SB_QBLOCKS_PER_STEP if cache is None else 1
    nq = keys // (sub * SB_BLOCK) if cache is None else 1
    width = SB_PAIRS_PER_STEP * LANES
    assert width == SB_WIDTH or cache is None
    kv_spec = pl.BlockSpec((keys, width), lambda b, p, i: (b, p))
    q_spec = pl.BlockSpec((sub * tq, width), lambda b, p, i: (b * nq + i, p))
    heads = 2 * SB_PAIRS_PER_STEP
    in_specs = [q_spec, kv_spec, kv_spec, _const_spec(upair.shape), _const_spec(uo.shape)]
    scratch = [pltpu.VMEM((sub, heads * tq, SB_BLOCK), F32), pltpu.VMEM((sub, heads * tq, LANES), F32),
               pltpu.SMEM((sub,), jnp.int32)]
    operands = [q16, k16, v16, upair, uo]
    q_block = None
    if cache is not None:
        past = cache[0].shape[3]
        assert past % SB_BLOCK == 0 and keys == tq and SB_WINDOW * SB_BLOCK <= past + SB_BLOCK
        q_block = past // SB_BLOCK
        in_specs += [pl.BlockSpec(memory_space=pl.ANY)] * 2
        scratch += ([pltpu.VMEM((SB_HEADS, SB_HEAD_DIM, (SB_WINDOW - 1) * SB_BLOCK), F32)] * 2
                    + [pltpu.SemaphoreType.DMA((2,))]
                    + [pltpu.VMEM((SB_WINDOW * SB_BLOCK, SB_WIDTH), BF16)] * 2)
        operands += list(cache)
    return pl.pallas_call(
        functools.partial(_sb_kernel, tq=tq, q_block=q_block),
        grid=(batch, SB_WIDTH // width, nq),
        in_specs=in_specs,
        out_specs=q_spec,
        out_shape=jax.ShapeDtypeStruct((batch * nq * sub * tq, SB_WIDTH), F32),
        scratch_shapes=scratch,
        compiler_params=_params(3),
        name="sb_prompt" if cache is None else "sb_sample",
    )(*operands)


def _gla_levels(c):
    return int(np.log2(c))


def _gla_tables(c):
    nlev = _gla_levels(c)
    t = np.arange(c)[:, None]
    j = np.arange(c)[None, :]
    mats = [(j <= t), (j > t)]
    for lev in range(1, nlev + 1):
        m = 1 << (lev - 1)
        start = (t // (2 * m)) * (2 * m)
        mats.append(((t % (2 * m)) >= m) & (j >= start + m) & (j <= t))
    for lev in range(1, nlev + 1):
        m = 1 << (lev - 1)
        start = (t // (2 * m)) * (2 * m)
        mats.append(((t % (2 * m)) < m) & (j > t) & (j <= start + m - 1))
    mats = np.concatenate([a.astype(np.float32) for a in mats], axis=0)
    x = t ^ j
    level = np.where(j > t, -1, np.where(x == 0, 0, np.floor(np.log2(np.maximum(x, 1))).astype(np.int64) + 1))
    return jnp.asarray(mats, BF16), jnp.asarray(level, jnp.int32)


def _gla_kernel(q_ref, k_ref, la_ref, v_ref, s0_ref, mats_ref, lv_ref, o_ref, sfin_ref, st_ref):
    c = GLA_CHUNK
    nlev = _gla_levels(c)
    streams, t_rows = q_ref.shape[:2]

    @pl.when(pl.program_id(1) == 0)
    def _():
        st_ref[...] = s0_ref[...]

    lane = lax.broadcasted_iota(jnp.int32, (c, LANES), 1)
    head_lanes = (lane < GLA_HEAD_K, lane >= GLA_HEAD_K)
    lane_sq = lax.broadcasted_iota(jnp.int32, (GLA_HEAD_V, LANES), 1) < GLA_HEAD_K

    def pick(x, hl):
        return jnp.where(hl, x, jnp.zeros_like(x))

    def stack_heads(x):
        return jnp.concatenate([pick(x, hl) for hl in head_lanes], axis=0)

    def one_chunk(rows):
        n_rows = rows.stop - rows.start

        def chunk(x):
            return x if n_rows == c else jnp.concatenate([x, jnp.zeros((c - n_rows, x.shape[1]), x.dtype)], axis=0)

        def finish(s, p, scores, qe, ke, b_last):
            sl = slice(p * LANES, (p + 1) * LANES)
            st = st_ref[s, p]
            inter = _dot_nt(stack_heads(qe[:, sl]), st.astype(BF16))
            upd = []
            for r in range(2):
                hd = 2 * p + r
                vh = chunk(v_ref[s, rows, hd * GLA_HEAD_V:(hd + 1) * GLA_HEAD_V])
                o_ref[s, rows, hd * GLA_HEAD_V:(hd + 1) * GLA_HEAD_V] = (
                    inter[r * c:(r + 1) * c] + _dot(scores[r * c:(r + 1) * c], vh))[:n_rows]
                upd.append(_dot_tn(vh, ke[:, sl]))
            st_ref[s, p] = st * jnp.exp(b_last[:, sl]) + jnp.where(lane_sq, upd[0], upd[1])

        la_all = jnp.concatenate([chunk(la_ref[s, rows, :]) for s in range(streams)], axis=1)
        b_all = _dot_hilo_left(mats_ref[:c, :], la_all)
        mild = jnp.min(b_all[c - 1:c, :]) >= GLA_DIRECT_MIN

        def direct():
            rowi = lax.broadcasted_iota(jnp.int32, (2 * c, c), 0) & (c - 1)
            causal = lax.broadcasted_iota(jnp.int32, (2 * c, c), 1) <= rowi
            for s in range(streams):
                b = b_all[:, s * GLA_KEY_WIDTH:(s + 1) * GLA_KEY_WIDTH]
                b_last = b[c - 1:c, :]
                q = chunk(q_ref[s, rows, :])
                k = chunk(k_ref[s, rows, :])
                qe = (q * jnp.exp(b)).astype(BF16)
                kd = (k * jnp.exp(-b)).astype(BF16)
                ke = (k * jnp.exp(b_last - b)).astype(BF16)
                for p in range(GLA_HEADS // 2):
                    sl = slice(p * LANES, (p + 1) * LANES)
                    scores = jnp.where(causal, _dot_nt(stack_heads(qe[:, sl]), kd[:, sl]), 0.0)
                    finish(s, p, scores.astype(BF16), qe, ke, b_last)

        def by_levels():
            lv = jnp.concatenate([lv_ref[...], lv_ref[...]], axis=0)
            for s in range(streams):
                q = chunk(q_ref[s, rows, :])
                k = chunk(k_ref[s, rows, :])
                sums = _dot_hilo_left(mats_ref[...], chunk(la_ref[s, rows, :]))

                def level_rows(i):
                    return sums[i * c:(i + 1) * c]

                qe = (q * jnp.exp(level_rows(0))).astype(BF16)
                ke = (k * jnp.exp(level_rows(1))).astype(BF16)
                q_lev = [q.astype(BF16)] + [(q * jnp.exp(level_rows(1 + lev))).astype(BF16)
                                            for lev in range(1, nlev + 1)]
                k_lev = [k.astype(BF16)] + [(k * jnp.exp(level_rows(1 + nlev + lev))).astype(BF16)
                                            for lev in range(1, nlev + 1)]
                for p in range(GLA_HEADS // 2):
                    sl = slice(p * LANES, (p + 1) * LANES)
                    scores = jnp.zeros((2 * c, c), F32)
                    for lev in range(nlev + 1):
                        scores = jnp.where(lv == lev, _dot_nt(stack_heads(q_lev[lev][:, sl]), k_lev[lev][:, sl]),
                                           scores)
                    finish(s, p, scores.astype(BF16), qe, ke, level_rows(0)[c - 1:c, :])

        return mild, direct, by_levels

    n_chunks = max(1, t_rows // c)
    parts = [one_chunk(slice(u * c, min((u + 1) * c, t_rows))) for u in range(n_chunks)]
    all_mild = functools.reduce(jnp.logical_and, [mild for mild, _, _ in parts])

    @pl.when(all_mild)
    def _():
        for _, direct, _ in parts:
            direct()

    @pl.when(jnp.logical_not(all_mild))
    def _():
        for mild, direct, by_levels in parts:
            pl.when(mild)(direct)
            pl.when(jnp.logical_not(mild))(by_levels)

    sfin_ref[...] = st_ref[...]


def _dot_hilo_left(m, x):
    hi, lo = _split_bf16(x)
    return _dot(m, hi) + _dot(m, lo)


def _gla(qg, kg, la, vg16, s0, mats, lv, *, batch, seq):
    c = min(GLA_CHUNK * GLA_CHUNKS_PER_STEP, seq)
    assert seq % c == 0 and (c < GLA_CHUNK or c % GLA_CHUNK == 0)
    g = GLA_BATCH_PER_STEP
    pairs = GLA_HEADS // 2
    blk = lambda width: pl.BlockSpec((g, c, width), lambda b, t: (b, t, 0))
    st_spec = pl.BlockSpec((g, pairs, GLA_HEAD_V, LANES), lambda b, t: (b, 0, 0, 0))
    per_stream = lambda a: a.reshape(batch, seq, a.shape[-1])
    o, s_fin = pl.pallas_call(
        _gla_kernel,
        grid=(batch // g, seq // c),
        in_specs=[blk(GLA_KEY_WIDTH), blk(GLA_KEY_WIDTH), blk(GLA_KEY_WIDTH), blk(GLA_WIDTH), st_spec,
                  _const_spec(mats.shape), _const_spec(lv.shape)],
        out_specs=[blk(GLA_WIDTH), st_spec],
        out_shape=[jax.ShapeDtypeStruct((batch, seq, GLA_WIDTH), F32),
                   jax.ShapeDtypeStruct((batch, pairs, GLA_HEAD_V, LANES), F32)],
        scratch_shapes=[pltpu.VMEM((g, pairs, GLA_HEAD_V, LANES), F32)],
        compiler_params=_params(2),
        name="gla",
    )(per_stream(qg), per_stream(kg), per_stream(la), per_stream(vg16), s0, mats, lv)
    return o.reshape(batch * seq, GLA_WIDTH), s_fin


def _mixout_kernel(x_ref, osb_ref, og_ref, r_ref, gsb_ref, ggla_ref, w_ref,
                   g_ref, wg_ref, wu_ref, wd_ref, gfin_ref, o_ref, a_ref):
    o_sb = _head_norm_lanes(osb_ref[...], SB_HEAD_DIM, gsb_ref[...])
    r = r_ref[...]
    o_g = _head_norm_lanes(og_ref[...], GLA_HEAD_V, ggla_ref[...]) * (r * jax.nn.sigmoid(r))
    mix = _dot(o_sb.astype(BF16), w_ref[:SB_WIDTH, :]) + _dot(o_g.astype(BF16), w_ref[SB_WIDTH:, :])
    y = _half_step_ffn(x_ref[...] + mix, g_ref, wg_ref, wu_ref, wd_ref, a_ref)
    o_ref[...] = _rms(y, gfin_ref[...])


def _mixout(x, osb, og, r, gsb, ggla, w, g, wg, wu, wd, gfin, *, tm):
    n = x.shape[0]
    row = lambda i: (i, 0)
    return pl.pallas_call(
        _mixout_kernel,
        grid=(n // tm,),
        in_specs=[pl.BlockSpec((tm, D_MODEL), row),
                  pl.BlockSpec((tm, SB_WIDTH), row),
                  pl.BlockSpec((tm, GLA_WIDTH), row),
                  pl.BlockSpec((tm, GLA_WIDTH), row),
                  _const_spec((1, SB_WIDTH)),
                  _const_spec((1, GLA_WIDTH)),
                  _const_spec((D_MODEL, D_MODEL))] + _ffn_specs() + [_const_spec((1, D_MODEL))],
        out_specs=pl.BlockSpec((tm, D_MODEL), row),
        out_shape=jax.ShapeDtypeStruct((n, D_MODEL), F32),
        scratch_shapes=[pltpu.VMEM((tm, D_FF), BF16)],
        compiler_params=_params(1),
        name="mixer_out_ffn2",
    )(x, osb, og, r, gsb, ggla, w, g, wg, wu, wd, gfin)


def _suffix_matrices():
    j = np.arange(SB_BLOCK)[:, None]
    s = np.arange(SB_BLOCK)[None, :]
    later = j > s
    zero = np.zeros_like(later)
    pair = np.block([[later, zero], [zero, later]])
    single = np.concatenate([later, np.ones_like(later)], axis=1)
    twice = lambda m: jnp.asarray(np.concatenate([m, m], axis=0), BF16)
    return twice(pair), twice(single)


def _state_to_kernel(s):
    b = s.shape[0]
    s = s.reshape(b, GLA_HEADS // 2, 2, GLA_HEAD_K, GLA_HEAD_V)
    return s.transpose(0, 1, 4, 2, 3).reshape(b, GLA_HEADS // 2, GLA_HEAD_V, LANES)


def _state_from_kernel(s):
    b = s.shape[0]
    s = s.reshape(b, GLA_HEADS // 2, GLA_HEAD_V, 2, GLA_HEAD_K)
    return s.transpose(0, 1, 3, 4, 2).reshape(b, GLA_HEADS, GLA_HEAD_K, GLA_HEAD_V)


def _layer(x1, lw, consts, *, batch, seq, tm, cache=None, state=None):
    (g_mix, w_in, wgu, bg, gq, gk, gsb, ggla, w_out, g2, w2g, w2u, w2d, gfin) = lw
    upair, uo, mats, lv = consts
    (qsb16, ksb, ksb16, vsb, vsb16, qg, kg, vg16, r, la) = _mixin(x1, g_mix, w_in, wgu, bg, gq, gk, tm=tm)

    if cache is None:
        o_sb = _sb(qsb16, ksb16, vsb16, upair, uo, batch=batch, tq=SB_BLOCK, keys=seq)
    else:
        o_sb = _sb(qsb16, ksb16, vsb16, upair, uo, batch=batch, tq=seq, keys=seq, cache=cache)

    if state is None:
        state = jnp.zeros((batch, GLA_HEADS // 2, GLA_HEAD_V, LANES), F32)
    o_g, s_fin = _gla(qg, kg, la, vg16, state, mats, lv, batch=batch, seq=seq)

    y = _mixout(x1, o_sb, o_g, r, gsb, ggla, w_out, g2, w2g, w2u, w2d, gfin, tm=tm)
    return y, ksb, vsb, _state_from_kernel(s_fin)


def kernel(x_prompt, x_sample, cache_sb_k, cache_sb_v, state_gla, g_ffn1, w_ffn1_gate, w_ffn1_up,
           w_ffn1_down, g_mix, w_in, w_gate_up, b_gate, g_q, g_k, g_sb_out, g_gla_out, w_out,
           g_ffn2, w_ffn2_gate, w_ffn2_up, w_ffn2_down, g_final):
    depth = w_in.shape[0]
    batch, seq, _ = x_prompt.shape
    dec_batch, dec_seq, _ = x_sample.shape
    consts = (*_suffix_matrices(), *_gla_tables(GLA_CHUNK))

    y_p = x_prompt.reshape(batch * seq, D_MODEL)
    y_s = x_sample.reshape(dec_batch * dec_seq, D_MODEL)
    outs = [[] for _ in range(6)]
    for l in range(depth):
        row = lambda a: a[l].reshape(1, -1)
        x1_s, ffn1 = _ffn_stream(y_s, row(g_ffn1), w_ffn1_gate[l], w_ffn1_up[l], w_ffn1_down[l])
        x1_p, (w2g, w2u, w2d, w_in16, w_out16) = _ffn(
            y_p, row(g_ffn1), *ffn1, tm=FFN1_ROWS,
            cast=[w_ffn2_gate[l], w_ffn2_up[l], w_ffn2_down[l], (w_in[l].T, IN_PAD), w_out[l]])
        lw = (row(g_mix), w_in16,
              jnp.pad(w_gate_up[l], ((0, GATE_PAD - GATE_RANK), (0, 0))).astype(BF16), row(b_gate),
              jnp.tile(g_q[l], SB_HEADS).reshape(1, -1), jnp.tile(g_k[l], SB_HEADS).reshape(1, -1),
              jnp.tile(g_sb_out[l], SB_HEADS).reshape(1, -1), jnp.tile(g_gla_out[l], GLA_HEADS).reshape(1, -1),
              w_out16, row(g_ffn2), w2g, w2u, w2d, row(g_final))
        y_p, k_p, v_p, s_p = _layer(x1_p, lw, consts, batch=batch, seq=seq, tm=512)
        cache = (cache_sb_k[l].transpose(0, 2, 3, 1), cache_sb_v[l].transpose(0, 2, 3, 1))
        y_s, k_s, v_s, s_s = _layer(x1_s, lw, consts, batch=dec_batch, seq=dec_seq, tm=256,
                                    cache=cache, state=_state_to_kernel(state_gla[l]))
        shape_p = (batch, seq, SB_HEADS, SB_HEAD_DIM)
        shape_s = (dec_batch, dec_seq, SB_HEADS, SB_HEAD_DIM)
        for lst, val in zip(outs, (k_p.reshape(shape_p), v_p.reshape(shape_p), s_p,
                                   k_s.reshape(shape_s), v_s.reshape(shape_s), s_s)):
            lst.append(val)
    return (y_p.reshape(batch, seq, D_MODEL), y_s.reshape(dec_batch, dec_seq, D_MODEL),
            *[jnp.stack(o) for o in outs])
```

```python
import functools

import numpy as np
import jax
import jax.numpy as jnp
from jax import lax
from jax.experimental import pallas as pl
from jax.experimental.pallas import tpu as pltpu

F32 = jnp.float32
BF16 = jnp.bfloat16

D_MODEL = 1024
SB_HEADS = 8
SB_HEAD_DIM = 64
SB_WIDTH = SB_HEADS * SB_HEAD_DIM
SB_SCALE = SB_HEAD_DIM ** -0.5
GLA_HEADS = 4
GLA_HEAD_K = 64
GLA_HEAD_V = 128
GLA_KEY_WIDTH = GLA_HEADS * GLA_HEAD_K
GLA_WIDTH = GLA_HEADS * GLA_HEAD_V
GLA_SCALE = GLA_HEAD_K ** -0.5
GATE_RANK = 16
GATE_TAU = 16.0
D_FF = 2816
EPS = 1e-6

LANES = 128
SUBLANES = 8
BF16_SUBLANES = 16
GATE_PAD = LANES
MAIN_WIDTH = 3 * SB_WIDTH + 2 * GLA_KEY_WIDTH + 2 * GLA_WIDTH
IN_PAD = MAIN_WIDTH + GATE_PAD

FF_CHUNK = 256
FFN1_ROWS = 1024
FF_STREAM_CHUNK = 256
SB_BLOCK = 128
SB_WINDOW = 3
SB_PAIRS_PER_STEP = 4
SB_QBLOCKS_PER_STEP = 8
HALF_WINDOW = 2
GLA_CHUNK = 128
GLA_BATCH_PER_STEP = 2
GLA_CHUNKS_PER_STEP = 4
GLA_DIRECT_MIN = -40.0
SB_STOP = 152.0
LOG2_E = 1.4426950408889634

VMEM_LIMIT = 56 * 1024 * 1024


def _dot(a, b):
    return jnp.dot(a, b, preferred_element_type=F32)


def _dot_nt(a, b):
    return lax.dot_general(a, b, (((1,), (1,)), ((), ())), preferred_element_type=F32)


def _dot_tn(a, b):
    return lax.dot_general(a, b, (((0,), (0,)), ((), ())), preferred_element_type=F32)


def _split_bf16(x):
    hi = x.astype(BF16)
    lo = (x - hi.astype(F32)).astype(BF16)
    return hi, lo


def _rms(x, g):
    ms = jnp.mean(x * x, axis=-1, keepdims=True)
    return x * lax.rsqrt(ms + EPS) * g


def _log_sigmoid(x):
    return jnp.minimum(x, 0.0) - jnp.log1p(jnp.exp(-jnp.abs(x)))


def _const_spec(shape):
    nd = len(shape)
    return pl.BlockSpec(shape, lambda *_: (0,) * nd, pipeline_mode=pl.Buffered(1))


def _params(n_grid):
    return pltpu.CompilerParams(dimension_semantics=("arbitrary",) * n_grid,
                                vmem_limit_bytes=VMEM_LIMIT)


def _half_step_ffn(x, g_ref, wg_ref, wu_ref, wd_ref, a_ref):
    h = _rms(x, g_ref[...]).astype(BF16)
    for c in range(D_FF // FF_CHUNK):
        sl = slice(c * FF_CHUNK, (c + 1) * FF_CHUNK)
        gt = _dot(h, wg_ref[:, sl])
        up = _dot(h, wu_ref[:, sl])
        a_ref[:, sl] = (gt * jax.nn.sigmoid(gt) * up).astype(BF16)
    return x + 0.5 * _dot(a_ref[...], wd_ref[...])


def _ffn_kernel(x_ref, g_ref, wg_ref, wu_ref, wd_ref, *refs, transposed):
    n_cast = (len(refs) - 2) // 2
    o_ref, a_ref = refs[n_cast], refs[-1]
    o_ref[...] = _half_step_ffn(x_ref[...], g_ref, wg_ref, wu_ref, wd_ref, a_ref)
    for src, dst, tr in zip(refs[:n_cast], refs[n_cast + 1:-1], transposed):
        if tr is None:
            dst[...] = src[...].astype(BF16)
        else:
            rows, blocks = tr
            first = jnp.minimum(pl.program_id(0), blocks - 1) * src.shape[0]
            inside = lax.broadcasted_iota(jnp.int32, src.shape, 0) < rows - first
            dst[...] = jnp.where(inside, src[...], 0.0).T.astype(BF16)


def _ffn_specs():
    return [_const_spec((1, D_MODEL)), _const_spec((D_MODEL, D_FF)), _const_spec((D_MODEL, D_FF)),
            _const_spec((D_FF, D_MODEL))]


def _cast_block_rows(rows, steps):
    br = next(b for b in range(BF16_SUBLANES, rows + 1, BF16_SUBLANES) if rows % b == 0 and rows // b <= steps)
    return br, rows // br


def _ffn(x, g, wg, wu, wd, *, tm, cast=()):
    n = x.shape[0]
    steps = n // tm
    row = lambda i: (i, 0)
    cast_in, cast_out, cast_shape, transposed = [], [], [], []
    for w in cast:
        if isinstance(w, tuple):
            w, width = w
            tb = LANES * -(-width // (LANES * steps))
            nblk = -(-width // tb)
            width = nblk * tb
            assert nblk <= steps and w.shape[0] <= width
            cast_in.append(pl.BlockSpec((tb, w.shape[1]), lambda i, last=nblk - 1: (jnp.minimum(i, last), 0)))
            cast_out.append(pl.BlockSpec((w.shape[1], tb), lambda i, last=nblk - 1: (0, jnp.minimum(i, last))))
            cast_shape.append(jax.ShapeDtypeStruct((w.shape[1], width), BF16))
            transposed.append((w.shape[0], nblk))
        else:
            br, nblk = _cast_block_rows(w.shape[0], steps)
            idx = lambda i, last=nblk - 1: (jnp.minimum(i, last), 0)
            cast_in.append(pl.BlockSpec((br, w.shape[1]), idx))
            cast_out.append(pl.BlockSpec((br, w.shape[1]), idx))
            cast_shape.append(jax.ShapeDtypeStruct(w.shape, BF16))
            transposed.append(None)
    cast = [w[0] if isinstance(w, tuple) else w for w in cast]
    out = pl.pallas_call(
        functools.partial(_ffn_kernel, transposed=tuple(transposed)),
        grid=(steps,),
        in_specs=[pl.BlockSpec((tm, D_MODEL), row)] + _ffn_specs() + cast_in,
        out_specs=[pl.BlockSpec((tm, D_MODEL), row)] + cast_out,
        out_shape=[jax.ShapeDtypeStruct((n, D_MODEL), F32)] + cast_shape,
        scratch_shapes=[pltpu.VMEM((tm, D_FF), BF16)],
        compiler_params=_params(1),
        name="ffn1",
    )(x, g, wg, wu, wd, *cast)
    return out[0], out[1:]


def _ffn_stream_kernel(x_ref, g_ref, wg_ref, wu_ref, wd_ref, o_ref, wg16_ref, wu16_ref, wd16_ref,
                       h_ref, acc_ref):
    c = pl.program_id(0)

    @pl.when(c == 0)
    def _():
        h_ref[...] = _rms(x_ref[...], g_ref[...]).astype(BF16)
        acc_ref[...] = jnp.zeros_like(acc_ref)

    wg = wg_ref[...].astype(BF16)
    wu = wu_ref[...].astype(BF16)
    wd = wd_ref[...].astype(BF16)
    wg16_ref[...] = wg
    wu16_ref[...] = wu
    wd16_ref[...] = wd
    h = h_ref[...]
    gt = _dot(h, wg)
    acc_ref[...] += _dot((gt * jax.nn.sigmoid(gt) * _dot(h, wu)).astype(BF16), wd)

    @pl.when(c == pl.num_programs(0) - 1)
    def _():
        o_ref[...] = x_ref[...] + 0.5 * acc_ref[...]


def _ffn_stream(x, g, wg, wu, wd):
    n = x.shape[0]
    whole = lambda shape: pl.BlockSpec(shape, lambda c: (0, 0))
    cols = pl.BlockSpec((D_MODEL, FF_STREAM_CHUNK), lambda c: (0, c))
    rows = pl.BlockSpec((FF_STREAM_CHUNK, D_MODEL), lambda c: (c, 0))
    out = pl.pallas_call(
        _ffn_stream_kernel,
        grid=(D_FF // FF_STREAM_CHUNK,),
        in_specs=[whole((n, D_MODEL)), whole((1, D_MODEL)), cols, cols, rows],
        out_specs=[whole((n, D_MODEL)), cols, cols, rows],
        out_shape=[jax.ShapeDtypeStruct((n, D_MODEL), F32),
                   jax.ShapeDtypeStruct((D_MODEL, D_FF), BF16), jax.ShapeDtypeStruct((D_MODEL, D_FF), BF16),
                   jax.ShapeDtypeStruct((D_FF, D_MODEL), BF16)],
        scratch_shapes=[pltpu.VMEM((n, D_MODEL), BF16), pltpu.VMEM((n, D_MODEL), F32)],
        compiler_params=_params(1),
        name="ffn1_stream",
    )(x, g, wg, wu, wd)
    return out[0], out[1:]


def _head_norm_lanes(y, head, gain):
    tm, width = y.shape
    low = lax.broadcasted_iota(jnp.int32, (tm, LANES), 1) < head
    cols = []
    for j in range(width // LANES):
        yj = y[:, j * LANES:(j + 1) * LANES]
        sq = yj * yj

        def inv_rms(part):
            return lax.rsqrt(jnp.sum(part, axis=1, keepdims=True) * (1.0 / head) + EPS)

        if head == LANES:
            cols.append(yj * inv_rms(sq))
        else:
            cols.append(yj * jnp.where(low, inv_rms(jnp.where(low, sq, 0.0)), inv_rms(jnp.where(low, 0.0, sq))))
    return jnp.concatenate(cols, axis=1) * gain


def _store_heads(ref, y):
    groups = y.shape[0] // SUBLANES
    sub = lax.broadcasted_iota(jnp.int32, (groups, SUBLANES, LANES), 1)
    rows = []
    for j in range(SB_WIDTH // LANES):
        pair = y[:, j * LANES:(j + 1) * LANES].reshape(groups, SUBLANES, LANES)
        rows += [pair, pltpu.roll(pair, SB_HEAD_DIM, axis=2)]
    for dist in (4, 2, 1):
        low = (sub & dist) == 0
        new = list(rows)
        for h in range(SUBLANES):
            if h & dist == 0:
                a, b = rows[h], rows[h + dist]
                new[h] = jnp.where(low, a, pltpu.roll(b, dist, axis=1))
                new[h + dist] = jnp.where(low, pltpu.roll(a, SUBLANES - dist, axis=1), b)
        rows = new
    for t in range(SUBLANES):
        ref[:, t] = rows[t][:, :, :SB_HEAD_DIM]


def _mixin_kernel(x_ref, g_ref, w_ref, wgu_ref, bg_ref, gq_ref, gk_ref,
                  qsb_ref, ksb_ref, ksb16_ref, vsb_ref, vsb16_ref,
                  qg_ref, kg_ref, vg16_ref, r_ref, la_ref):
    h = _rms(x_ref[...], g_ref[...]).astype(BF16)

    def proj(lo, width):
        return _dot(h, w_ref[:, lo:lo + width])

    q = _head_norm_lanes(proj(0, SB_WIDTH), SB_HEAD_DIM, gq_ref[...])
    qsb_ref[...] = (q * SB_SCALE).astype(BF16)
    k = _head_norm_lanes(proj(SB_WIDTH, SB_WIDTH), SB_HEAD_DIM, gk_ref[...])
    _store_heads(ksb_ref, k)
    ksb16_ref[...] = k.astype(BF16)
    v = proj(2 * SB_WIDTH, SB_WIDTH)
    _store_heads(vsb_ref, v)
    vsb16_ref[...] = v.astype(BF16)
    off = 3 * SB_WIDTH
    qg_ref[...] = proj(off, GLA_KEY_WIDTH) * GLA_SCALE
    kg_ref[...] = proj(off + GLA_KEY_WIDTH, GLA_KEY_WIDTH)
    vg16_ref[...] = proj(off + 2 * GLA_KEY_WIDTH, GLA_WIDTH).astype(BF16)
    r_ref[...] = proj(off + 2 * GLA_KEY_WIDTH + GLA_WIDTH, GLA_WIDTH)
    lr = proj(MAIN_WIDTH, GATE_PAD).astype(BF16)
    gate = _dot(lr, wgu_ref[...]) + bg_ref[...]
    la_ref[...] = _log_sigmoid(gate) * (1.0 / GATE_TAU)


def _mixin(x, g, w, wgu, bg, gq, gk, *, tm):
    n = x.shape[0]
    row = lambda i: (i, 0)
    heads = (SB_HEADS, SB_HEAD_DIM)
    assert SB_HEADS == SUBLANES and tm % SUBLANES == 0
    lead = lambda rows, wd: (rows // SUBLANES, SUBLANES) if len(wd) == 2 else (rows,)
    widths = [((SB_WIDTH,), BF16), (heads, F32), ((SB_WIDTH,), BF16), (heads, F32), ((SB_WIDTH,), BF16),
              ((GLA_KEY_WIDTH,), F32), ((GLA_KEY_WIDTH,), F32), ((GLA_WIDTH,), BF16), ((GLA_WIDTH,), F32),
              ((GLA_KEY_WIDTH,), F32)]
    return pl.pallas_call(
        _mixin_kernel,
        grid=(n // tm,),
        in_specs=[pl.BlockSpec((tm, D_MODEL), row),
                  _const_spec((1, D_MODEL)),
                  _const_spec(w.shape),
                  _const_spec((GATE_PAD, GLA_KEY_WIDTH)),
                  _const_spec((1, GLA_KEY_WIDTH)),
                  _const_spec((1, SB_WIDTH)),
                  _const_spec((1, SB_WIDTH))],
        out_specs=[pl.BlockSpec(lead(tm, wd) + wd, lambda i, nd=len(lead(tm, wd) + wd) - 1: (i,) + (0,) * nd)
                   for wd, _ in widths],
        out_shape=[jax.ShapeDtypeStruct(lead(n, wd) + wd, dt) for wd, dt in widths],
        compiler_params=_params(1),
        name="mixer_in",
    )(x, g, w, wgu, bg, gq, gk)


def _neg_abs(x):
    return -jnp.abs(x)


def _sb_log2_terms(z):
    z2 = z * LOG2_E
    t = jnp.log2(1.0 + jnp.exp2(_neg_abs(z2)))
    return jnp.maximum(z2, 0.0) + t, jnp.minimum(z2, 0.0) - t


def _sb_kernel(q_ref, k_ref, v_ref, upair_ref, uo_ref, *refs, tq, q_block):
    if q_block is None:
        o_ref, carry_all, acc_all, live_ref = refs
    else:
        kc_hbm, vc_hbm, o_ref, carry_all, acc_all, live_ref, kbuf, vbuf, sems, kwin, vwin = refs
    pairs = q_ref.shape[1] // LANES
    heads = 2 * pairs

    def one_block(i, row0, u, phase):
        carry_ref, acc_ref = carry_all.at[u], acc_all.at[u]
        lane = lax.broadcasted_iota(jnp.int32, (tq, LANES), 1)
        head_lanes = (lane < SB_HEAD_DIM, lane >= SB_HEAD_DIM)
        rowi = lax.broadcasted_iota(jnp.int32, (heads * tq, SB_BLOCK), 0)
        coli = lax.broadcasted_iota(jnp.int32, (heads * tq, SB_BLOCK), 1)
        causal = coli < (rowi & (tq - 1))

        def pair_lanes(p):
            return slice(p * LANES, (p + 1) * LANES)

        def q_pair(p):
            q2 = q_ref[row0:row0 + tq, pair_lanes(p)]
            zero = jnp.zeros_like(q2)
            return jnp.concatenate([jnp.where(hl, q2, zero) for hl in head_lanes], axis=0)

        def rows(ref, p, first_block, nblocks):
            start = first_block * SB_BLOCK
            if not isinstance(start, int):
                start = pl.multiple_of(start, SB_BLOCK)
            return ref[pl.ds(start, nblocks * SB_BLOCK), pair_lanes(p)]

        def suffix_sums(x):
            nb = x.shape[1] // SB_BLOCK
            hi, lo = _split_bf16(x)

            def mm(cols, m2):
                return _dot(jnp.concatenate([hi[:, cols], lo[:, cols]], axis=1), m2)

            within, total = [], []
            if nb >= 3:
                for j in range(0, nb - 1, 2):
                    y = mm(slice(j * SB_BLOCK, (j + 2) * SB_BLOCK), upair_ref[...])
                    within += [y[:, :SB_BLOCK], y[:, SB_BLOCK:]]
                    total += [jnp.sum(x[:, b * SB_BLOCK:(b + 1) * SB_BLOCK], axis=1, keepdims=True) for b in (j, j + 1)]
            for j in range(len(within), nb):
                y = mm(slice(j * SB_BLOCK, (j + 1) * SB_BLOCK), uo_ref[...])
                within.append(y[:, :SB_BLOCK])
                total.append(y[:, SB_BLOCK:])
            later = total[nb - 1]
            out = [within[nb - 1]]
            for j in range(nb - 2, -1, -1):
                out.insert(0, within[j] + later)
                later = later + total[j]
            return (out[0] if nb == 1 else jnp.concatenate(out, axis=1)), later

        def resident(first_block, nblocks, kv_refs=(k_ref, v_ref)):
            return tuple((lambda p, ref=ref: rows(ref, p, first_block, nblocks)) for ref in kv_refs)

        def sweep(kv, nblocks, own, carry, valid=None):
            nk = nblocks * SB_BLOCK

            def masked(x):
                if valid is not None:
                    return jnp.where(valid, x, 0.0)
                if not own:
                    return x
                tail = jnp.where(causal, x[:, nk - SB_BLOCK:], 0.0)
                return tail if nblocks == 1 else jnp.concatenate([x[:, :nk - SB_BLOCK], tail], axis=1)

            z = jnp.concatenate([_dot_nt(q_pair(p), kv[0](p)) for p in range(pairs)], axis=0)
            fail, log_beta = _sb_log2_terms(z)
            after, total = suffix_sums(masked(fail))
            if carry is not None:
                after = after + carry
            w = masked(jnp.exp2(log_beta - after)).astype(BF16)
            pv = jnp.concatenate([_dot(w[2 * p * tq:(2 * p + 2) * tq], kv[1](p))
                                  for p in range(pairs)], axis=0)
            if carry is None:
                carry_ref[...] = total
                acc_ref[...] = pv
            else:
                carry_ref[...] = carry + total
                acc_ref[...] += pv

        def live():
            return jnp.min(carry_ref[...]) < SB_STOP

        def mark_live():
            live_ref[u] = jnp.where(live(), 1, 0)

        half = SB_BLOCK // 2
        assert HALF_WINDOW == 2

        def by_halves(x, f):
            y = [f(s, jnp.concatenate([x[s * half:(s + 1) * half], x[tq + s * half:tq + (s + 1) * half]], axis=0))
                 for s in range(2)]
            return jnp.concatenate([y[0][:half], y[1][:half], y[0][half:], y[1][half:]], axis=0)

        def half_windows():
            nk = HALF_WINDOW * SB_BLOCK
            starts = [pl.multiple_of(i * SB_BLOCK + (s + 1) * half - nk, half) for s in range(2)]
            keep = coli < (rowi & (half - 1)) + half

            def masked(x):
                return jnp.concatenate([x[:, :SB_BLOCK], jnp.where(keep, x[:, SB_BLOCK:], 0.0)], axis=1)

            z = jnp.concatenate(
                [by_halves(q_pair(p), lambda s, lhs: _dot_nt(lhs, k_ref[pl.ds(starts[s], nk), pair_lanes(p)]))
                 for p in range(pairs)], axis=0)
            fail, log_beta = _sb_log2_terms(z)
            after, total = suffix_sums(masked(fail))
            w = masked(jnp.exp2(log_beta - after)).astype(BF16)
            carry_ref[...] = total
            acc_ref[...] = jnp.concatenate(
                [by_halves(w[2 * p * tq:(2 * p + 2) * tq],
                           lambda s, lhs: _dot(lhs, v_ref[pl.ds(starts[s], nk), pair_lanes(p)]))
                 for p in range(pairs)], axis=0)

        def own_block_only():
            sweep(resident(i, 1), 1, True, None)

        if q_block is None:
            assert tq == SB_BLOCK
            windowed = i >= HALF_WINDOW
            if phase == "window":
                half_windows()
                return
            if phase == "first":
                pl.when(windowed)(half_windows)
                pl.when(jnp.logical_not(windowed))(own_block_only)
                return

            mark_live()

            @pl.when(jnp.logical_and(windowed, live_ref[u] != 0))
            def _():
                unseen = jnp.logical_or(coli < half, (rowi & half) != 0)
                sweep(resident(i - HALF_WINDOW, 1), 1, False, carry_ref[...], unseen)
                mark_live()

            n_prev = jnp.where(windowed, i - HALF_WINDOW, i)
            earlier = lambda kb: resident(kb, 1)
        else:
            recent = SB_WINDOW - 1
            assert q_block >= recent and k_ref.shape[0] == tq <= SB_BLOCK

            def fetch(first_block, nblocks):
                nk = nblocks * SB_BLOCK
                start = first_block * SB_BLOCK
                if not isinstance(start, int):
                    start = pl.multiple_of(start, SB_BLOCK)
                copies = [pltpu.make_async_copy(hbm.at[pl.program_id(0), :, :, pl.ds(start, nk)],
                                                buf.at[:, :, pl.ds(0, nk)], sems.at[n])
                          for n, (hbm, buf) in enumerate(((kc_hbm, kbuf), (vc_hbm, vbuf)))]
                for c in copies:
                    c.start()
                for c in copies:
                    c.wait()

                def slabs(buf):
                    piece = lambda hd, j: buf[hd, :, j * SB_BLOCK:(j + 1) * SB_BLOCK].T
                    return lambda j, p: jnp.concatenate([piece(2 * p, j), piece(2 * p + 1, j)], axis=1).astype(BF16)

                return slabs(kbuf), slabs(vbuf)

            def earlier(kb):
                return tuple((lambda p, slab=slab: slab(0, p)) for slab in fetch(kb, 1))

            for slab, win in zip(fetch(q_block - recent, recent), (kwin, vwin)):
                for j in range(recent):
                    for p in range(pairs):
                        win[j * SB_BLOCK:(j + 1) * SB_BLOCK, pair_lanes(p)] = slab(j, p)
            for new, win in ((k_ref, kwin), (v_ref, vwin)):
                win[recent * SB_BLOCK:recent * SB_BLOCK + tq, :] = new[...]
                win[recent * SB_BLOCK + tq:, :] = jnp.zeros((SB_BLOCK - tq, win.shape[1]), BF16)
            sweep(resident(0, SB_WINDOW, (kwin, vwin)), SB_WINDOW, True, None)
            mark_live()
            n_prev = q_block - recent

        def body(st):
            kb, _ = st
            sweep(earlier(kb), 1, False, carry_ref[...])
            return kb - 1, live()

        lax.while_loop(lambda st: jnp.logical_and(st[0] >= 0, st[1]), body, (n_prev - 1, live_ref[u] != 0))
        for p in range(pairs):
            o_ref[row0:row0 + tq, pair_lanes(p)] = jnp.where(head_lanes[0], acc_ref[2 * p * tq:(2 * p + 1) * tq],
                                                              acc_ref[(2 * p + 1) * tq:(2 * p + 2) * tq])

    if q_block is None:
        per_step = q_ref.shape[0] // tq
        blocks = [(pl.program_id(2) * per_step + u, u * tq, u) for u in range(per_step)]
        all_windowed = blocks[0][0] >= HALF_WINDOW

        @pl.when(all_windowed)
        def _():
            for blk in blocks:
                one_block(*blk, "window")

        @pl.when(jnp.logical_not(all_windowed))
        def _():
            for blk in blocks:
                one_block(*blk, "first")

        for blk in blocks:
            one_block(*blk, "rest")
    else:
        one_block(q_block, 0, 0, "rest")


def _sb(q16, k16, v16, upair, uo, *, batch, tq, keys, cache=None):
    sub = SB_QBLOCKS_PER_STEP if cache is None else 1
    nq = keys // (sub * SB_BLOCK) if cache is None else 1
    width = SB_PAIRS_PER_STEP * LANES
    assert width == SB_WIDTH or cache is None
    kv_spec = pl.BlockSpec((keys, width), lambda b, p, i: (b, p))
    q_spec = pl.BlockSpec((sub * tq, width), lambda b, p, i: (b * nq + i, p))
    heads = 2 * SB_PAIRS_PER_STEP
    in_specs = [q_spec, kv_spec, kv_spec, _const_spec(upair.shape), _const_spec(uo.shape)]
    scratch = [pltpu.VMEM((sub, heads * tq, SB_BLOCK), F32), pltpu.VMEM((sub, heads * tq, LANES), F32),
               pltpu.SMEM((sub,), jnp.int32)]
    operands = [q16, k16, v16, upair, uo]
    q_block = None
    if cache is not None:
        past = cache[0].shape[3]
        assert past % SB_BLOCK == 0 and keys == tq and SB_WINDOW * SB_BLOCK <= past + SB_BLOCK
        q_block = past // SB_BLOCK
        in_specs += [pl.BlockSpec(memory_space=pl.ANY)] * 2
        scratch += ([pltpu.VMEM((SB_HEADS, SB_HEAD_DIM, (SB_WINDOW - 1) * SB_BLOCK), F32)] * 2
                    + [pltpu.SemaphoreType.DMA((2,))]
                    + [pltpu.VMEM((SB_WINDOW * SB_BLOCK, SB_WIDTH), BF16)] * 2)
        operands += list(cache)
    return pl.pallas_call(
        functools.partial(_sb_kernel, tq=tq, q_block=q_block),
        grid=(batch, SB_WIDTH // width, nq),
        in_specs=in_specs,
        out_specs=q_spec,
        out_shape=jax.ShapeDtypeStruct((batch * nq * sub * tq, SB_WIDTH), F32),
        scratch_shapes=scratch,
        compiler_params=_params(3),
        name="sb_prompt" if cache is None else "sb_sample",
    )(*operands)


def _gla_levels(c):
    return int(np.log2(c))


def _gla_tables(c):
    nlev = _gla_levels(c)
    t = np.arange(c)[:, None]
    j = np.arange(c)[None, :]
    mats = [(j <= t), (j > t)]
    for lev in range(1, nlev + 1):
        m = 1 << (lev - 1)
        start = (t // (2 * m)) * (2 * m)
        mats.append(((t % (2 * m)) >= m) & (j >= start + m) & (j <= t))
    for lev in range(1, nlev + 1):
        m = 1 << (lev - 1)
        start = (t // (2 * m)) * (2 * m)
        mats.append(((t % (2 * m)) < m) & (j > t) & (j <= start + m - 1))
    mats = np.concatenate([a.astype(np.float32) for a in mats], axis=0)
    x = t ^ j
    level = np.where(j > t, -1, np.where(x == 0, 0, np.floor(np.log2(np.maximum(x, 1))).astype(np.int64) + 1))
    return jnp.asarray(mats, BF16), jnp.asarray(level, jnp.int32)


def _gla_kernel(q_ref, k_ref, la_ref, v_ref, s0_ref, mats_ref, lv_ref, o_ref, sfin_ref, st_ref):
    c = GLA_CHUNK
    nlev = _gla_levels(c)
    streams, t_rows = q_ref.shape[:2]

    @pl.when(pl.program_id(1) == 0)
    def _():
        st_ref[...] = s0_ref[...]

    lane = lax.broadcasted_iota(jnp.int32, (c, LANES), 1)
    head_lanes = (lane < GLA_HEAD_K, lane >= GLA_HEAD_K)
    lane_sq = lax.broadcasted_iota(jnp.int32, (GLA_HEAD_V, LANES), 1) < GLA_HEAD_K

    def pick(x, hl):
        return jnp.where(hl, x, jnp.zeros_like(x))

    def stack_heads(x):
        return jnp.concatenate([pick(x, hl) for hl in head_lanes], axis=0)

    def one_chunk(rows):
        n_rows = rows.stop - rows.start

        def chunk(x):
            return x if n_rows == c else jnp.concatenate([x, jnp.zeros((c - n_rows, x.shape[1]), x.dtype)], axis=0)

        def finish(s, p, scores, qe, ke, b_last):
            sl = slice(p * LANES, (p + 1) * LANES)
            st = st_ref[s, p]
            inter = _dot_nt(stack_heads(qe[:, sl]), st.astype(BF16))
            upd = []
            for r in range(2):
                hd = 2 * p + r
                vh = chunk(v_ref[s, rows, hd * GLA_HEAD_V:(hd + 1) * GLA_HEAD_V])
                o_ref[s, rows, hd * GLA_HEAD_V:(hd + 1) * GLA_HEAD_V] = (
                    inter[r * c:(r + 1) * c] + _dot(scores[r * c:(r + 1) * c], vh))[:n_rows]
                upd.append(_dot_tn(vh, ke[:, sl]))
            st_ref[s, p] = st * jnp.exp(b_last[:, sl]) + jnp.where(lane_sq, upd[0], upd[1])

        la_all = jnp.concatenate([chunk(la_ref[s, rows, :]) for s in range(streams)], axis=1)
        b_all = _dot_hilo_left(mats_ref[:c, :], la_all)
        mild = jnp.min(b_all[c - 1:c, :]) >= GLA_DIRECT_MIN

        def direct():
            rowi = lax.broadcasted_iota(jnp.int32, (2 * c, c), 0) & (c - 1)
            causal = lax.broadcasted_iota(jnp.int32, (2 * c, c), 1) <= rowi
            for s in range(streams):
                b = b_all[:, s * GLA_KEY_WIDTH:(s + 1) * GLA_KEY_WIDTH]
                b_last = b[c - 1:c, :]
                q = chunk(q_ref[s, rows, :])
                k = chunk(k_ref[s, rows, :])
                qe = (q * jnp.exp(b)).astype(BF16)
                kd = (k * jnp.exp(-b)).astype(BF16)
                ke = (k * jnp.exp(b_last - b)).astype(BF16)
                for p in range(GLA_HEADS // 2):
                    sl = slice(p * LANES, (p + 1) * LANES)
                    scores = jnp.where(causal, _dot_nt(stack_heads(qe[:, sl]), kd[:, sl]), 0.0)
                    finish(s, p, scores.astype(BF16), qe, ke, b_last)

        def by_levels():
            lv = jnp.concatenate([lv_ref[...], lv_ref[...]], axis=0)
            for s in range(streams):
                q = chunk(q_ref[s, rows, :])
                k = chunk(k_ref[s, rows, :])
                sums = _dot_hilo_left(mats_ref[...], chunk(la_ref[s, rows, :]))

                def level_rows(i):
                    return sums[i * c:(i + 1) * c]

                qe = (q * jnp.exp(level_rows(0))).astype(BF16)
                ke = (k * jnp.exp(level_rows(1))).astype(BF16)
                q_lev = [q.astype(BF16)] + [(q * jnp.exp(level_rows(1 + lev))).astype(BF16)
                                            for lev in range(1, nlev + 1)]
                k_lev = [k.astype(BF16)] + [(k * jnp.exp(level_rows(1 + nlev + lev))).astype(BF16)
                                            for lev in range(1, nlev + 1)]
                for p in range(GLA_HEADS // 2):
                    sl = slice(p * LANES, (p + 1) * LANES)
                    scores = jnp.zeros((2 * c, c), F32)
                    for lev in range(nlev + 1):
                        scores = jnp.where(lv == lev, _dot_nt(stack_heads(q_lev[lev][:, sl]), k_lev[lev][:, sl]),
                                           scores)
                    finish(s, p, scores.astype(BF16), qe, ke, level_rows(0)[c - 1:c, :])

        return mild, direct, by_levels

    n_chunks = max(1, t_rows // c)
    parts = [one_chunk(slice(u * c, min((u + 1) * c, t_rows))) for u in range(n_chunks)]
    all_mild = functools.reduce(jnp.logical_and, [mild for mild, _, _ in parts])

    @pl.when(all_mild)
    def _():
        for _, direct, _ in parts:
            direct()

    @pl.when(jnp.logical_not(all_mild))
    def _():
        for mild, direct, by_levels in parts:
            pl.when(mild)(direct)
            pl.when(jnp.logical_not(mild))(by_levels)

    sfin_ref[...] = st_ref[...]


def _dot_hilo_left(m, x):
    hi, lo = _split_bf16(x)
    return _dot(m, hi) + _dot(m, lo)


def _gla(qg, kg, la, vg16, s0, mats, lv, *, batch, seq):
    c = min(GLA_CHUNK * GLA_CHUNKS_PER_STEP, seq)
    assert seq % c == 0 and (c < GLA_CHUNK or c % GLA_CHUNK == 0)
    g = GLA_BATCH_PER_STEP
    pairs = GLA_HEADS // 2
    blk = lambda width: pl.BlockSpec((g, c, width), lambda b, t: (b, t, 0))
    st_spec = pl.BlockSpec((g, pairs, GLA_HEAD_V, LANES), lambda b, t: (b, 0, 0, 0))
    per_stream = lambda a: a.reshape(batch, seq, a.shape[-1])
    o, s_fin = pl.pallas_call(
        _gla_kernel,
        grid=(batch // g, seq // c),
        in_specs=[blk(GLA_KEY_WIDTH), blk(GLA_KEY_WIDTH), blk(GLA_KEY_WIDTH), blk(GLA_WIDTH), st_spec,
                  _const_spec(mats.shape), _const_spec(lv.shape)],
        out_specs=[blk(GLA_WIDTH), st_spec],
        out_shape=[jax.ShapeDtypeStruct((batch, seq, GLA_WIDTH), F32),
                   jax.ShapeDtypeStruct((batch, pairs, GLA_HEAD_V, LANES), F32)],
        scratch_shapes=[pltpu.VMEM((g, pairs, GLA_HEAD_V, LANES), F32)],
        compiler_params=_params(2),
        name="gla",
    )(per_stream(qg), per_stream(kg), per_stream(la), per_stream(vg16), s0, mats, lv)
    return o.reshape(batch * seq, GLA_WIDTH), s_fin


def _mixout_kernel(x_ref, osb_ref, og_ref, r_ref, gsb_ref, ggla_ref, w_ref,
                   g_ref, wg_ref, wu_ref, wd_ref, gfin_ref, o_ref, a_ref):
    o_sb = _head_norm_lanes(osb_ref[...], SB_HEAD_DIM, gsb_ref[...])
    r = r_ref[...]
    o_g = _head_norm_lanes(og_ref[...], GLA_HEAD_V, ggla_ref[...]) * (r * jax.nn.sigmoid(r))
    mix = _dot(o_sb.astype(BF16), w_ref[:SB_WIDTH, :]) + _dot(o_g.astype(BF16), w_ref[SB_WIDTH:, :])
    y = _half_step_ffn(x_ref[...] + mix, g_ref, wg_ref, wu_ref, wd_ref, a_ref)
    o_ref[...] = _rms(y, gfin_ref[...])


def _mixout(x, osb, og, r, gsb, ggla, w, g, wg, wu, wd, gfin, *, tm):
    n = x.shape[0]
    row = lambda i: (i, 0)
    return pl.pallas_call(
        _mixout_kernel,
        grid=(n // tm,),
        in_specs=[pl.BlockSpec((tm, D_MODEL), row),
                  pl.BlockSpec((tm, SB_WIDTH), row),
                  pl.BlockSpec((tm, GLA_WIDTH), row),
                  pl.BlockSpec((tm, GLA_WIDTH), row),
                  _const_spec((1, SB_WIDTH)),
                  _const_spec((1, GLA_WIDTH)),
                  _const_spec((D_MODEL, D_MODEL))] + _ffn_specs() + [_const_spec((1, D_MODEL))],
        out_specs=pl.BlockSpec((tm, D_MODEL), row),
        out_shape=jax.ShapeDtypeStruct((n, D_MODEL), F32),
        scratch_shapes=[pltpu.VMEM((tm, D_FF), BF16)],
        compiler_params=_params(1),
        name="mixer_out_ffn2",
    )(x, osb, og, r, gsb, ggla, w, g, wg, wu, wd, gfin)


def _suffix_matrices():
    j = np.arange(SB_BLOCK)[:, None]
    s = np.arange(SB_BLOCK)[None, :]
    later = j > s
    zero = np.zeros_like(later)
    pair = np.block([[later, zero], [zero, later]])
    single = np.concatenate([later, np.ones_like(later)], axis=1)
    twice = lambda m: jnp.asarray(np.concatenate([m, m], axis=0), BF16)
    return twice(pair), twice(single)


def _state_to_kernel(s):
    b = s.shape[0]
    s = s.reshape(b, GLA_HEADS // 2, 2, GLA_HEAD_K, GLA_HEAD_V)
    return s.transpose(0, 1, 4, 2, 3).reshape(b, GLA_HEADS // 2, GLA_HEAD_V, LANES)


def _state_from_kernel(s):
    b = s.shape[0]
    s = s.reshape(b, GLA_HEADS // 2, GLA_HEAD_V, 2, GLA_HEAD_K)
    return s.transpose(0, 1, 3, 4, 2).reshape(b, GLA_HEADS, GLA_HEAD_K, GLA_HEAD_V)


def _layer(x1, lw, consts, *, batch, seq, tm, cache=None, state=None):
    (g_mix, w_in, wgu, bg, gq, gk, gsb, ggla, w_out, g2, w2g, w2u, w2d, gfin) = lw
    upair, uo, mats, lv = consts
    (qsb16, ksb, ksb16, vsb, vsb16, qg, kg, vg16, r, la) = _mixin(x1, g_mix, w_in, wgu, bg, gq, gk, tm=tm)

    if cache is None:
        o_sb = _sb(qsb16, ksb16, vsb16, upair, uo, batch=batch, tq=SB_BLOCK, keys=seq)
    else:
        o_sb = _sb(qsb16, ksb16, vsb16, upair, uo, batch=batch, tq=seq, keys=seq, cache=cache)

    if state is None:
        state = jnp.zeros((batch, GLA_HEADS // 2, GLA_HEAD_V, LANES), F32)
    o_g, s_fin = _gla(qg, kg, la, vg16, state, mats, lv, batch=batch, seq=seq)

    y = _mixout(x1, o_sb, o_g, r, gsb, ggla, w_out, g2, w2g, w2u, w2d, gfin, tm=tm)
    return y, ksb, vsb, _state_from_kernel(s_fin)


def kernel(x_prompt, x_sample, cache_sb_k, cache_sb_v, state_gla, g_ffn1, w_ffn1_gate, w_ffn1_up,
           w_ffn1_down, g_mix, w_in, w_gate_up, b_gate, g_q, g_k, g_sb_out, g_gla_out, w_out,
           g_ffn2, w_ffn2_gate, w_ffn2_up, w_ffn2_down, g_final):
    depth = w_in.shape[0]
    batch, seq, _ = x_prompt.shape
    dec_batch, dec_seq, _ = x_sample.shape
    consts = (*_suffix_matrices(), *_gla_tables(GLA_CHUNK))

    y_p = x_prompt.reshape(batch * seq, D_MODEL)
    y_s = x_sample.reshape(dec_batch * dec_seq, D_MODEL)
    outs = [[] for _ in range(6)]
    for l in range(depth):
        row = lambda a: a[l].reshape(1, -1)
        x1_s, ffn1 = _ffn_stream(y_s, row(g_ffn1), w_ffn1_gate[l], w_ffn1_up[l], w_ffn1_down[l])
        x1_p, (w2g, w2u, w2d, w_in16, w_out16) = _ffn(
            y_p, row(g_ffn1), *ffn1, tm=FFN1_ROWS,
            cast=[w_ffn2_gate[l], w_ffn2_up[l], w_ffn2_down[l], (w_in[l].T, IN_PAD), w_out[l]])
        lw = (row(g_mix), w_in16,
              jnp.pad(w_gate_up[l], ((0, GATE_PAD - GATE_RANK), (0, 0))).astype(BF16), row(b_gate),
              jnp.tile(g_q[l], SB_HEADS).reshape(1, -1), jnp.tile(g_k[l], SB_HEADS).reshape(1, -1),
              jnp.tile(g_sb_out[l], SB_HEADS).reshape(1, -1), jnp.tile(g_gla_out[l], GLA_HEADS).reshape(1, -1),
              w_out16, row(g_ffn2), w2g, w2u, w2d, row(g_final))
        y_p, k_p, v_p, s_p = _layer(x1_p, lw, consts, batch=batch, seq=seq, tm=512)
        cache = (cache_sb_k[l].transpose(0, 2, 3, 1), cache_sb_v[l].transpose(0, 2, 3, 1))
        y_s, k_s, v_s, s_s = _layer(x1_s, lw, consts, batch=dec_batch, seq=dec_seq, tm=256,
                                    cache=cache, state=_state_to_kernel(state_gla[l]))
        shape_p = (batch, seq, SB_HEADS, SB_HEAD_DIM)
        shape_s = (dec_batch, dec_seq, SB_HEADS, SB_HEAD_DIM)
        for lst, val in zip(outs, (k_p.reshape(shape_p), v_p.reshape(shape_p), s_p,
                                   k_s.reshape(shape_s), v_s.reshape(shape_s), s_s)):
            lst.append(val)
    return (y_p.reshape(batch, seq, D_MODEL), y_s.reshape(dec_batch, dec_seq, D_MODEL),
            *[jnp.stack(o) for o in outs])
```
